```python
import jax
import jax.numpy as jnp
from jax import lax
import numpy as np

D_MODEL = 1024
BATCH = 2
SEQ = 16384
DEPTH = 2

MIX_W = D_MODEL // 2
N_BRANCHES = 3
RET_HEADS = 4
RET_DV = MIX_W // RET_HEADS
RET_DK = RET_DV // 2
RET_CHUNK = 128
ML_HEADS = 4
ML_DH = MIX_W // ML_HEADS
ML_CONV = 4
ML_QK_BLOCK = 4
ML_CHUNK = 128
ATT_PATTERNS = ((128, 1), (512, 4), (2048, 16))
ATT_GROUPS = len(ATT_PATTERNS)
ATT_HEADS = 4
ATT_DV = MIX_W // ATT_HEADS
ATT_DK = ATT_DV // 2
N_EXPERTS = 32
TOP_K = 4
D_FF = D_MODEL
SWIGLU_LIMIT = 7.0
SWIGLU_ALPHA = 1.702
MOE_BLOCK = 128
DN_ALPHA = (2.0 * DEPTH) ** 0.25
DN_BETA = (8.0 * DEPTH) ** -0.25
EPS = 1e-5

IN_SIZES = (RET_HEADS * RET_DK, RET_HEADS * RET_DK, MIX_W, MIX_W,
            MIX_W, ML_HEADS, ML_HEADS, MIX_W,
            ATT_GROUPS * ATT_HEADS * ATT_DK, ATT_GROUPS * ATT_HEADS * ATT_DK, ATT_GROUPS * ATT_HEADS * ATT_DV,
            N_BRANCHES * D_MODEL)
IN_SPLITS = tuple(int(v) for v in np.cumsum(IN_SIZES)[:-1])
D_IN = int(sum(IN_SIZES))

kernel_name = 'hybrid_retention_mlstm_dilattn_moe_deepnorm'


def _layer_norm(x, g, b):
    xf = x.astype(jnp.float32)
    mu = jnp.mean(xf, -1, keepdims=True)
    var = jnp.mean(jnp.square(xf - mu), -1, keepdims=True)
    return ((xf - mu) * lax.rsqrt(var + EPS) * g.astype(jnp.float32) + b.astype(jnp.float32)).astype(x.dtype)


def _head_norm(h, g):
    mu = jnp.mean(h, -1, keepdims=True)
    var = jnp.mean(jnp.square(h - mu), -1, keepdims=True)
    y = (h - mu) * lax.rsqrt(var + EPS)
    return y.reshape(h.shape[:-2] + (-1,)) * g.astype(jnp.float32)


def _causal_conv(x, w, b):
    k_, c_ = w.shape
    y = lax.conv_general_dilated(x, w[:, None, :].astype(x.dtype), window_strides=(1,), padding=[(k_ - 1, 0)],
                                 dimension_numbers=('NWC', 'WIO', 'NWC'), feature_group_count=c_)
    return y + b


def _retention(q, k, v):
    f32 = jnp.float32
    b_, s_, h_, dk = q.shape
    c = RET_CHUNK
    n = s_ // c
    log_gamma = jnp.log1p(-jnp.exp2(-5.0 - jnp.arange(h_, dtype=f32)))
    q = q.astype(f32).reshape(b_, n, c, h_, dk)
    k = k.astype(f32).reshape(b_, n, c, h_, dk) * (dk ** -0.5)
    v = v.astype(f32).reshape(b_, n, c, h_, -1)
    pos = jnp.arange(c, dtype=f32)
    diff = pos[:, None] - pos[None, :]
    decay = jnp.where(diff >= 0, jnp.exp(log_gamma[:, None, None] * jnp.maximum(diff, 0.0)), 0.0)
    inner = jnp.einsum('bnhij,bnjhe->bnihe', jnp.einsum('bnihd,bnjhd->bnhij', q, k) * decay, v)
    k_decay = jnp.exp(log_gamma[:, None] * (c - 1.0 - pos))
    q_decay = jnp.exp(log_gamma[:, None] * (pos + 1.0))
    chunk_kv = jnp.einsum('bnjhd,bnjhe,hj->nbhde', k, v, k_decay)
    chunk_decay = jnp.exp(log_gamma * c)[None, :, None, None]

    def step(state, kv):
        return chunk_decay * state + kv, state

    _, prev = lax.scan(step, jnp.zeros(chunk_kv.shape[1:], f32), chunk_kv)
    cross = jnp.einsum('bnihd,nbhde,hi->bnihe', q, prev, q_decay)
    return (inner + cross).reshape(b_, s_, h_, -1)


def _mlstm(q, k, v, i_pre, f_pre):
    f32 = jnp.float32
    b_, s_, h_, dh = q.shape
    c = ML_CHUNK
    n = s_ // c

    def chunks(t):
        return t.astype(f32).reshape(b_, n, c, h_, dh).transpose(0, 3, 1, 2, 4)

    q = chunks(q)
    k = chunks(k) * (dh ** -0.5)
    v = chunks(v)
    ig = i_pre.reshape(b_, n, c, h_).transpose(0, 3, 1, 2)
    lf = jax.nn.log_sigmoid(f_pre).reshape(b_, n, c, h_).transpose(0, 3, 1, 2)
    cum = jnp.cumsum(lf, axis=-1)
    tot = cum[..., -1]
    a = tot[..., None] - cum + ig
    a_max = jnp.max(a, -1)
    wa = jnp.exp(a - a_max[..., None])
    chunk_c = jnp.einsum('bhnl,bhnld,bhnle->nbhde', wa, k, v)
    chunk_n = jnp.einsum('bhnl,bhnld->nbhd', wa, k)

    def step(carry, inp):
        cm, nv, m = carry
        cc, cn, g, am = inp
        m_new = jnp.maximum(g + m, am)
        s_old = jnp.exp(g + m - m_new)
        s_new = jnp.exp(am - m_new)
        cm_new = s_old[..., None, None] * cm + s_new[..., None, None] * cc
        nv_new = s_old[..., None] * nv + s_new[..., None] * cn
        return (cm_new, nv_new, m_new), (cm, nv, m)

    init = (jnp.zeros((b_, h_, dh, dh), f32), jnp.zeros((b_, h_, dh), f32), jnp.zeros((b_, h_), f32))
    _, (c_prev, n_prev, m_prev) = lax.scan(step, init, (chunk_c, chunk_n, tot.transpose(2, 0, 1), a_max.transpose(2, 0, 1)))
    m_inter = cum + m_prev.transpose(1, 2, 0)[..., None]
    causal = jnp.tril(jnp.ones((c, c), bool))
    dlog = jnp.where(causal, cum[..., :, None] - cum[..., None, :] + ig[..., None, :], -jnp.inf)
    m_q = jnp.maximum(m_inter, jnp.max(dlog, -1))
    w_qk = jnp.exp(dlog - m_q[..., None]) * jnp.einsum('bhnjd,bhnsd->bhnjs', q, k)
    inter = jnp.exp(m_inter - m_q)
    num = jnp.einsum('bhnjs,bhnse->bhnje', w_qk, v) + inter[..., None] * jnp.einsum('bhnjd,nbhde->bhnje', q, c_prev)
    den = jnp.sum(w_qk, -1) + inter * jnp.einsum('bhnjd,nbhd->bhnj', q, n_prev)
    h = num / jnp.maximum(jnp.abs(den), jnp.exp(-m_q))[..., None]
    return h.transpose(0, 2, 3, 1, 4).reshape(b_, s_, h_, dh)


def _alibi_slopes(n):
    return jnp.exp2(-8.0 * jnp.arange(1, n + 1, dtype=jnp.float32) / n)


def _dilated_group(q, k, v, window, dilation, slopes):
    f32 = jnp.float32
    b_, s_, h_, dk = q.shape
    wb = window // dilation
    l_sub = s_ // dilation
    n_blk = -(-l_sub // wb)
    l_pad = n_blk * wb

    def strided(t):
        t = t.astype(f32).reshape(b_, l_sub, dilation, h_, -1).transpose(0, 2, 1, 3, 4).reshape(b_ * dilation, l_sub, h_, -1)
        t = jnp.pad(t, ((0, 0), (0, l_pad - l_sub), (0, 0), (0, 0)))
        return t.reshape(b_ * dilation, n_blk, wb, h_, -1)

    def unstrided(t):
        return t.reshape(b_, dilation, l_pad, h_, -1)[:, :, :l_sub].transpose(0, 2, 1, 3, 4).reshape(b_, s_, h_, -1)

    def shift(t):
        return jnp.pad(t, ((0, 0), (1, 0), (0, 0), (0, 0), (0, 0)))[:, :-1]

    qs = strided(q) * (dk ** -0.5)
    ks = strided(k)
    vs = strided(v)
    kk = jnp.concatenate([shift(ks), ks], axis=2)
    vv = jnp.concatenate([shift(vs), vs], axis=2)
    scores = jnp.einsum('znihd,znjhd->znhij', qs, kk)
    qi = jnp.arange(wb)[:, None]
    kj = jnp.arange(2 * wb)[None, :]
    delta = qi + wb - kj
    valid = ((delta >= 0) & (delta <= wb))[None] & ((jnp.arange(n_blk)[:, None, None] > 0) | (kj >= wb)[None])
    bias = -slopes[:, None, None] * (dilation * delta).astype(f32)
    scores = jnp.where(valid[:, None], scores + bias, -jnp.inf)
    lse = jax.nn.logsumexp(scores, axis=-1)
    out = jnp.einsum('znhij,znjhe->znihe', jnp.exp(scores - lse[..., None]), vv)
    return unstrided(out), unstrided(jnp.swapaxes(lse, 2, 3)[..., None])


def _dilated_attention(q, k, v):
    slopes = _alibi_slopes(ATT_GROUPS * ATT_HEADS).reshape(ATT_GROUPS, ATT_HEADS)
    outs, lses = [], []
    for g, (window, dilation) in enumerate(ATT_PATTERNS):
        o, l = _dilated_group(q[:, :, g], k[:, :, g], v[:, :, g], window, dilation, slopes[g])
        outs.append(o)
        lses.append(l)
    wts = jax.nn.softmax(jnp.stack(lses), axis=0)
    return jnp.sum(wts * jnp.stack(outs), axis=0)


def _moe(x, w_router, b_router, w_gate, b_gate, w_up, b_up, w_down, b_down):
    b_, s_, d_ = x.shape
    xt = x.reshape(-1, d_)
    t_ = xt.shape[0]
    logits = (xt @ w_router + b_router).astype(jnp.float32)
    top_val, top_idx = lax.top_k(logits, TOP_K)
    probs = jax.nn.softmax(top_val, axis=-1)
    n_assign = t_ * TOP_K
    flat_e = top_idx.reshape(-1)
    order = jnp.argsort(flat_e)
    e_sorted = flat_e[order]
    tok_sorted = (order // TOP_K).astype(jnp.int32)
    p_sorted = probs.reshape(-1)[order]
    counts = jnp.bincount(flat_e, length=N_EXPERTS)
    padded = (counts + MOE_BLOCK - 1) // MOE_BLOCK * MOE_BLOCK
    start = jnp.cumsum(counts) - counts
    pstart = jnp.cumsum(padded) - padded
    n_blocks = -(-n_assign // MOE_BLOCK) + N_EXPERTS
    dest = pstart[e_sorted] + jnp.arange(n_assign) - start[e_sorted]
    row_tok = jnp.zeros((n_blocks * MOE_BLOCK,), jnp.int32).at[dest].set(tok_sorted)
    row_p = jnp.zeros((n_blocks * MOE_BLOCK,), jnp.float32).at[dest].set(p_sorted)
    block_e = jnp.minimum(jnp.searchsorted(jnp.cumsum(padded) // MOE_BLOCK, jnp.arange(n_blocks), side='right'), N_EXPERTS - 1)

    def block(acc, inp):
        e, tok, p = inp
        xb = xt[tok]
        gate = jnp.minimum(xb @ w_gate[e] + b_gate[e], SWIGLU_LIMIT)
        up = jnp.clip(xb @ w_up[e] + b_up[e], -SWIGLU_LIMIT, SWIGLU_LIMIT)
        y = ((up + 1.0) * gate * jax.nn.sigmoid(SWIGLU_ALPHA * gate)) @ w_down[e] + b_down[e]
        return acc.at[tok].add(y * p[:, None].astype(y.dtype)), None

    out, _ = lax.scan(block, jnp.zeros_like(xt), (block_e, row_tok.reshape(n_blocks, MOE_BLOCK), row_p.reshape(n_blocks, MOE_BLOCK)))
    return out.reshape(b_, s_, d_)


def _layer(x, w_in, ret_gn, ml_conv_w, ml_conv_b, ml_wq, ml_wk, ml_wv, ml_bi, ml_bf, ml_gn, ml_skip,
           w_branch, w_out, ln1_g, ln1_b, w_router, b_router, w_gate, b_gate, w_up, b_up, w_down, b_down, ln2_g, ln2_b):
    f32 = jnp.float32
    b_, s_, d_ = x.shape
    (r_q, r_k, r_v, r_g, m_x, m_i, m_f, m_o, a_q, a_k, a_v, gates) = jnp.split(x @ w_in, IN_SPLITS, axis=-1)
    ret = _retention(r_q.reshape(b_, s_, RET_HEADS, RET_DK), r_k.reshape(b_, s_, RET_HEADS, RET_DK),
                     r_v.reshape(b_, s_, RET_HEADS, RET_DV))
    y_ret = _head_norm(ret, ret_gn) * jax.nn.silu(r_g.astype(f32))
    m_c = jax.nn.silu(_causal_conv(m_x, ml_conv_w, ml_conv_b))

    def blockdiag(t, w):
        return jnp.einsum('bsnc,ncd->bsnd', t.reshape(b_, s_, -1, ML_QK_BLOCK), w).reshape(b_, s_, ML_HEADS, ML_DH)

    h_ml = _mlstm(blockdiag(m_c, ml_wq), blockdiag(m_c, ml_wk), blockdiag(m_x, ml_wv),
                  m_i.astype(f32) + ml_bi.astype(f32), m_f.astype(f32) + ml_bf.astype(f32))
    y_ml = jax.nn.sigmoid(m_o.astype(f32)) * (_head_norm(h_ml, ml_gn) + ml_skip.astype(f32) * m_c.astype(f32))
    att = _dilated_attention(a_q.reshape(b_, s_, ATT_GROUPS, ATT_HEADS, ATT_DK),
                             a_k.reshape(b_, s_, ATT_GROUPS, ATT_HEADS, ATT_DK),
                             a_v.reshape(b_, s_, ATT_GROUPS, ATT_HEADS, ATT_DV))
    y_att = att.reshape(b_, s_, MIX_W)
    gate_pre = gates.reshape(b_, s_, N_BRANCHES, d_)
    merged = jax.nn.sigmoid(gate_pre[:, :, 0]) * (y_ret.astype(x.dtype) @ w_branch[0])
    merged = merged + jax.nn.sigmoid(gate_pre[:, :, 1]) * (y_ml.astype(x.dtype) @ w_branch[1])
    merged = merged + jax.nn.sigmoid(gate_pre[:, :, 2]) * (y_att.astype(x.dtype) @ w_branch[2])
    x = _layer_norm(DN_ALPHA * x + merged @ w_out, ln1_g, ln1_b)
    x = _layer_norm(DN_ALPHA * x + _moe(x, w_router, b_router, w_gate, b_gate, w_up, b_up, w_down, b_down), ln2_g, ln2_b)
    return x


def setup_inputs(seed: int = 0) -> dict:
    key = jax.random.key(seed)
    ks = jax.random.split(key, 26)
    f32 = jnp.float32
    L = DEPTH
    nb = MIX_W // ML_QK_BLOCK

    def nrm(k, shape, scale):
        return jax.random.normal(k, shape, f32) * scale

    return {
        'x': nrm(ks[0], (BATCH, SEQ, D_MODEL), 1.0),
        'w_in': nrm(ks[1], (L, D_MODEL, D_IN), D_MODEL ** -0.5),
        'ret_gn': 1.0 + nrm(ks[2], (L, MIX_W), 0.01),
        'ml_conv_w': nrm(ks[3], (L, ML_CONV, MIX_W), ML_CONV ** -0.5),
        'ml_conv_b': nrm(ks[4], (L, MIX_W), 0.01),
        'ml_wq': nrm(ks[5], (L, nb, ML_QK_BLOCK, ML_QK_BLOCK), ML_QK_BLOCK ** -0.5),
        'ml_wk': nrm(ks[6], (L, nb, ML_QK_BLOCK, ML_QK_BLOCK), ML_QK_BLOCK ** -0.5),
        'ml_wv': nrm(ks[7], (L, nb, ML_QK_BLOCK, ML_QK_BLOCK), ML_QK_BLOCK ** -0.5),
        'ml_bi': nrm(ks[8], (L, ML_HEADS), 0.1),
        'ml_bf': jnp.linspace(3.0, 6.0, ML_HEADS, dtype=f32) + nrm(ks[9], (L, ML_HEADS), 0.1),
        'ml_gn': 1.0 + nrm(ks[10], (L, MIX_W), 0.01),
        'ml_skip': 1.0 + nrm(ks[11], (L, MIX_W), 0.1),
        'w_branch': nrm(ks[12], (L, N_BRANCHES, MIX_W, D_MODEL), MIX_W ** -0.5 * DN_BETA),
        'w_out': nrm(ks[13], (L, D_MODEL, D_MODEL), D_MODEL ** -0.5 * DN_BETA),
        'ln1_g': 1.0 + nrm(ks[14], (L, D_MODEL), 0.01),
        'ln1_b': nrm(ks[15], (L, D_MODEL), 0.01),
        'w_router': nrm(ks[16], (L, D_MODEL, N_EXPERTS), D_MODEL ** -0.5),
        'b_router': nrm(ks[17], (L, N_EXPERTS), 0.01),
        'w_gate': nrm(ks[18], (L, N_EXPERTS, D_MODEL, D_FF), D_MODEL ** -0.5),
        'b_gate': nrm(ks[19], (L, N_EXPERTS, D_FF), 0.01),
        'w_up': nrm(ks[20], (L, N_EXPERTS, D_MODEL, D_FF), D_MODEL ** -0.5),
        'b_up': nrm(ks[21], (L, N_EXPERTS, D_FF), 0.01),
        'w_down': nrm(ks[22], (L, N_EXPERTS, D_FF, D_MODEL), D_FF ** -0.5 * DN_BETA),
        'b_down': nrm(ks[23], (L, N_EXPERTS, D_MODEL), 0.01),
        'ln2_g': 1.0 + nrm(ks[24], (L, D_MODEL), 0.01),
        'ln2_b': nrm(ks[25], (L, D_MODEL), 0.01),
    }


def reference(x, w_in, ret_gn, ml_conv_w, ml_conv_b, ml_wq, ml_wk, ml_wv, ml_bi, ml_bf, ml_gn, ml_skip,
              w_branch, w_out, ln1_g, ln1_b, w_router, b_router, w_gate, b_gate, w_up, b_up, w_down, b_down, ln2_g, ln2_b):
    for l in range(DEPTH):
        x = _layer(x, w_in[l], ret_gn[l], ml_conv_w[l], ml_conv_b[l], ml_wq[l], ml_wk[l], ml_wv[l], ml_bi[l], ml_bf[l],
                   ml_gn[l], ml_skip[l], w_branch[l], w_out[l], ln1_g[l], ln1_b[l], w_router[l], b_router[l],
                   w_gate[l], b_gate[l], w_up[l], b_up[l], w_down[l], b_down[l], ln2_g[l], ln2_b[l])
    return x
```

```python
import functools

import jax
import jax.numpy as jnp
import numpy as np
from jax import lax
from jax.experimental import pallas as pl
from jax.experimental.pallas import tpu as pltpu

F32 = jnp.float32
BF16 = jnp.bfloat16

D_MODEL = 1024
DEPTH = 2
MIX_W = D_MODEL // 2
N_BRANCHES = 3
RET_HEADS = 4
RET_DV = MIX_W // RET_HEADS
RET_DK = RET_DV // 2
ML_HEADS = 4
ML_DH = MIX_W // ML_HEADS
ML_CONV = 4
ML_QK_BLOCK = 4
ATT_PATTERNS = ((128, 1), (512, 4), (2048, 16))
ATT_GROUPS = len(ATT_PATTERNS)
ATT_HEADS = 4
ATT_DV = MIX_W // ATT_HEADS
ATT_DK = ATT_DV // 2
N_EXPERTS = 32
TOP_K = 4
D_FF = D_MODEL
SWIGLU_LIMIT = 7.0
SWIGLU_ALPHA = 1.702
DN_ALPHA = (2.0 * DEPTH) ** 0.25
EPS = 1e-5

CHUNK = 128
LANES = 128
NEG = -1e30
VMEM_LIMIT = 56 * 1024 * 1024

IN_SIZES = (RET_HEADS * RET_DK, RET_HEADS * RET_DK, MIX_W, MIX_W,
            MIX_W, ML_HEADS, ML_HEADS, MIX_W,
            ATT_GROUPS * ATT_HEADS * ATT_DK, ATT_GROUPS * ATT_HEADS * ATT_DK, ATT_GROUPS * ATT_HEADS * ATT_DV,
            N_BRANCHES * D_MODEL)
IN_OFFS = tuple(int(v) for v in np.cumsum((0,) + IN_SIZES))


def _params(sem):
    return pltpu.CompilerParams(dimension_semantics=sem, vmem_limit_bytes=VMEM_LIMIT)


def _dot(a, b):
    return jnp.dot(a, b, preferred_element_type=F32)


def _dot_nt(a, b):
    return lax.dot_general(a, b, (((1,), (1,)), ((), ())), preferred_element_type=F32)


def _dot_tn(a, b):
    return lax.dot_general(a, b, (((0,), (0,)), ((), ())), preferred_element_type=F32)


def _sigmoid(x):
    return 1.0 / (1.0 + jnp.exp(-x))


def _mm_kernel(x_ref, w_ref, o_ref):
    o_ref[...] = _dot(x_ref[...], w_ref[...]).astype(o_ref.dtype)


def _matmul(x, w, out_dtype, tm, tn):
    t, k = x.shape
    n = w.shape[1]
    tm = min(tm, t)
    tn = max(c for c in range(LANES, min(tn, n) + 1, LANES) if n % c == 0)
    assert t % tm == 0
    return pl.pallas_call(
        _mm_kernel,
        grid=(n // tn, t // tm),
        in_specs=[pl.BlockSpec((tm, k), lambda j, i: (i, 0)),
                  pl.BlockSpec((k, tn), lambda j, i: (0, j))],
        out_specs=pl.BlockSpec((tm, tn), lambda j, i: (i, j)),
        out_shape=jax.ShapeDtypeStruct((t, n), out_dtype),
        compiler_params=_params(("arbitrary", "arbitrary")),
        name="in_proj",
    )(x, w)


def _head_norm(o, gn):
    mu = jnp.mean(o, axis=-1, keepdims=True)
    oc = o - mu
    var = jnp.mean(oc * oc, axis=-1, keepdims=True)
    return oc * lax.rsqrt(var + EPS) * gn


def _ret_kernel(blk_ref, dm_ref, qd_ref, kd_ref, cd_ref, gn_ref, o_ref, st_ref, *, nc):
    @pl.when(pl.program_id(1) == 0)
    def _():
        st_ref[...] = jnp.zeros_like(st_ref)

    hq = RET_HEADS * RET_DK
    for c in range(nc):
        rows = pl.ds(c * CHUNK, CHUNK)
        for h in range(RET_HEADS):
            q = blk_ref[rows, h * RET_DK:(h + 1) * RET_DK]
            k = blk_ref[rows, hq + h * RET_DK:hq + (h + 1) * RET_DK]
            v = blk_ref[rows, 2 * hq + h * RET_DV:2 * hq + (h + 1) * RET_DV]
            g = blk_ref[rows, 2 * hq + MIX_W + h * RET_DV:2 * hq + MIX_W + (h + 1) * RET_DV].astype(F32)
            st = st_ref[h]
            a = (_dot_nt(q, k) * dm_ref[h]).astype(BF16)
            o = _dot(a, v) + _dot(q, st.astype(BF16)) * qd_ref[h]
            kv = _dot_tn(k, (v.astype(F32) * kd_ref[h]).astype(BF16))
            st_ref[h] = cd_ref[h] * st + kv
            y = _head_norm(o, gn_ref[:, h * RET_DV:(h + 1) * RET_DV]) * (g * _sigmoid(g))
            o_ref[rows, h * RET_DV:(h + 1) * RET_DV] = y.astype(o_ref.dtype)


def _retention(ret_in, ret_gn, b, s, rows):
    nc = rows // CHUNK
    scale = RET_DK ** -0.5
    log_gamma = jnp.log1p(-jnp.exp2(-5.0 - jnp.arange(RET_HEADS, dtype=F32)))
    pos = jnp.arange(CHUNK, dtype=F32)
    diff = pos[:, None] - pos[None, :]
    dm = jnp.where(diff >= 0, jnp.exp(log_gamma[:, None, None] * jnp.maximum(diff, 0.0)), 0.0) * scale
    kd = jnp.broadcast_to(jnp.exp(log_gamma[:, None] * (CHUNK - 1.0 - pos))[:, :, None], (RET_HEADS, CHUNK, RET_DV))
    qd = jnp.broadcast_to((jnp.exp(log_gamma[:, None] * (pos + 1.0)) * scale)[:, :, None], (RET_HEADS, CHUNK, RET_DV))
    cd = jnp.broadcast_to(jnp.exp(log_gamma * CHUNK)[:, None, None], (RET_HEADS, 1, RET_DV))
    w_in = ret_in.shape[-1]
    full = lambda shp: pl.BlockSpec(shp, lambda bi, si: (0,) * len(shp))
    return pl.pallas_call(
        functools.partial(_ret_kernel, nc=nc),
        grid=(b, s // rows),
        in_specs=[pl.BlockSpec((None, rows, w_in), lambda bi, si: (bi, si, 0)),
                  full((RET_HEADS, CHUNK, CHUNK)), full((RET_HEADS, CHUNK, RET_DV)),
                  full((RET_HEADS, CHUNK, RET_DV)), full((RET_HEADS, 1, RET_DV)), full((1, MIX_W))],
        out_specs=pl.BlockSpec((None, rows, MIX_W), lambda bi, si: (bi, si, 0)),
        out_shape=jax.ShapeDtypeStruct((b, s, MIX_W), BF16),
        scratch_shapes=[pltpu.VMEM((RET_HEADS, RET_DK, RET_DV), F32)],
        compiler_params=_params(("arbitrary", "arbitrary")),
        name="retention",
    )(ret_in.reshape(b, s, w_in), dm, qd, kd, cd, ret_gn.reshape(1, MIX_W).astype(F32))


def _log_sigmoid(x):
    return jnp.minimum(x, 0.0) - jnp.log(1.0 + jnp.exp(-jnp.abs(x)))


def _ml_kernel(ml_ref, if_ref, cw_ref, cb_ref, bq_ref, bk_ref, bv_ref, gb_ref, gn_ref, sk_ref,
               o_ref, c_ref, n_ref, m_ref, halo_ref, *, nc):
    rows_total = nc * CHUNK

    @pl.when(pl.program_id(1) == 0)
    def _():
        c_ref[...] = jnp.zeros_like(c_ref)
        n_ref[...] = jnp.zeros_like(n_ref)
        m_ref[...] = jnp.zeros_like(m_ref)
        halo_ref[...] = jnp.zeros_like(halo_ref)

    x_all = ml_ref[:, 0:MIX_W].astype(F32)
    xf = jnp.concatenate([halo_ref[...], x_all], axis=0)
    acc = jnp.broadcast_to(cb_ref[...], (rows_total, MIX_W))
    for j in range(ML_CONV):
        sh = ML_CONV - 1 - j
        xs = xf if sh == 0 else pltpu.roll(xf, sh, 0)
        acc = acc + xs[8:8 + rows_total] * cw_ref[j:j + 1, :]
    halo_ref[...] = x_all[rows_total - 8:rows_total]
    mc_all = acc * _sigmoid(acc)

    ri = lax.broadcasted_iota(jnp.int32, (CHUNK, CHUNK), 0)
    ci = lax.broadcasted_iota(jnp.int32, (CHUNK, CHUNK), 1)
    tri = ri >= ci
    tril = tri.astype(F32)
    kscale = ML_DH ** -0.5

    for c in range(nc):
        r0 = c * CHUNK
        g_pre = if_ref[r0:r0 + CHUNK, :] + gb_ref[...]
        lf = _log_sigmoid(g_pre)
        cum = jnp.dot(tril, lf, preferred_element_type=F32, precision=lax.Precision.HIGHEST)
        g_t = g_pre.T
        cum_t = cum.T
        for h in range(ML_HEADS):
            cols = slice(h * ML_DH, (h + 1) * ML_DH)
            mc = mc_all[r0:r0 + CHUNK, cols]
            xv = x_all[r0:r0 + CHUNK, cols]
            q = _dot(mc.astype(BF16), bq_ref[h]).astype(BF16)
            k = (_dot(mc.astype(BF16), bk_ref[h]) * kscale).astype(BF16)
            v = _dot(xv.astype(BF16), bv_ref[h]).astype(BF16)
            cum_col = cum[:, ML_HEADS + h:ML_HEADS + h + 1]
            ig_col = g_pre[:, h:h + 1]
            cum_row = cum_t[ML_HEADS + h:ML_HEADS + h + 1, :]
            ig_row = g_t[h:h + 1, :]
            tot = cum[CHUNK - 1:CHUNK, ML_HEADS + h:ML_HEADS + h + 1]
            m_prev = m_ref[h:h + 1, 0:1]
            c_prev = c_ref[h]
            n_prev = n_ref[h:h + 1, :]
            dlog = jnp.where(tri, cum_col + (ig_row - cum_row), NEG)
            m_inter = cum_col + m_prev
            m_q = jnp.maximum(m_inter, jnp.max(dlog, axis=-1, keepdims=True))
            w_qk = jnp.exp(dlog - m_q) * _dot_nt(q, k)
            inter = jnp.exp(m_inter - m_q)
            num = _dot(w_qk.astype(BF16), v) + inter * _dot(q, c_prev.astype(BF16))
            den = jnp.sum(w_qk, axis=-1, keepdims=True) + inter * jnp.sum(q.astype(F32) * n_prev, axis=-1, keepdims=True)
            hh = num * (1.0 / jnp.maximum(jnp.abs(den), jnp.exp(-m_q)))
            a_col = tot - cum_col + ig_col
            a_max = jnp.max(a_col, axis=0, keepdims=True)
            wa = jnp.exp(a_col - a_max)
            kf = k.astype(F32)
            chunk_c = _dot_tn(k, (wa * v.astype(F32)).astype(BF16))
            chunk_n = jnp.sum(wa * kf, axis=0, keepdims=True)
            m_new = jnp.maximum(tot + m_prev, a_max)
            s_old = jnp.exp(tot + m_prev - m_new)
            s_new = jnp.exp(a_max - m_new)
            c_ref[h] = s_old * c_prev + s_new * chunk_c
            n_ref[h:h + 1, :] = s_old * n_prev + s_new * chunk_n
            m_ref[h:h + 1, :] = jnp.broadcast_to(m_new, (1, LANES))
            mo = ml_ref[r0:r0 + CHUNK, MIX_W + h * ML_DH:MIX_W + (h + 1) * ML_DH].astype(F32)
            y = _sigmoid(mo) * (_head_norm(hh, gn_ref[:, cols]) + sk_ref[:, cols] * mc)
            o_ref[r0:r0 + CHUNK, cols] = y.astype(o_ref.dtype)


def _block_diag(w):
    nb = w.shape[0]
    per = nb // ML_HEADS
    wh = w.reshape(ML_HEADS, per, ML_QK_BLOCK, ML_QK_BLOCK)
    eye = jnp.eye(per, dtype=w.dtype)
    bd = jnp.einsum('hncd,nm->hncmd', wh, eye)
    return bd.reshape(ML_HEADS, per * ML_QK_BLOCK, per * ML_QK_BLOCK).astype(BF16)


def _mlstm(ml_in, if_in, p, b, s, rows):
    nc = rows // CHUNK
    full = lambda shp: pl.BlockSpec(shp, lambda bi, si: (0,) * len(shp))
    gb = jnp.zeros((1, LANES), F32).at[0, 0:ML_HEADS].set(p['ml_bi'].astype(F32)).at[0, ML_HEADS:2 * ML_HEADS].set(p['ml_bf'].astype(F32))
    row = lambda a: a.reshape(1, MIX_W).astype(F32)
    return pl.pallas_call(
        functools.partial(_ml_kernel, nc=nc),
        grid=(b, s // rows),
        in_specs=[pl.BlockSpec((None, rows, 2 * MIX_W), lambda bi, si: (bi, si, 0)),
                  pl.BlockSpec((None, rows, LANES), lambda bi, si: (bi, si, 0)),
                  full((ML_CONV, MIX_W)), full((1, MIX_W)),
                  full((ML_HEADS, ML_DH, ML_DH)), full((ML_HEADS, ML_DH, ML_DH)), full((ML_HEADS, ML_DH, ML_DH)),
                  full((1, LANES)), full((1, MIX_W)), full((1, MIX_W))],
        out_specs=pl.BlockSpec((None, rows, MIX_W), lambda bi, si: (bi, si, 0)),
        out_shape=jax.ShapeDtypeStruct((b, s, MIX_W), BF16),
        scratch_shapes=[pltpu.VMEM((ML_HEADS, ML_DH, ML_DH), F32), pltpu.VMEM((8, ML_DH), F32),
                        pltpu.VMEM((8, LANES), F32), pltpu.VMEM((8, MIX_W), F32)],
        compiler_params=_params(("arbitrary", "arbitrary")),
        name="mlstm",
    )(ml_in.reshape(b, s, 2 * MIX_W), if_in.reshape(b, s, LANES),
      p['ml_conv_w'].astype(F32), row(p['ml_conv_b']),
      _block_diag(p['ml_wq']), _block_diag(p['ml_wk']), _block_diag(p['ml_wv']),
      gb, row(p['ml_gn']), row(p['ml_skip']))


def _att_kernel(q_ref, kc_ref, kp_ref, vc_ref, vp_ref, bias_ref, o_ref, lse_ref):
    lane = lax.broadcasted_iota(jnp.int32, (CHUNK, LANES), 1)
    lse_out = jnp.zeros((CHUNK, LANES), F32)
    for h in range(ATT_HEADS):
        dk = slice(h * ATT_DK, (h + 1) * ATT_DK)
        dv = slice(h * ATT_DV, (h + 1) * ATT_DV)
        q = q_ref[:, dk]
        sp = _dot_nt(q, kp_ref[:, dk]) * (ATT_DK ** -0.5) + bias_ref[h, :, 0:CHUNK]
        sc = _dot_nt(q, kc_ref[:, dk]) * (ATT_DK ** -0.5) + bias_ref[h, :, CHUNK:2 * CHUNK]
        m = jnp.maximum(jnp.max(sp, axis=-1, keepdims=True), jnp.max(sc, axis=-1, keepdims=True))
        pp = jnp.exp(sp - m)
        pc = jnp.exp(sc - m)
        l = jnp.sum(pp, axis=-1, keepdims=True) + jnp.sum(pc, axis=-1, keepdims=True)
        o = _dot(pp.astype(BF16), vp_ref[:, dv]) + _dot(pc.astype(BF16), vc_ref[:, dv])
        o_ref[:, dv] = (o * (1.0 / l)).astype(o_ref.dtype)
        lse_out = jnp.where(lane == h, m + jnp.log(l), lse_out)
    lse_ref[...] = lse_out


def _att_bias(window, dilation, slopes):
    wb = window // dilation
    qi = jnp.arange(wb)[:, None]
    kj = jnp.arange(2 * wb)[None, :]
    delta = qi + wb - kj
    band = (delta >= 0) & (delta <= wb)
    bias = -slopes[:, None, None] * (dilation * delta).astype(F32)
    later = jnp.where(band[None], bias, NEG)
    first = jnp.where((band & (kj >= wb))[None], bias, NEG)
    return jnp.stack([first, later]).astype(F32)


def _dilated_group(qkv, g, b, s):
    window, d = ATT_PATTERNS[g]
    assert window // d == CHUNK
    l_sub = s // d
    assert l_sub % CHUNK == 0
    n_blk = l_sub // CHUNK
    slopes = jnp.exp2(-8.0 * jnp.arange(1, ATT_GROUPS * ATT_HEADS + 1, dtype=F32) / (ATT_GROUPS * ATT_HEADS))
    bias = _att_bias(window, d, slopes.reshape(ATT_GROUPS, ATT_HEADS)[g])
    wq = ATT_HEADS * ATT_DK
    wv = ATT_HEADS * ATT_DV
    per_tok = 2 * wq + wv
    x = qkv.reshape(b, l_sub, d * per_tok)
    qb = per_tok // wq
    vb = per_tok // wv
    prev = lambda n: jnp.maximum(n - 1, 0)
    o, lse = pl.pallas_call(
        _att_kernel,
        grid=(b, d, n_blk),
        in_specs=[pl.BlockSpec((None, CHUNK, wq), lambda bi, r, n: (bi, n, r * qb)),
                  pl.BlockSpec((None, CHUNK, wq), lambda bi, r, n: (bi, n, r * qb + 1)),
                  pl.BlockSpec((None, CHUNK, wq), lambda bi, r, n: (bi, prev(n), r * qb + 1)),
                  pl.BlockSpec((None, CHUNK, wv), lambda bi, r, n: (bi, n, r * vb + 1)),
                  pl.BlockSpec((None, CHUNK, wv), lambda bi, r, n: (bi, prev(n), r * vb + 1)),
                  pl.BlockSpec((None, ATT_HEADS, CHUNK, 2 * CHUNK), lambda bi, r, n: (jnp.minimum(n, 1), 0, 0, 0))],
        out_specs=[pl.BlockSpec((None, CHUNK, wv), lambda bi, r, n: (bi, n, r)),
                   pl.BlockSpec((None, CHUNK, LANES), lambda bi, r, n: (bi, n, r))],
        out_shape=[jax.ShapeDtypeStruct((b, l_sub, d * wv), BF16),
                   jax.ShapeDtypeStruct((b, l_sub, d * LANES), F32)],
        compiler_params=_params(("arbitrary", "arbitrary", "arbitrary")),
        name=f"dilated_attn_g{g}",
    )(x, x, x, x, x, bias)
    return o.reshape(b * s, wv), lse.reshape(b * s, LANES)


def _layer_norm(z, g, b):
    mu = jnp.mean(z, axis=-1, keepdims=True)
    zc = z - mu
    var = jnp.mean(zc * zc, axis=-1, keepdims=True)
    return zc * lax.rsqrt(var + EPS) * g + b


def _merge_kernel(x_ref, yr_ref, ym_ref, o0_ref, o1_ref, o2_ref, l0_ref, l1_ref, l2_ref, gt_ref,
                  wb_ref, wo_ref, g1_ref, b1_ref, wrh_ref, wrl_ref, br_ref,
                  x1_ref, ir_ref, pr_ref, cnt_ref, carry_ref, *, tm):
    i = pl.program_id(0)

    @pl.when(i == 0)
    def _():
        carry_ref[...] = jnp.zeros_like(carry_ref)

    l0, l1, l2 = l0_ref[...], l1_ref[...], l2_ref[...]
    lm = jnp.maximum(jnp.maximum(l0, l1), l2)
    e0, e1, e2 = jnp.exp(l0 - lm), jnp.exp(l1 - lm), jnp.exp(l2 - lm)
    inv = 1.0 / (e0 + e1 + e2)
    parts = []
    for h in range(ATT_HEADS):
        dv = slice(h * ATT_DV, (h + 1) * ATT_DV)
        hs = slice(h, h + 1)
        parts.append((e0[:, hs] * o0_ref[:, dv].astype(F32) + e1[:, hs] * o1_ref[:, dv].astype(F32)
                      + e2[:, hs] * o2_ref[:, dv].astype(F32)) * inv[:, hs])
    y_att = jnp.concatenate(parts, axis=-1).astype(BF16)

    merged = _sigmoid(gt_ref[:, 0:D_MODEL].astype(F32)) * _dot(yr_ref[...], wb_ref[0])
    merged = merged + _sigmoid(gt_ref[:, D_MODEL:2 * D_MODEL].astype(F32)) * _dot(ym_ref[...], wb_ref[1])
    merged = merged + _sigmoid(gt_ref[:, 2 * D_MODEL:3 * D_MODEL].astype(F32)) * _dot(y_att, wb_ref[2])
    z = DN_ALPHA * x_ref[...] + _dot(merged.astype(BF16), wo_ref[...])
    x1 = _layer_norm(z, g1_ref[...], b1_ref[...])
    x1_ref[...] = x1

    xh = x1.astype(BF16)
    xl = (x1 - xh.astype(F32)).astype(BF16)
    logits = _dot(xh, wrh_ref[...]) + _dot(xl, wrh_ref[...]) + _dot(xh, wrl_ref[...]) + br_ref[...]
    lane = lax.broadcasted_iota(jnp.int32, (tm, LANES), 1)
    work = jnp.where(lane < N_EXPERTS, logits, -jnp.inf)
    vals, idxs = [], []
    for _ in range(TOP_K):
        m = jnp.max(work, axis=-1, keepdims=True)
        idx = jnp.min(jnp.where(work == m, lane, LANES), axis=-1, keepdims=True)
        vals.append(m)
        idxs.append(idx)
        work = jnp.where(lane == idx, -jnp.inf, work)
    es = [jnp.exp(v - vals[0]) for v in vals]
    tot = es[0] + es[1] + es[2] + es[3]
    pr = jnp.zeros((tm, LANES), F32)
    for k in range(TOP_K):
        pr = jnp.where(lane == k, es[k] / tot, pr)
    pr_ref[...] = pr

    onehot = jnp.zeros((tm, LANES), F32)
    for k in range(TOP_K):
        onehot = onehot + (lane == idxs[k]).astype(F32)
    ri = lax.broadcasted_iota(jnp.int32, (tm, tm), 0)
    ci = lax.broadcasted_iota(jnp.int32, (tm, tm), 1)
    before = _dot((ri > ci).astype(BF16), onehot.astype(BF16)) + carry_ref[...]
    ir = jnp.zeros((tm, LANES), jnp.int32)
    for k in range(TOP_K):
        rank = jnp.sum(jnp.where(lane == idxs[k], before, 0.0), axis=-1, keepdims=True).astype(jnp.int32)
        ir = jnp.where(lane == k, idxs[k], ir)
        ir = jnp.where(lane == TOP_K + k, rank, ir)
    ir_ref[...] = ir
    carry = carry_ref[...] + jnp.sum(onehot, axis=0, keepdims=True)
    carry_ref[...] = carry
    cnt_ref[...] = jnp.broadcast_to(carry, (8, LANES)).astype(jnp.int32)


def _merge(x, y_ret, y_ml, outs, lses, gates, p, tm):
    t = x.shape[0]
    tm = min(tm, t)
    rowb = lambda w: pl.BlockSpec((tm, w), lambda i: (i, 0))
    full = lambda shp: pl.BlockSpec(shp, lambda i: (0,) * len(shp))
    wr = jnp.zeros((D_MODEL, LANES), F32).at[:, :N_EXPERTS].set(p['w_router'].astype(F32))
    wrh = wr.astype(BF16)
    wrl = (wr - wrh.astype(F32)).astype(BF16)
    br = jnp.zeros((1, LANES), F32).at[0, :N_EXPERTS].set(p['b_router'].astype(F32))
    row = lambda a: a.reshape(1, D_MODEL).astype(F32)
    return pl.pallas_call(
        functools.partial(_merge_kernel, tm=tm),
        grid=(t // tm,),
        in_specs=[rowb(D_MODEL), rowb(MIX_W), rowb(MIX_W), rowb(MIX_W), rowb(MIX_W), rowb(MIX_W),
                  rowb(LANES), rowb(LANES), rowb(LANES), rowb(N_BRANCHES * D_MODEL),
                  full((N_BRANCHES, MIX_W, D_MODEL)), full((D_MODEL, D_MODEL)),
                  full((1, D_MODEL)), full((1, D_MODEL)),
                  full((D_MODEL, LANES)), full((D_MODEL, LANES)), full((1, LANES))],
        out_specs=[rowb(D_MODEL), rowb(LANES), rowb(LANES), full((8, LANES))],
        out_shape=[jax.ShapeDtypeStruct((t, D_MODEL), F32),
                   jax.ShapeDtypeStruct((t, LANES), jnp.int32),
                   jax.ShapeDtypeStruct((t, LANES), F32),
                   jax.ShapeDtypeStruct((8, LANES), jnp.int32)],
        scratch_shapes=[pltpu.VMEM((1, LANES), F32)],
        compiler_params=_params(("arbitrary",)),
        name="merge_ln1_router",
    )(x, y_ret, y_ml, outs[0], outs[1], outs[2], lses[0], lses[1], lses[2], gates,
      p['w_branch'].astype(BF16), p['w_out'].astype(BF16), row(p['ln1_g']), row(p['ln1_b']), wrh, wrl, br)


def _dispatch_kernel(dest_hbm, x_ref, xs_in_hbm, xs_hbm, dest_smem, sem_idx, sem, *, tm):
    del xs_in_hbm
    i = pl.program_id(0)
    cp = pltpu.make_async_copy(dest_hbm.at[i], dest_smem, sem_idx)
    cp.start()
    cp.wait()

    def issue(t, carry):
        for k in range(TOP_K):
            d = dest_smem[t * TOP_K + k]
            pltpu.make_async_copy(x_ref.at[pl.ds(t, 1), :], xs_hbm.at[pl.ds(d, 1), :], sem).start()
        return carry

    lax.fori_loop(0, tm, issue, 0)

    def drain(t, carry):
        for k in range(TOP_K):
            pltpu.make_async_copy(x_ref.at[pl.ds(0, 1), :], xs_hbm.at[pl.ds(0, 1), :], sem).wait()
        return carry

    lax.fori_loop(0, tm, drain, 0)


def _dispatch(x1, dest, n_slots, tm):
    t = x1.shape[0]
    tm = min(tm, t)
    xs0 = jnp.zeros((n_slots, D_MODEL), x1.dtype)
    return pl.pallas_call(
        functools.partial(_dispatch_kernel, tm=tm),
        grid=(t // tm,),
        in_specs=[pl.BlockSpec(memory_space=pl.ANY),
                  pl.BlockSpec((tm, D_MODEL), lambda i: (i, 0)),
                  pl.BlockSpec(memory_space=pl.ANY)],
        out_specs=pl.BlockSpec(memory_space=pl.ANY),
        out_shape=jax.ShapeDtypeStruct((n_slots, D_MODEL), x1.dtype),
        scratch_shapes=[pltpu.SMEM((tm * TOP_K,), jnp.int32), pltpu.SemaphoreType.DMA(()), pltpu.SemaphoreType.DMA(())],
        input_output_aliases={2: 0},
        compiler_params=_params(("arbitrary",)),
        name="moe_dispatch",
    )(dest.reshape(t // tm, tm * TOP_K), x1, xs0)


def _ffn_kernel(be_ref, nv_ref, xs_ref, wg_ref, bg_ref, wu_ref, bu_ref, wd_ref, bd_ref, o_ref):
    del be_ref

    @pl.when(pl.program_id(0) < nv_ref[0])
    def _():
        x = xs_ref[...].astype(BF16)
        gate = jnp.minimum(_dot(x, wg_ref[...]) + bg_ref[...], SWIGLU_LIMIT)
        up = jnp.clip(_dot(x, wu_ref[...]) + bu_ref[...], -SWIGLU_LIMIT, SWIGLU_LIMIT)
        hid = (up + 1.0) * gate * _sigmoid(SWIGLU_ALPHA * gate)
        o_ref[...] = _dot(hid.astype(BF16), wd_ref[...]) + bd_ref[...]


def _experts(xs, block_e, n_valid, p, bm):
    n_slots = xs.shape[0]
    nb = n_slots // bm
    blk = lambda i, be, nv: (jnp.minimum(i, nv[0] - 1), 0)
    wsp = pl.BlockSpec((None, D_MODEL, D_FF), lambda i, be, nv: (be[i], 0, 0))
    bsp = pl.BlockSpec((None, 1, D_FF), lambda i, be, nv: (be[i], 0, 0))
    b3 = lambda a: a.reshape(N_EXPERTS, 1, -1).astype(F32)
    return pl.pallas_call(
        _ffn_kernel,
        grid_spec=pltpu.PrefetchScalarGridSpec(
            num_scalar_prefetch=2,
            grid=(nb,),
            in_specs=[pl.BlockSpec((bm, D_MODEL), blk), wsp, bsp, wsp, bsp, wsp, bsp],
            out_specs=pl.BlockSpec((bm, D_MODEL), blk)),
        out_shape=jax.ShapeDtypeStruct((n_slots, D_MODEL), F32),
        compiler_params=_params(("arbitrary",)),
        name="moe_experts",
    )(block_e, n_valid, xs, p['w_gate'].astype(BF16), b3(p['b_gate']), p['w_up'].astype(BF16), b3(p['b_up']),
      p['w_down'].astype(BF16), b3(p['b_down']))


def _combine_kernel(dest_hbm, x1_ref, pr_ref, ys_hbm, g2_ref, b2_ref, o_ref, ob_ref, dest_smem, buf, sem_idx, sem, *, tm):
    i = pl.program_id(0)
    cp = pltpu.make_async_copy(dest_hbm.at[i], dest_smem, sem_idx)
    cp.start()
    cp.wait()

    def issue(t, carry):
        for k in range(TOP_K):
            d = dest_smem[t * TOP_K + k]
            pltpu.make_async_copy(ys_hbm.at[pl.ds(d, 1), :], buf.at[k, pl.ds(t, 1), :], sem).start()
        return carry

    lax.fori_loop(0, tm, issue, 0)

    def drain(t, carry):
        for k in range(TOP_K):
            pltpu.make_async_copy(ys_hbm.at[pl.ds(0, 1), :], buf.at[0, pl.ds(0, 1), :], sem).wait()
        return carry

    lax.fori_loop(0, tm, drain, 0)

    pr = pr_ref[...]
    moe = pr[:, 0:1] * buf[0]
    for k in range(1, TOP_K):
        moe = moe + pr[:, k:k + 1] * buf[k]
    out = _layer_norm(DN_ALPHA * x1_ref[...] + moe, g2_ref[...], b2_ref[...])
    o_ref[...] = out
    ob_ref[...] = out.astype(BF16)


def _combine(x1, probs, ys, dest, p, tm):
    t = x1.shape[0]
    tm = min(tm, t)
    row = lambda a: a.reshape(1, D_MODEL).astype(F32)
    full = lambda shp: pl.BlockSpec(shp, lambda i: (0,) * len(shp))
    return pl.pallas_call(
        functools.partial(_combine_kernel, tm=tm),
        grid=(t // tm,),
        in_specs=[pl.BlockSpec(memory_space=pl.ANY),
                  pl.BlockSpec((tm, D_MODEL), lambda i: (i, 0)),
                  pl.BlockSpec((tm, LANES), lambda i: (i, 0)),
                  pl.BlockSpec(memory_space=pl.ANY),
                  full((1, D_MODEL)), full((1, D_MODEL))],
        out_specs=[pl.BlockSpec((tm, D_MODEL), lambda i: (i, 0)), pl.BlockSpec((tm, D_MODEL), lambda i: (i, 0))],
        out_shape=[jax.ShapeDtypeStruct((t, D_MODEL), F32), jax.ShapeDtypeStruct((t, D_MODEL), BF16)],
        scratch_shapes=[pltpu.SMEM((tm * TOP_K,), jnp.int32), pltpu.VMEM((TOP_K, tm, D_MODEL), F32),
                        pltpu.SemaphoreType.DMA(()), pltpu.SemaphoreType.DMA(())],
        compiler_params=_params(("arbitrary",)),
        name="moe_combine_ln2",
    )(dest.reshape(t // tm, tm * TOP_K), x1, probs, ys, row(p['ln2_g']), row(p['ln2_b']))


def _moe(x1, idx_rank, probs, counts, p, bm, tm_dispatch, tm_combine):
    t = x1.shape[0]
    n_assign = t * TOP_K
    nb = -(-n_assign // bm) + N_EXPERTS
    cnt = counts[0, :N_EXPERTS]
    padded = (cnt + bm - 1) // bm * bm
    pend = jnp.cumsum(padded)
    pstart = pend - padded
    n_valid = (pend[-1] // bm).astype(jnp.int32).reshape(1)
    block_e = jnp.minimum(jnp.searchsorted(pend // bm, jnp.arange(nb), side='right'), N_EXPERTS - 1).astype(jnp.int32)
    idx = idx_rank[:, 0:TOP_K]
    rank = idx_rank[:, TOP_K:2 * TOP_K]
    dest = (pstart[idx] + rank).astype(jnp.int32).reshape(-1)
    xs = _dispatch(x1, dest, nb * bm, tm_dispatch)
    ys = _experts(xs, block_e, n_valid, p, bm)
    return _combine(x1, probs, ys, dest, p, tm_combine)


def _layer(x, xb, p, b, s, cfg):
    w_in = p['w_in']
    o = IN_OFFS
    pad = jnp.zeros((D_MODEL, LANES - 2 * ML_HEADS), w_in.dtype)
    w_ret = w_in[:, o[0]:o[4]].astype(BF16)
    w_ml = jnp.concatenate([w_in[:, o[4]:o[5]], w_in[:, o[7]:o[8]]], axis=1).astype(BF16)
    w_if = jnp.concatenate([w_in[:, o[5]:o[7]], pad], axis=1).astype(BF16)
    w_gates = w_in[:, o[11]:o[12]].astype(BF16)
    mm = functools.partial(_matmul, tm=cfg['mm_tm'], tn=cfg['mm_tn'])
    ret_in = mm(xb, w_ret, BF16)
    ml_in = mm(xb, w_ml, BF16)
    if_in = mm(xb, w_if, F32)
    gates = mm(xb, w_gates, BF16)
    y_ret = _retention(ret_in, p['ret_gn'], b, s, cfg['seq_rows']).reshape(b * s, MIX_W)
    y_ml = _mlstm(ml_in, if_in, p, b, s, cfg['seq_rows']).reshape(b * s, MIX_W)
    outs, lses = [], []
    wq = ATT_HEADS * ATT_DK
    wv = ATT_HEADS * ATT_DV
    for g in range(ATT_GROUPS):
        w_att = jnp.concatenate([w_in[:, o[8] + g * wq:o[8] + (g + 1) * wq],
                                 w_in[:, o[9] + g * wq:o[9] + (g + 1) * wq],
                                 w_in[:, o[10] + g * wv:o[10] + (g + 1) * wv]], axis=1).astype(BF16)
        og, lg = _dilated_group(mm(xb, w_att, BF16), g, b, s)
        outs.append(og)
        lses.append(lg)
    x1, idx_rank, probs, counts = _merge(x, y_ret, y_ml, outs, lses, gates, p, cfg['merge_tm'])
    return _moe(x1, idx_rank, probs, counts, p, cfg['moe_bm'], cfg['dispatch_tm'], cfg['combine_tm'])


CFG = dict(mm_tm=1024, mm_tn=1024, seq_rows=512, merge_tm=512, moe_bm=512, dispatch_tm=512, combine_tm=256)

_PARAM_NAMES = ('w_in', 'ret_gn', 'ml_conv_w', 'ml_conv_b', 'ml_wq', 'ml_wk', 'ml_wv', 'ml_bi', 'ml_bf', 'ml_gn',
                'ml_skip', 'w_branch', 'w_out', 'ln1_g', 'ln1_b', 'w_router', 'b_router', 'w_gate', 'b_gate',
                'w_up', 'b_up', 'w_down', 'b_down', 'ln2_g', 'ln2_b')


def _forward(x, params, cfg):
    b, s, d = x.shape
    xf = x.reshape(b * s, d).astype(F32)
    xb = xf.astype(BF16)
    for l in range(DEPTH):
        p = {k: v[l] for k, v in params.items()}
        xf, xb = _layer(xf, xb, p, b, s, cfg)
    return xf.reshape(b, s, d).astype(x.dtype)


def kernel(x, w_in, ret_gn, ml_conv_w, ml_conv_b, ml_wq, ml_wk, ml_wv, ml_bi, ml_bf, ml_gn, ml_skip, w_branch, w_out, ln1_g, ln1_b, w_router, b_router, w_gate, b_gate, w_up, b_up, w_down, b_down, ln2_g, ln2_b):
    params = dict(zip(_PARAM_NAMES, (w_in, ret_gn, ml_conv_w, ml_conv_b, ml_wq, ml_wk, ml_wv, ml_bi, ml_bf, ml_gn,
                                     ml_skip, w_branch, w_out, ln1_g, ln1_b, w_router, b_router, w_gate, b_gate,
                                     w_up, b_up, w_down, b_down, ln2_g, ln2_b)))
    return _forward(x, params, CFG)
```

```python
import functools

import jax
import jax.numpy as jnp
import numpy as np
from jax import lax
from jax.experimental import pallas as pl
from jax.experimental.pallas import tpu as pltpu

F32 = jnp.float32
BF16 = jnp.bfloat16

D_MODEL = 1024
DEPTH = 2
MIX_W = D_MODEL // 2
N_BRANCHES = 3
RET_HEADS = 4
RET_DV = MIX_W // RET_HEADS
RET_DK = RET_DV // 2
ML_HEADS = 4
ML_DH = MIX_W // ML_HEADS
ML_CONV = 4
ML_QK_BLOCK = 4
ATT_PATTERNS = ((128, 1), (512, 4), (2048, 16))
ATT_GROUPS = len(ATT_PATTERNS)
ATT_HEADS = 4
ATT_DV = MIX_W // ATT_HEADS
ATT_DK = ATT_DV // 2
N_EXPERTS = 32
TOP_K = 4
D_FF = D_MODEL
SWIGLU_LIMIT = 7.0
SWIGLU_ALPHA = 1.702
DN_ALPHA = (2.0 * DEPTH) ** 0.25
EPS = 1e-5

CHUNK = 128
LANES = 128
NEG = -1e30
VMEM_LIMIT = 56 * 1024 * 1024

IN_SIZES = (RET_HEADS * RET_DK, RET_HEADS * RET_DK, MIX_W, MIX_W,
            MIX_W, ML_HEADS, ML_HEADS, MIX_W,
            ATT_GROUPS * ATT_HEADS * ATT_DK, ATT_GROUPS * ATT_HEADS * ATT_DK, ATT_GROUPS * ATT_HEADS * ATT_DV,
            N_BRANCHES * D_MODEL)
IN_OFFS = tuple(int(v) for v in np.cumsum((0,) + IN_SIZES))


def _params(sem):
    return pltpu.CompilerParams(dimension_semantics=sem, vmem_limit_bytes=VMEM_LIMIT)


def _dot(a, b):
    return jnp.dot(a, b, preferred_element_type=F32)


def _dot_nt(a, b):
    return lax.dot_general(a, b, (((1,), (1,)), ((), ())), preferred_element_type=F32)


def _dot_tn(a, b):
    return lax.dot_general(a, b, (((0,), (0,)), ((), ())), preferred_element_type=F32)


def _sigmoid(x):
    return 1.0 / (1.0 + jnp.exp(-x))


def _mm_kernel(x_ref, w_ref, o_ref):
    o_ref[...] = _dot(x_ref[...], w_ref[...]).astype(o_ref.dtype)


def _matmul(x, w, out_dtype, tm, tn):
    t, k = x.shape
    n = w.shape[1]
    tm = min(tm, t)
    tn = max(c for c in range(LANES, min(tn, n) + 1, LANES) if n % c == 0)
    assert t % tm == 0
    return pl.pallas_call(
        _mm_kernel,
        grid=(n // tn, t // tm),
        in_specs=[pl.BlockSpec((tm, k), lambda j, i: (i, 0)),
                  pl.BlockSpec((k, tn), lambda j, i: (0, j))],
        out_specs=pl.BlockSpec((tm, tn), lambda j, i: (i, j)),
        out_shape=jax.ShapeDtypeStruct((t, n), out_dtype),
        compiler_params=_params(("arbitrary", "arbitrary")),
        name="in_proj",
    )(x, w)


def _mm_strided_kernel(x_ref, w_ref, o_ref, y_ref, *, d):
    tm = x_ref.shape[0]
    n = tm // d
    width = w_ref.shape[1]
    y = _dot(x_ref[...], w_ref[...])
    if d == 1:
        o_ref[...] = y.astype(o_ref.dtype)
        return
    for c in range(width // LANES):
        y_ref[c] = y[:, c * LANES:(c + 1) * LANES]
    for r in range(d):
        for c in range(width // LANES):
            o_ref[:, r * width + c * LANES:r * width + (c + 1) * LANES] = (
                y_ref[c, pl.ds(r, n, stride=d), :].astype(o_ref.dtype))


def _matmul_strided(x, w, d, tm):
    t, k = x.shape
    n = w.shape[1]
    tm = min(tm, t)
    assert t % tm == 0 and tm % (16 * d) == 0
    return pl.pallas_call(
        functools.partial(_mm_strided_kernel, d=d),
        grid=(t // tm,),
        in_specs=[pl.BlockSpec((tm, k), lambda i: (i, 0)),
                  pl.BlockSpec((k, n), lambda i: (0, 0))],
        out_specs=pl.BlockSpec((tm // d, d * n), lambda i: (i, 0)),
        out_shape=jax.ShapeDtypeStruct((t // d, d * n), BF16),
        scratch_shapes=[pltpu.VMEM((n // LANES, tm, LANES), F32)],
        compiler_params=_params(("arbitrary",)),
        name=f"att_in_proj_d{d}",
    )(x, w)


def _head_norm(o, gn):
    mu = jnp.mean(o, axis=-1, keepdims=True)
    oc = o - mu
    var = jnp.mean(oc * oc, axis=-1, keepdims=True)
    return oc * lax.rsqrt(var + EPS) * gn


def _ret_kernel(blk_ref, dm_ref, qd_ref, kd_ref, cd_ref, gn_ref, o_ref, st_ref, *, nc):
    @pl.when(pl.program_id(1) == 0)
    def _():
        st_ref[...] = jnp.zeros_like(st_ref)

    hq = RET_HEADS * RET_DK
    for c in range(nc):
        rows = pl.ds(c * CHUNK, CHUNK)
        for h in range(RET_HEADS):
            q = blk_ref[rows, h * RET_DK:(h + 1) * RET_DK]
            k = blk_ref[rows, hq + h * RET_DK:hq + (h + 1) * RET_DK]
            v = blk_ref[rows, 2 * hq + h * RET_DV:2 * hq + (h + 1) * RET_DV]
            g = blk_ref[rows, 2 * hq + MIX_W + h * RET_DV:2 * hq + MIX_W + (h + 1) * RET_DV].astype(F32)
            st = st_ref[h]
            a = (_dot_nt(q, k) * dm_ref[h]).astype(BF16)
            o = _dot(a, v) + _dot(q, st.astype(BF16)) * qd_ref[h]
            kv = _dot_tn(k, (v.astype(F32) * kd_ref[h]).astype(BF16))
            st_ref[h] = cd_ref[h] * st + kv
            y = _head_norm(o, gn_ref[:, h * RET_DV:(h + 1) * RET_DV]) * (g * _sigmoid(g))
            o_ref[rows, h * RET_DV:(h + 1) * RET_DV] = y.astype(o_ref.dtype)


def _retention(ret_in, ret_gn, b, s, rows):
    nc = rows // CHUNK
    scale = RET_DK ** -0.5
    log_gamma = jnp.log1p(-jnp.exp2(-5.0 - jnp.arange(RET_HEADS, dtype=F32)))
    pos = jnp.arange(CHUNK, dtype=F32)
    diff = pos[:, None] - pos[None, :]
    dm = jnp.where(diff >= 0, jnp.exp(log_gamma[:, None, None] * jnp.maximum(diff, 0.0)), 0.0) * scale
    kd = jnp.broadcast_to(jnp.exp(log_gamma[:, None] * (CHUNK - 1.0 - pos))[:, :, None], (RET_HEADS, CHUNK, RET_DV))
    qd = jnp.broadcast_to((jnp.exp(log_gamma[:, None] * (pos + 1.0)) * scale)[:, :, None], (RET_HEADS, CHUNK, RET_DV))
    cd = jnp.broadcast_to(jnp.exp(log_gamma * CHUNK)[:, None, None], (RET_HEADS, 1, RET_DV))
    w_in = ret_in.shape[-1]
    full = lambda shp: pl.BlockSpec(shp, lambda bi, si: (0,) * len(shp))
    return pl.pallas_call(
        functools.partial(_ret_kernel, nc=nc),
        grid=(b, s // rows),
        in_specs=[pl.BlockSpec((None, rows, w_in), lambda bi, si: (bi, si, 0)),
                  full((RET_HEADS, CHUNK, CHUNK)), full((RET_HEADS, CHUNK, RET_DV)),
                  full((RET_HEADS, CHUNK, RET_DV)), full((RET_HEADS, 1, RET_DV)), full((1, MIX_W))],
        out_specs=pl.BlockSpec((None, rows, MIX_W), lambda bi, si: (bi, si, 0)),
        out_shape=jax.ShapeDtypeStruct((b, s, MIX_W), BF16),
        scratch_shapes=[pltpu.VMEM((RET_HEADS, RET_DK, RET_DV), F32)],
        compiler_params=_params(("arbitrary", "arbitrary")),
        name="retention",
    )(ret_in.reshape(b, s, w_in), dm, qd, kd, cd, ret_gn.reshape(1, MIX_W).astype(F32))


def _log_sigmoid(x):
    return jnp.minimum(x, 0.0) - jnp.log(1.0 + jnp.exp(-jnp.abs(x)))


def _ml_kernel(ml_ref, if_ref, cw_ref, cb_ref, bq_ref, bk_ref, bv_ref, gb_ref, gn_ref, sk_ref,
               o_ref, c_ref, n_ref, m_ref, halo_ref, *, nc):
    rows_total = nc * CHUNK

    @pl.when(pl.program_id(1) == 0)
    def _():
        c_ref[...] = jnp.zeros_like(c_ref)
        n_ref[...] = jnp.zeros_like(n_ref)
        m_ref[...] = jnp.zeros_like(m_ref)
        halo_ref[...] = jnp.zeros_like(halo_ref)

    x_all = ml_ref[:, 0:MIX_W].astype(F32)
    xf = jnp.concatenate([halo_ref[...], x_all], axis=0)
    acc = jnp.broadcast_to(cb_ref[...], (rows_total, MIX_W))
    for j in range(ML_CONV):
        sh = ML_CONV - 1 - j
        xs = xf if sh == 0 else pltpu.roll(xf, sh, 0)
        acc = acc + xs[8:8 + rows_total] * cw_ref[j:j + 1, :]
    halo_ref[...] = x_all[rows_total - 8:rows_total]
    mc_all = acc * _sigmoid(acc)

    ri = lax.broadcasted_iota(jnp.int32, (CHUNK, CHUNK), 0)
    ci = lax.broadcasted_iota(jnp.int32, (CHUNK, CHUNK), 1)
    tri = ri >= ci
    tril = tri.astype(F32)
    kscale = ML_DH ** -0.5

    for c in range(nc):
        r0 = c * CHUNK
        g_pre = if_ref[r0:r0 + CHUNK, :] + gb_ref[...]
        lf = _log_sigmoid(g_pre)
        cum = jnp.dot(tril, lf, preferred_element_type=F32, precision=lax.Precision.HIGHEST)
        g_t = g_pre.T
        cum_t = cum.T
        for h in range(ML_HEADS):
            cols = slice(h * ML_DH, (h + 1) * ML_DH)
            mc = mc_all[r0:r0 + CHUNK, cols]
            xv = x_all[r0:r0 + CHUNK, cols]
            q = _dot(mc.astype(BF16), bq_ref[h]).astype(BF16)
            k = (_dot(mc.astype(BF16), bk_ref[h]) * kscale).astype(BF16)
            v = _dot(xv.astype(BF16), bv_ref[h]).astype(BF16)
            cum_col = cum[:, ML_HEADS + h:ML_HEADS + h + 1]
            ig_col = g_pre[:, h:h + 1]
            cum_row = cum_t[ML_HEADS + h:ML_HEADS + h + 1, :]
            ig_row = g_t[h:h + 1, :]
            tot = cum[CHUNK - 1:CHUNK, ML_HEADS + h:ML_HEADS + h + 1]
            m_prev = m_ref[h:h + 1, 0:1]
            c_prev = c_ref[h]
            n_prev = n_ref[h:h + 1, :]
            dlog = jnp.where(tri, cum_col + (ig_row - cum_row), NEG)
            m_inter = cum_col + m_prev
            m_q = jnp.maximum(m_inter, jnp.max(dlog, axis=-1, keepdims=True))
            w_qk = jnp.exp(dlog - m_q) * _dot_nt(q, k)
            inter = jnp.exp(m_inter - m_q)
            num = _dot(w_qk.astype(BF16), v) + inter * _dot(q, c_prev.astype(BF16))
            den = jnp.sum(w_qk, axis=-1, keepdims=True) + inter * jnp.sum(q.astype(F32) * n_prev, axis=-1, keepdims=True)
            hh = num * (1.0 / jnp.maximum(jnp.abs(den), jnp.exp(-m_q)))
            a_col = tot - cum_col + ig_col
            a_max = jnp.max(a_col, axis=0, keepdims=True)
            wa = jnp.exp(a_col - a_max)
            kf = k.astype(F32)
            chunk_c = _dot_tn(k, (wa * v.astype(F32)).astype(BF16))
            chunk_n = jnp.sum(wa * kf, axis=0, keepdims=True)
            m_new = jnp.maximum(tot + m_prev, a_max)
            s_old = jnp.exp(tot + m_prev - m_new)
            s_new = jnp.exp(a_max - m_new)
            c_ref[h] = s_old * c_prev + s_new * chunk_c
            n_ref[h:h + 1, :] = s_old * n_prev + s_new * chunk_n
            m_ref[h:h + 1, :] = jnp.broadcast_to(m_new, (1, LANES))
            mo = ml_ref[r0:r0 + CHUNK, MIX_W + h * ML_DH:MIX_W + (h + 1) * ML_DH].astype(F32)
            y = _sigmoid(mo) * (_head_norm(hh, gn_ref[:, cols]) + sk_ref[:, cols] * mc)
            o_ref[r0:r0 + CHUNK, cols] = y.astype(o_ref.dtype)


def _block_diag(w):
    nb = w.shape[0]
    per = nb // ML_HEADS
    wh = w.reshape(ML_HEADS, per, ML_QK_BLOCK, ML_QK_BLOCK)
    eye = jnp.eye(per, dtype=w.dtype)
    bd = jnp.einsum('hncd,nm->hncmd', wh, eye)
    return bd.reshape(ML_HEADS, per * ML_QK_BLOCK, per * ML_QK_BLOCK).astype(BF16)


def _mlstm(ml_in, if_in, p, b, s, rows):
    nc = rows // CHUNK
    full = lambda shp: pl.BlockSpec(shp, lambda bi, si: (0,) * len(shp))
    gb = jnp.zeros((1, LANES), F32).at[0, 0:ML_HEADS].set(p['ml_bi'].astype(F32)).at[0, ML_HEADS:2 * ML_HEADS].set(p['ml_bf'].astype(F32))
    row = lambda a: a.reshape(1, MIX_W).astype(F32)
    return pl.pallas_call(
        functools.partial(_ml_kernel, nc=nc),
        grid=(b, s // rows),
        in_specs=[pl.BlockSpec((None, rows, 2 * MIX_W), lambda bi, si: (bi, si, 0)),
                  pl.BlockSpec((None, rows, LANES), lambda bi, si: (bi, si, 0)),
                  full((ML_CONV, MIX_W)), full((1, MIX_W)),
                  full((ML_HEADS, ML_DH, ML_DH)), full((ML_HEADS, ML_DH, ML_DH)), full((ML_HEADS, ML_DH, ML_DH)),
                  full((1, LANES)), full((1, MIX_W)), full((1, MIX_W))],
        out_specs=pl.BlockSpec((None, rows, MIX_W), lambda bi, si: (bi, si, 0)),
        out_shape=jax.ShapeDtypeStruct((b, s, MIX_W), BF16),
        scratch_shapes=[pltpu.VMEM((ML_HEADS, ML_DH, ML_DH), F32), pltpu.VMEM((8, ML_DH), F32),
                        pltpu.VMEM((8, LANES), F32), pltpu.VMEM((8, MIX_W), F32)],
        compiler_params=_params(("arbitrary", "arbitrary")),
        name="mlstm",
    )(ml_in.reshape(b, s, 2 * MIX_W), if_in.reshape(b, s, LANES),
      p['ml_conv_w'].astype(F32), row(p['ml_conv_b']),
      _block_diag(p['ml_wq']), _block_diag(p['ml_wk']), _block_diag(p['ml_wv']),
      gb, row(p['ml_gn']), row(p['ml_skip']))


def _att_kernel(q_ref, kc_ref, kp_ref, vc_ref, vp_ref, bias_ref, o_ref, lse_ref):
    lane = lax.broadcasted_iota(jnp.int32, (CHUNK, LANES), 1)
    lse_out = jnp.zeros((CHUNK, LANES), F32)
    for h in range(ATT_HEADS):
        dk = slice(h * ATT_DK, (h + 1) * ATT_DK)
        dv = slice(h * ATT_DV, (h + 1) * ATT_DV)
        q = q_ref[:, dk]
        sp = _dot_nt(q, kp_ref[:, dk]) * (ATT_DK ** -0.5) + bias_ref[h, :, 0:CHUNK]
        sc = _dot_nt(q, kc_ref[:, dk]) * (ATT_DK ** -0.5) + bias_ref[h, :, CHUNK:2 * CHUNK]
        m = jnp.maximum(jnp.max(sp, axis=-1, keepdims=True), jnp.max(sc, axis=-1, keepdims=True))
        pp = jnp.exp(sp - m)
        pc = jnp.exp(sc - m)
        l = jnp.sum(pp, axis=-1, keepdims=True) + jnp.sum(pc, axis=-1, keepdims=True)
        o = _dot(pp.astype(BF16), vp_ref[:, dv]) + _dot(pc.astype(BF16), vc_ref[:, dv])
        o_ref[:, dv] = (o * (1.0 / l)).astype(o_ref.dtype)
        lse_out = jnp.where(lane == h, m + jnp.log(l), lse_out)
    lse_ref[...] = lse_out


def _att_bias(window, dilation, slopes):
    wb = window // dilation
    qi = jnp.arange(wb)[:, None]
    kj = jnp.arange(2 * wb)[None, :]
    delta = qi + wb - kj
    band = (delta >= 0) & (delta <= wb)
    bias = -slopes[:, None, None] * (dilation * delta).astype(F32)
    later = jnp.where(band[None], bias, NEG)
    first = jnp.where((band & (kj >= wb))[None], bias, NEG)
    return jnp.stack([first, later]).astype(F32)


def _dilated_group(qkv, g, b, s):
    window, d = ATT_PATTERNS[g]
    assert window // d == CHUNK
    l_sub = s // d
    assert l_sub % CHUNK == 0
    n_blk = l_sub // CHUNK
    slopes = jnp.exp2(-8.0 * jnp.arange(1, ATT_GROUPS * ATT_HEADS + 1, dtype=F32) / (ATT_GROUPS * ATT_HEADS))
    bias = _att_bias(window, d, slopes.reshape(ATT_GROUPS, ATT_HEADS)[g])
    wq = ATT_HEADS * ATT_DK
    wv = ATT_HEADS * ATT_DV
    per_tok = 2 * wq + wv
    x = qkv.reshape(b, l_sub, d * per_tok)
    qb = per_tok // wq
    vb = per_tok // wv
    prev = lambda n: jnp.maximum(n - 1, 0)
    o, lse = pl.pallas_call(
        _att_kernel,
        grid=(b, d, n_blk),
        in_specs=[pl.BlockSpec((None, CHUNK, wq), lambda bi, r, n: (bi, n, r * qb)),
                  pl.BlockSpec((None, CHUNK, wq), lambda bi, r, n: (bi, n, r * qb + 1)),
                  pl.BlockSpec((None, CHUNK, wq), lambda bi, r, n: (bi, prev(n), r * qb + 1)),
                  pl.BlockSpec((None, CHUNK, wv), lambda bi, r, n: (bi, n, r * vb + 1)),
                  pl.BlockSpec((None, CHUNK, wv), lambda bi, r, n: (bi, prev(n), r * vb + 1)),
                  pl.BlockSpec((None, ATT_HEADS, CHUNK, 2 * CHUNK), lambda bi, r, n: (jnp.minimum(n, 1), 0, 0, 0))],
        out_specs=[pl.BlockSpec((None, CHUNK, wv), lambda bi, r, n: (bi, n, r)),
                   pl.BlockSpec((None, CHUNK, LANES), lambda bi, r, n: (bi, n, r))],
        out_shape=[jax.ShapeDtypeStruct((b, l_sub, d * wv), BF16),
                   jax.ShapeDtypeStruct((b, l_sub, d * LANES), F32)],
        compiler_params=_params(("arbitrary", "arbitrary", "arbitrary")),
        name=f"dilated_attn_g{g}",
    )(x, x, x, x, x, bias)
    return o.reshape(b * l_sub, d * wv), lse.reshape(b * l_sub, d * LANES)


def _layer_norm(z, g, b):
    mu = jnp.mean(z, axis=-1, keepdims=True)
    zc = z - mu
    var = jnp.mean(zc * zc, axis=-1, keepdims=True)
    return zc * lax.rsqrt(var + EPS) * g + b


def _merge_kernel(x_ref, yr_ref, ym_ref, o0_ref, o1_ref, o2_ref, l0_ref, l1_ref, l2_ref, gt_ref,
                  wb_ref, wo_ref, g1_ref, b1_ref, wrh_ref, wrl_ref, br_ref,
                  x1_ref, ir_ref, pr_ref, cnt_ref, carry_ref, og_ref, lg_ref, *, tm):
    i = pl.program_id(0)

    @pl.when(i == 0)
    def _():
        carry_ref[...] = jnp.zeros_like(carry_ref)

    for g, (src_o, src_l) in enumerate(((o0_ref, l0_ref), (o1_ref, l1_ref), (o2_ref, l2_ref))):
        d = ATT_PATTERNS[g][1]
        n = tm // d
        for r in range(d):
            rows = pl.ds(r, n, stride=d) if d > 1 else pl.ds(0, n)
            for h in range(ATT_HEADS):
                og_ref[g, h, rows, :] = src_o[:, r * MIX_W + h * ATT_DV:r * MIX_W + (h + 1) * ATT_DV].astype(F32)
            lg_ref[g, rows, :] = src_l[:, r * LANES:(r + 1) * LANES]

    l0, l1, l2 = lg_ref[0], lg_ref[1], lg_ref[2]
    lm = jnp.maximum(jnp.maximum(l0, l1), l2)
    e0, e1, e2 = jnp.exp(l0 - lm), jnp.exp(l1 - lm), jnp.exp(l2 - lm)
    inv = 1.0 / (e0 + e1 + e2)
    parts = []
    for h in range(ATT_HEADS):
        dv = slice(h * ATT_DV, (h + 1) * ATT_DV)
        hs = slice(h, h + 1)
        parts.append((e0[:, hs] * og_ref[0, h] + e1[:, hs] * og_ref[1, h] + e2[:, hs] * og_ref[2, h]) * inv[:, hs])
    y_att = jnp.concatenate(parts, axis=-1).astype(BF16)

    merged = _sigmoid(gt_ref[:, 0:D_MODEL].astype(F32)) * _dot(yr_ref[...], wb_ref[0])
    merged = merged + _sigmoid(gt_ref[:, D_MODEL:2 * D_MODEL].astype(F32)) * _dot(ym_ref[...], wb_ref[1])
    merged = merged + _sigmoid(gt_ref[:, 2 * D_MODEL:3 * D_MODEL].astype(F32)) * _dot(y_att, wb_ref[2])
    z = DN_ALPHA * x_ref[...] + _dot(merged.astype(BF16), wo_ref[...])
    x1 = _layer_norm(z, g1_ref[...], b1_ref[...])
    x1_ref[...] = x1

    xh = x1.astype(BF16)
    xl = (x1 - xh.astype(F32)).astype(BF16)
    logits = _dot(xh, wrh_ref[...]) + _dot(xl, wrh_ref[...]) + _dot(xh, wrl_ref[...]) + br_ref[...]
    lane = lax.broadcasted_iota(jnp.int32, (tm, LANES), 1)
    work = jnp.where(lane < N_EXPERTS, logits, -jnp.inf)
    vals, idxs = [], []
    for _ in range(TOP_K):
        m = jnp.max(work, axis=-1, keepdims=True)
        idx = jnp.min(jnp.where(work == m, lane, LANES), axis=-1, keepdims=True)
        vals.append(m)
        idxs.append(idx)
        work = jnp.where(lane == idx, -jnp.inf, work)
    es = [jnp.exp(v - vals[0]) for v in vals]
    tot = es[0] + es[1] + es[2] + es[3]
    pr = jnp.zeros((tm, LANES), F32)
    for k in range(TOP_K):
        pr = jnp.where(lane == k, es[k] / tot, pr)
    pr_ref[...] = pr

    onehot = jnp.zeros((tm, LANES), F32)
    for k in range(TOP_K):
        onehot = onehot + (lane == idxs[k]).astype(F32)
    ri = lax.broadcasted_iota(jnp.int32, (tm, tm), 0)
    ci = lax.broadcasted_iota(jnp.int32, (tm, tm), 1)
    before = _dot((ri > ci).astype(BF16), onehot.astype(BF16)) + carry_ref[...]
    ir = jnp.zeros((tm, LANES), jnp.int32)
    for k in range(TOP_K):
        rank = jnp.sum(jnp.where(lane == idxs[k], before, 0.0), axis=-1, keepdims=True).astype(jnp.int32)
        ir = jnp.where(lane == k, idxs[k], ir)
        ir = jnp.where(lane == TOP_K + k, rank, ir)
    ir_ref[...] = ir
    carry = carry_ref[...] + jnp.sum(onehot, axis=0, keepdims=True)
    carry_ref[...] = carry
    cnt_ref[...] = jnp.broadcast_to(carry, (8, LANES)).astype(jnp.int32)


def _merge(x, y_ret, y_ml, outs, lses, gates, p, tm):
    t = x.shape[0]
    tm = min(tm, t)
    rowb = lambda w: pl.BlockSpec((tm, w), lambda i: (i, 0))
    full = lambda shp: pl.BlockSpec(shp, lambda i: (0,) * len(shp))
    wr = jnp.zeros((D_MODEL, LANES), F32).at[:, :N_EXPERTS].set(p['w_router'].astype(F32))
    wrh = wr.astype(BF16)
    wrl = (wr - wrh.astype(F32)).astype(BF16)
    br = jnp.zeros((1, LANES), F32).at[0, :N_EXPERTS].set(p['b_router'].astype(F32))
    row = lambda a: a.reshape(1, D_MODEL).astype(F32)
    return pl.pallas_call(
        functools.partial(_merge_kernel, tm=tm),
        grid=(t // tm,),
        in_specs=[rowb(D_MODEL), rowb(MIX_W), rowb(MIX_W)]
                 + [pl.BlockSpec((tm // d, d * MIX_W), lambda i: (i, 0)) for _, d in ATT_PATTERNS]
                 + [pl.BlockSpec((tm // d, d * LANES), lambda i: (i, 0)) for _, d in ATT_PATTERNS]
                 + [rowb(N_BRANCHES * D_MODEL),
                  full((N_BRANCHES, MIX_W, D_MODEL)), full((D_MODEL, D_MODEL)),
                  full((1, D_MODEL)), full((1, D_MODEL)),
                  full((D_MODEL, LANES)), full((D_MODEL, LANES)), full((1, LANES))],
        out_specs=[rowb(D_MODEL), rowb(LANES), rowb(LANES), full((8, LANES))],
        out_shape=[jax.ShapeDtypeStruct((t, D_MODEL), F32),
                   jax.ShapeDtypeStruct((t, LANES), jnp.int32),
                   jax.ShapeDtypeStruct((t, LANES), F32),
                   jax.ShapeDtypeStruct((8, LANES), jnp.int32)],
        scratch_shapes=[pltpu.VMEM((1, LANES), F32), pltpu.VMEM((ATT_GROUPS, ATT_HEADS, tm, ATT_DV), F32),
                        pltpu.VMEM((ATT_GROUPS, tm, LANES), F32)],
        compiler_params=_params(("arbitrary",)),
        name="merge_ln1_router",
    )(x, y_ret, y_ml, outs[0], outs[1], outs[2], lses[0], lses[1], lses[2], gates,
      p['w_branch'].astype(BF16), p['w_out'].astype(BF16), row(p['ln1_g']), row(p['ln1_b']), wrh, wrl, br)


def _dispatch_kernel(dest_hbm, x_ref, xs_hbm, dest_smem, sem_idx, sem, *, tm):
    i = pl.program_id(0)
    cp = pltpu.make_async_copy(dest_hbm.at[i], dest_smem, sem_idx)
    cp.start()
    cp.wait()

    def issue(t, carry):
        for k in range(TOP_K):
            d = dest_smem[t * TOP_K + k]
            pltpu.make_async_copy(x_ref.at[pl.ds(t, 1), :], xs_hbm.at[pl.ds(d, 1), :], sem).start()
        return carry

    lax.fori_loop(0, tm, issue, 0)

    def drain(t, carry):
        for k in range(TOP_K):
            pltpu.make_async_copy(x_ref.at[pl.ds(0, 1), :], xs_hbm.at[pl.ds(0, 1), :], sem).wait()
        return carry

    lax.fori_loop(0, tm, drain, 0)


def _dispatch(x1, dest, n_slots, tm):
    t = x1.shape[0]
    tm = min(tm, t)
    return pl.pallas_call(
        functools.partial(_dispatch_kernel, tm=tm),
        grid=(t // tm,),
        in_specs=[pl.BlockSpec(memory_space=pl.ANY),
                  pl.BlockSpec((tm, D_MODEL), lambda i: (i, 0))],
        out_specs=pl.BlockSpec(memory_space=pl.ANY),
        out_shape=jax.ShapeDtypeStruct((n_slots, D_MODEL), x1.dtype),
        scratch_shapes=[pltpu.SMEM((tm * TOP_K,), jnp.int32), pltpu.SemaphoreType.DMA(()), pltpu.SemaphoreType.DMA(())],
        compiler_params=_params(("arbitrary",)),
        name="moe_dispatch",
    )(dest.reshape(t // tm, tm * TOP_K), x1)


def _ffn_kernel(be_ref, nr_ref, nv_ref, xs_ref, wg_ref, bg_ref, wu_ref, bu_ref, wd_ref, bd_ref, o_ref,
                wgb_ref, wub_ref, wdb_ref):
    i = pl.program_id(0)

    @pl.when(i < nv_ref[0])
    def _():
        @pl.when((i == 0) | (be_ref[i] != be_ref[jnp.maximum(i - 1, 0)]))
        def _():
            wgb_ref[...] = wg_ref[...].astype(BF16)
            wub_ref[...] = wu_ref[...].astype(BF16)
            wdb_ref[...] = wd_ref[...].astype(BF16)

        rows = lax.broadcasted_iota(jnp.int32, xs_ref.shape, 0)
        x = jnp.where(rows < nr_ref[i], xs_ref[...], 0.0).astype(BF16)
        gate = jnp.minimum(_dot(x, wgb_ref[...]) + bg_ref[...], SWIGLU_LIMIT)
        up = jnp.clip(_dot(x, wub_ref[...]) + bu_ref[...], -SWIGLU_LIMIT, SWIGLU_LIMIT)
        hid = (up + 1.0) * gate * _sigmoid(SWIGLU_ALPHA * gate)
        o_ref[...] = _dot(hid.astype(BF16), wdb_ref[...]) + bd_ref[...]


def _experts(xs, block_e, block_rows, n_valid, params, l, bm):
    n_slots = xs.shape[0]
    nb = n_slots // bm
    blk = lambda i, be, nr, nv: (jnp.minimum(i, nv[0] - 1), 0)
    wsp = pl.BlockSpec((None, None, D_MODEL, D_FF), lambda i, be, nr, nv: (l, be[i], 0, 0))
    bsp = pl.BlockSpec((None, None, 1, D_FF), lambda i, be, nr, nv: (l, be[i], 0, 0))
    b4 = lambda a: a.reshape(DEPTH, N_EXPERTS, 1, -1)
    return pl.pallas_call(
        _ffn_kernel,
        grid_spec=pltpu.PrefetchScalarGridSpec(
            num_scalar_prefetch=3,
            grid=(nb,),
            in_specs=[pl.BlockSpec((bm, D_MODEL), blk), wsp, bsp, wsp, bsp, wsp, bsp],
            out_specs=pl.BlockSpec((bm, D_MODEL), blk),
            scratch_shapes=[pltpu.VMEM((D_MODEL, D_FF), BF16), pltpu.VMEM((D_MODEL, D_FF), BF16),
                            pltpu.VMEM((D_FF, D_MODEL), BF16)]),
        out_shape=jax.ShapeDtypeStruct((n_slots, D_MODEL), F32),
        compiler_params=_params(("arbitrary",)),
        name="moe_experts",
    )(block_e, block_rows, n_valid, xs, params['w_gate'], b4(params['b_gate']), params['w_up'], b4(params['b_up']),
      params['w_down'], b4(params['b_down']))


def _combine_kernel(dest_hbm, x1_ref, pr_ref, ys_hbm, g2_ref, b2_ref, o_ref, ob_ref, dest_smem, buf, sem_idx, sem, *, tm):
    i = pl.program_id(0)
    cp = pltpu.make_async_copy(dest_hbm.at[i], dest_smem, sem_idx)
    cp.start()
    cp.wait()

    def issue(t, carry):
        for k in range(TOP_K):
            d = dest_smem[t * TOP_K + k]
            pltpu.make_async_copy(ys_hbm.at[pl.ds(d, 1), :], buf.at[k, pl.ds(t, 1), :], sem).start()
        return carry

    lax.fori_loop(0, tm, issue, 0)

    def drain(t, carry):
        for k in range(TOP_K):
            pltpu.make_async_copy(ys_hbm.at[pl.ds(0, 1), :], buf.at[0, pl.ds(0, 1), :], sem).wait()
        return carry

    lax.fori_loop(0, tm, drain, 0)

    pr = pr_ref[...]
    moe = pr[:, 0:1] * buf[0]
    for k in range(1, TOP_K):
        moe = moe + pr[:, k:k + 1] * buf[k]
    out = _layer_norm(DN_ALPHA * x1_ref[...] + moe, g2_ref[...], b2_ref[...])
    o_ref[...] = out
    ob_ref[...] = out.astype(BF16)


def _combine(x1, probs, ys, dest, p, tm):
    t = x1.shape[0]
    tm = min(tm, t)
    row = lambda a: a.reshape(1, D_MODEL).astype(F32)
    full = lambda shp: pl.BlockSpec(shp, lambda i: (0,) * len(shp))
    return pl.pallas_call(
        functools.partial(_combine_kernel, tm=tm),
        grid=(t // tm,),
        in_specs=[pl.BlockSpec(memory_space=pl.ANY),
                  pl.BlockSpec((tm, D_MODEL), lambda i: (i, 0)),
                  pl.BlockSpec((tm, LANES), lambda i: (i, 0)),
                  pl.BlockSpec(memory_space=pl.ANY),
                  full((1, D_MODEL)), full((1, D_MODEL))],
        out_specs=[pl.BlockSpec((tm, D_MODEL), lambda i: (i, 0)), pl.BlockSpec((tm, D_MODEL), lambda i: (i, 0))],
        out_shape=[jax.ShapeDtypeStruct((t, D_MODEL), F32), jax.ShapeDtypeStruct((t, D_MODEL), BF16)],
        scratch_shapes=[pltpu.SMEM((tm * TOP_K,), jnp.int32), pltpu.VMEM((TOP_K, tm, D_MODEL), F32),
                        pltpu.SemaphoreType.DMA(()), pltpu.SemaphoreType.DMA(())],
        compiler_params=_params(("arbitrary",)),
        name="moe_combine_ln2",
    )(dest.reshape(t // tm, tm * TOP_K), x1, probs, ys, row(p['ln2_g']), row(p['ln2_b']))


def _moe(x1, idx_rank, probs, counts, p, params, l, bm, tm_dispatch, tm_combine):
    t = x1.shape[0]
    n_assign = t * TOP_K
    nb = -(-n_assign // bm) + N_EXPERTS
    cnt = counts[0, :N_EXPERTS]
    padded = (cnt + bm - 1) // bm * bm
    pend = jnp.cumsum(padded)
    pstart = pend - padded
    n_valid = (pend[-1] // bm).astype(jnp.int32).reshape(1)
    blocks = jnp.arange(nb, dtype=jnp.int32)
    block_e = jnp.minimum(jnp.sum((pend // bm)[None, :] <= blocks[:, None], axis=1), N_EXPERTS - 1).astype(jnp.int32)
    block_rows = jnp.clip(cnt[block_e] - (blocks * bm - pstart[block_e]), 0, bm).astype(jnp.int32)
    idx = idx_rank[:, 0:TOP_K]
    rank = idx_rank[:, TOP_K:2 * TOP_K]
    onehot = idx[:, :, None] == jnp.arange(N_EXPERTS, dtype=jnp.int32)[None, None, :]
    dest = (jnp.sum(jnp.where(onehot, pstart[None, None, :], 0), axis=-1) + rank).astype(jnp.int32).reshape(-1)
    xs = _dispatch(x1, dest, nb * bm, tm_dispatch)
    ys = _experts(xs, block_e, block_rows, n_valid, params, l, bm)
    return _combine(x1, probs, ys, dest, p, tm_combine)


_EXPERT_WEIGHTS = ('w_gate', 'b_gate', 'w_up', 'b_up', 'w_down', 'b_down')


def _layer(x, xb, params, l, b, s, cfg):
    p = {k: v[l] for k, v in params.items() if k not in _EXPERT_WEIGHTS}
    w_in = p['w_in']
    o = IN_OFFS
    pad = jnp.zeros((D_MODEL, LANES - 2 * ML_HEADS), w_in.dtype)
    w_ret = w_in[:, o[0]:o[4]].astype(BF16)
    w_ml = jnp.concatenate([w_in[:, o[4]:o[5]], w_in[:, o[7]:o[8]]], axis=1).astype(BF16)
    w_if = jnp.concatenate([w_in[:, o[5]:o[7]], pad], axis=1).astype(BF16)
    w_gates = w_in[:, o[11]:o[12]].astype(BF16)
    mm = functools.partial(_matmul, tm=cfg['mm_tm'], tn=cfg['mm_tn'])
    ret_in = mm(xb, w_ret, BF16)
    ml_in = mm(xb, w_ml, BF16)
    if_in = mm(xb, w_if, F32)
    gates = mm(xb, w_gates, BF16)
    y_ret = _retention(ret_in, p['ret_gn'], b, s, cfg['seq_rows']).reshape(b * s, MIX_W)
    y_ml = _mlstm(ml_in, if_in, p, b, s, cfg['seq_rows']).reshape(b * s, MIX_W)
    outs, lses = [], []
    wq = ATT_HEADS * ATT_DK
    wv = ATT_HEADS * ATT_DV
    for g in range(ATT_GROUPS):
        w_att = jnp.concatenate([w_in[:, o[8] + g * wq:o[8] + (g + 1) * wq],
                                 w_in[:, o[9] + g * wq:o[9] + (g + 1) * wq],
                                 w_in[:, o[10] + g * wv:o[10] + (g + 1) * wv]], axis=1).astype(BF16)
        og, lg = _dilated_group(_matmul_strided(xb, w_att, ATT_PATTERNS[g][1], cfg['mm_tm']), g, b, s)
        outs.append(og)
        lses.append(lg)
    x1, idx_rank, probs, counts = _merge(x, y_ret, y_ml, outs, lses, gates, p, cfg['merge_tm'])
    return _moe(x1, idx_rank, probs, counts, p, params, l, cfg['moe_bm'], cfg['dispatch_tm'], cfg['combine_tm'])


CFG = dict(mm_tm=1024, mm_tn=1024, seq_rows=512, merge_tm=512, moe_bm=512, dispatch_tm=512, combine_tm=256)

_PARAM_NAMES = ('w_in', 'ret_gn', 'ml_conv_w', 'ml_conv_b', 'ml_wq', 'ml_wk', 'ml_wv', 'ml_bi', 'ml_bf', 'ml_gn',
                'ml_skip', 'w_branch', 'w_out', 'ln1_g', 'ln1_b', 'w_router', 'b_router', 'w_gate', 'b_gate',
                'w_up', 'b_up', 'w_down', 'b_down', 'ln2_g', 'ln2_b')


def _forward(x, params, cfg):
    b, s, d = x.shape
    xf = x.reshape(b * s, d).astype(F32)
    xb = xf.astype(BF16)
    for l in range(DEPTH):
        xf, xb = _layer(xf, xb, params, l, b, s, cfg)
    return xf.reshape(b, s, d).astype(x.dtype)


def kernel(x, w_in, ret_gn, ml_conv_w, ml_conv_b, ml_wq, ml_wk, ml_wv, ml_bi, ml_bf, ml_gn, ml_skip, w_branch, w_out, ln1_g, ln1_b, w_router, b_router, w_gate, b_gate, w_up, b_up, w_down, b_down, ln2_g, ln2_b):
    params = dict(zip(_PARAM_NAMES, (w_in, ret_gn, ml_conv_w, ml_conv_b, ml_wq, ml_wk, ml_wv, ml_bi, ml_bf, ml_gn,
                                     ml_skip, w_branch, w_out, ln1_g, ln1_b, w_router, b_router, w_gate, b_gate,
                                     w_up, b_up, w_down, b_down, ln2_g, ln2_b)))
    return _forward(x, params, CFG)
```

```python
import functools

import jax
import jax.numpy as jnp
import numpy as np
from jax import lax
from jax.experimental import pallas as pl
from jax.experimental.pallas import tpu as pltpu

F32 = jnp.float32
BF16 = jnp.bfloat16

D_MODEL = 1024
DEPTH = 2
MIX_W = D_MODEL // 2
N_BRANCHES = 3
RET_HEADS = 4
RET_DV = MIX_W // RET_HEADS
RET_DK = RET_DV // 2
ML_HEADS = 4
ML_DH = MIX_W // ML_HEADS
ML_CONV = 4
ML_QK_BLOCK = 4
ATT_PATTERNS = ((128, 1), (512, 4), (2048, 16))
ATT_GROUPS = len(ATT_PATTERNS)
ATT_HEADS = 4
ATT_DV = MIX_W // ATT_HEADS
ATT_DK = ATT_DV // 2
N_EXPERTS = 32
TOP_K = 4
D_FF = D_MODEL
SWIGLU_LIMIT = 7.0
SWIGLU_ALPHA = 1.702
DN_ALPHA = (2.0 * DEPTH) ** 0.25
EPS = 1e-5

CHUNK = 128
LANES = 128
NEG = -1e30
VMEM_LIMIT = 56 * 1024 * 1024

IN_SIZES = (RET_HEADS * RET_DK, RET_HEADS * RET_DK, MIX_W, MIX_W,
            MIX_W, ML_HEADS, ML_HEADS, MIX_W,
            ATT_GROUPS * ATT_HEADS * ATT_DK, ATT_GROUPS * ATT_HEADS * ATT_DK, ATT_GROUPS * ATT_HEADS * ATT_DV,
            N_BRANCHES * D_MODEL)
IN_OFFS = tuple(int(v) for v in np.cumsum((0,) + IN_SIZES))


def _params(sem):
    return pltpu.CompilerParams(dimension_semantics=sem, vmem_limit_bytes=VMEM_LIMIT)


def _dot(a, b):
    return jnp.dot(a, b, preferred_element_type=F32)


def _dot_nt(a, b):
    return lax.dot_general(a, b, (((1,), (1,)), ((), ())), preferred_element_type=F32)


def _dot_tn(a, b):
    return lax.dot_general(a, b, (((0,), (0,)), ((), ())), preferred_element_type=F32)


def _sigmoid(x):
    return 1.0 / (1.0 + jnp.exp(-x))


def _mm_kernel(x_ref, w_ref, o_ref):
    o_ref[...] = _dot(x_ref[...], w_ref[...]).astype(o_ref.dtype)


def _matmul(x, w, out_dtype, tm, tn):
    t, k = x.shape
    n = w.shape[1]
    tm = min(tm, t)
    tn = max(c for c in range(LANES, min(tn, n) + 1, LANES) if n % c == 0)
    assert t % tm == 0
    return pl.pallas_call(
        _mm_kernel,
        grid=(n // tn, t // tm),
        in_specs=[pl.BlockSpec((tm, k), lambda j, i: (i, 0)),
                  pl.BlockSpec((k, tn), lambda j, i: (0, j))],
        out_specs=pl.BlockSpec((tm, tn), lambda j, i: (i, j)),
        out_shape=jax.ShapeDtypeStruct((t, n), out_dtype),
        compiler_params=_params(("arbitrary", "arbitrary")),
        name="in_proj",
    )(x, w)


def _mm_strided_kernel(x_ref, w_ref, o_ref, y_ref, *, d):
    tm = x_ref.shape[0]
    n = tm // d
    width = w_ref.shape[1]
    y = _dot(x_ref[...], w_ref[...])
    if d == 1:
        o_ref[...] = y.astype(o_ref.dtype)
        return
    for c in range(width // LANES):
        y_ref[c] = y[:, c * LANES:(c + 1) * LANES]
    for r in range(d):
        for c in range(width // LANES):
            o_ref[:, r * width + c * LANES:r * width + (c + 1) * LANES] = (
                y_ref[c, pl.ds(r, n, stride=d), :].astype(o_ref.dtype))


def _matmul_strided(x, w, d, tm):
    t, k = x.shape
    n = w.shape[1]
    tm = min(tm, t)
    assert t % tm == 0 and tm % (16 * d) == 0
    return pl.pallas_call(
        functools.partial(_mm_strided_kernel, d=d),
        grid=(t // tm,),
        in_specs=[pl.BlockSpec((tm, k), lambda i: (i, 0)),
                  pl.BlockSpec((k, n), lambda i: (0, 0))],
        out_specs=pl.BlockSpec((tm // d, d * n), lambda i: (i, 0)),
        out_shape=jax.ShapeDtypeStruct((t // d, d * n), BF16),
        scratch_shapes=[pltpu.VMEM((n // LANES, tm, LANES), F32)],
        compiler_params=_params(("arbitrary",)),
        name=f"att_in_proj_d{d}",
    )(x, w)


def _head_norm(o, gn):
    mu = jnp.mean(o, axis=-1, keepdims=True)
    oc = o - mu
    var = jnp.mean(oc * oc, axis=-1, keepdims=True)
    return oc * lax.rsqrt(var + EPS) * gn


def _ret_kernel(blk_ref, dm_ref, qd_ref, kd_ref, cd_ref, gn_ref, o_ref, st_ref, *, nc):
    @pl.when(pl.program_id(1) == 0)
    def _():
        st_ref[...] = jnp.zeros_like(st_ref)

    hq = RET_HEADS * RET_DK
    for c in range(nc):
        rows = pl.ds(c * CHUNK, CHUNK)
        for h in range(RET_HEADS):
            q = blk_ref[rows, h * RET_DK:(h + 1) * RET_DK]
            k = blk_ref[rows, hq + h * RET_DK:hq + (h + 1) * RET_DK]
            v = blk_ref[rows, 2 * hq + h * RET_DV:2 * hq + (h + 1) * RET_DV]
            g = blk_ref[rows, 2 * hq + MIX_W + h * RET_DV:2 * hq + MIX_W + (h + 1) * RET_DV].astype(F32)
            st = st_ref[h]
            a = (_dot_nt(q, k) * dm_ref[h]).astype(BF16)
            o = _dot(a, v) + _dot(q, st.astype(BF16)) * qd_ref[h]
            kv = _dot_tn(k, (v.astype(F32) * kd_ref[h]).astype(BF16))
            st_ref[h] = cd_ref[h] * st + kv
            y = _head_norm(o, gn_ref[:, h * RET_DV:(h + 1) * RET_DV]) * (g * _sigmoid(g))
            o_ref[rows, h * RET_DV:(h + 1) * RET_DV] = y.astype(o_ref.dtype)


def _retention(ret_in, ret_gn, b, s, rows):
    nc = rows // CHUNK
    scale = RET_DK ** -0.5
    log_gamma = jnp.log1p(-jnp.exp2(-5.0 - jnp.arange(RET_HEADS, dtype=F32)))
    pos = jnp.arange(CHUNK, dtype=F32)
    diff = pos[:, None] - pos[None, :]
    dm = jnp.where(diff >= 0, jnp.exp(log_gamma[:, None, None] * jnp.maximum(diff, 0.0)), 0.0) * scale
    kd = jnp.broadcast_to(jnp.exp(log_gamma[:, None] * (CHUNK - 1.0 - pos))[:, :, None], (RET_HEADS, CHUNK, RET_DV))
    qd = jnp.broadcast_to((jnp.exp(log_gamma[:, None] * (pos + 1.0)) * scale)[:, :, None], (RET_HEADS, CHUNK, RET_DV))
    cd = jnp.broadcast_to(jnp.exp(log_gamma * CHUNK)[:, None, None], (RET_HEADS, 1, RET_DV))
    w_in = ret_in.shape[-1]
    full = lambda shp: pl.BlockSpec(shp, lambda bi, si: (0,) * len(shp))
    return pl.pallas_call(
        functools.partial(_ret_kernel, nc=nc),
        grid=(b, s // rows),
        in_specs=[pl.BlockSpec((None, rows, w_in), lambda bi, si: (bi, si, 0)),
                  full((RET_HEADS, CHUNK, CHUNK)), full((RET_HEADS, CHUNK, RET_DV)),
                  full((RET_HEADS, CHUNK, RET_DV)), full((RET_HEADS, 1, RET_DV)), full((1, MIX_W))],
        out_specs=pl.BlockSpec((None, rows, MIX_W), lambda bi, si: (bi, si, 0)),
        out_shape=jax.ShapeDtypeStruct((b, s, MIX_W), BF16),
        scratch_shapes=[pltpu.VMEM((RET_HEADS, RET_DK, RET_DV), F32)],
        compiler_params=_params(("arbitrary", "arbitrary")),
        name="retention",
    )(ret_in.reshape(b, s, w_in), dm, qd, kd, cd, ret_gn.reshape(1, MIX_W).astype(F32))


def _log_sigmoid(x):
    return jnp.minimum(x, 0.0) - jnp.log(1.0 + jnp.exp(-jnp.abs(x)))


def _ml_kernel(ml_ref, if_ref, cw_ref, cb_ref, bq_ref, bk_ref, bv_ref, gb_ref, gn_ref, sk_ref,
               o_ref, c_ref, n_ref, m_ref, halo_ref, *, nc):
    rows_total = nc * CHUNK

    @pl.when(pl.program_id(1) == 0)
    def _():
        c_ref[...] = jnp.zeros_like(c_ref)
        n_ref[...] = jnp.zeros_like(n_ref)
        m_ref[...] = jnp.zeros_like(m_ref)
        halo_ref[...] = jnp.zeros_like(halo_ref)

    x_all = ml_ref[:, 0:MIX_W].astype(F32)
    xf = jnp.concatenate([halo_ref[...], x_all], axis=0)
    acc = jnp.broadcast_to(cb_ref[...], (rows_total, MIX_W))
    for j in range(ML_CONV):
        sh = ML_CONV - 1 - j
        xs = xf if sh == 0 else pltpu.roll(xf, sh, 0)
        acc = acc + xs[8:8 + rows_total] * cw_ref[j:j + 1, :]
    halo_ref[...] = x_all[rows_total - 8:rows_total]
    mc_all = acc * _sigmoid(acc)

    ri = lax.broadcasted_iota(jnp.int32, (CHUNK, CHUNK), 0)
    ci = lax.broadcasted_iota(jnp.int32, (CHUNK, CHUNK), 1)
    tri = ri >= ci
    tril = tri.astype(F32)
    kscale = ML_DH ** -0.5

    for c in range(nc):
        r0 = c * CHUNK
        g_pre = if_ref[r0:r0 + CHUNK, :] + gb_ref[...]
        lf = _log_sigmoid(g_pre)
        cum = jnp.dot(tril, lf, preferred_element_type=F32, precision=lax.Precision.HIGHEST)
        g_t = g_pre.T
        cum_t = cum.T
        for h in range(ML_HEADS):
            cols = slice(h * ML_DH, (h + 1) * ML_DH)
            mc = mc_all[r0:r0 + CHUNK, cols]
            xv = x_all[r0:r0 + CHUNK, cols]
            q = _dot(mc.astype(BF16), bq_ref[h]).astype(BF16)
            k = (_dot(mc.astype(BF16), bk_ref[h]) * kscale).astype(BF16)
            v = _dot(xv.astype(BF16), bv_ref[h]).astype(BF16)
            cum_col = cum[:, ML_HEADS + h:ML_HEADS + h + 1]
            ig_col = g_pre[:, h:h + 1]
            cum_row = cum_t[ML_HEADS + h:ML_HEADS + h + 1, :]
            ig_row = g_t[h:h + 1, :]
            tot = cum[CHUNK - 1:CHUNK, ML_HEADS + h:ML_HEADS + h + 1]
            m_prev = m_ref[h:h + 1, 0:1]
            c_prev = c_ref[h]
            n_prev = n_ref[h:h + 1, :]
            dlog = jnp.where(tri, cum_col + (ig_row - cum_row), NEG)
            m_inter = cum_col + m_prev
            m_q = jnp.maximum(m_inter, jnp.max(dlog, axis=-1, keepdims=True))
            w_qk = jnp.exp(dlog - m_q) * _dot_nt(q, k)
            inter = jnp.exp(m_inter - m_q)
            num = _dot(w_qk.astype(BF16), v) + inter * _dot(q, c_prev.astype(BF16))
            den = jnp.sum(w_qk, axis=-1, keepdims=True) + inter * jnp.sum(q.astype(F32) * n_prev, axis=-1, keepdims=True)
            hh = num * (1.0 / jnp.maximum(jnp.abs(den), jnp.exp(-m_q)))
            a_col = tot - cum_col + ig_col
            a_max = jnp.max(a_col, axis=0, keepdims=True)
            wa = jnp.exp(a_col - a_max)
            kf = k.astype(F32)
            chunk_c = _dot_tn(k, (wa * v.astype(F32)).astype(BF16))
            chunk_n = jnp.sum(wa * kf, axis=0, keepdims=True)
            m_new = jnp.maximum(tot + m_prev, a_max)
            s_old = jnp.exp(tot + m_prev - m_new)
            s_new = jnp.exp(a_max - m_new)
            c_ref[h] = s_old * c_prev + s_new * chunk_c
            n_ref[h:h + 1, :] = s_old * n_prev + s_new * chunk_n
            m_ref[h:h + 1, :] = jnp.broadcast_to(m_new, (1, LANES))
            mo = ml_ref[r0:r0 + CHUNK, MIX_W + h * ML_DH:MIX_W + (h + 1) * ML_DH].astype(F32)
            y = _sigmoid(mo) * (_head_norm(hh, gn_ref[:, cols]) + sk_ref[:, cols] * mc)
            o_ref[r0:r0 + CHUNK, cols] = y.astype(o_ref.dtype)


def _block_diag(w):
    nb = w.shape[0]
    per = nb // ML_HEADS
    wh = w.reshape(ML_HEADS, per, ML_QK_BLOCK, ML_QK_BLOCK)
    eye = jnp.eye(per, dtype=w.dtype)
    bd = jnp.einsum('hncd,nm->hncmd', wh, eye)
    return bd.reshape(ML_HEADS, per * ML_QK_BLOCK, per * ML_QK_BLOCK).astype(BF16)


def _mlstm(ml_in, if_in, p, b, s, rows):
    nc = rows // CHUNK
    full = lambda shp: pl.BlockSpec(shp, lambda bi, si: (0,) * len(shp))
    gb = jnp.zeros((1, LANES), F32).at[0, 0:ML_HEADS].set(p['ml_bi'].astype(F32)).at[0, ML_HEADS:2 * ML_HEADS].set(p['ml_bf'].astype(F32))
    row = lambda a: a.reshape(1, MIX_W).astype(F32)
    return pl.pallas_call(
        functools.partial(_ml_kernel, nc=nc),
        grid=(b, s // rows),
        in_specs=[pl.BlockSpec((None, rows, 2 * MIX_W), lambda bi, si: (bi, si, 0)),
                  pl.BlockSpec((None, rows, LANES), lambda bi, si: (bi, si, 0)),
                  full((ML_CONV, MIX_W)), full((1, MIX_W)),
                  full((ML_HEADS, ML_DH, ML_DH)), full((ML_HEADS, ML_DH, ML_DH)), full((ML_HEADS, ML_DH, ML_DH)),
                  full((1, LANES)), full((1, MIX_W)), full((1, MIX_W))],
        out_specs=pl.BlockSpec((None, rows, MIX_W), lambda bi, si: (bi, si, 0)),
        out_shape=jax.ShapeDtypeStruct((b, s, MIX_W), BF16),
        scratch_shapes=[pltpu.VMEM((ML_HEADS, ML_DH, ML_DH), F32), pltpu.VMEM((8, ML_DH), F32),
                        pltpu.VMEM((8, LANES), F32), pltpu.VMEM((8, MIX_W), F32)],
        compiler_params=_params(("arbitrary", "arbitrary")),
        name="mlstm",
    )(ml_in.reshape(b, s, 2 * MIX_W), if_in.reshape(b, s, LANES),
      p['ml_conv_w'].astype(F32), row(p['ml_conv_b']),
      _block_diag(p['ml_wq']), _block_diag(p['ml_wk']), _block_diag(p['ml_wv']),
      gb, row(p['ml_gn']), row(p['ml_skip']))


def _att_kernel(q_ref, kc_ref, kp_ref, vc_ref, vp_ref, bias_ref, o_ref, lse_ref, *, nq):
    first = pl.program_id(2) == 0
    lane = lax.broadcasted_iota(jnp.int32, (CHUNK, LANES), 1)
    ones = jnp.ones((2 * CHUNK, ATT_DV), BF16)
    for j in range(nq):
        rows = slice(j * CHUNK, (j + 1) * CHUNK)
        lse_out = jnp.zeros((CHUNK, LANES), F32)
        for h in range(ATT_HEADS):
            dk = slice(h * ATT_DK, (h + 1) * ATT_DK)
            dv = slice(h * ATT_DV, (h + 1) * ATT_DV)
            q = q_ref[rows, dk]
            if j == 0:
                kk = jnp.concatenate([kp_ref[:, dk], kc_ref[rows, dk]], axis=0)
                vv = jnp.concatenate([vp_ref[:, dv], vc_ref[rows, dv]], axis=0)
                bias = bias_ref[jnp.where(first, 0, 1), h]
            else:
                kk = kc_ref[(j - 1) * CHUNK:(j + 1) * CHUNK, dk]
                vv = vc_ref[(j - 1) * CHUNK:(j + 1) * CHUNK, dv]
                bias = bias_ref[1, h]
            s = _dot_nt(q, kk) * (ATT_DK ** -0.5) + bias
            m = jnp.max(s, axis=-1, keepdims=True)
            p = jnp.exp(s - m).astype(BF16)
            oa = _dot(p, jnp.concatenate([vv, ones], axis=1))
            l = oa[:, ATT_DV:2 * ATT_DV]
            o_ref[rows, dv] = (oa[:, 0:ATT_DV] * (1.0 / l)).astype(o_ref.dtype)
            lse_out = jnp.where(lane == h, m + jnp.log(l), lse_out)
        lse_ref[rows, :] = lse_out


def _att_bias(window, dilation, slopes):
    wb = window // dilation
    qi = jnp.arange(wb)[:, None]
    kj = jnp.arange(2 * wb)[None, :]
    delta = qi + wb - kj
    band = (delta >= 0) & (delta <= wb)
    bias = -slopes[:, None, None] * (dilation * delta).astype(F32)
    later = jnp.where(band[None], bias, NEG)
    first = jnp.where((band & (kj >= wb))[None], bias, NEG)
    return jnp.stack([first, later]).astype(F32)


def _dilated_group(qkv, g, b, s, nq):
    window, d = ATT_PATTERNS[g]
    assert window // d == CHUNK
    l_sub = s // d
    assert l_sub % CHUNK == 0
    n_blk = l_sub // CHUNK
    slopes = jnp.exp2(-8.0 * jnp.arange(1, ATT_GROUPS * ATT_HEADS + 1, dtype=F32) / (ATT_GROUPS * ATT_HEADS))
    bias = _att_bias(window, d, slopes.reshape(ATT_GROUPS, ATT_HEADS)[g])
    wq = ATT_HEADS * ATT_DK
    wv = ATT_HEADS * ATT_DV
    per_tok = 2 * wq + wv
    x = qkv.reshape(b, l_sub, d * per_tok)
    qb = per_tok // wq
    vb = per_tok // wv
    nq = min(nq, n_blk)
    assert n_blk % nq == 0
    span = nq * CHUNK
    prev = lambda n: jnp.maximum(n * nq - 1, 0)
    o, lse = pl.pallas_call(
        functools.partial(_att_kernel, nq=nq),
        grid=(b, d, n_blk // nq),
        in_specs=[pl.BlockSpec((None, span, wq), lambda bi, r, n: (bi, n, r * qb)),
                  pl.BlockSpec((None, span, wq), lambda bi, r, n: (bi, n, r * qb + 1)),
                  pl.BlockSpec((None, CHUNK, wq), lambda bi, r, n: (bi, prev(n), r * qb + 1)),
                  pl.BlockSpec((None, span, wv), lambda bi, r, n: (bi, n, r * vb + 1)),
                  pl.BlockSpec((None, CHUNK, wv), lambda bi, r, n: (bi, prev(n), r * vb + 1)),
                  pl.BlockSpec((2, ATT_HEADS, CHUNK, 2 * CHUNK), lambda bi, r, n: (0, 0, 0, 0))],
        out_specs=[pl.BlockSpec((None, span, wv), lambda bi, r, n: (bi, n, r)),
                   pl.BlockSpec((None, span, LANES), lambda bi, r, n: (bi, n, r))],
        out_shape=[jax.ShapeDtypeStruct((b, l_sub, d * wv), BF16),
                   jax.ShapeDtypeStruct((b, l_sub, d * LANES), F32)],
        compiler_params=_params(("arbitrary", "arbitrary", "arbitrary")),
        name=f"dilated_attn_g{g}",
    )(x, x, x, x, x, bias)
    return o.reshape(b * l_sub, d * wv), lse.reshape(b * l_sub, d * LANES)


def _layer_norm(z, g, b):
    mu = jnp.mean(z, axis=-1, keepdims=True)
    zc = z - mu
    var = jnp.mean(zc * zc, axis=-1, keepdims=True)
    return zc * lax.rsqrt(var + EPS) * g + b


def _merge_kernel(x_ref, yr_ref, ym_ref, o0_ref, o1_ref, o2_ref, l0_ref, l1_ref, l2_ref, gt_ref,
                  wb_ref, wo_ref, g1_ref, b1_ref, wrh_ref, wrl_ref, br_ref,
                  x1_ref, ir_ref, pr_ref, cnt_ref, carry_ref, og_ref, lg_ref, *, tm):
    i = pl.program_id(0)

    @pl.when(i == 0)
    def _():
        carry_ref[...] = jnp.zeros_like(carry_ref)

    for g, (src_o, src_l) in enumerate(((o0_ref, l0_ref), (o1_ref, l1_ref), (o2_ref, l2_ref))):
        d = ATT_PATTERNS[g][1]
        n = tm // d
        for r in range(d):
            rows = pl.ds(r, n, stride=d) if d > 1 else pl.ds(0, n)
            for h in range(ATT_HEADS):
                og_ref[g, h, rows, :] = src_o[:, r * MIX_W + h * ATT_DV:r * MIX_W + (h + 1) * ATT_DV].astype(F32)
            lg_ref[g, rows, :] = src_l[:, r * LANES:(r + 1) * LANES]

    l0, l1, l2 = lg_ref[0], lg_ref[1], lg_ref[2]
    lm = jnp.maximum(jnp.maximum(l0, l1), l2)
    e0, e1, e2 = jnp.exp(l0 - lm), jnp.exp(l1 - lm), jnp.exp(l2 - lm)
    inv = 1.0 / (e0 + e1 + e2)
    parts = []
    for h in range(ATT_HEADS):
        dv = slice(h * ATT_DV, (h + 1) * ATT_DV)
        hs = slice(h, h + 1)
        parts.append((e0[:, hs] * og_ref[0, h] + e1[:, hs] * og_ref[1, h] + e2[:, hs] * og_ref[2, h]) * inv[:, hs])
    y_att = jnp.concatenate(parts, axis=-1).astype(BF16)

    merged = _sigmoid(gt_ref[:, 0:D_MODEL].astype(F32)) * _dot(yr_ref[...], wb_ref[0])
    merged = merged + _sigmoid(gt_ref[:, D_MODEL:2 * D_MODEL].astype(F32)) * _dot(ym_ref[...], wb_ref[1])
    merged = merged + _sigmoid(gt_ref[:, 2 * D_MODEL:3 * D_MODEL].astype(F32)) * _dot(y_att, wb_ref[2])
    z = DN_ALPHA * x_ref[...] + _dot(merged.astype(BF16), wo_ref[...])
    x1 = _layer_norm(z, g1_ref[...], b1_ref[...])
    x1_ref[...] = x1

    xh = x1.astype(BF16)
    xl = (x1 - xh.astype(F32)).astype(BF16)
    logits = _dot(xh, wrh_ref[...]) + _dot(xl, wrh_ref[...]) + _dot(xh, wrl_ref[...]) + br_ref[...]
    lane = lax.broadcasted_iota(jnp.int32, (tm, LANES), 1)
    work = jnp.where(lane < N_EXPERTS, logits, -jnp.inf)
    vals, idxs = [], []
    for _ in range(TOP_K):
        m = jnp.max(work, axis=-1, keepdims=True)
        idx = jnp.min(jnp.where(work == m, lane, LANES), axis=-1, keepdims=True)
        vals.append(m)
        idxs.append(idx)
        work = jnp.where(lane == idx, -jnp.inf, work)
    es = [jnp.exp(v - vals[0]) for v in vals]
    tot = es[0] + es[1] + es[2] + es[3]
    pr = jnp.zeros((tm, LANES), F32)
    for k in range(TOP_K):
        pr = jnp.where(lane == k, es[k] / tot, pr)
    pr_ref[...] = pr

    onehot = jnp.zeros((tm, LANES), F32)
    for k in range(TOP_K):
        onehot = onehot + (lane == idxs[k]).astype(F32)
    ri = lax.broadcasted_iota(jnp.int32, (tm, tm), 0)
    ci = lax.broadcasted_iota(jnp.int32, (tm, tm), 1)
    before = _dot((ri > ci).astype(BF16), onehot.astype(BF16)) + carry_ref[...]
    ir = jnp.zeros((tm, LANES), jnp.int32)
    for k in range(TOP_K):
        rank = jnp.sum(jnp.where(lane == idxs[k], before, 0.0), axis=-1, keepdims=True).astype(jnp.int32)
        ir = jnp.where(lane == k, idxs[k], ir)
        ir = jnp.where(lane == TOP_K + k, rank, ir)
    ir_ref[...] = ir
    carry = carry_ref[...] + jnp.sum(onehot, axis=0, keepdims=True)
    carry_ref[...] = carry
    cnt_ref[...] = jnp.broadcast_to(carry, (8, LANES)).astype(jnp.int32)


def _merge(x, y_ret, y_ml, outs, lses, gates, p, tm):
    t = x.shape[0]
    tm = min(tm, t)
    rowb = lambda w: pl.BlockSpec((tm, w), lambda i: (i, 0))
    full = lambda shp: pl.BlockSpec(shp, lambda i: (0,) * len(shp))
    wr = jnp.zeros((D_MODEL, LANES), F32).at[:, :N_EXPERTS].set(p['w_router'].astype(F32))
    wrh = wr.astype(BF16)
    wrl = (wr - wrh.astype(F32)).astype(BF16)
    br = jnp.zeros((1, LANES), F32).at[0, :N_EXPERTS].set(p['b_router'].astype(F32))
    row = lambda a: a.reshape(1, D_MODEL).astype(F32)
    return pl.pallas_call(
        functools.partial(_merge_kernel, tm=tm),
        grid=(t // tm,),
        in_specs=[rowb(D_MODEL), rowb(MIX_W), rowb(MIX_W)]
                 + [pl.BlockSpec((tm // d, d * MIX_W), lambda i: (i, 0)) for _, d in ATT_PATTERNS]
                 + [pl.BlockSpec((tm // d, d * LANES), lambda i: (i, 0)) for _, d in ATT_PATTERNS]
                 + [rowb(N_BRANCHES * D_MODEL),
                  full((N_BRANCHES, MIX_W, D_MODEL)), full((D_MODEL, D_MODEL)),
                  full((1, D_MODEL)), full((1, D_MODEL)),
                  full((D_MODEL, LANES)), full((D_MODEL, LANES)), full((1, LANES))],
        out_specs=[rowb(D_MODEL), rowb(LANES), rowb(LANES), full((8, LANES))],
        out_shape=[jax.ShapeDtypeStruct((t, D_MODEL), F32),
                   jax.ShapeDtypeStruct((t, LANES), jnp.int32),
                   jax.ShapeDtypeStruct((t, LANES), F32),
                   jax.ShapeDtypeStruct((8, LANES), jnp.int32)],
        scratch_shapes=[pltpu.VMEM((1, LANES), F32), pltpu.VMEM((ATT_GROUPS, ATT_HEADS, tm, ATT_DV), F32),
                        pltpu.VMEM((ATT_GROUPS, tm, LANES), F32)],
        compiler_params=_params(("arbitrary",)),
        name="merge_ln1_router",
    )(x, y_ret, y_ml, outs[0], outs[1], outs[2], lses[0], lses[1], lses[2], gates,
      p['w_branch'].astype(BF16), p['w_out'].astype(BF16), row(p['ln1_g']), row(p['ln1_b']), wrh, wrl, br)


def _dispatch_kernel(dest_hbm, x_ref, xs_hbm, dest_smem, sem_idx, sem, *, tm):
    i = pl.program_id(0)
    cp = pltpu.make_async_copy(dest_hbm.at[i], dest_smem, sem_idx)
    cp.start()
    cp.wait()

    def issue(t, carry):
        for k in range(TOP_K):
            d = dest_smem[t * TOP_K + k]
            pltpu.make_async_copy(x_ref.at[pl.ds(t, 1), :], xs_hbm.at[pl.ds(d, 1), :], sem).start()
        return carry

    lax.fori_loop(0, tm, issue, 0)

    def drain(t, carry):
        for k in range(TOP_K):
            pltpu.make_async_copy(x_ref.at[pl.ds(0, 1), :], xs_hbm.at[pl.ds(0, 1), :], sem).wait()
        return carry

    lax.fori_loop(0, tm, drain, 0)


def _dispatch(x1, dest, n_slots, tm):
    t = x1.shape[0]
    tm = min(tm, t)
    return pl.pallas_call(
        functools.partial(_dispatch_kernel, tm=tm),
        grid=(t // tm,),
        in_specs=[pl.BlockSpec(memory_space=pl.ANY),
                  pl.BlockSpec((tm, D_MODEL), lambda i: (i, 0))],
        out_specs=pl.BlockSpec(memory_space=pl.ANY),
        out_shape=jax.ShapeDtypeStruct((n_slots, D_MODEL), x1.dtype),
        scratch_shapes=[pltpu.SMEM((tm * TOP_K,), jnp.int32), pltpu.SemaphoreType.DMA(()), pltpu.SemaphoreType.DMA(())],
        compiler_params=_params(("arbitrary",)),
        name="moe_dispatch",
    )(dest.reshape(t // tm, tm * TOP_K), x1)


def _ffn_kernel(be_ref, nr_ref, nv_ref, xs_ref, wg_ref, bg_ref, wu_ref, bu_ref, wd_ref, bd_ref, o_ref,
                wgb_ref, wub_ref, wdb_ref):
    i = pl.program_id(0)

    @pl.when(i < nv_ref[0])
    def _():
        @pl.when((i == 0) | (be_ref[i] != be_ref[jnp.maximum(i - 1, 0)]))
        def _():
            wgb_ref[...] = wg_ref[...].astype(BF16)
            wub_ref[...] = wu_ref[...].astype(BF16)
            wdb_ref[...] = wd_ref[...].astype(BF16)

        rows = lax.broadcasted_iota(jnp.int32, xs_ref.shape, 0)
        x = jnp.where(rows < nr_ref[i], xs_ref[...], 0.0).astype(BF16)
        gate = jnp.minimum(_dot(x, wgb_ref[...]) + bg_ref[...], SWIGLU_LIMIT)
        up = jnp.clip(_dot(x, wub_ref[...]) + bu_ref[...], -SWIGLU_LIMIT, SWIGLU_LIMIT)
        hid = (up + 1.0) * gate * _sigmoid(SWIGLU_ALPHA * gate)
        o_ref[...] = _dot(hid.astype(BF16), wdb_ref[...]) + bd_ref[...]


def _experts(xs, block_e, block_rows, n_valid, params, l, bm):
    n_slots = xs.shape[0]
    nb = n_slots // bm
    blk = lambda i, be, nr, nv: (jnp.minimum(i, nv[0] - 1), 0)
    wsp = pl.BlockSpec((None, None, D_MODEL, D_FF), lambda i, be, nr, nv: (l, be[i], 0, 0))
    bsp = pl.BlockSpec((None, None, 1, D_FF), lambda i, be, nr, nv: (l, be[i], 0, 0))
    b4 = lambda a: a.reshape(DEPTH, N_EXPERTS, 1, -1)
    return pl.pallas_call(
        _ffn_kernel,
        grid_spec=pltpu.PrefetchScalarGridSpec(
            num_scalar_prefetch=3,
            grid=(nb,),
            in_specs=[pl.BlockSpec((bm, D_MODEL), blk), wsp, bsp, wsp, bsp, wsp, bsp],
            out_specs=pl.BlockSpec((bm, D_MODEL), blk),
            scratch_shapes=[pltpu.VMEM((D_MODEL, D_FF), BF16), pltpu.VMEM((D_MODEL, D_FF), BF16),
                            pltpu.VMEM((D_FF, D_MODEL), BF16)]),
        out_shape=jax.ShapeDtypeStruct((n_slots, D_MODEL), F32),
        compiler_params=_params(("arbitrary",)),
        name="moe_experts",
    )(block_e, block_rows, n_valid, xs, params['w_gate'], b4(params['b_gate']), params['w_up'], b4(params['b_up']),
      params['w_down'], b4(params['b_down']))


def _combine_kernel(dest_hbm, x1_ref, pr_ref, ys_hbm, g2_ref, b2_ref, o_ref, ob_ref, dest_smem, buf, sem_idx, sem, *, tm):
    i = pl.program_id(0)
    cp = pltpu.make_async_copy(dest_hbm.at[i], dest_smem, sem_idx)
    cp.start()
    cp.wait()

    def issue(t, carry):
        for k in range(TOP_K):
            d = dest_smem[t * TOP_K + k]
            pltpu.make_async_copy(ys_hbm.at[pl.ds(d, 1), :], buf.at[k, pl.ds(t, 1), :], sem).start()
        return carry

    lax.fori_loop(0, tm, issue, 0)

    def drain(t, carry):
        for k in range(TOP_K):
            pltpu.make_async_copy(ys_hbm.at[pl.ds(0, 1), :], buf.at[0, pl.ds(0, 1), :], sem).wait()
        return carry

    lax.fori_loop(0, tm, drain, 0)

    pr = pr_ref[...]
    moe = pr[:, 0:1] * buf[0]
    for k in range(1, TOP_K):
        moe = moe + pr[:, k:k + 1] * buf[k]
    out = _layer_norm(DN_ALPHA * x1_ref[...] + moe, g2_ref[...], b2_ref[...])
    o_ref[...] = out
    ob_ref[...] = out.astype(BF16)


def _combine(x1, probs, ys, dest, p, tm):
    t = x1.shape[0]
    tm = min(tm, t)
    row = lambda a: a.reshape(1, D_MODEL).astype(F32)
    full = lambda shp: pl.BlockSpec(shp, lambda i: (0,) * len(shp))
    return pl.pallas_call(
        functools.partial(_combine_kernel, tm=tm),
        grid=(t // tm,),
        in_specs=[pl.BlockSpec(memory_space=pl.ANY),
                  pl.BlockSpec((tm, D_MODEL), lambda i: (i, 0)),
                  pl.BlockSpec((tm, LANES), lambda i: (i, 0)),
                  pl.BlockSpec(memory_space=pl.ANY),
                  full((1, D_MODEL)), full((1, D_MODEL))],
        out_specs=[pl.BlockSpec((tm, D_MODEL), lambda i: (i, 0)), pl.BlockSpec((tm, D_MODEL), lambda i: (i, 0))],
        out_shape=[jax.ShapeDtypeStruct((t, D_MODEL), F32), jax.ShapeDtypeStruct((t, D_MODEL), BF16)],
        scratch_shapes=[pltpu.SMEM((tm * TOP_K,), jnp.int32), pltpu.VMEM((TOP_K, tm, D_MODEL), F32),
                        pltpu.SemaphoreType.DMA(()), pltpu.SemaphoreType.DMA(())],
        compiler_params=_params(("arbitrary",)),
        name="moe_combine_ln2",
    )(dest.reshape(t // tm, tm * TOP_K), x1, probs, ys, row(p['ln2_g']), row(p['ln2_b']))


def _moe(x1, idx_rank, probs, counts, p, params, l, bm, tm_dispatch, tm_combine):
    t = x1.shape[0]
    n_assign = t * TOP_K
    nb = -(-n_assign // bm) + N_EXPERTS
    cnt = counts[0, :N_EXPERTS]
    padded = (cnt + bm - 1) // bm * bm
    pend = jnp.cumsum(padded)
    pstart = pend - padded
    n_valid = (pend[-1] // bm).astype(jnp.int32).reshape(1)
    blocks = jnp.arange(nb, dtype=jnp.int32)
    block_e = jnp.minimum(jnp.sum((pend // bm)[None, :] <= blocks[:, None], axis=1), N_EXPERTS - 1).astype(jnp.int32)
    block_rows = jnp.clip(cnt[block_e] - (blocks * bm - pstart[block_e]), 0, bm).astype(jnp.int32)
    idx = idx_rank[:, 0:TOP_K]
    rank = idx_rank[:, TOP_K:2 * TOP_K]
    onehot = idx[:, :, None] == jnp.arange(N_EXPERTS, dtype=jnp.int32)[None, None, :]
    dest = (jnp.sum(jnp.where(onehot, pstart[None, None, :], 0), axis=-1) + rank).astype(jnp.int32).reshape(-1)
    xs = _dispatch(x1, dest, nb * bm, tm_dispatch)
    ys = _experts(xs, block_e, block_rows, n_valid, params, l, bm)
    return _combine(x1, probs, ys, dest, p, tm_combine)


_EXPERT_WEIGHTS = ('w_gate', 'b_gate', 'w_up', 'b_up', 'w_down', 'b_down')


def _layer(x, xb, params, l, b, s, cfg):
    p = {k: v[l] for k, v in params.items() if k not in _EXPERT_WEIGHTS}
    w_in = p['w_in']
    o = IN_OFFS
    pad = jnp.zeros((D_MODEL, LANES - 2 * ML_HEADS), w_in.dtype)
    w_ret = w_in[:, o[0]:o[4]].astype(BF16)
    w_ml = jnp.concatenate([w_in[:, o[4]:o[5]], w_in[:, o[7]:o[8]]], axis=1).astype(BF16)
    w_if = jnp.concatenate([w_in[:, o[5]:o[7]], pad], axis=1).astype(BF16)
    w_gates = w_in[:, o[11]:o[12]].astype(BF16)
    mm = functools.partial(_matmul, tm=cfg['mm_tm'], tn=cfg['mm_tn'])
    ret_in = mm(xb, w_ret, BF16)
    ml_in = mm(xb, w_ml, BF16)
    if_in = mm(xb, w_if, F32)
    gates = mm(xb, w_gates, BF16)
    y_ret = _retention(ret_in, p['ret_gn'], b, s, cfg['seq_rows']).reshape(b * s, MIX_W)
    y_ml = _mlstm(ml_in, if_in, p, b, s, cfg['seq_rows']).reshape(b * s, MIX_W)
    outs, lses = [], []
    wq = ATT_HEADS * ATT_DK
    wv = ATT_HEADS * ATT_DV
    for g in range(ATT_GROUPS):
        w_att = jnp.concatenate([w_in[:, o[8] + g * wq:o[8] + (g + 1) * wq],
                                 w_in[:, o[9] + g * wq:o[9] + (g + 1) * wq],
                                 w_in[:, o[10] + g * wv:o[10] + (g + 1) * wv]], axis=1).astype(BF16)
        og, lg = _dilated_group(_matmul_strided(xb, w_att, ATT_PATTERNS[g][1], cfg['mm_tm']), g, b, s, cfg['att_nq'])
        outs.append(og)
        lses.append(lg)
    x1, idx_rank, probs, counts = _merge(x, y_ret, y_ml, outs, lses, gates, p, cfg['merge_tm'])
    return _moe(x1, idx_rank, probs, counts, p, params, l, cfg['moe_bm'], cfg['dispatch_tm'], cfg['combine_tm'])


CFG = dict(mm_tm=1024, mm_tn=1024, seq_rows=512, att_nq=4, merge_tm=512, moe_bm=512, dispatch_tm=512, combine_tm=256)

_PARAM_NAMES = ('w_in', 'ret_gn', 'ml_conv_w', 'ml_conv_b', 'ml_wq', 'ml_wk', 'ml_wv', 'ml_bi', 'ml_bf', 'ml_gn',
                'ml_skip', 'w_branch', 'w_out', 'ln1_g', 'ln1_b', 'w_router', 'b_router', 'w_gate', 'b_gate',
                'w_up', 'b_up', 'w_down', 'b_down', 'ln2_g', 'ln2_b')


def _forward(x, params, cfg):
    b, s, d = x.shape
    xf = x.reshape(b * s, d).astype(F32)
    xb = xf.astype(BF16)
    for l in range(DEPTH):
        xf, xb = _layer(xf, xb, params, l, b, s, cfg)
    return xf.reshape(b, s, d).astype(x.dtype)


def kernel(x, w_in, ret_gn, ml_conv_w, ml_conv_b, ml_wq, ml_wk, ml_wv, ml_bi, ml_bf, ml_gn, ml_skip, w_branch, w_out, ln1_g, ln1_b, w_router, b_router, w_gate, b_gate, w_up, b_up, w_down, b_down, ln2_g, ln2_b):
    params = dict(zip(_PARAM_NAMES, (w_in, ret_gn, ml_conv_w, ml_conv_b, ml_wq, ml_wk, ml_wv, ml_bi, ml_bf, ml_gn,
                                     ml_skip, w_branch, w_out, ln1_g, ln1_b, w_router, b_router, w_gate, b_gate,
                                     w_up, b_up, w_down, b_down, ln2_g, ln2_b)))
    return _forward(x, params, CFG)
```

```python
import functools

import jax
import jax.numpy as jnp
import numpy as np
from jax import lax
from jax.experimental import pallas as pl
from jax.experimental.pallas import tpu as pltpu

F32 = jnp.float32
BF16 = jnp.bfloat16

D_MODEL = 1024
DEPTH = 2
MIX_W = D_MODEL // 2
N_BRANCHES = 3
RET_HEADS = 4
RET_DV = MIX_W // RET_HEADS
RET_DK = RET_DV // 2
ML_HEADS = 4
ML_DH = MIX_W // ML_HEADS
ML_CONV = 4
ML_QK_BLOCK = 4
ATT_PATTERNS = ((128, 1), (512, 4), (2048, 16))
ATT_GROUPS = len(ATT_PATTERNS)
ATT_HEADS = 4
ATT_DV = MIX_W // ATT_HEADS
ATT_DK = ATT_DV // 2
N_EXPERTS = 32
TOP_K = 4
D_FF = D_MODEL
SWIGLU_LIMIT = 7.0
SWIGLU_ALPHA = 1.702
DN_ALPHA = (2.0 * DEPTH) ** 0.25
EPS = 1e-5

CHUNK = 128
LANES = 128
NEG = -1e30
VMEM_LIMIT = 56 * 1024 * 1024

IN_SIZES = (RET_HEADS * RET_DK, RET_HEADS * RET_DK, MIX_W, MIX_W,
            MIX_W, ML_HEADS, ML_HEADS, MIX_W,
            ATT_GROUPS * ATT_HEADS * ATT_DK, ATT_GROUPS * ATT_HEADS * ATT_DK, ATT_GROUPS * ATT_HEADS * ATT_DV,
            N_BRANCHES * D_MODEL)
IN_OFFS = tuple(int(v) for v in np.cumsum((0,) + IN_SIZES))


def _params(sem):
    return pltpu.CompilerParams(dimension_semantics=sem, vmem_limit_bytes=VMEM_LIMIT)


def _dot(a, b):
    return jnp.dot(a, b, preferred_element_type=F32)


def _dot_nt(a, b):
    return lax.dot_general(a, b, (((1,), (1,)), ((), ())), preferred_element_type=F32)


def _dot_tn(a, b):
    return lax.dot_general(a, b, (((0,), (0,)), ((), ())), preferred_element_type=F32)


def _sigmoid(x):
    return 1.0 / (1.0 + jnp.exp(-x))


def _mm_kernel(x_ref, w_ref, o_ref):
    o_ref[...] = _dot(x_ref[...], w_ref[...]).astype(o_ref.dtype)


def _matmul(x, w, out_dtype, tm, tn):
    t, k = x.shape
    n = w.shape[1]
    tm = min(tm, t)
    tn = max(c for c in range(LANES, min(tn, n) + 1, LANES) if n % c == 0)
    assert t % tm == 0
    return pl.pallas_call(
        _mm_kernel,
        grid=(n // tn, t // tm),
        in_specs=[pl.BlockSpec((tm, k), lambda j, i: (i, 0)),
                  pl.BlockSpec((k, tn), lambda j, i: (0, j))],
        out_specs=pl.BlockSpec((tm, tn), lambda j, i: (i, j)),
        out_shape=jax.ShapeDtypeStruct((t, n), out_dtype),
        compiler_params=_params(("arbitrary", "arbitrary")),
        name="in_proj",
    )(x, w)


def _mm_strided_kernel(x_ref, w_ref, o_ref, y_ref, *, d):
    tm = x_ref.shape[0]
    n = tm // d
    width = w_ref.shape[1]
    y = _dot(x_ref[...], w_ref[...])
    if d == 1:
        o_ref[...] = y.astype(o_ref.dtype)
        return
    for c in range(width // LANES):
        y_ref[c] = y[:, c * LANES:(c + 1) * LANES]
    for r in range(d):
        for c in range(width // LANES):
            o_ref[:, r * width + c * LANES:r * width + (c + 1) * LANES] = (
                y_ref[c, pl.ds(r, n, stride=d), :].astype(o_ref.dtype))


def _matmul_strided(x, w, d, tm):
    t, k = x.shape
    n = w.shape[1]
    tm = min(tm, t)
    assert t % tm == 0 and tm % (16 * d) == 0
    return pl.pallas_call(
        functools.partial(_mm_strided_kernel, d=d),
        grid=(t // tm,),
        in_specs=[pl.BlockSpec((tm, k), lambda i: (i, 0)),
                  pl.BlockSpec((k, n), lambda i: (0, 0))],
        out_specs=pl.BlockSpec((tm // d, d * n), lambda i: (i, 0)),
        out_shape=jax.ShapeDtypeStruct((t // d, d * n), BF16),
        scratch_shapes=[pltpu.VMEM((n // LANES, tm, LANES), F32)],
        compiler_params=_params(("arbitrary",)),
        name=f"att_in_proj_d{d}",
    )(x, w)


def _head_norm(o, gn):
    mu = jnp.mean(o, axis=-1, keepdims=True)
    oc = o - mu
    var = jnp.mean(oc * oc, axis=-1, keepdims=True)
    return oc * lax.rsqrt(var + EPS) * gn


def _ret_kernel(blk_ref, dm_ref, qd_ref, kd_ref, cd_ref, gn_ref, o_ref, st_ref, *, nc):
    @pl.when(pl.program_id(1) == 0)
    def _():
        st_ref[...] = jnp.zeros_like(st_ref)

    hq = RET_HEADS * RET_DK
    for c in range(nc):
        rows = pl.ds(c * CHUNK, CHUNK)
        for h in range(RET_HEADS):
            q = blk_ref[rows, h * RET_DK:(h + 1) * RET_DK]
            k = blk_ref[rows, hq + h * RET_DK:hq + (h + 1) * RET_DK]
            v = blk_ref[rows, 2 * hq + h * RET_DV:2 * hq + (h + 1) * RET_DV]
            g = blk_ref[rows, 2 * hq + MIX_W + h * RET_DV:2 * hq + MIX_W + (h + 1) * RET_DV].astype(F32)
            st = st_ref[h]
            a = (_dot_nt(q, k) * dm_ref[h]).astype(BF16)
            o = _dot(a, v) + _dot(q, st.astype(BF16)) * qd_ref[h]
            kv = _dot_tn(k, (v.astype(F32) * kd_ref[h]).astype(BF16))
            st_ref[h] = cd_ref[h] * st + kv
            y = _head_norm(o, gn_ref[:, h * RET_DV:(h + 1) * RET_DV]) * (g * _sigmoid(g))
            o_ref[rows, h * RET_DV:(h + 1) * RET_DV] = y.astype(o_ref.dtype)


def _retention(ret_in, ret_gn, b, s, rows):
    nc = rows // CHUNK
    scale = RET_DK ** -0.5
    log_gamma = jnp.log1p(-jnp.exp2(-5.0 - jnp.arange(RET_HEADS, dtype=F32)))
    pos = jnp.arange(CHUNK, dtype=F32)
    diff = pos[:, None] - pos[None, :]
    dm = jnp.where(diff >= 0, jnp.exp(log_gamma[:, None, None] * jnp.maximum(diff, 0.0)), 0.0) * scale
    kd = jnp.broadcast_to(jnp.exp(log_gamma[:, None] * (CHUNK - 1.0 - pos))[:, :, None], (RET_HEADS, CHUNK, RET_DV))
    qd = jnp.broadcast_to((jnp.exp(log_gamma[:, None] * (pos + 1.0)) * scale)[:, :, None], (RET_HEADS, CHUNK, RET_DV))
    cd = jnp.broadcast_to(jnp.exp(log_gamma * CHUNK)[:, None, None], (RET_HEADS, 1, RET_DV))
    w_in = ret_in.shape[-1]
    full = lambda shp: pl.BlockSpec(shp, lambda bi, si: (0,) * len(shp))
    return pl.pallas_call(
        functools.partial(_ret_kernel, nc=nc),
        grid=(b, s // rows),
        in_specs=[pl.BlockSpec((None, rows, w_in), lambda bi, si: (bi, si, 0)),
                  full((RET_HEADS, CHUNK, CHUNK)), full((RET_HEADS, CHUNK, RET_DV)),
                  full((RET_HEADS, CHUNK, RET_DV)), full((RET_HEADS, 1, RET_DV)), full((1, MIX_W))],
        out_specs=pl.BlockSpec((None, rows, MIX_W), lambda bi, si: (bi, si, 0)),
        out_shape=jax.ShapeDtypeStruct((b, s, MIX_W), BF16),
        scratch_shapes=[pltpu.VMEM((RET_HEADS, RET_DK, RET_DV), F32)],
        compiler_params=_params(("arbitrary", "arbitrary")),
        name="retention",
    )(ret_in.reshape(b, s, w_in), dm, qd, kd, cd, ret_gn.reshape(1, MIX_W).astype(F32))


def _log_sigmoid(x):
    return jnp.minimum(x, 0.0) - jnp.log(1.0 + jnp.exp(-jnp.abs(x)))


def _ml_kernel(ml_ref, if_ref, cw_ref, cb_ref, bq_ref, bk_ref, bv_ref, gb_ref, gn_ref, sk_ref,
               o_ref, c_ref, n_ref, m_ref, halo_ref, *, nc):
    rows_total = nc * CHUNK

    @pl.when(pl.program_id(1) == 0)
    def _():
        c_ref[...] = jnp.zeros_like(c_ref)
        n_ref[...] = jnp.zeros_like(n_ref)
        m_ref[...] = jnp.zeros_like(m_ref)
        halo_ref[...] = jnp.zeros_like(halo_ref)

    x_all = ml_ref[:, 0:MIX_W].astype(F32)
    xf = jnp.concatenate([halo_ref[...], x_all], axis=0)
    acc = jnp.broadcast_to(cb_ref[...], (rows_total, MIX_W))
    for j in range(ML_CONV):
        sh = ML_CONV - 1 - j
        xs = xf if sh == 0 else pltpu.roll(xf, sh, 0)
        acc = acc + xs[8:8 + rows_total] * cw_ref[j:j + 1, :]
    halo_ref[...] = x_all[rows_total - 8:rows_total]
    mc_all = acc * _sigmoid(acc)

    ri = lax.broadcasted_iota(jnp.int32, (CHUNK, CHUNK), 0)
    ci = lax.broadcasted_iota(jnp.int32, (CHUNK, CHUNK), 1)
    tri = ri >= ci
    tril = tri.astype(F32)
    kscale = ML_DH ** -0.5

    for c in range(nc):
        r0 = c * CHUNK
        g_pre = if_ref[r0:r0 + CHUNK, :] + gb_ref[...]
        lf = _log_sigmoid(g_pre)
        cum = jnp.dot(tril, lf, preferred_element_type=F32, precision=lax.Precision.HIGHEST)
        g_t = g_pre.T
        cum_t = cum.T
        for h in range(ML_HEADS):
            cols = slice(h * ML_DH, (h + 1) * ML_DH)
            mc = mc_all[r0:r0 + CHUNK, cols]
            xv = x_all[r0:r0 + CHUNK, cols]
            q = _dot(mc.astype(BF16), bq_ref[h]).astype(BF16)
            k = (_dot(mc.astype(BF16), bk_ref[h]) * kscale).astype(BF16)
            v = _dot(xv.astype(BF16), bv_ref[h]).astype(BF16)
            cum_col = cum[:, ML_HEADS + h:ML_HEADS + h + 1]
            ig_col = g_pre[:, h:h + 1]
            cum_row = cum_t[ML_HEADS + h:ML_HEADS + h + 1, :]
            ig_row = g_t[h:h + 1, :]
            tot = cum[CHUNK - 1:CHUNK, ML_HEADS + h:ML_HEADS + h + 1]
            m_prev = m_ref[h:h + 1, 0:1]
            c_prev = c_ref[h]
            n_prev = n_ref[h:h + 1, :]
            dlog = jnp.where(tri, cum_col + (ig_row - cum_row), NEG)
            m_inter = cum_col + m_prev
            m_q = jnp.maximum(m_inter, jnp.max(dlog, axis=-1, keepdims=True))
            w_qk = jnp.exp(dlog - m_q) * _dot_nt(q, k)
            inter = jnp.exp(m_inter - m_q)
            num = _dot(w_qk.astype(BF16), v) + inter * _dot(q, c_prev.astype(BF16))
            den = jnp.sum(w_qk, axis=-1, keepdims=True) + inter * jnp.sum(q.astype(F32) * n_prev, axis=-1, keepdims=True)
            hh = num * (1.0 / jnp.maximum(jnp.abs(den), jnp.exp(-m_q)))
            a_col = tot - cum_col + ig_col
            a_max = jnp.max(a_col, axis=0, keepdims=True)
            wa = jnp.exp(a_col - a_max)
            kf = k.astype(F32)
            chunk_c = _dot_tn(k, (wa * v.astype(F32)).astype(BF16))
            chunk_n = jnp.sum(wa * kf, axis=0, keepdims=True)
            m_new = jnp.maximum(tot + m_prev, a_max)
            s_old = jnp.exp(tot + m_prev - m_new)
            s_new = jnp.exp(a_max - m_new)
            c_ref[h] = s_old * c_prev + s_new * chunk_c
            n_ref[h:h + 1, :] = s_old * n_prev + s_new * chunk_n
            m_ref[h:h + 1, :] = jnp.broadcast_to(m_new, (1, LANES))
            mo = ml_ref[r0:r0 + CHUNK, MIX_W + h * ML_DH:MIX_W + (h + 1) * ML_DH].astype(F32)
            y = _sigmoid(mo) * (_head_norm(hh, gn_ref[:, cols]) + sk_ref[:, cols] * mc)
            o_ref[r0:r0 + CHUNK, cols] = y.astype(o_ref.dtype)


def _block_diag(w):
    nb = w.shape[0]
    per = nb // ML_HEADS
    wh = w.reshape(ML_HEADS, per, ML_QK_BLOCK, ML_QK_BLOCK)
    eye = jnp.eye(per, dtype=w.dtype)
    bd = jnp.einsum('hncd,nm->hncmd', wh, eye)
    return bd.reshape(ML_HEADS, per * ML_QK_BLOCK, per * ML_QK_BLOCK).astype(BF16)


def _mlstm(ml_in, if_in, p, b, s, rows):
    nc = rows // CHUNK
    full = lambda shp: pl.BlockSpec(shp, lambda bi, si: (0,) * len(shp))
    gb = jnp.zeros((1, LANES), F32).at[0, 0:ML_HEADS].set(p['ml_bi'].astype(F32)).at[0, ML_HEADS:2 * ML_HEADS].set(p['ml_bf'].astype(F32))
    row = lambda a: a.reshape(1, MIX_W).astype(F32)
    return pl.pallas_call(
        functools.partial(_ml_kernel, nc=nc),
        grid=(b, s // rows),
        in_specs=[pl.BlockSpec((None, rows, 2 * MIX_W), lambda bi, si: (bi, si, 0)),
                  pl.BlockSpec((None, rows, LANES), lambda bi, si: (bi, si, 0)),
                  full((ML_CONV, MIX_W)), full((1, MIX_W)),
                  full((ML_HEADS, ML_DH, ML_DH)), full((ML_HEADS, ML_DH, ML_DH)), full((ML_HEADS, ML_DH, ML_DH)),
                  full((1, LANES)), full((1, MIX_W)), full((1, MIX_W))],
        out_specs=pl.BlockSpec((None, rows, MIX_W), lambda bi, si: (bi, si, 0)),
        out_shape=jax.ShapeDtypeStruct((b, s, MIX_W), BF16),
        scratch_shapes=[pltpu.VMEM((ML_HEADS, ML_DH, ML_DH), F32), pltpu.VMEM((8, ML_DH), F32),
                        pltpu.VMEM((8, LANES), F32), pltpu.VMEM((8, MIX_W), F32)],
        compiler_params=_params(("arbitrary", "arbitrary")),
        name="mlstm",
    )(ml_in.reshape(b, s, 2 * MIX_W), if_in.reshape(b, s, LANES),
      p['ml_conv_w'].astype(F32), row(p['ml_conv_b']),
      _block_diag(p['ml_wq']), _block_diag(p['ml_wk']), _block_diag(p['ml_wv']),
      gb, row(p['ml_gn']), row(p['ml_skip']))


def _att_kernel(q_ref, kc_ref, kp_ref, vc_ref, vp_ref, bias_ref, o_ref, lse_ref, *, nq):
    first = pl.program_id(2) == 0
    lane = lax.broadcasted_iota(jnp.int32, (CHUNK, LANES), 1)
    ones = jnp.ones((2 * CHUNK, ATT_DV), BF16)
    for j in range(nq):
        rows = slice(j * CHUNK, (j + 1) * CHUNK)
        lse_out = jnp.zeros((CHUNK, LANES), F32)
        for h in range(ATT_HEADS):
            dk = slice(h * ATT_DK, (h + 1) * ATT_DK)
            dv = slice(h * ATT_DV, (h + 1) * ATT_DV)
            q = q_ref[rows, dk]
            if j == 0:
                kk = jnp.concatenate([kp_ref[:, dk], kc_ref[rows, dk]], axis=0)
                vv = jnp.concatenate([vp_ref[:, dv], vc_ref[rows, dv]], axis=0)
                bias = bias_ref[jnp.where(first, 0, 1), h]
            else:
                kk = kc_ref[(j - 1) * CHUNK:(j + 1) * CHUNK, dk]
                vv = vc_ref[(j - 1) * CHUNK:(j + 1) * CHUNK, dv]
                bias = bias_ref[1, h]
            s = _dot_nt(q, kk) * (ATT_DK ** -0.5) + bias
            m = jnp.max(s, axis=-1, keepdims=True)
            p = jnp.exp(s - m).astype(BF16)
            oa = _dot(p, jnp.concatenate([vv, ones], axis=1))
            l = oa[:, ATT_DV:2 * ATT_DV]
            o_ref[rows, dv] = (oa[:, 0:ATT_DV] * (1.0 / l)).astype(o_ref.dtype)
            lse_out = jnp.where(lane == h, m + jnp.log(l), lse_out)
        lse_ref[rows, :] = lse_out


def _att_bias(window, dilation, slopes):
    wb = window // dilation
    qi = jnp.arange(wb)[:, None]
    kj = jnp.arange(2 * wb)[None, :]
    delta = qi + wb - kj
    band = (delta >= 0) & (delta <= wb)
    bias = -slopes[:, None, None] * (dilation * delta).astype(F32)
    later = jnp.where(band[None], bias, NEG)
    first = jnp.where((band & (kj >= wb))[None], bias, NEG)
    return jnp.stack([first, later]).astype(F32)


def _dilated_group(qkv, g, b, s, nq):
    window, d = ATT_PATTERNS[g]
    assert window // d == CHUNK
    l_sub = s // d
    assert l_sub % CHUNK == 0
    n_blk = l_sub // CHUNK
    slopes = jnp.exp2(-8.0 * jnp.arange(1, ATT_GROUPS * ATT_HEADS + 1, dtype=F32) / (ATT_GROUPS * ATT_HEADS))
    bias = _att_bias(window, d, slopes.reshape(ATT_GROUPS, ATT_HEADS)[g])
    wq = ATT_HEADS * ATT_DK
    wv = ATT_HEADS * ATT_DV
    per_tok = 2 * wq + wv
    x = qkv.reshape(b, l_sub, d * per_tok)
    qb = per_tok // wq
    vb = per_tok // wv
    nq = min(nq, n_blk)
    assert n_blk % nq == 0
    span = nq * CHUNK
    prev = lambda n: jnp.maximum(n * nq - 1, 0)
    o, lse = pl.pallas_call(
        functools.partial(_att_kernel, nq=nq),
        grid=(b, d, n_blk // nq),
        in_specs=[pl.BlockSpec((None, span, wq), lambda bi, r, n: (bi, n, r * qb)),
                  pl.BlockSpec((None, span, wq), lambda bi, r, n: (bi, n, r * qb + 1)),
                  pl.BlockSpec((None, CHUNK, wq), lambda bi, r, n: (bi, prev(n), r * qb + 1)),
                  pl.BlockSpec((None, span, wv), lambda bi, r, n: (bi, n, r * vb + 1)),
                  pl.BlockSpec((None, CHUNK, wv), lambda bi, r, n: (bi, prev(n), r * vb + 1)),
                  pl.BlockSpec((2, ATT_HEADS, CHUNK, 2 * CHUNK), lambda bi, r, n: (0, 0, 0, 0))],
        out_specs=[pl.BlockSpec((None, span, wv), lambda bi, r, n: (bi, n, r)),
                   pl.BlockSpec((None, span, LANES), lambda bi, r, n: (bi, n, r))],
        out_shape=[jax.ShapeDtypeStruct((b, l_sub, d * wv), BF16),
                   jax.ShapeDtypeStruct((b, l_sub, d * LANES), F32)],
        compiler_params=_params(("arbitrary", "arbitrary", "arbitrary")),
        name=f"dilated_attn_g{g}",
    )(x, x, x, x, x, bias)
    return o.reshape(b * l_sub, d * wv), lse.reshape(b * l_sub, d * LANES)


def _layer_norm(z, g, b):
    mu = jnp.mean(z, axis=-1, keepdims=True)
    zc = z - mu
    var = jnp.mean(zc * zc, axis=-1, keepdims=True)
    return zc * lax.rsqrt(var + EPS) * g + b


def _merge_kernel(x_ref, yr_ref, ym_ref, o0_ref, o1_ref, o2_ref, l0_ref, l1_ref, l2_ref, gt_ref,
                  wb_ref, wo_ref, g1_ref, b1_ref, wrh_ref, wrl_ref, br_ref,
                  x1_ref, x1b_ref, ir_ref, irt_ref, pr_ref, cnt_ref, og_ref, lg_ref, *, tm):
    for g, (src_o, src_l) in enumerate(((o0_ref, l0_ref), (o1_ref, l1_ref), (o2_ref, l2_ref))):
        d = ATT_PATTERNS[g][1]
        n = tm // d
        for r in range(d):
            rows = pl.ds(r, n, stride=d) if d > 1 else pl.ds(0, n)
            for h in range(ATT_HEADS):
                og_ref[g, h, rows, :] = src_o[:, r * MIX_W + h * ATT_DV:r * MIX_W + (h + 1) * ATT_DV].astype(F32)
            lg_ref[g, rows, :] = src_l[:, r * LANES:(r + 1) * LANES]

    l0, l1, l2 = lg_ref[0], lg_ref[1], lg_ref[2]
    lm = jnp.maximum(jnp.maximum(l0, l1), l2)
    e0, e1, e2 = jnp.exp(l0 - lm), jnp.exp(l1 - lm), jnp.exp(l2 - lm)
    inv = 1.0 / (e0 + e1 + e2)
    parts = []
    for h in range(ATT_HEADS):
        dv = slice(h * ATT_DV, (h + 1) * ATT_DV)
        hs = slice(h, h + 1)
        parts.append((e0[:, hs] * og_ref[0, h] + e1[:, hs] * og_ref[1, h] + e2[:, hs] * og_ref[2, h]) * inv[:, hs])
    y_att = jnp.concatenate(parts, axis=-1).astype(BF16)

    merged = _sigmoid(gt_ref[:, 0:D_MODEL].astype(F32)) * _dot(yr_ref[...], wb_ref[0])
    merged = merged + _sigmoid(gt_ref[:, D_MODEL:2 * D_MODEL].astype(F32)) * _dot(ym_ref[...], wb_ref[1])
    merged = merged + _sigmoid(gt_ref[:, 2 * D_MODEL:3 * D_MODEL].astype(F32)) * _dot(y_att, wb_ref[2])
    z = DN_ALPHA * x_ref[...] + _dot(merged.astype(BF16), wo_ref[...])
    x1 = _layer_norm(z, g1_ref[...], b1_ref[...])
    x1_ref[...] = x1
    x1b_ref[...] = x1.astype(BF16)

    xh = x1.astype(BF16)
    xl = (x1 - xh.astype(F32)).astype(BF16)
    logits = _dot(xh, wrh_ref[...]) + _dot(xl, wrh_ref[...]) + _dot(xh, wrl_ref[...]) + br_ref[...]
    lane = lax.broadcasted_iota(jnp.int32, (tm, LANES), 1)
    work = jnp.where(lane < N_EXPERTS, logits, -jnp.inf)
    vals, idxs = [], []
    for _ in range(TOP_K):
        m = jnp.max(work, axis=-1, keepdims=True)
        idx = jnp.min(jnp.where(work == m, lane, LANES), axis=-1, keepdims=True)
        vals.append(m)
        idxs.append(idx)
        work = jnp.where(lane == idx, -jnp.inf, work)
    es = [jnp.exp(v - vals[0]) for v in vals]
    tot = es[0] + es[1] + es[2] + es[3]
    pr = jnp.zeros((tm, LANES), F32)
    for k in range(TOP_K):
        pr = jnp.where(lane == k, es[k] / tot, pr)
    pr_ref[...] = pr

    onehot = jnp.zeros((tm, LANES), F32)
    for k in range(TOP_K):
        onehot = onehot + (lane == idxs[k]).astype(F32)
    ri = lax.broadcasted_iota(jnp.int32, (tm, tm), 0)
    ci = lax.broadcasted_iota(jnp.int32, (tm, tm), 1)
    before = _dot((ri > ci).astype(BF16), onehot.astype(BF16))
    ir = jnp.zeros((tm, LANES), F32)
    for k in range(TOP_K):
        rank = jnp.sum(jnp.where(lane == idxs[k], before, 0.0), axis=-1, keepdims=True)
        ir = jnp.where(lane == k, idxs[k].astype(F32), ir)
        ir = jnp.where(lane == TOP_K + k, rank, ir)
    ir_ref[...] = ir.astype(jnp.int32)
    irt_ref[...] = ir.T[0:8, :]
    cnt_ref[...] = jnp.broadcast_to(jnp.sum(onehot, axis=0, keepdims=True), (8, LANES)).astype(jnp.int32)


def _merge(x, y_ret, y_ml, outs, lses, gates, p, tm):
    t = x.shape[0]
    tm = min(tm, t)
    rowb = lambda w: pl.BlockSpec((tm, w), lambda i: (i, 0))
    full = lambda shp: pl.BlockSpec(shp, lambda i: (0,) * len(shp))
    wr = jnp.zeros((D_MODEL, LANES), F32).at[:, :N_EXPERTS].set(p['w_router'].astype(F32))
    wrh = wr.astype(BF16)
    wrl = (wr - wrh.astype(F32)).astype(BF16)
    br = jnp.zeros((1, LANES), F32).at[0, :N_EXPERTS].set(p['b_router'].astype(F32))
    row = lambda a: a.reshape(1, D_MODEL).astype(F32)
    return pl.pallas_call(
        functools.partial(_merge_kernel, tm=tm),
        grid=(t // tm,),
        in_specs=[rowb(D_MODEL), rowb(MIX_W), rowb(MIX_W)]
                 + [pl.BlockSpec((tm // d, d * MIX_W), lambda i: (i, 0)) for _, d in ATT_PATTERNS]
                 + [pl.BlockSpec((tm // d, d * LANES), lambda i: (i, 0)) for _, d in ATT_PATTERNS]
                 + [rowb(N_BRANCHES * D_MODEL),
                  full((N_BRANCHES, MIX_W, D_MODEL)), full((D_MODEL, D_MODEL)),
                  full((1, D_MODEL)), full((1, D_MODEL)),
                  full((D_MODEL, LANES)), full((D_MODEL, LANES)), full((1, LANES))],
        out_specs=[rowb(D_MODEL), rowb(D_MODEL), rowb(LANES), pl.BlockSpec((8, tm), lambda i: (0, i)), rowb(LANES),
                   pl.BlockSpec((None, 8, LANES), lambda i: (i, 0, 0))],
        out_shape=[jax.ShapeDtypeStruct((t, D_MODEL), F32),
                   jax.ShapeDtypeStruct((t, D_MODEL), BF16),
                   jax.ShapeDtypeStruct((t, LANES), jnp.int32),
                   jax.ShapeDtypeStruct((8, t), F32),
                   jax.ShapeDtypeStruct((t, LANES), F32),
                   jax.ShapeDtypeStruct((t // tm, 8, LANES), jnp.int32)],
        scratch_shapes=[pltpu.VMEM((ATT_GROUPS, ATT_HEADS, tm, ATT_DV), F32),
                        pltpu.VMEM((ATT_GROUPS, tm, LANES), F32)],
        compiler_params=_params(("arbitrary",)),
        name="merge_ln1_router",
    )(x, y_ret, y_ml, outs[0], outs[1], outs[2], lses[0], lses[1], lses[2], gates,
      p['w_branch'].astype(BF16), p['w_out'].astype(BF16), row(p['ln1_g']), row(p['ln1_b']), wrh, wrl, br)


ROUTE_TM = 512
REGION_CAP = 128
REGION_PIECES = (128, 64, 32, 16, 8)
EXPERTS_PER_CHUNK = 4
HALF = D_MODEL // 2
HI_MASK = -65536


def _pack_bf16_pairs(y):
    lo = lax.shift_right_logical(lax.bitcast_convert_type(y[:, :HALF], jnp.int32), 16)
    hi = lax.bitcast_convert_type(y[:, HALF:], jnp.int32) & HI_MASK
    return lo | hi


def _unpack_bf16_pairs(w):
    lo = lax.bitcast_convert_type(lax.shift_left(w, 16), F32)
    hi = lax.bitcast_convert_type(w & HI_MASK, F32)
    return jnp.concatenate([lo, hi], axis=1).astype(BF16)


def _region_pieces(n, fn):
    for size in REGION_PIECES:
        if size == REGION_CAP:
            cond, pos = n >= REGION_CAP, 0
        else:
            cond, pos = (n & size) != 0, pl.multiple_of(n & (-2 * size), 8)

        @pl.when(cond)
        def _(pos=pos, size=size):
            fn(pos, size)


def _region_copies(off_ref, cnt_ref, tile, p, make_copy, start):
    for e in range(N_EXPERTS):
        n = jnp.clip(cnt_ref[tile * N_EXPERTS + e] - p * REGION_CAP, 0, REGION_CAP)
        base = off_ref[tile * N_EXPERTS + e] + p * REGION_CAP

        def piece(pos, size, e=e, base=base):
            cp = make_copy(pl.ds(e * REGION_CAP + pos, size), pl.ds(pl.multiple_of(base + pos, 8), size))
            if start:
                cp.start()
            else:
                cp.wait()

        _region_pieces(n, piece)


def _slot_targets(idx, lrank, p):
    lr = lrank - p * REGION_CAP
    return jnp.where((lr >= 0) & (lr < REGION_CAP), idx * REGION_CAP + lr, -1)


def _dispatch_kernel(off_ref, cnt_ref, np_ref, x_ref, irt_ref, xs_hbm, stage_ref, sem, *, tm):
    i = pl.program_id(0)
    rows_c = EXPERTS_PER_CHUNK * REGION_CAP
    row_id = lax.broadcasted_iota(jnp.int32, (rows_c, tm), 0)
    meta = irt_ref[...].astype(jnp.int32)
    x = x_ref[...]

    def one_pass(p, carry):
        tgt = [_slot_targets(meta[k:k + 1, :], meta[TOP_K + k:TOP_K + k + 1, :], p) for k in range(TOP_K)]
        for c in range(N_EXPERTS // EXPERTS_PER_CHUNK):
            hit = row_id == (tgt[0] - c * rows_c)
            for k in range(1, TOP_K):
                hit = hit | (row_id == (tgt[k] - c * rows_c))
            sel = jnp.where(hit, 1.0, 0.0).astype(BF16)
            stage_ref[c * rows_c:(c + 1) * rows_c, :] = _pack_bf16_pairs(_dot(sel, x))
        mk = lambda src, dst: pltpu.make_async_copy(stage_ref.at[src], xs_hbm.at[dst], sem)
        _region_copies(off_ref, cnt_ref, i, p, mk, True)
        _region_copies(off_ref, cnt_ref, i, p, mk, False)
        return carry

    lax.fori_loop(0, np_ref[i], one_pass, 0)


def _dispatch(x1b, irt, off, cnt8, n_pass, n_slots, tm):
    t = x1b.shape[0]
    return pl.pallas_call(
        functools.partial(_dispatch_kernel, tm=tm),
        grid_spec=pltpu.PrefetchScalarGridSpec(
            num_scalar_prefetch=3,
            grid=(t // tm,),
            in_specs=[pl.BlockSpec((tm, D_MODEL), lambda i, *_: (i, 0)),
                      pl.BlockSpec((8, tm), lambda i, *_: (0, i))],
            out_specs=pl.BlockSpec(memory_space=pl.ANY),
            scratch_shapes=[pltpu.VMEM((N_EXPERTS * REGION_CAP, HALF), jnp.int32), pltpu.SemaphoreType.DMA(())]),
        out_shape=jax.ShapeDtypeStruct((n_slots, HALF), jnp.int32),
        compiler_params=_params(("arbitrary",)),
        name="moe_dispatch",
    )(off, cnt8, n_pass, x1b, irt)


def _ffn_kernel(be_ref, nr_ref, nv_ref, xs_ref, wg_ref, bg_ref, wu_ref, bu_ref, wd_ref, bd_ref, o_ref,
                wgb_ref, wub_ref, wdb_ref):
    i = pl.program_id(0)

    @pl.when(i < nv_ref[0])
    def _():
        @pl.when((i == 0) | (be_ref[i] != be_ref[jnp.maximum(i - 1, 0)]))
        def _():
            wgb_ref[...] = wg_ref[...].astype(BF16)
            wub_ref[...] = wu_ref[...].astype(BF16)
            wdb_ref[...] = wd_ref[...].astype(BF16)

        rows = lax.broadcasted_iota(jnp.int32, xs_ref.shape, 0)
        x = _unpack_bf16_pairs(jnp.where(rows < nr_ref[i], xs_ref[...], 0))
        gate = jnp.minimum(_dot(x, wgb_ref[...]) + bg_ref[...], SWIGLU_LIMIT)
        up = jnp.clip(_dot(x, wub_ref[...]) + bu_ref[...], -SWIGLU_LIMIT, SWIGLU_LIMIT)
        hid = (up + 1.0) * gate * _sigmoid(SWIGLU_ALPHA * gate)
        y = _dot(hid.astype(BF16), wdb_ref[...]) + bd_ref[...]
        o_ref[...] = _pack_bf16_pairs(y.astype(BF16).astype(F32))


def _experts(xs, block_e, block_rows, n_valid, params, l, bm):
    n_slots = xs.shape[0]
    nb = n_slots // bm
    blk = lambda i, be, nr, nv: (jnp.minimum(i, nv[0] - 1), 0)
    wsp = pl.BlockSpec((None, None, D_MODEL, D_FF), lambda i, be, nr, nv: (l, be[i], 0, 0))
    bsp = pl.BlockSpec((None, None, 1, D_FF), lambda i, be, nr, nv: (l, be[i], 0, 0))
    b4 = lambda a: a.reshape(DEPTH, N_EXPERTS, 1, -1)
    return pl.pallas_call(
        _ffn_kernel,
        grid_spec=pltpu.PrefetchScalarGridSpec(
            num_scalar_prefetch=3,
            grid=(nb,),
            in_specs=[pl.BlockSpec((bm, HALF), blk), wsp, bsp, wsp, bsp, wsp, bsp],
            out_specs=pl.BlockSpec((bm, HALF), blk),
            scratch_shapes=[pltpu.VMEM((D_MODEL, D_FF), BF16), pltpu.VMEM((D_MODEL, D_FF), BF16),
                            pltpu.VMEM((D_FF, D_MODEL), BF16)]),
        out_shape=jax.ShapeDtypeStruct((n_slots, HALF), jnp.int32),
        compiler_params=_params(("arbitrary",)),
        name="moe_experts",
    )(block_e, block_rows, n_valid, xs, params['w_gate'], b4(params['b_gate']), params['w_up'], b4(params['b_up']),
      params['w_down'], b4(params['b_down']))


def _combine_kernel(off_ref, cnt_ref, np_ref, x1_ref, ir_ref, pr_ref, ys_hbm, g2_ref, b2_ref, o_ref, ob_ref,
                    stage_ref, acc_ref, sem, *, tm):
    i = pl.program_id(0)

    @pl.when(i == 0)
    def _():
        stage_ref[...] = jnp.zeros_like(stage_ref)

    rows_c = EXPERTS_PER_CHUNK * REGION_CAP
    col_id = lax.broadcasted_iota(jnp.int32, (tm, rows_c), 1)
    ir = ir_ref[...]
    pr = pr_ref[...]
    acc_ref[...] = jnp.zeros_like(acc_ref)

    def one_pass(p, carry):
        mk = lambda dst, src: pltpu.make_async_copy(ys_hbm.at[src], stage_ref.at[dst], sem)
        _region_copies(off_ref, cnt_ref, i, p, mk, True)
        _region_copies(off_ref, cnt_ref, i, p, mk, False)
        tgt = [_slot_targets(ir[:, k:k + 1], ir[:, TOP_K + k:TOP_K + k + 1], p) for k in range(TOP_K)]
        for c in range(N_EXPERTS // EXPERTS_PER_CHUNK):
            wgt = jnp.where(col_id == (tgt[0] - c * rows_c), pr[:, 0:1], 0.0)
            for k in range(1, TOP_K):
                wgt = wgt + jnp.where(col_id == (tgt[k] - c * rows_c), pr[:, k:k + 1], 0.0)
            ys = _unpack_bf16_pairs(stage_ref[c * rows_c:(c + 1) * rows_c, :])
            acc_ref[...] += _dot(wgt.astype(BF16), ys)
        return carry

    lax.fori_loop(0, np_ref[i], one_pass, 0)
    out = _layer_norm(DN_ALPHA * x1_ref[...] + acc_ref[...], g2_ref[...], b2_ref[...])
    o_ref[...] = out
    ob_ref[...] = out.astype(BF16)


def _combine(x1, idx_rank, probs, ys, off, cnt8, n_pass, p, tm):
    t = x1.shape[0]
    row = lambda a: a.reshape(1, D_MODEL).astype(F32)
    full = lambda shp: pl.BlockSpec(shp, lambda i, *_: (0,) * len(shp))
    rowb = lambda w: pl.BlockSpec((tm, w), lambda i, *_: (i, 0))
    return pl.pallas_call(
        functools.partial(_combine_kernel, tm=tm),
        grid_spec=pltpu.PrefetchScalarGridSpec(
            num_scalar_prefetch=3,
            grid=(t // tm,),
            in_specs=[rowb(D_MODEL), rowb(LANES), rowb(LANES), pl.BlockSpec(memory_space=pl.ANY),
                      full((1, D_MODEL)), full((1, D_MODEL))],
            out_specs=[rowb(D_MODEL), rowb(D_MODEL)],
            scratch_shapes=[pltpu.VMEM((N_EXPERTS * REGION_CAP, HALF), jnp.int32), pltpu.VMEM((tm, D_MODEL), F32),
                            pltpu.SemaphoreType.DMA(())]),
        out_shape=[jax.ShapeDtypeStruct((t, D_MODEL), F32), jax.ShapeDtypeStruct((t, D_MODEL), BF16)],
        compiler_params=_params(("arbitrary",)),
        name="moe_combine_ln2",
    )(off, cnt8, n_pass, x1, idx_rank, probs, ys, row(p['ln2_g']), row(p['ln2_b']))


def _moe(x1, x1b, idx_rank, irt, probs, tile_counts, p, params, l, bm, tm):
    t = x1.shape[0]
    n_tiles = t // tm
    cnt = tile_counts[:, 0, :N_EXPERTS]
    cnt8 = (cnt + 7) // 8 * 8
    seg = jnp.sum(cnt8, axis=0)
    padded = (seg + bm - 1) // bm * bm
    pend = jnp.cumsum(padded)
    pstart = pend - padded
    off = pstart[None, :] + jnp.cumsum(cnt8, axis=0) - cnt8
    n_pass = jnp.maximum((jnp.max(cnt, axis=1) + REGION_CAP - 1) // REGION_CAP, 1).astype(jnp.int32)
    nb = -(-(t * TOP_K + 7 * n_tiles * N_EXPERTS) // bm) + N_EXPERTS
    n_valid = (pend[-1] // bm).astype(jnp.int32).reshape(1)
    blocks = jnp.arange(nb, dtype=jnp.int32)
    block_e = jnp.minimum(jnp.sum((pend // bm)[None, :] <= blocks[:, None], axis=1), N_EXPERTS - 1).astype(jnp.int32)
    block_rows = jnp.clip(seg[block_e] - (blocks * bm - pstart[block_e]), 0, bm).astype(jnp.int32)
    off = off.astype(jnp.int32).reshape(-1)
    cnt8 = cnt8.astype(jnp.int32).reshape(-1)
    xs = _dispatch(x1b, irt, off, cnt8, n_pass, nb * bm, tm)
    ys = _experts(xs, block_e, block_rows, n_valid, params, l, bm)
    return _combine(x1, idx_rank, probs, ys, off, cnt8, n_pass, p, tm)


_EXPERT_WEIGHTS = ('w_gate', 'b_gate', 'w_up', 'b_up', 'w_down', 'b_down')


def _layer(x, xb, params, l, b, s, cfg):
    p = {k: v[l] for k, v in params.items() if k not in _EXPERT_WEIGHTS}
    w_in = p['w_in']
    o = IN_OFFS
    pad = jnp.zeros((D_MODEL, LANES - 2 * ML_HEADS), w_in.dtype)
    w_ret = w_in[:, o[0]:o[4]].astype(BF16)
    w_ml = jnp.concatenate([w_in[:, o[4]:o[5]], w_in[:, o[7]:o[8]]], axis=1).astype(BF16)
    w_if = jnp.concatenate([w_in[:, o[5]:o[7]], pad], axis=1).astype(BF16)
    w_gates = w_in[:, o[11]:o[12]].astype(BF16)
    mm = functools.partial(_matmul, tm=cfg['mm_tm'], tn=cfg['mm_tn'])
    ret_in = mm(xb, w_ret, BF16)
    ml_in = mm(xb, w_ml, BF16)
    if_in = mm(xb, w_if, F32)
    gates = mm(xb, w_gates, BF16)
    y_ret = _retention(ret_in, p['ret_gn'], b, s, cfg['seq_rows']).reshape(b * s, MIX_W)
    y_ml = _mlstm(ml_in, if_in, p, b, s, cfg['seq_rows']).reshape(b * s, MIX_W)
    outs, lses = [], []
    wq = ATT_HEADS * ATT_DK
    wv = ATT_HEADS * ATT_DV
    for g in range(ATT_GROUPS):
        w_att = jnp.concatenate([w_in[:, o[8] + g * wq:o[8] + (g + 1) * wq],
                                 w_in[:, o[9] + g * wq:o[9] + (g + 1) * wq],
                                 w_in[:, o[10] + g * wv:o[10] + (g + 1) * wv]], axis=1).astype(BF16)
        og, lg = _dilated_group(_matmul_strided(xb, w_att, ATT_PATTERNS[g][1], cfg['mm_tm']), g, b, s, cfg['att_nq'])
        outs.append(og)
        lses.append(lg)
    tm = min(ROUTE_TM, b * s)
    x1, x1b, idx_rank, irt, probs, tile_counts = _merge(x, y_ret, y_ml, outs, lses, gates, p, tm)
    return _moe(x1, x1b, idx_rank, irt, probs, tile_counts, p, params, l, cfg['moe_bm'], tm)


CFG = dict(mm_tm=1024, mm_tn=1024, seq_rows=512, att_nq=4, moe_bm=512)

_PARAM_NAMES = ('w_in', 'ret_gn', 'ml_conv_w', 'ml_conv_b', 'ml_wq', 'ml_wk', 'ml_wv', 'ml_bi', 'ml_bf', 'ml_gn',
                'ml_skip', 'w_branch', 'w_out', 'ln1_g', 'ln1_b', 'w_router', 'b_router', 'w_gate', 'b_gate',
                'w_up', 'b_up', 'w_down', 'b_down', 'ln2_g', 'ln2_b')


def _forward(x, params, cfg):
    b, s, d = x.shape
    xf = x.reshape(b * s, d).astype(F32)
    xb = xf.astype(BF16)
    for l in range(DEPTH):
        xf, xb = _layer(xf, xb, params, l, b, s, cfg)
    return xf.reshape(b, s, d).astype(x.dtype)


def kernel(x, w_in, ret_gn, ml_conv_w, ml_conv_b, ml_wq, ml_wk, ml_wv, ml_bi, ml_bf, ml_gn, ml_skip, w_branch, w_out, ln1_g, ln1_b, w_router, b_router, w_gate, b_gate, w_up, b_up, w_down, b_down, ln2_g, ln2_b):
    params = dict(zip(_PARAM_NAMES, (w_in, ret_gn, ml_conv_w, ml_conv_b, ml_wq, ml_wk, ml_wv, ml_bi, ml_bf, ml_gn,
                                     ml_skip, w_branch, w_out, ln1_g, ln1_b, w_router, b_router, w_gate, b_gate,
                                     w_up, b_up, w_down, b_down, ln2_g, ln2_b)))
    return _forward(x, params, CFG)
```

```python
import functools

import jax
import jax.numpy as jnp
import numpy as np
from jax import lax
from jax.experimental import pallas as pl
from jax.experimental.pallas import tpu as pltpu

F32 = jnp.float32
BF16 = jnp.bfloat16

D_MODEL = 1024
DEPTH = 2
MIX_W = D_MODEL // 2
N_BRANCHES = 3
RET_HEADS = 4
RET_DV = MIX_W // RET_HEADS
RET_DK = RET_DV // 2
ML_HEADS = 4
ML_DH = MIX_W // ML_HEADS
ML_CONV = 4
ML_QK_BLOCK = 4
ATT_PATTERNS = ((128, 1), (512, 4), (2048, 16))
ATT_GROUPS = len(ATT_PATTERNS)
ATT_HEADS = 4
ATT_DV = MIX_W // ATT_HEADS
ATT_DK = ATT_DV // 2
N_EXPERTS = 32
TOP_K = 4
D_FF = D_MODEL
SWIGLU_LIMIT = 7.0
SWIGLU_ALPHA = 1.702
DN_ALPHA = (2.0 * DEPTH) ** 0.25
EPS = 1e-5

CHUNK = 128
LANES = 128
NEG = -1e30
VMEM_LIMIT = 56 * 1024 * 1024

IN_SIZES = (RET_HEADS * RET_DK, RET_HEADS * RET_DK, MIX_W, MIX_W,
            MIX_W, ML_HEADS, ML_HEADS, MIX_W,
            ATT_GROUPS * ATT_HEADS * ATT_DK, ATT_GROUPS * ATT_HEADS * ATT_DK, ATT_GROUPS * ATT_HEADS * ATT_DV,
            N_BRANCHES * D_MODEL)
IN_OFFS = tuple(int(v) for v in np.cumsum((0,) + IN_SIZES))


def _params(sem):
    return pltpu.CompilerParams(dimension_semantics=sem, vmem_limit_bytes=VMEM_LIMIT)


def _dot(a, b):
    return jnp.dot(a, b, preferred_element_type=F32)


def _dot_nt(a, b):
    return lax.dot_general(a, b, (((1,), (1,)), ((), ())), preferred_element_type=F32)


def _dot_tn(a, b):
    return lax.dot_general(a, b, (((0,), (0,)), ((), ())), preferred_element_type=F32)


def _sigmoid(x):
    return 1.0 / (1.0 + jnp.exp(-x))


def _mm_kernel(x_ref, w_ref, o_ref):
    o_ref[...] = _dot(x_ref[...], w_ref[...]).astype(o_ref.dtype)


def _matmul(x, w, out_dtype, tm, tn):
    t, k = x.shape
    n = w.shape[1]
    tm = min(tm, t)
    tn = max(c for c in range(LANES, min(tn, n) + 1, LANES) if n % c == 0)
    assert t % tm == 0
    return pl.pallas_call(
        _mm_kernel,
        grid=(n // tn, t // tm),
        in_specs=[pl.BlockSpec((tm, k), lambda j, i: (i, 0)),
                  pl.BlockSpec((k, tn), lambda j, i: (0, j))],
        out_specs=pl.BlockSpec((tm, tn), lambda j, i: (i, j)),
        out_shape=jax.ShapeDtypeStruct((t, n), out_dtype),
        compiler_params=_params(("arbitrary", "arbitrary")),
        name="in_proj",
    )(x, w)


def _mm_strided_kernel(x_ref, w_ref, o_ref, y_ref, *, d):
    tm = x_ref.shape[0]
    n = tm // d
    width = w_ref.shape[1]
    y = _dot(x_ref[...], w_ref[...])
    if d == 1:
        o_ref[...] = y.astype(o_ref.dtype)
        return
    for c in range(width // LANES):
        y_ref[c] = y[:, c * LANES:(c + 1) * LANES]
    for r in range(d):
        for c in range(width // LANES):
            o_ref[:, r * width + c * LANES:r * width + (c + 1) * LANES] = (
                y_ref[c, pl.ds(r, n, stride=d), :].astype(o_ref.dtype))


def _matmul_strided(x, w, d, tm):
    t, k = x.shape
    n = w.shape[1]
    tm = min(tm, t)
    assert t % tm == 0 and tm % (16 * d) == 0
    return pl.pallas_call(
        functools.partial(_mm_strided_kernel, d=d),
        grid=(t // tm,),
        in_specs=[pl.BlockSpec((tm, k), lambda i: (i, 0)),
                  pl.BlockSpec((k, n), lambda i: (0, 0))],
        out_specs=pl.BlockSpec((tm // d, d * n), lambda i: (i, 0)),
        out_shape=jax.ShapeDtypeStruct((t // d, d * n), BF16),
        scratch_shapes=[pltpu.VMEM((n // LANES, tm, LANES), F32)],
        compiler_params=_params(("arbitrary",)),
        name=f"att_in_proj_d{d}",
    )(x, w)


def _head_norm(o, gn):
    mu = jnp.mean(o, axis=-1, keepdims=True)
    oc = o - mu
    var = jnp.mean(oc * oc, axis=-1, keepdims=True)
    return oc * lax.rsqrt(var + EPS) * gn


def _ret_kernel(blk_ref, dm_ref, qd_ref, kd_ref, cd_ref, gn_ref, o_ref, st_ref, *, nc):
    @pl.when(pl.program_id(1) == 0)
    def _():
        st_ref[...] = jnp.zeros_like(st_ref)

    hq = RET_HEADS * RET_DK
    for c in range(nc):
        rows = pl.ds(c * CHUNK, CHUNK)
        for h in range(RET_HEADS):
            q = blk_ref[rows, h * RET_DK:(h + 1) * RET_DK]
            k = blk_ref[rows, hq + h * RET_DK:hq + (h + 1) * RET_DK]
            v = blk_ref[rows, 2 * hq + h * RET_DV:2 * hq + (h + 1) * RET_DV]
            g = blk_ref[rows, 2 * hq + MIX_W + h * RET_DV:2 * hq + MIX_W + (h + 1) * RET_DV].astype(F32)
            st = st_ref[h]
            a = (_dot_nt(q, k) * dm_ref[h]).astype(BF16)
            o = _dot(a, v) + _dot(q, st.astype(BF16)) * qd_ref[h]
            kv = _dot_tn(k, (v.astype(F32) * kd_ref[h]).astype(BF16))
            st_ref[h] = cd_ref[h] * st + kv
            y = _head_norm(o, gn_ref[:, h * RET_DV:(h + 1) * RET_DV]) * (g * _sigmoid(g))
            o_ref[rows, h * RET_DV:(h + 1) * RET_DV] = y.astype(o_ref.dtype)


def _retention(ret_in, ret_gn, b, s, rows):
    nc = rows // CHUNK
    scale = RET_DK ** -0.5
    log_gamma = jnp.log1p(-jnp.exp2(-5.0 - jnp.arange(RET_HEADS, dtype=F32)))
    pos = jnp.arange(CHUNK, dtype=F32)
    diff = pos[:, None] - pos[None, :]
    dm = jnp.where(diff >= 0, jnp.exp(log_gamma[:, None, None] * jnp.maximum(diff, 0.0)), 0.0) * scale
    kd = jnp.broadcast_to(jnp.exp(log_gamma[:, None] * (CHUNK - 1.0 - pos))[:, :, None], (RET_HEADS, CHUNK, RET_DV))
    qd = jnp.broadcast_to((jnp.exp(log_gamma[:, None] * (pos + 1.0)) * scale)[:, :, None], (RET_HEADS, CHUNK, RET_DV))
    cd = jnp.broadcast_to(jnp.exp(log_gamma * CHUNK)[:, None, None], (RET_HEADS, 1, RET_DV))
    w_in = ret_in.shape[-1]
    full = lambda shp: pl.BlockSpec(shp, lambda bi, si: (0,) * len(shp))
    return pl.pallas_call(
        functools.partial(_ret_kernel, nc=nc),
        grid=(b, s // rows),
        in_specs=[pl.BlockSpec((None, rows, w_in), lambda bi, si: (bi, si, 0)),
                  full((RET_HEADS, CHUNK, CHUNK)), full((RET_HEADS, CHUNK, RET_DV)),
                  full((RET_HEADS, CHUNK, RET_DV)), full((RET_HEADS, 1, RET_DV)), full((1, MIX_W))],
        out_specs=pl.BlockSpec((None, rows, MIX_W), lambda bi, si: (bi, si, 0)),
        out_shape=jax.ShapeDtypeStruct((b, s, MIX_W), BF16),
        scratch_shapes=[pltpu.VMEM((RET_HEADS, RET_DK, RET_DV), F32)],
        compiler_params=_params(("arbitrary", "arbitrary")),
        name="retention",
    )(ret_in.reshape(b, s, w_in), dm, qd, kd, cd, ret_gn.reshape(1, MIX_W).astype(F32))


def _log_sigmoid(x):
    return jnp.minimum(x, 0.0) - jnp.log(1.0 + jnp.exp(-jnp.abs(x)))


def _ml_kernel(ml_ref, if_ref, cw_ref, cb_ref, bq_ref, bk_ref, bv_ref, gb_ref, gn_ref, sk_ref,
               o_ref, c_ref, n_ref, m_ref, halo_ref, *, nc):
    rows_total = nc * CHUNK

    @pl.when(pl.program_id(1) == 0)
    def _():
        c_ref[...] = jnp.zeros_like(c_ref)
        n_ref[...] = jnp.zeros_like(n_ref)
        m_ref[...] = jnp.zeros_like(m_ref)
        halo_ref[...] = jnp.zeros_like(halo_ref)

    x_all = ml_ref[:, 0:MIX_W].astype(F32)
    xf = jnp.concatenate([halo_ref[...], x_all], axis=0)
    acc = jnp.broadcast_to(cb_ref[...], (rows_total, MIX_W))
    for j in range(ML_CONV):
        sh = ML_CONV - 1 - j
        xs = xf if sh == 0 else pltpu.roll(xf, sh, 0)
        acc = acc + xs[8:8 + rows_total] * cw_ref[j:j + 1, :]
    halo_ref[...] = x_all[rows_total - 8:rows_total]
    mc_all = acc * _sigmoid(acc)

    ri = lax.broadcasted_iota(jnp.int32, (CHUNK, CHUNK), 0)
    ci = lax.broadcasted_iota(jnp.int32, (CHUNK, CHUNK), 1)
    tri = ri >= ci
    tril = tri.astype(F32)
    kscale = ML_DH ** -0.5

    for c in range(nc):
        r0 = c * CHUNK
        g_pre = if_ref[r0:r0 + CHUNK, :] + gb_ref[...]
        lf = _log_sigmoid(g_pre)
        cum = jnp.dot(tril, lf, preferred_element_type=F32, precision=lax.Precision.HIGHEST)
        g_t = g_pre.T
        cum_t = cum.T
        for h in range(ML_HEADS):
            cols = slice(h * ML_DH, (h + 1) * ML_DH)
            mc = mc_all[r0:r0 + CHUNK, cols]
            xv = x_all[r0:r0 + CHUNK, cols]
            q = _dot(mc.astype(BF16), bq_ref[h]).astype(BF16)
            k = (_dot(mc.astype(BF16), bk_ref[h]) * kscale).astype(BF16)
            v = _dot(xv.astype(BF16), bv_ref[h]).astype(BF16)
            cum_col = cum[:, ML_HEADS + h:ML_HEADS + h + 1]
            ig_col = g_pre[:, h:h + 1]
            cum_row = cum_t[ML_HEADS + h:ML_HEADS + h + 1, :]
            ig_row = g_t[h:h + 1, :]
            tot = cum[CHUNK - 1:CHUNK, ML_HEADS + h:ML_HEADS + h + 1]
            m_prev = m_ref[h:h + 1, 0:1]
            c_prev = c_ref[h]
            n_prev = n_ref[h:h + 1, :]
            dlog = jnp.where(tri, cum_col + (ig_row - cum_row), NEG)
            m_inter = cum_col + m_prev
            m_q = jnp.maximum(m_inter, jnp.max(dlog, axis=-1, keepdims=True))
            w_qk = jnp.exp(dlog - m_q) * _dot_nt(q, k)
            inter = jnp.exp(m_inter - m_q)
            num = _dot(w_qk.astype(BF16), v) + inter * _dot(q, c_prev.astype(BF16))
            den = jnp.sum(w_qk, axis=-1, keepdims=True) + inter * jnp.sum(q.astype(F32) * n_prev, axis=-1, keepdims=True)
            hh = num * (1.0 / jnp.maximum(jnp.abs(den), jnp.exp(-m_q)))
            a_col = tot - cum_col + ig_col
            a_max = jnp.max(a_col, axis=0, keepdims=True)
            wa = jnp.exp(a_col - a_max)
            kf = k.astype(F32)
            chunk_c = _dot_tn(k, (wa * v.astype(F32)).astype(BF16))
            chunk_n = jnp.sum(wa * kf, axis=0, keepdims=True)
            m_new = jnp.maximum(tot + m_prev, a_max)
            s_old = jnp.exp(tot + m_prev - m_new)
            s_new = jnp.exp(a_max - m_new)
            c_ref[h] = s_old * c_prev + s_new * chunk_c
            n_ref[h:h + 1, :] = s_old * n_prev + s_new * chunk_n
            m_ref[h:h + 1, :] = jnp.broadcast_to(m_new, (1, LANES))
            mo = ml_ref[r0:r0 + CHUNK, MIX_W + h * ML_DH:MIX_W + (h + 1) * ML_DH].astype(F32)
            y = _sigmoid(mo) * (_head_norm(hh, gn_ref[:, cols]) + sk_ref[:, cols] * mc)
            o_ref[r0:r0 + CHUNK, cols] = y.astype(o_ref.dtype)


def _block_diag(w):
    nb = w.shape[0]
    per = nb // ML_HEADS
    wh = w.reshape(ML_HEADS, per, ML_QK_BLOCK, ML_QK_BLOCK)
    eye = jnp.eye(per, dtype=w.dtype)
    bd = jnp.einsum('hncd,nm->hncmd', wh, eye)
    return bd.reshape(ML_HEADS, per * ML_QK_BLOCK, per * ML_QK_BLOCK).astype(BF16)


def _mlstm(ml_in, if_in, p, b, s, rows):
    nc = rows // CHUNK
    full = lambda shp: pl.BlockSpec(shp, lambda bi, si: (0,) * len(shp))
    gb = jnp.zeros((1, LANES), F32).at[0, 0:ML_HEADS].set(p['ml_bi'].astype(F32)).at[0, ML_HEADS:2 * ML_HEADS].set(p['ml_bf'].astype(F32))
    row = lambda a: a.reshape(1, MIX_W).astype(F32)
    return pl.pallas_call(
        functools.partial(_ml_kernel, nc=nc),
        grid=(b, s // rows),
        in_specs=[pl.BlockSpec((None, rows, 2 * MIX_W), lambda bi, si: (bi, si, 0)),
                  pl.BlockSpec((None, rows, LANES), lambda bi, si: (bi, si, 0)),
                  full((ML_CONV, MIX_W)), full((1, MIX_W)),
                  full((ML_HEADS, ML_DH, ML_DH)), full((ML_HEADS, ML_DH, ML_DH)), full((ML_HEADS, ML_DH, ML_DH)),
                  full((1, LANES)), full((1, MIX_W)), full((1, MIX_W))],
        out_specs=pl.BlockSpec((None, rows, MIX_W), lambda bi, si: (bi, si, 0)),
        out_shape=jax.ShapeDtypeStruct((b, s, MIX_W), BF16),
        scratch_shapes=[pltpu.VMEM((ML_HEADS, ML_DH, ML_DH), F32), pltpu.VMEM((8, ML_DH), F32),
                        pltpu.VMEM((8, LANES), F32), pltpu.VMEM((8, MIX_W), F32)],
        compiler_params=_params(("arbitrary", "arbitrary")),
        name="mlstm",
    )(ml_in.reshape(b, s, 2 * MIX_W), if_in.reshape(b, s, LANES),
      p['ml_conv_w'].astype(F32), row(p['ml_conv_b']),
      _block_diag(p['ml_wq']), _block_diag(p['ml_wk']), _block_diag(p['ml_wv']),
      gb, row(p['ml_gn']), row(p['ml_skip']))


def _att_kernel(q_ref, kc_ref, kp_ref, vc_ref, vp_ref, bias_ref, o_ref, lse_ref, *, nq):
    first = pl.program_id(2) == 0
    lane = lax.broadcasted_iota(jnp.int32, (CHUNK, LANES), 1)
    ones = jnp.ones((2 * CHUNK, ATT_DV), BF16)
    for j in range(nq):
        rows = slice(j * CHUNK, (j + 1) * CHUNK)
        lse_out = jnp.zeros((CHUNK, LANES), F32)
        for h in range(ATT_HEADS):
            dk = slice(h * ATT_DK, (h + 1) * ATT_DK)
            dv = slice(h * ATT_DV, (h + 1) * ATT_DV)
            q = q_ref[rows, dk]
            if j == 0:
                kk = jnp.concatenate([kp_ref[:, dk], kc_ref[rows, dk]], axis=0)
                vv = jnp.concatenate([vp_ref[:, dv], vc_ref[rows, dv]], axis=0)
                bias = bias_ref[jnp.where(first, 0, 1), h]
            else:
                kk = kc_ref[(j - 1) * CHUNK:(j + 1) * CHUNK, dk]
                vv = vc_ref[(j - 1) * CHUNK:(j + 1) * CHUNK, dv]
                bias = bias_ref[1, h]
            s = _dot_nt(q, kk) * (ATT_DK ** -0.5) + bias
            m = jnp.max(s, axis=-1, keepdims=True)
            p = jnp.exp(s - m).astype(BF16)
            oa = _dot(p, jnp.concatenate([vv, ones], axis=1))
            l = oa[:, ATT_DV:2 * ATT_DV]
            o_ref[rows, dv] = (oa[:, 0:ATT_DV] * (1.0 / l)).astype(o_ref.dtype)
            lse_out = jnp.where(lane == h, m + jnp.log(l), lse_out)
        lse_ref[rows, :] = lse_out


def _att_bias(window, dilation, slopes):
    wb = window // dilation
    qi = jnp.arange(wb)[:, None]
    kj = jnp.arange(2 * wb)[None, :]
    delta = qi + wb - kj
    band = (delta >= 0) & (delta <= wb)
    bias = -slopes[:, None, None] * (dilation * delta).astype(F32)
    later = jnp.where(band[None], bias, NEG)
    first = jnp.where((band & (kj >= wb))[None], bias, NEG)
    return jnp.stack([first, later]).astype(F32)


def _dilated_group(qkv, g, b, s, nq):
    window, d = ATT_PATTERNS[g]
    assert window // d == CHUNK
    l_sub = s // d
    assert l_sub % CHUNK == 0
    n_blk = l_sub // CHUNK
    slopes = jnp.exp2(-8.0 * jnp.arange(1, ATT_GROUPS * ATT_HEADS + 1, dtype=F32) / (ATT_GROUPS * ATT_HEADS))
    bias = _att_bias(window, d, slopes.reshape(ATT_GROUPS, ATT_HEADS)[g])
    wq = ATT_HEADS * ATT_DK
    wv = ATT_HEADS * ATT_DV
    per_tok = 2 * wq + wv
    x = qkv.reshape(b, l_sub, d * per_tok)
    qb = per_tok // wq
    vb = per_tok // wv
    nq = min(nq, n_blk)
    assert n_blk % nq == 0
    span = nq * CHUNK
    prev = lambda n: jnp.maximum(n * nq - 1, 0)
    o, lse = pl.pallas_call(
        functools.partial(_att_kernel, nq=nq),
        grid=(b, d, n_blk // nq),
        in_specs=[pl.BlockSpec((None, span, wq), lambda bi, r, n: (bi, n, r * qb)),
                  pl.BlockSpec((None, span, wq), lambda bi, r, n: (bi, n, r * qb + 1)),
                  pl.BlockSpec((None, CHUNK, wq), lambda bi, r, n: (bi, prev(n), r * qb + 1)),
                  pl.BlockSpec((None, span, wv), lambda bi, r, n: (bi, n, r * vb + 1)),
                  pl.BlockSpec((None, CHUNK, wv), lambda bi, r, n: (bi, prev(n), r * vb + 1)),
                  pl.BlockSpec((2, ATT_HEADS, CHUNK, 2 * CHUNK), lambda bi, r, n: (0, 0, 0, 0))],
        out_specs=[pl.BlockSpec((None, span, wv), lambda bi, r, n: (bi, n, r)),
                   pl.BlockSpec((None, span, LANES), lambda bi, r, n: (bi, n, r))],
        out_shape=[jax.ShapeDtypeStruct((b, l_sub, d * wv), BF16),
                   jax.ShapeDtypeStruct((b, l_sub, d * LANES), F32)],
        compiler_params=_params(("arbitrary", "arbitrary", "arbitrary")),
        name=f"dilated_attn_g{g}",
    )(x, x, x, x, x, bias)
    return o.reshape(b * l_sub, d * wv), lse.reshape(b * l_sub, d * LANES)


def _layer_norm(z, g, b):
    mu = jnp.mean(z, axis=-1, keepdims=True)
    zc = z - mu
    var = jnp.mean(zc * zc, axis=-1, keepdims=True)
    return zc * lax.rsqrt(var + EPS) * g + b


def _merge_kernel(x_ref, yr_ref, ym_ref, o0_ref, o1_ref, o2_ref, l0_ref, l1_ref, l2_ref, gt_ref,
                  wb_ref, wo_ref, g1_ref, b1_ref, wrh_ref, wrl_ref, br_ref,
                  x1_ref, x1b_ref, ir_ref, irt_ref, pr_ref, cnt_ref, og_ref, lg_ref, *, tm):
    for g, (src_o, src_l) in enumerate(((o0_ref, l0_ref), (o1_ref, l1_ref), (o2_ref, l2_ref))):
        d = ATT_PATTERNS[g][1]
        n = tm // d
        for r in range(d):
            rows = pl.ds(r, n, stride=d) if d > 1 else pl.ds(0, n)
            for h in range(ATT_HEADS):
                og_ref[g, h, rows, :] = src_o[:, r * MIX_W + h * ATT_DV:r * MIX_W + (h + 1) * ATT_DV].astype(F32)
            lg_ref[g, rows, :] = src_l[:, r * LANES:(r + 1) * LANES]

    l0, l1, l2 = lg_ref[0], lg_ref[1], lg_ref[2]
    lm = jnp.maximum(jnp.maximum(l0, l1), l2)
    e0, e1, e2 = jnp.exp(l0 - lm), jnp.exp(l1 - lm), jnp.exp(l2 - lm)
    inv = 1.0 / (e0 + e1 + e2)
    parts = []
    for h in range(ATT_HEADS):
        dv = slice(h * ATT_DV, (h + 1) * ATT_DV)
        hs = slice(h, h + 1)
        parts.append((e0[:, hs] * og_ref[0, h] + e1[:, hs] * og_ref[1, h] + e2[:, hs] * og_ref[2, h]) * inv[:, hs])
    y_att = jnp.concatenate(parts, axis=-1).astype(BF16)

    merged = _sigmoid(gt_ref[:, 0:D_MODEL].astype(F32)) * _dot(yr_ref[...], wb_ref[0])
    merged = merged + _sigmoid(gt_ref[:, D_MODEL:2 * D_MODEL].astype(F32)) * _dot(ym_ref[...], wb_ref[1])
    merged = merged + _sigmoid(gt_ref[:, 2 * D_MODEL:3 * D_MODEL].astype(F32)) * _dot(y_att, wb_ref[2])
    z = DN_ALPHA * x_ref[...] + _dot(merged.astype(BF16), wo_ref[...])
    x1 = _layer_norm(z, g1_ref[...], b1_ref[...])
    x1_ref[...] = x1
    x1b_ref[...] = x1.astype(BF16)

    xh = x1.astype(BF16)
    xl = (x1 - xh.astype(F32)).astype(BF16)
    logits = _dot(xh, wrh_ref[...]) + _dot(xl, wrh_ref[...]) + _dot(xh, wrl_ref[...]) + br_ref[...]
    lane = lax.broadcasted_iota(jnp.int32, (tm, LANES), 1)
    work = jnp.where(lane < N_EXPERTS, logits, -jnp.inf)
    vals, idxs = [], []
    for _ in range(TOP_K):
        m = jnp.max(work, axis=-1, keepdims=True)
        idx = jnp.min(jnp.where(work == m, lane, LANES), axis=-1, keepdims=True)
        vals.append(m)
        idxs.append(idx)
        work = jnp.where(lane == idx, -jnp.inf, work)
    es = [jnp.exp(v - vals[0]) for v in vals]
    tot = es[0] + es[1] + es[2] + es[3]
    pr = jnp.zeros((tm, LANES), F32)
    for k in range(TOP_K):
        pr = jnp.where(lane == k, es[k] / tot, pr)
    pr_ref[...] = pr

    onehot = jnp.zeros((tm, LANES), F32)
    for k in range(TOP_K):
        onehot = onehot + (lane == idxs[k]).astype(F32)
    ri = lax.broadcasted_iota(jnp.int32, (tm, tm), 0)
    ci = lax.broadcasted_iota(jnp.int32, (tm, tm), 1)
    before = _dot((ri > ci).astype(BF16), onehot.astype(BF16))
    ir = jnp.zeros((tm, LANES), F32)
    for k in range(TOP_K):
        rank = jnp.sum(jnp.where(lane == idxs[k], before, 0.0), axis=-1, keepdims=True)
        ir = jnp.where(lane == k, idxs[k].astype(F32), ir)
        ir = jnp.where(lane == TOP_K + k, rank, ir)
    ir_ref[...] = ir.astype(jnp.int32)
    irt_ref[...] = ir.T[0:8, :]
    cnt_ref[...] = jnp.broadcast_to(jnp.sum(onehot, axis=0, keepdims=True), (8, LANES)).astype(jnp.int32)


def _merge(x, y_ret, y_ml, outs, lses, gates, p, tm):
    t = x.shape[0]
    tm = min(tm, t)
    rowb = lambda w: pl.BlockSpec((tm, w), lambda i: (i, 0))
    full = lambda shp: pl.BlockSpec(shp, lambda i: (0,) * len(shp))
    wr = jnp.zeros((D_MODEL, LANES), F32).at[:, :N_EXPERTS].set(p['w_router'].astype(F32))
    wrh = wr.astype(BF16)
    wrl = (wr - wrh.astype(F32)).astype(BF16)
    br = jnp.zeros((1, LANES), F32).at[0, :N_EXPERTS].set(p['b_router'].astype(F32))
    row = lambda a: a.reshape(1, D_MODEL).astype(F32)
    return pl.pallas_call(
        functools.partial(_merge_kernel, tm=tm),
        grid=(t // tm,),
        in_specs=[rowb(D_MODEL), rowb(MIX_W), rowb(MIX_W)]
                 + [pl.BlockSpec((tm // d, d * MIX_W), lambda i: (i, 0)) for _, d in ATT_PATTERNS]
                 + [pl.BlockSpec((tm // d, d * LANES), lambda i: (i, 0)) for _, d in ATT_PATTERNS]
                 + [rowb(N_BRANCHES * D_MODEL),
                  full((N_BRANCHES, MIX_W, D_MODEL)), full((D_MODEL, D_MODEL)),
                  full((1, D_MODEL)), full((1, D_MODEL)),
                  full((D_MODEL, LANES)), full((D_MODEL, LANES)), full((1, LANES))],
        out_specs=[rowb(D_MODEL), rowb(D_MODEL), rowb(LANES), pl.BlockSpec((8, tm), lambda i: (0, i)), rowb(LANES),
                   pl.BlockSpec((None, 8, LANES), lambda i: (i, 0, 0))],
        out_shape=[jax.ShapeDtypeStruct((t, D_MODEL), F32),
                   jax.ShapeDtypeStruct((t, D_MODEL), BF16),
                   jax.ShapeDtypeStruct((t, LANES), jnp.int32),
                   jax.ShapeDtypeStruct((8, t), F32),
                   jax.ShapeDtypeStruct((t, LANES), F32),
                   jax.ShapeDtypeStruct((t // tm, 8, LANES), jnp.int32)],
        scratch_shapes=[pltpu.VMEM((ATT_GROUPS, ATT_HEADS, tm, ATT_DV), F32),
                        pltpu.VMEM((ATT_GROUPS, tm, LANES), F32)],
        compiler_params=_params(("arbitrary",)),
        name="merge_ln1_router",
    )(x, y_ret, y_ml, outs[0], outs[1], outs[2], lses[0], lses[1], lses[2], gates,
      p['w_branch'].astype(BF16), p['w_out'].astype(BF16), row(p['ln1_g']), row(p['ln1_b']), wrh, wrl, br)


ROUTE_TM = 512
REGION_CAP = 128
REGION_PIECES = (128, 64, 32, 16, 8)
PAD_PIECES = (256, 128, 64, 32, 16, 8)
ONEHOT_CHUNK = 256
HALF = D_MODEL // 2
HI_MASK = -65536


def _pack_bf16_pairs(y):
    lo = lax.shift_right_logical(lax.bitcast_convert_type(y[:, :HALF], jnp.int32), 16)
    hi = lax.bitcast_convert_type(y[:, HALF:], jnp.int32) & HI_MASK
    return lo | hi


def _unpack_bf16_pairs(w):
    lo = lax.bitcast_convert_type(lax.shift_left(w, 16), F32)
    hi = lax.bitcast_convert_type(w & HI_MASK, F32)
    return jnp.concatenate([lo, hi], axis=1).astype(BF16)


def _row_pieces(n, pieces, fn):
    for size in pieces:
        if size == pieces[0]:
            cond, pos = n >= size, 0
        else:
            cond, pos = (n & size) != 0, pl.multiple_of(n & (-2 * size), 8)

        @pl.when(cond)
        def _(pos=pos, size=size):
            fn(pos, size)


def _region_copies(off_ref, cnt_ref, tile, p, make_copy, start):
    for e in range(N_EXPERTS):
        n = jnp.clip(cnt_ref[tile * N_EXPERTS + e] - p * REGION_CAP, 0, REGION_CAP)
        base = off_ref[tile * N_EXPERTS + e] + p * REGION_CAP

        def piece(pos, size, e=e, base=base):
            cp = make_copy(pl.ds(e * REGION_CAP + pos, size), pl.ds(pl.multiple_of(base + pos, 8), size))
            if start:
                cp.start()
            else:
                cp.wait()

        _row_pieces(n, REGION_PIECES, piece)


def _slot_targets(idx, lrank, p):
    lr = lrank - p * REGION_CAP
    return jnp.where((lr >= 0) & (lr < REGION_CAP), idx * REGION_CAP + lr, -1)


def _small_int_bf16(v):
    return v.astype(F32).astype(BF16)


def _dispatch_kernel(off_ref, cnt_ref, np_ref, pad_ref, nv_ref, x_ref, irt_ref, xs_hbm,
                     stage_ref, zero_ref, sem, zsem, *, tm, n_tiles, bm, nb):
    i = pl.program_id(0)
    slot = i % 2
    row_id = _small_int_bf16(lax.broadcasted_iota(jnp.int32, (ONEHOT_CHUNK, tm), 0))
    meta = irt_ref[...].astype(jnp.int32)
    x = x_ref[...]
    mk = lambda s: (lambda src, dst: pltpu.make_async_copy(stage_ref.at[s, src], xs_hbm.at[dst], sem.at[s]))

    def one_pass(p, carry):
        tgt = [_slot_targets(meta[k:k + 1, :], meta[TOP_K + k:TOP_K + k + 1, :], p) for k in range(TOP_K)]
        for c in range(N_EXPERTS * REGION_CAP // ONEHOT_CHUNK):
            hit = row_id == _small_int_bf16(tgt[0] - c * ONEHOT_CHUNK)
            for k in range(1, TOP_K):
                hit = hit | (row_id == _small_int_bf16(tgt[k] - c * ONEHOT_CHUNK))
            sel = jnp.where(hit, jnp.ones((), BF16), jnp.zeros((), BF16))
            stage_ref[slot, c * ONEHOT_CHUNK:(c + 1) * ONEHOT_CHUNK, :] = _pack_bf16_pairs(_dot(sel, x))
        _region_copies(off_ref, cnt_ref, i, p, mk(slot), True)

        @pl.when(p < np_ref[i] - 1)
        def _():
            _region_copies(off_ref, cnt_ref, i, p, mk(slot), False)
        return carry

    lax.fori_loop(0, np_ref[i], one_pass, 0)

    @pl.when(i > 0)
    def _():
        _region_copies(off_ref, cnt_ref, i - 1, np_ref[jnp.maximum(i - 1, 0)] - 1, mk(1 - slot), False)

    @pl.when(i == n_tiles - 1)
    def _():
        _region_copies(off_ref, cnt_ref, i, np_ref[i] - 1, mk(slot), False)
        zero_ref[...] = jnp.zeros_like(zero_ref)
        for phase in (True, False):
            for e in range(N_EXPERTS):
                def piece(pos, size, e=e):
                    cp = pltpu.make_async_copy(zero_ref.at[pl.ds(0, size)],
                                               xs_hbm.at[pl.ds(pl.multiple_of(pad_ref[e] + pos, 8), size)], zsem)
                    cp.start() if phase else cp.wait()
                _row_pieces(pad_ref[N_EXPERTS + e], PAD_PIECES, piece)

        def zero_block(blk, carry):
            for half in range(bm // PAD_PIECES[0]):
                cp = pltpu.make_async_copy(
                    zero_ref, xs_hbm.at[pl.ds(pl.multiple_of(blk * bm + half * PAD_PIECES[0], 8), PAD_PIECES[0])], zsem)
                cp.start()
                cp.wait()
            return carry

        lax.fori_loop(nv_ref[0], nb, zero_block, 0)


def _dispatch(x1b, irt, off, cnt8, n_pass, pad, n_valid, nb, bm, tm):
    t = x1b.shape[0]
    assert bm % PAD_PIECES[0] == 0 and bm <= 2 * PAD_PIECES[0]
    return pl.pallas_call(
        functools.partial(_dispatch_kernel, tm=tm, n_tiles=t // tm, bm=bm, nb=nb),
        grid_spec=pltpu.PrefetchScalarGridSpec(
            num_scalar_prefetch=5,
            grid=(t // tm,),
            in_specs=[pl.BlockSpec((tm, D_MODEL), lambda i, *_: (i, 0)),
                      pl.BlockSpec((8, tm), lambda i, *_: (0, i))],
            out_specs=pl.BlockSpec(memory_space=pl.ANY),
            scratch_shapes=[pltpu.VMEM((2, N_EXPERTS * REGION_CAP, HALF), jnp.int32),
                            pltpu.VMEM((PAD_PIECES[0], HALF), jnp.int32),
                            pltpu.SemaphoreType.DMA((2,)), pltpu.SemaphoreType.DMA(())]),
        out_shape=jax.ShapeDtypeStruct((nb * bm, HALF), jnp.int32),
        compiler_params=_params(("arbitrary",)),
        name="moe_dispatch",
    )(off, cnt8, n_pass, pad, n_valid, x1b, irt)


def _ffn_kernel(be_ref, nr_ref, nv_ref, xs_ref, wg_ref, bg_ref, wu_ref, bu_ref, wd_ref, bd_ref, o_ref,
                wgb_ref, wub_ref, wdb_ref):
    i = pl.program_id(0)

    @pl.when(i < nv_ref[0])
    def _():
        @pl.when((i == 0) | (be_ref[i] != be_ref[jnp.maximum(i - 1, 0)]))
        def _():
            wgb_ref[...] = wg_ref[...].astype(BF16)
            wub_ref[...] = wu_ref[...].astype(BF16)
            wdb_ref[...] = wd_ref[...].astype(BF16)

        rows = lax.broadcasted_iota(jnp.int32, xs_ref.shape, 0)
        x = _unpack_bf16_pairs(jnp.where(rows < nr_ref[i], xs_ref[...], 0))
        gate = jnp.minimum(_dot(x, wgb_ref[...]) + bg_ref[...], SWIGLU_LIMIT)
        up = jnp.clip(_dot(x, wub_ref[...]) + bu_ref[...], -SWIGLU_LIMIT, SWIGLU_LIMIT)
        hid = (up + 1.0) * gate * _sigmoid(SWIGLU_ALPHA * gate)
        y = _dot(hid.astype(BF16), wdb_ref[...]) + bd_ref[...]
        o_ref[...] = _pack_bf16_pairs(y.astype(BF16).astype(F32))

    @pl.when(i >= nv_ref[0])
    def _():
        o_ref[...] = jnp.zeros_like(o_ref)


def _experts(xs, block_e, block_rows, n_valid, params, l, bm):
    n_slots = xs.shape[0]
    nb = n_slots // bm
    blk = lambda i, be, nr, nv: (jnp.minimum(i, nv[0] - 1), 0)
    oblk = lambda i, be, nr, nv: (i, 0)
    wsp = pl.BlockSpec((None, None, D_MODEL, D_FF), lambda i, be, nr, nv: (l, be[i], 0, 0))
    bsp = pl.BlockSpec((None, None, 1, D_FF), lambda i, be, nr, nv: (l, be[i], 0, 0))
    b4 = lambda a: a.reshape(DEPTH, N_EXPERTS, 1, -1)
    return pl.pallas_call(
        _ffn_kernel,
        grid_spec=pltpu.PrefetchScalarGridSpec(
            num_scalar_prefetch=3,
            grid=(nb,),
            in_specs=[pl.BlockSpec((bm, HALF), blk), wsp, bsp, wsp, bsp, wsp, bsp],
            out_specs=pl.BlockSpec((bm, HALF), oblk),
            scratch_shapes=[pltpu.VMEM((D_MODEL, D_FF), BF16), pltpu.VMEM((D_MODEL, D_FF), BF16),
                            pltpu.VMEM((D_FF, D_MODEL), BF16)]),
        out_shape=jax.ShapeDtypeStruct((n_slots, HALF), jnp.int32),
        compiler_params=_params(("arbitrary",)),
        name="moe_experts",
    )(block_e, block_rows, n_valid, xs, params['w_gate'], b4(params['b_gate']), params['w_up'], b4(params['b_up']),
      params['w_down'], b4(params['b_down']))


def _combine_kernel(off_ref, cnt_ref, np_ref, x1_ref, ir_ref, pr_ref, ys_hbm, g2_ref, b2_ref, o_ref, ob_ref,
                    stage_ref, ysb_ref, wgt_ref, acc_ref, sem, *, tm, n_tiles):
    i = pl.program_id(0)
    slot = i % 2
    mk = lambda s: (lambda dst, src: pltpu.make_async_copy(ys_hbm.at[src], stage_ref.at[s, dst], sem.at[s]))

    @pl.when(i == 0)
    def _():
        stage_ref[...] = jnp.zeros_like(stage_ref)
        _region_copies(off_ref, cnt_ref, 0, 0, mk(0), True)

    col_id = _small_int_bf16(lax.broadcasted_iota(jnp.int32, (tm, ONEHOT_CHUNK), 1))
    ir = ir_ref[...]
    pr = pr_ref[...].astype(BF16)
    zero = jnp.zeros((), BF16)

    def weighted_sum(p):
        tgt = [_slot_targets(ir[:, k:k + 1], ir[:, TOP_K + k:TOP_K + k + 1], p) for k in range(TOP_K)]
        for c in range(N_EXPERTS * REGION_CAP // ONEHOT_CHUNK):
            cols = slice(c * ONEHOT_CHUNK, (c + 1) * ONEHOT_CHUNK)
            wgt = zero
            for k in reversed(range(TOP_K)):
                wgt = jnp.where(col_id == _small_int_bf16(tgt[k] - c * ONEHOT_CHUNK), pr[:, k:k + 1], wgt)
            wgt_ref[:, cols] = wgt
            ysb_ref[cols, :] = _unpack_bf16_pairs(stage_ref[slot, cols, :])
        return _dot(wgt_ref[...], ysb_ref[...])

    _region_copies(off_ref, cnt_ref, i, 0, mk(slot), False)

    @pl.when(i + 1 < n_tiles)
    def _():
        _region_copies(off_ref, cnt_ref, i + 1, 0, mk(1 - slot), True)

    acc_ref[...] = weighted_sum(0)

    def extra_pass(p, carry):
        _region_copies(off_ref, cnt_ref, i, p, mk(slot), True)
        _region_copies(off_ref, cnt_ref, i, p, mk(slot), False)
        acc_ref[...] += weighted_sum(p)
        return carry

    lax.fori_loop(1, np_ref[i], extra_pass, 0)
    out = _layer_norm(DN_ALPHA * x1_ref[...] + acc_ref[...], g2_ref[...], b2_ref[...])
    o_ref[...] = out
    ob_ref[...] = out.astype(BF16)


def _combine(x1, idx_rank, probs, ys, off, cnt8, n_pass, p, tm):
    t = x1.shape[0]
    row = lambda a: a.reshape(1, D_MODEL).astype(F32)
    full = lambda shp: pl.BlockSpec(shp, lambda i, *_: (0,) * len(shp))
    rowb = lambda w: pl.BlockSpec((tm, w), lambda i, *_: (i, 0))
    return pl.pallas_call(
        functools.partial(_combine_kernel, tm=tm, n_tiles=t // tm),
        grid_spec=pltpu.PrefetchScalarGridSpec(
            num_scalar_prefetch=3,
            grid=(t // tm,),
            in_specs=[rowb(D_MODEL), rowb(LANES), rowb(LANES), pl.BlockSpec(memory_space=pl.ANY),
                      full((1, D_MODEL)), full((1, D_MODEL))],
            out_specs=[rowb(D_MODEL), rowb(D_MODEL)],
            scratch_shapes=[pltpu.VMEM((2, N_EXPERTS * REGION_CAP, HALF), jnp.int32),
                            pltpu.VMEM((N_EXPERTS * REGION_CAP, D_MODEL), BF16),
                            pltpu.VMEM((tm, N_EXPERTS * REGION_CAP), BF16),
                            pltpu.VMEM((tm, D_MODEL), F32),
                            pltpu.SemaphoreType.DMA((2,))]),
        out_shape=[jax.ShapeDtypeStruct((t, D_MODEL), F32), jax.ShapeDtypeStruct((t, D_MODEL), BF16)],
        compiler_params=_params(("arbitrary",)),
        name="moe_combine_ln2",
    )(off, cnt8, n_pass, x1, idx_rank, probs, ys, row(p['ln2_g']), row(p['ln2_b']))


def _moe(x1, x1b, idx_rank, irt, probs, tile_counts, p, params, l, bm, tm):
    t = x1.shape[0]
    n_tiles = t // tm
    cnt = tile_counts[:, 0, :N_EXPERTS]
    cnt8 = (cnt + 7) // 8 * 8
    seg = jnp.sum(cnt8, axis=0)
    padded = (seg + bm - 1) // bm * bm
    pend = jnp.cumsum(padded)
    pstart = pend - padded
    off = pstart[None, :] + jnp.cumsum(cnt8, axis=0) - cnt8
    n_pass = jnp.maximum((jnp.max(cnt, axis=1) + REGION_CAP - 1) // REGION_CAP, 1).astype(jnp.int32)
    nb = -(-(t * TOP_K + 7 * n_tiles * N_EXPERTS) // bm) + N_EXPERTS
    n_valid = (pend[-1] // bm).astype(jnp.int32).reshape(1)
    blocks = jnp.arange(nb, dtype=jnp.int32)
    block_e = jnp.minimum(jnp.sum((pend // bm)[None, :] <= blocks[:, None], axis=1), N_EXPERTS - 1).astype(jnp.int32)
    block_rows = jnp.clip(seg[block_e] - (blocks * bm - pstart[block_e]), 0, bm).astype(jnp.int32)
    off = off.astype(jnp.int32).reshape(-1)
    cnt8 = cnt8.astype(jnp.int32).reshape(-1)
    pad = jnp.concatenate([pstart + seg, padded - seg]).astype(jnp.int32)
    xs = _dispatch(x1b, irt, off, cnt8, n_pass, pad, n_valid, nb, bm, tm)
    ys = _experts(xs, block_e, block_rows, n_valid, params, l, bm)
    return _combine(x1, idx_rank, probs, ys, off, cnt8, n_pass, p, tm)


_EXPERT_WEIGHTS = ('w_gate', 'b_gate', 'w_up', 'b_up', 'w_down', 'b_down')


def _layer(x, xb, params, l, b, s, cfg):
    p = {k: v[l] for k, v in params.items() if k not in _EXPERT_WEIGHTS}
    w_in = p['w_in']
    o = IN_OFFS
    pad = jnp.zeros((D_MODEL, LANES - 2 * ML_HEADS), w_in.dtype)
    w_ret = w_in[:, o[0]:o[4]].astype(BF16)
    w_ml = jnp.concatenate([w_in[:, o[4]:o[5]], w_in[:, o[7]:o[8]]], axis=1).astype(BF16)
    w_if = jnp.concatenate([w_in[:, o[5]:o[7]], pad], axis=1).astype(BF16)
    w_gates = w_in[:, o[11]:o[12]].astype(BF16)
    mm = functools.partial(_matmul, tm=cfg['mm_tm'], tn=cfg['mm_tn'])
    ret_in = mm(xb, w_ret, BF16)
    ml_in = mm(xb, w_ml, BF16)
    if_in = mm(xb, w_if, F32)
    gates = mm(xb, w_gates, BF16)
    y_ret = _retention(ret_in, p['ret_gn'], b, s, cfg['seq_rows']).reshape(b * s, MIX_W)
    y_ml = _mlstm(ml_in, if_in, p, b, s, cfg['seq_rows']).reshape(b * s, MIX_W)
    outs, lses = [], []
    wq = ATT_HEADS * ATT_DK
    wv = ATT_HEADS * ATT_DV
    for g in range(ATT_GROUPS):
        w_att = jnp.concatenate([w_in[:, o[8] + g * wq:o[8] + (g + 1) * wq],
                                 w_in[:, o[9] + g * wq:o[9] + (g + 1) * wq],
                                 w_in[:, o[10] + g * wv:o[10] + (g + 1) * wv]], axis=1).astype(BF16)
        og, lg = _dilated_group(_matmul_strided(xb, w_att, ATT_PATTERNS[g][1], cfg['mm_tm']), g, b, s, cfg['att_nq'])
        outs.append(og)
        lses.append(lg)
    tm = min(ROUTE_TM, b * s)
    x1, x1b, idx_rank, irt, probs, tile_counts = _merge(x, y_ret, y_ml, outs, lses, gates, p, tm)
    return _moe(x1, x1b, idx_rank, irt, probs, tile_counts, p, params, l, cfg['moe_bm'], tm)


CFG = dict(mm_tm=1024, mm_tn=1024, seq_rows=512, att_nq=4, moe_bm=512)

_PARAM_NAMES = ('w_in', 'ret_gn', 'ml_conv_w', 'ml_conv_b', 'ml_wq', 'ml_wk', 'ml_wv', 'ml_bi', 'ml_bf', 'ml_gn',
                'ml_skip', 'w_branch', 'w_out', 'ln1_g', 'ln1_b', 'w_router', 'b_router', 'w_gate', 'b_gate',
                'w_up', 'b_up', 'w_down', 'b_down', 'ln2_g', 'ln2_b')


def _forward(x, params, cfg):
    b, s, d = x.shape
    xf = x.reshape(b * s, d).astype(F32)
    xb = xf.astype(BF16)
    for l in range(DEPTH):
        xf, xb = _layer(xf, xb, params, l, b, s, cfg)
    return xf.reshape(b, s, d).astype(x.dtype)


def kernel(x, w_in, ret_gn, ml_conv_w, ml_conv_b, ml_wq, ml_wk, ml_wv, ml_bi, ml_bf, ml_gn, ml_skip, w_branch, w_out, ln1_g, ln1_b, w_router, b_router, w_gate, b_gate, w_up, b_up, w_down, b_down, ln2_g, ln2_b):
    params = dict(zip(_PARAM_NAMES, (w_in, ret_gn, ml_conv_w, ml_conv_b, ml_wq, ml_wk, ml_wv, ml_bi, ml_bf, ml_gn,
                                     ml_skip, w_branch, w_out, ln1_g, ln1_b, w_router, b_router, w_gate, b_gate,
                                     w_up, b_up, w_down, b_down, ln2_g, ln2_b)))
    return _forward(x, params, CFG)
```

```python
import functools

import jax
import jax.numpy as jnp
import numpy as np
from jax import lax
from jax.experimental import pallas as pl
from jax.experimental.pallas import tpu as pltpu

F32 = jnp.float32
BF16 = jnp.bfloat16

D_MODEL = 1024
DEPTH = 2
MIX_W = D_MODEL // 2
N_BRANCHES = 3
RET_HEADS = 4
RET_DV = MIX_W // RET_HEADS
RET_DK = RET_DV // 2
ML_HEADS = 4
ML_DH = MIX_W // ML_HEADS
ML_CONV = 4
ML_QK_BLOCK = 4
ATT_PATTERNS = ((128, 1), (512, 4), (2048, 16))
ATT_GROUPS = len(ATT_PATTERNS)
ATT_HEADS = 4
ATT_DV = MIX_W // ATT_HEADS
ATT_DK = ATT_DV // 2
N_EXPERTS = 32
TOP_K = 4
D_FF = D_MODEL
SWIGLU_LIMIT = 7.0
SWIGLU_ALPHA = 1.702
DN_ALPHA = (2.0 * DEPTH) ** 0.25
EPS = 1e-5

CHUNK = 128
LANES = 128
NEG = -1e30
VMEM_LIMIT = 56 * 1024 * 1024

IN_SIZES = (RET_HEADS * RET_DK, RET_HEADS * RET_DK, MIX_W, MIX_W,
            MIX_W, ML_HEADS, ML_HEADS, MIX_W,
            ATT_GROUPS * ATT_HEADS * ATT_DK, ATT_GROUPS * ATT_HEADS * ATT_DK, ATT_GROUPS * ATT_HEADS * ATT_DV,
            N_BRANCHES * D_MODEL)
IN_OFFS = tuple(int(v) for v in np.cumsum((0,) + IN_SIZES))


def _params(sem):
    return pltpu.CompilerParams(dimension_semantics=sem, vmem_limit_bytes=VMEM_LIMIT)


def _dot(a, b):
    return jnp.dot(a, b, preferred_element_type=F32)


def _dot_nt(a, b):
    return lax.dot_general(a, b, (((1,), (1,)), ((), ())), preferred_element_type=F32)


def _dot_tn(a, b):
    return lax.dot_general(a, b, (((0,), (0,)), ((), ())), preferred_element_type=F32)


def _sigmoid(x):
    return 1.0 / (1.0 + jnp.exp(-x))


def _mm_kernel(x_ref, w_ref, o_ref):
    o_ref[...] = _dot(x_ref[...], w_ref[...]).astype(o_ref.dtype)


def _matmul(x, w, out_dtype, tm, tn):
    t, k = x.shape
    n = w.shape[1]
    tm = min(tm, t)
    tn = max(c for c in range(LANES, min(tn, n) + 1, LANES) if n % c == 0)
    assert t % tm == 0
    return pl.pallas_call(
        _mm_kernel,
        grid=(n // tn, t // tm),
        in_specs=[pl.BlockSpec((tm, k), lambda j, i: (i, 0)),
                  pl.BlockSpec((k, tn), lambda j, i: (0, j))],
        out_specs=pl.BlockSpec((tm, tn), lambda j, i: (i, j)),
        out_shape=jax.ShapeDtypeStruct((t, n), out_dtype),
        compiler_params=_params(("arbitrary", "arbitrary")),
        name="in_proj",
    )(x, w)


def _mm_strided_kernel(x_ref, w_ref, o_ref, y_ref, *, d):
    tm = x_ref.shape[0]
    n = tm // d
    width = w_ref.shape[1]
    y = _dot(x_ref[...], w_ref[...])
    if d == 1:
        o_ref[...] = y.astype(o_ref.dtype)
        return
    for c in range(width // LANES):
        y_ref[c] = y[:, c * LANES:(c + 1) * LANES]
    for r in range(d):
        for c in range(width // LANES):
            o_ref[:, r * width + c * LANES:r * width + (c + 1) * LANES] = (
                y_ref[c, pl.ds(r, n, stride=d), :].astype(o_ref.dtype))


def _matmul_strided(x, w, d, tm):
    t, k = x.shape
    n = w.shape[1]
    tm = min(tm, t)
    assert t % tm == 0 and tm % (16 * d) == 0
    return pl.pallas_call(
        functools.partial(_mm_strided_kernel, d=d),
        grid=(t // tm,),
        in_specs=[pl.BlockSpec((tm, k), lambda i: (i, 0)),
                  pl.BlockSpec((k, n), lambda i: (0, 0))],
        out_specs=pl.BlockSpec((tm // d, d * n), lambda i: (i, 0)),
        out_shape=jax.ShapeDtypeStruct((t // d, d * n), BF16),
        scratch_shapes=[pltpu.VMEM((n // LANES, tm, LANES), F32)],
        compiler_params=_params(("arbitrary",)),
        name=f"att_in_proj_d{d}",
    )(x, w)


def _head_norm(o, gn):
    mu = jnp.mean(o, axis=-1, keepdims=True)
    oc = o - mu
    var = jnp.mean(oc * oc, axis=-1, keepdims=True)
    return oc * lax.rsqrt(var + EPS) * gn


def _ret_kernel(blk_ref, dm_ref, qd_ref, kd_ref, cd_ref, gn_ref, o_ref, st_ref, *, nc, nb):
    @pl.when(pl.program_id(0) == 0)
    def _():
        st_ref[...] = jnp.zeros_like(st_ref)

    hq = RET_HEADS * RET_DK
    for c in range(nc):
        rows = pl.ds(c * CHUNK, CHUNK)
        for h in range(RET_HEADS):
            for bi in range(nb):
                q = blk_ref[bi, rows, h * RET_DK:(h + 1) * RET_DK]
                k = blk_ref[bi, rows, hq + h * RET_DK:hq + (h + 1) * RET_DK]
                v = blk_ref[bi, rows, 2 * hq + h * RET_DV:2 * hq + (h + 1) * RET_DV]
                g = blk_ref[bi, rows, 2 * hq + MIX_W + h * RET_DV:2 * hq + MIX_W + (h + 1) * RET_DV].astype(F32)
                st = st_ref[bi * RET_HEADS + h]
                a = (_dot_nt(q, k) * dm_ref[h]).astype(BF16)
                o = _dot(a, v) + _dot(q, st.astype(BF16)) * qd_ref[h]
                kv = _dot_tn(k, (v.astype(F32) * kd_ref[h]).astype(BF16))
                st_ref[bi * RET_HEADS + h] = cd_ref[h] * st + kv
                y = _head_norm(o, gn_ref[:, h * RET_DV:(h + 1) * RET_DV]) * (g * _sigmoid(g))
                o_ref[bi, rows, h * RET_DV:(h + 1) * RET_DV] = y.astype(o_ref.dtype)


def _retention(ret_in, ret_gn, b, s, rows):
    nc = rows // CHUNK
    scale = RET_DK ** -0.5
    log_gamma = jnp.log1p(-jnp.exp2(-5.0 - jnp.arange(RET_HEADS, dtype=F32)))
    pos = jnp.arange(CHUNK, dtype=F32)
    diff = pos[:, None] - pos[None, :]
    dm = jnp.where(diff >= 0, jnp.exp(log_gamma[:, None, None] * jnp.maximum(diff, 0.0)), 0.0) * scale
    kd = jnp.broadcast_to(jnp.exp(log_gamma[:, None] * (CHUNK - 1.0 - pos))[:, :, None], (RET_HEADS, CHUNK, RET_DV))
    qd = jnp.broadcast_to((jnp.exp(log_gamma[:, None] * (pos + 1.0)) * scale)[:, :, None], (RET_HEADS, CHUNK, RET_DV))
    cd = jnp.broadcast_to(jnp.exp(log_gamma * CHUNK)[:, None, None], (RET_HEADS, 1, RET_DV))
    w_in = ret_in.shape[-1]
    full = lambda shp: pl.BlockSpec(shp, lambda si: (0,) * len(shp))
    return pl.pallas_call(
        functools.partial(_ret_kernel, nc=nc, nb=b),
        grid=(s // rows,),
        in_specs=[pl.BlockSpec((b, rows, w_in), lambda si: (0, si, 0)),
                  full((RET_HEADS, CHUNK, CHUNK)), full((RET_HEADS, CHUNK, RET_DV)),
                  full((RET_HEADS, CHUNK, RET_DV)), full((RET_HEADS, 1, RET_DV)), full((1, MIX_W))],
        out_specs=pl.BlockSpec((b, rows, MIX_W), lambda si: (0, si, 0)),
        out_shape=jax.ShapeDtypeStruct((b, s, MIX_W), BF16),
        scratch_shapes=[pltpu.VMEM((b * RET_HEADS, RET_DK, RET_DV), F32)],
        compiler_params=_params(("arbitrary",)),
        name="retention",
    )(ret_in.reshape(b, s, w_in), dm, qd, kd, cd, ret_gn.reshape(1, MIX_W).astype(F32))


def _log_sigmoid(x):
    return jnp.minimum(x, 0.0) - jnp.log(1.0 + jnp.exp(-jnp.abs(x)))


def _ml_kernel(ml_ref, if_ref, cw_ref, cb_ref, bq_ref, bk_ref, bv_ref, gb_ref, gn_ref, sk_ref,
               o_ref, c_ref, n_ref, m_ref, halo_ref, *, nc, nb):
    rows_total = nc * CHUNK

    @pl.when(pl.program_id(0) == 0)
    def _():
        c_ref[...] = jnp.zeros_like(c_ref)
        n_ref[...] = jnp.zeros_like(n_ref)
        m_ref[...] = jnp.zeros_like(m_ref)
        halo_ref[...] = jnp.zeros_like(halo_ref)

    x_all, mc_all = [], []
    for bi in range(nb):
        xa = ml_ref[bi, :, 0:MIX_W].astype(F32)
        xf = jnp.concatenate([halo_ref[bi], xa], axis=0)
        acc = jnp.broadcast_to(cb_ref[...], (rows_total, MIX_W))
        for j in range(ML_CONV):
            sh = ML_CONV - 1 - j
            xs = xf if sh == 0 else pltpu.roll(xf, sh, 0)
            acc = acc + xs[8:8 + rows_total] * cw_ref[j:j + 1, :]
        halo_ref[bi] = xa[rows_total - 8:rows_total]
        x_all.append(xa)
        mc_all.append(acc * _sigmoid(acc))

    ri = lax.broadcasted_iota(jnp.int32, (CHUNK, CHUNK), 0)
    ci = lax.broadcasted_iota(jnp.int32, (CHUNK, CHUNK), 1)
    tri = ri >= ci
    tril = tri.astype(F32)
    kscale = ML_DH ** -0.5
    log_kscale = float(np.log(kscale))

    for c in range(nc):
        r0 = c * CHUNK
        gates = []
        for bi in range(nb):
            g_pre = if_ref[bi, r0:r0 + CHUNK, :] + gb_ref[...]
            cum = jnp.dot(tril, _log_sigmoid(g_pre), preferred_element_type=F32, precision=lax.Precision.HIGHEST)
            gates.append((g_pre, cum, g_pre.T, cum.T))
        for h in range(ML_HEADS):
            cols = slice(h * ML_DH, (h + 1) * ML_DH)
            for bi in range(nb):
                g_pre, cum, g_t, cum_t = gates[bi]
                mc = mc_all[bi][r0:r0 + CHUNK, cols]
                mcb = mc.astype(BF16)
                q = _dot(mcb, bq_ref[h]).astype(BF16)
                k = _dot(mcb, bk_ref[h]).astype(BF16)
                v = _dot(x_all[bi][r0:r0 + CHUNK, cols].astype(BF16), bv_ref[h]).astype(BF16)
                cum_col = cum[:, ML_HEADS + h:ML_HEADS + h + 1]
                ig_col = g_pre[:, h:h + 1]
                cum_row = cum_t[ML_HEADS + h:ML_HEADS + h + 1, :]
                ig_row = g_t[h:h + 1, :]
                tot = cum[CHUNK - 1:CHUNK, ML_HEADS + h:ML_HEADS + h + 1]
                m_prev = m_ref[bi, h:h + 1, 0:1]
                c_prev = c_ref[bi * ML_HEADS + h]
                n_prev = n_ref[bi, h:h + 1, :]
                dlog = jnp.where(tri, cum_col + (ig_row - cum_row), NEG)
                m_inter = cum_col + m_prev
                m_q = jnp.maximum(m_inter, jnp.max(dlog, axis=-1, keepdims=True))
                w_qk = jnp.exp(dlog - (m_q - log_kscale)) * _dot_nt(q, k)
                inter = jnp.exp(m_inter - m_q)
                num = _dot(w_qk.astype(BF16), v) + inter * _dot(q, c_prev.astype(BF16))
                den = (jnp.sum(w_qk, axis=-1, keepdims=True)
                       + inter * jnp.sum(q.astype(F32) * n_prev, axis=-1, keepdims=True))
                hh = num * (1.0 / jnp.maximum(jnp.abs(den), jnp.exp(-m_q)))
                a_col = tot - cum_col + ig_col
                a_max = jnp.max(a_col, axis=0, keepdims=True)
                wa = jnp.exp(a_col - a_max) * kscale
                chunk_c = _dot_tn(k, (wa * v.astype(F32)).astype(BF16))
                chunk_n = jnp.sum(wa * k.astype(F32), axis=0, keepdims=True)
                m_new = jnp.maximum(tot + m_prev, a_max)
                s_old = jnp.exp(tot + m_prev - m_new)
                s_new = jnp.exp(a_max - m_new)
                c_ref[bi * ML_HEADS + h] = s_old * c_prev + s_new * chunk_c
                n_ref[bi, h:h + 1, :] = s_old * n_prev + s_new * chunk_n
                m_ref[bi, h:h + 1, :] = jnp.broadcast_to(m_new, (1, LANES))
                mo = ml_ref[bi, r0:r0 + CHUNK, MIX_W + h * ML_DH:MIX_W + (h + 1) * ML_DH].astype(F32)
                y = _sigmoid(mo) * (_head_norm(hh, gn_ref[:, cols]) + sk_ref[:, cols] * mc)
                o_ref[bi, r0:r0 + CHUNK, cols] = y.astype(o_ref.dtype)


def _block_diag(w):
    nb = w.shape[0]
    per = nb // ML_HEADS
    wh = w.reshape(ML_HEADS, per, ML_QK_BLOCK, ML_QK_BLOCK)
    eye = jnp.eye(per, dtype=w.dtype)
    bd = jnp.einsum('hncd,nm->hncmd', wh, eye)
    return bd.reshape(ML_HEADS, per * ML_QK_BLOCK, per * ML_QK_BLOCK).astype(BF16)


def _mlstm(ml_in, if_in, p, b, s, rows):
    nc = rows // CHUNK
    full = lambda shp: pl.BlockSpec(shp, lambda si: (0,) * len(shp))
    gb = jnp.zeros((1, LANES), F32).at[0, 0:ML_HEADS].set(p['ml_bi'].astype(F32)).at[0, ML_HEADS:2 * ML_HEADS].set(p['ml_bf'].astype(F32))
    row = lambda a: a.reshape(1, MIX_W).astype(F32)
    return pl.pallas_call(
        functools.partial(_ml_kernel, nc=nc, nb=b),
        grid=(s // rows,),
        in_specs=[pl.BlockSpec((b, rows, 2 * MIX_W), lambda si: (0, si, 0)),
                  pl.BlockSpec((b, rows, LANES), lambda si: (0, si, 0)),
                  full((ML_CONV, MIX_W)), full((1, MIX_W)),
                  full((ML_HEADS, ML_DH, ML_DH)), full((ML_HEADS, ML_DH, ML_DH)), full((ML_HEADS, ML_DH, ML_DH)),
                  full((1, LANES)), full((1, MIX_W)), full((1, MIX_W))],
        out_specs=pl.BlockSpec((b, rows, MIX_W), lambda si: (0, si, 0)),
        out_shape=jax.ShapeDtypeStruct((b, s, MIX_W), BF16),
        scratch_shapes=[pltpu.VMEM((b * ML_HEADS, ML_DH, ML_DH), F32), pltpu.VMEM((b, 8, ML_DH), F32),
                        pltpu.VMEM((b, 8, LANES), F32), pltpu.VMEM((b, 8, MIX_W), F32)],
        compiler_params=_params(("arbitrary",)),
        name="mlstm",
    )(ml_in.reshape(b, s, 2 * MIX_W), if_in.reshape(b, s, LANES),
      p['ml_conv_w'].astype(F32), row(p['ml_conv_b']),
      _block_diag(p['ml_wq']), _block_diag(p['ml_wk']), _block_diag(p['ml_wv']),
      gb, row(p['ml_gn']), row(p['ml_skip']))


def _att_kernel(q_ref, kc_ref, kp_ref, vc_ref, vp_ref, bias_ref, o_ref, lse_ref, *, nq):
    first = pl.program_id(2) == 0
    lane = lax.broadcasted_iota(jnp.int32, (CHUNK, LANES), 1)
    ones = jnp.ones((2 * CHUNK, ATT_DV), BF16)
    for j in range(nq):
        rows = slice(j * CHUNK, (j + 1) * CHUNK)
        lse_out = jnp.zeros((CHUNK, LANES), F32)
        for h in range(ATT_HEADS):
            dk = slice(h * ATT_DK, (h + 1) * ATT_DK)
            dv = slice(h * ATT_DV, (h + 1) * ATT_DV)
            q = q_ref[rows, dk]
            if j == 0:
                kk = jnp.concatenate([kp_ref[:, dk], kc_ref[rows, dk]], axis=0)
                vv = jnp.concatenate([vp_ref[:, dv], vc_ref[rows, dv]], axis=0)
                bias = bias_ref[jnp.where(first, 0, 1), h]
            else:
                kk = kc_ref[(j - 1) * CHUNK:(j + 1) * CHUNK, dk]
                vv = vc_ref[(j - 1) * CHUNK:(j + 1) * CHUNK, dv]
                bias = bias_ref[1, h]
            s = _dot_nt(q, kk) + bias
            m = jnp.max(s, axis=-1, keepdims=True)
            p = jnp.exp(s - m).astype(BF16)
            oa = _dot(p, jnp.concatenate([vv, ones], axis=1))
            l = oa[:, ATT_DV:2 * ATT_DV]
            o_ref[rows, dv] = (oa[:, 0:ATT_DV] * (1.0 / l)).astype(o_ref.dtype)
            lse_out = jnp.where(lane == h, m + jnp.log(l), lse_out)
        lse_ref[rows, :] = lse_out


def _att_bias(window, dilation, slopes):
    wb = window // dilation
    qi = jnp.arange(wb)[:, None]
    kj = jnp.arange(2 * wb)[None, :]
    delta = qi + wb - kj
    band = (delta >= 0) & (delta <= wb)
    bias = -slopes[:, None, None] * (dilation * delta).astype(F32)
    later = jnp.where(band[None], bias, NEG)
    first = jnp.where((band & (kj >= wb))[None], bias, NEG)
    return jnp.stack([first, later]).astype(F32)


def _dilated_group(qkv, g, b, s, nq):
    window, d = ATT_PATTERNS[g]
    assert window // d == CHUNK
    l_sub = s // d
    assert l_sub % CHUNK == 0
    n_blk = l_sub // CHUNK
    slopes = jnp.exp2(-8.0 * jnp.arange(1, ATT_GROUPS * ATT_HEADS + 1, dtype=F32) / (ATT_GROUPS * ATT_HEADS))
    bias = _att_bias(window, d, slopes.reshape(ATT_GROUPS, ATT_HEADS)[g])
    wq = ATT_HEADS * ATT_DK
    wv = ATT_HEADS * ATT_DV
    per_tok = 2 * wq + wv
    x = qkv.reshape(b, l_sub, d * per_tok)
    qb = per_tok // wq
    vb = per_tok // wv
    nq = min(nq, n_blk)
    assert n_blk % nq == 0
    span = nq * CHUNK
    prev = lambda n: jnp.maximum(n * nq - 1, 0)
    o, lse = pl.pallas_call(
        functools.partial(_att_kernel, nq=nq),
        grid=(b, d, n_blk // nq),
        in_specs=[pl.BlockSpec((None, span, wq), lambda bi, r, n: (bi, n, r * qb)),
                  pl.BlockSpec((None, span, wq), lambda bi, r, n: (bi, n, r * qb + 1)),
                  pl.BlockSpec((None, CHUNK, wq), lambda bi, r, n: (bi, prev(n), r * qb + 1)),
                  pl.BlockSpec((None, span, wv), lambda bi, r, n: (bi, n, r * vb + 1)),
                  pl.BlockSpec((None, CHUNK, wv), lambda bi, r, n: (bi, prev(n), r * vb + 1)),
                  pl.BlockSpec((2, ATT_HEADS, CHUNK, 2 * CHUNK), lambda bi, r, n: (0, 0, 0, 0))],
        out_specs=[pl.BlockSpec((None, span, wv), lambda bi, r, n: (bi, n, r)),
                   pl.BlockSpec((None, span, LANES), lambda bi, r, n: (bi, n, r))],
        out_shape=[jax.ShapeDtypeStruct((b, l_sub, d * wv), BF16),
                   jax.ShapeDtypeStruct((b, l_sub, d * LANES), F32)],
        compiler_params=_params(("arbitrary", "arbitrary", "arbitrary")),
        name=f"dilated_attn_g{g}",
    )(x, x, x, x, x, bias)
    return o.reshape(b * l_sub, d * wv), lse.reshape(b * l_sub, d * LANES)


def _layer_norm(z, g, b):
    mu = jnp.mean(z, axis=-1, keepdims=True)
    zc = z - mu
    var = jnp.mean(zc * zc, axis=-1, keepdims=True)
    return zc * lax.rsqrt(var + EPS) * g + b


def _merge_kernel(x_ref, yr_ref, ym_ref, o0_ref, o1_ref, o2_ref, l0_ref, l1_ref, l2_ref, gt_ref,
                  wb_ref, wo_ref, g1_ref, b1_ref, wrh_ref, wrl_ref, br_ref,
                  x1_ref, x1b_ref, ir_ref, irt_ref, pr_ref, cnt_ref, og_ref, lg_ref, *, tm):
    for g, (src_o, src_l) in enumerate(((o0_ref, l0_ref), (o1_ref, l1_ref), (o2_ref, l2_ref))):
        d = ATT_PATTERNS[g][1]
        n = tm // d
        for r in range(d):
            rows = pl.ds(r, n, stride=d) if d > 1 else pl.ds(0, n)
            for h in range(ATT_HEADS):
                og_ref[g, h, rows, :] = src_o[:, r * MIX_W + h * ATT_DV:r * MIX_W + (h + 1) * ATT_DV].astype(F32)
            lg_ref[g, rows, :] = src_l[:, r * LANES:(r + 1) * LANES]

    l0, l1, l2 = lg_ref[0], lg_ref[1], lg_ref[2]
    lm = jnp.maximum(jnp.maximum(l0, l1), l2)
    e0, e1, e2 = jnp.exp(l0 - lm), jnp.exp(l1 - lm), jnp.exp(l2 - lm)
    inv = 1.0 / (e0 + e1 + e2)
    parts = []
    for h in range(ATT_HEADS):
        dv = slice(h * ATT_DV, (h + 1) * ATT_DV)
        hs = slice(h, h + 1)
        parts.append((e0[:, hs] * og_ref[0, h] + e1[:, hs] * og_ref[1, h] + e2[:, hs] * og_ref[2, h]) * inv[:, hs])
    y_att = jnp.concatenate(parts, axis=-1).astype(BF16)

    merged = _sigmoid(gt_ref[:, 0:D_MODEL].astype(F32)) * _dot(yr_ref[...], wb_ref[0])
    merged = merged + _sigmoid(gt_ref[:, D_MODEL:2 * D_MODEL].astype(F32)) * _dot(ym_ref[...], wb_ref[1])
    merged = merged + _sigmoid(gt_ref[:, 2 * D_MODEL:3 * D_MODEL].astype(F32)) * _dot(y_att, wb_ref[2])
    z = DN_ALPHA * x_ref[...] + _dot(merged.astype(BF16), wo_ref[...])
    x1 = _layer_norm(z, g1_ref[...], b1_ref[...])
    x1_ref[...] = x1
    x1b_ref[...] = x1.astype(BF16)

    xh = x1.astype(BF16)
    xl = (x1 - xh.astype(F32)).astype(BF16)
    logits = _dot(xh, wrh_ref[...]) + _dot(xl, wrh_ref[...]) + _dot(xh, wrl_ref[...]) + br_ref[...]
    lane = lax.broadcasted_iota(jnp.int32, (tm, LANES), 1)
    work = jnp.where(lane < N_EXPERTS, logits, -jnp.inf)
    vals, idxs = [], []
    for _ in range(TOP_K):
        m = jnp.max(work, axis=-1, keepdims=True)
        idx = jnp.min(jnp.where(work == m, lane, LANES), axis=-1, keepdims=True)
        vals.append(m)
        idxs.append(idx)
        work = jnp.where(lane == idx, -jnp.inf, work)
    es = [jnp.exp(v - vals[0]) for v in vals]
    tot = es[0] + es[1] + es[2] + es[3]
    pr = jnp.zeros((tm, LANES), F32)
    for k in range(TOP_K):
        pr = jnp.where(lane == k, es[k] / tot, pr)
    pr_ref[...] = pr

    onehot = jnp.zeros((tm, LANES), F32)
    for k in range(TOP_K):
        onehot = onehot + (lane == idxs[k]).astype(F32)
    ri = lax.broadcasted_iota(jnp.int32, (tm, tm), 0)
    ci = lax.broadcasted_iota(jnp.int32, (tm, tm), 1)
    same_tile = (ri // ROUTE_TM) == (ci // ROUTE_TM)
    before = _dot(((ri > ci) & same_tile).astype(BF16), onehot.astype(BF16))
    ir = jnp.zeros((tm, LANES), F32)
    for k in range(TOP_K):
        rank = jnp.sum(jnp.where(lane == idxs[k], before, 0.0), axis=-1, keepdims=True)
        ir = jnp.where(lane == k, idxs[k].astype(F32), ir)
        ir = jnp.where(lane == TOP_K + k, rank, ir)
    ir_ref[...] = ir.astype(jnp.int32)
    irt_ref[...] = ir.T[0:8, :]
    for j in range(tm // ROUTE_TM):
        cnt_j = jnp.sum(onehot[j * ROUTE_TM:(j + 1) * ROUTE_TM], axis=0, keepdims=True)
        cnt_ref[j] = jnp.broadcast_to(cnt_j, (8, LANES)).astype(jnp.int32)


def _merge(x, y_ret, y_ml, outs, lses, gates, p, tm):
    t = x.shape[0]
    tm = min(tm, t)
    rowb = lambda w: pl.BlockSpec((tm, w), lambda i: (i, 0))
    full = lambda shp: pl.BlockSpec(shp, lambda i: (0,) * len(shp))
    wr = jnp.zeros((D_MODEL, LANES), F32).at[:, :N_EXPERTS].set(p['w_router'].astype(F32))
    wrh = wr.astype(BF16)
    wrl = (wr - wrh.astype(F32)).astype(BF16)
    br = jnp.zeros((1, LANES), F32).at[0, :N_EXPERTS].set(p['b_router'].astype(F32))
    row = lambda a: a.reshape(1, D_MODEL).astype(F32)
    return pl.pallas_call(
        functools.partial(_merge_kernel, tm=tm),
        grid=(t // tm,),
        in_specs=[rowb(D_MODEL), rowb(MIX_W), rowb(MIX_W)]
                 + [pl.BlockSpec((tm // d, d * MIX_W), lambda i: (i, 0)) for _, d in ATT_PATTERNS]
                 + [pl.BlockSpec((tm // d, d * LANES), lambda i: (i, 0)) for _, d in ATT_PATTERNS]
                 + [rowb(N_BRANCHES * D_MODEL),
                  full((N_BRANCHES, MIX_W, D_MODEL)), full((D_MODEL, D_MODEL)),
                  full((1, D_MODEL)), full((1, D_MODEL)),
                  full((D_MODEL, LANES)), full((D_MODEL, LANES)), full((1, LANES))],
        out_specs=[rowb(D_MODEL), rowb(D_MODEL), rowb(LANES), pl.BlockSpec((8, tm), lambda i: (0, i)), rowb(LANES),
                   pl.BlockSpec((tm // ROUTE_TM, 8, LANES), lambda i: (i, 0, 0))],
        out_shape=[jax.ShapeDtypeStruct((t, D_MODEL), F32),
                   jax.ShapeDtypeStruct((t, D_MODEL), BF16),
                   jax.ShapeDtypeStruct((t, LANES), jnp.int32),
                   jax.ShapeDtypeStruct((8, t), F32),
                   jax.ShapeDtypeStruct((t, LANES), F32),
                   jax.ShapeDtypeStruct((t // ROUTE_TM, 8, LANES), jnp.int32)],
        scratch_shapes=[pltpu.VMEM((ATT_GROUPS, ATT_HEADS, tm, ATT_DV), F32),
                        pltpu.VMEM((ATT_GROUPS, tm, LANES), F32)],
        compiler_params=_params(("arbitrary",)),
        name="merge_ln1_router",
    )(x, y_ret, y_ml, outs[0], outs[1], outs[2], lses[0], lses[1], lses[2], gates,
      p['w_branch'].astype(BF16), p['w_out'].astype(BF16), row(p['ln1_g']), row(p['ln1_b']), wrh, wrl, br)


ROUTE_TM = 256
REGION_CAP = 64
REGION_PIECES = (64, 32, 16, 8)
PAD_PIECES = (256, 128, 64, 32, 16, 8)
ONEHOT_CHUNK = 256
HALF = D_MODEL // 2
HI_MASK = -65536


def _pack_bf16_pairs(y):
    lo = lax.shift_right_logical(lax.bitcast_convert_type(y[:, :HALF], jnp.int32), 16)
    hi = lax.bitcast_convert_type(y[:, HALF:], jnp.int32) & HI_MASK
    return lo | hi


def _unpack_bf16_pairs(w):
    lo = lax.bitcast_convert_type(lax.shift_left(w, 16), F32)
    hi = lax.bitcast_convert_type(w & HI_MASK, F32)
    return jnp.concatenate([lo, hi], axis=1).astype(BF16)


def _row_pieces(n, pieces, fn):
    for size in pieces:
        if size == pieces[0]:
            cond, pos = n >= size, 0
        else:
            cond, pos = (n & size) != 0, pl.multiple_of(n & (-2 * size), 8)

        @pl.when(cond)
        def _(pos=pos, size=size):
            fn(pos, size)


def _region_copies(off_ref, cnt_ref, tile, p, make_copy, start):
    for e in range(N_EXPERTS):
        n = jnp.clip(cnt_ref[tile * N_EXPERTS + e] - p * REGION_CAP, 0, REGION_CAP)
        base = off_ref[tile * N_EXPERTS + e] + p * REGION_CAP

        def piece(pos, size, e=e, base=base):
            cp = make_copy(pl.ds(e * REGION_CAP + pos, size), pl.ds(pl.multiple_of(base + pos, 8), size))
            if start:
                cp.start()
            else:
                cp.wait()

        _row_pieces(n, REGION_PIECES, piece)


def _slot_targets(idx, lrank, p):
    lr = lrank - p * REGION_CAP
    return jnp.where((lr >= 0) & (lr < REGION_CAP), idx * REGION_CAP + lr, -1)


def _small_int_bf16(v):
    return v.astype(F32).astype(BF16)


def _dispatch_kernel(off_ref, cnt_ref, np_ref, pad_ref, nv_ref, x_ref, irt_ref, xs_hbm,
                     stage_ref, zero_ref, sem, zsem, *, tm, n_tiles, bm, nb):
    i = pl.program_id(0)
    slot = i % 2
    row_id = _small_int_bf16(lax.broadcasted_iota(jnp.int32, (ONEHOT_CHUNK, tm), 0))
    meta = irt_ref[...].astype(jnp.int32)
    x = x_ref[...]
    mk = lambda s: (lambda src, dst: pltpu.make_async_copy(stage_ref.at[s, src], xs_hbm.at[dst], sem.at[s]))

    def one_pass(p, carry):
        tgt = [_slot_targets(meta[k:k + 1, :], meta[TOP_K + k:TOP_K + k + 1, :], p) for k in range(TOP_K)]
        for c in range(N_EXPERTS * REGION_CAP // ONEHOT_CHUNK):
            hit = row_id == _small_int_bf16(tgt[0] - c * ONEHOT_CHUNK)
            for k in range(1, TOP_K):
                hit = hit | (row_id == _small_int_bf16(tgt[k] - c * ONEHOT_CHUNK))
            sel = jnp.where(hit, jnp.ones((), BF16), jnp.zeros((), BF16))
            stage_ref[slot, c * ONEHOT_CHUNK:(c + 1) * ONEHOT_CHUNK, :] = _pack_bf16_pairs(_dot(sel, x))
        _region_copies(off_ref, cnt_ref, i, p, mk(slot), True)

        @pl.when(p < np_ref[i] - 1)
        def _():
            _region_copies(off_ref, cnt_ref, i, p, mk(slot), False)
        return carry

    lax.fori_loop(0, np_ref[i], one_pass, 0)

    @pl.when(i > 0)
    def _():
        _region_copies(off_ref, cnt_ref, i - 1, np_ref[jnp.maximum(i - 1, 0)] - 1, mk(1 - slot), False)

    @pl.when(i == n_tiles - 1)
    def _():
        _region_copies(off_ref, cnt_ref, i, np_ref[i] - 1, mk(slot), False)
        zero_ref[...] = jnp.zeros_like(zero_ref)
        for phase in (True, False):
            for e in range(N_EXPERTS):
                def piece(pos, size, e=e):
                    cp = pltpu.make_async_copy(zero_ref.at[pl.ds(0, size)],
                                               xs_hbm.at[pl.ds(pl.multiple_of(pad_ref[e] + pos, 8), size)], zsem)
                    cp.start() if phase else cp.wait()
                _row_pieces(pad_ref[N_EXPERTS + e], PAD_PIECES, piece)

        def zero_block(blk, carry):
            for half in range(bm // PAD_PIECES[0]):
                cp = pltpu.make_async_copy(
                    zero_ref, xs_hbm.at[pl.ds(pl.multiple_of(blk * bm + half * PAD_PIECES[0], 8), PAD_PIECES[0])], zsem)
                cp.start()
                cp.wait()
            return carry

        lax.fori_loop(nv_ref[0], nb, zero_block, 0)


def _dispatch(x1b, irt, off, cnt8, n_pass, pad, n_valid, nb, bm, tm):
    t = x1b.shape[0]
    assert bm % PAD_PIECES[0] == 0 and bm <= 2 * PAD_PIECES[0]
    return pl.pallas_call(
        functools.partial(_dispatch_kernel, tm=tm, n_tiles=t // tm, bm=bm, nb=nb),
        grid_spec=pltpu.PrefetchScalarGridSpec(
            num_scalar_prefetch=5,
            grid=(t // tm,),
            in_specs=[pl.BlockSpec((tm, D_MODEL), lambda i, *_: (i, 0)),
                      pl.BlockSpec((8, tm), lambda i, *_: (0, i))],
            out_specs=pl.BlockSpec(memory_space=pl.ANY),
            scratch_shapes=[pltpu.VMEM((2, N_EXPERTS * REGION_CAP, HALF), jnp.int32),
                            pltpu.VMEM((PAD_PIECES[0], HALF), jnp.int32),
                            pltpu.SemaphoreType.DMA((2,)), pltpu.SemaphoreType.DMA(())]),
        out_shape=jax.ShapeDtypeStruct((nb * bm, HALF), jnp.int32),
        compiler_params=_params(("arbitrary",)),
        name="moe_dispatch",
    )(off, cnt8, n_pass, pad, n_valid, x1b, irt)


def _ffn_kernel(be_ref, nr_ref, nv_ref, xs_ref, wg_ref, bg_ref, wu_ref, bu_ref, wd_ref, bd_ref, o_ref,
                wgb_ref, wub_ref, wdb_ref):
    i = pl.program_id(0)

    @pl.when(i < nv_ref[0])
    def _():
        @pl.when((i == 0) | (be_ref[i] != be_ref[jnp.maximum(i - 1, 0)]))
        def _():
            wgb_ref[...] = wg_ref[...].astype(BF16)
            wub_ref[...] = wu_ref[...].astype(BF16)
            wdb_ref[...] = wd_ref[...].astype(BF16)

        rows = lax.broadcasted_iota(jnp.int32, xs_ref.shape, 0)
        x = _unpack_bf16_pairs(jnp.where(rows < nr_ref[i], xs_ref[...], 0))
        gate = jnp.minimum(_dot(x, wgb_ref[...]) + bg_ref[...], SWIGLU_LIMIT)
        up = jnp.clip(_dot(x, wub_ref[...]) + bu_ref[...], -SWIGLU_LIMIT, SWIGLU_LIMIT)
        hid = (up + 1.0) * gate * _sigmoid(SWIGLU_ALPHA * gate)
        y = _dot(hid.astype(BF16), wdb_ref[...]) + bd_ref[...]
        o_ref[...] = _pack_bf16_pairs(y.astype(BF16).astype(F32))

    @pl.when(i >= nv_ref[0])
    def _():
        o_ref[...] = jnp.zeros_like(o_ref)


def _experts(xs, block_e, block_rows, n_valid, params, l, bm):
    n_slots = xs.shape[0]
    nb = n_slots // bm
    blk = lambda i, be, nr, nv: (jnp.minimum(i, nv[0] - 1), 0)
    oblk = lambda i, be, nr, nv: (i, 0)
    wsp = pl.BlockSpec((None, None, D_MODEL, D_FF), lambda i, be, nr, nv: (l, be[i], 0, 0))
    bsp = pl.BlockSpec((None, None, 1, D_FF), lambda i, be, nr, nv: (l, be[i], 0, 0))
    b4 = lambda a: a.reshape(DEPTH, N_EXPERTS, 1, -1)
    return pl.pallas_call(
        _ffn_kernel,
        grid_spec=pltpu.PrefetchScalarGridSpec(
            num_scalar_prefetch=3,
            grid=(nb,),
            in_specs=[pl.BlockSpec((bm, HALF), blk), wsp, bsp, wsp, bsp, wsp, bsp],
            out_specs=pl.BlockSpec((bm, HALF), oblk),
            scratch_shapes=[pltpu.VMEM((D_MODEL, D_FF), BF16), pltpu.VMEM((D_MODEL, D_FF), BF16),
                            pltpu.VMEM((D_FF, D_MODEL), BF16)]),
        out_shape=jax.ShapeDtypeStruct((n_slots, HALF), jnp.int32),
        compiler_params=_params(("arbitrary",)),
        name="moe_experts",
    )(block_e, block_rows, n_valid, xs, params['w_gate'], b4(params['b_gate']), params['w_up'], b4(params['b_up']),
      params['w_down'], b4(params['b_down']))


def _combine_kernel(off_ref, cnt_ref, np_ref, x1_ref, ir_ref, pr_ref, ys_hbm, g2_ref, b2_ref, o_ref, ob_ref,
                    stage_ref, ysb_ref, wgt_ref, acc_ref, sem, *, tm, n_tiles):
    i = pl.program_id(0)
    slot = i % 2
    mk = lambda s: (lambda dst, src: pltpu.make_async_copy(ys_hbm.at[src], stage_ref.at[s, dst], sem.at[s]))

    @pl.when(i == 0)
    def _():
        stage_ref[...] = jnp.zeros_like(stage_ref)
        _region_copies(off_ref, cnt_ref, 0, 0, mk(0), True)

    col_id = _small_int_bf16(lax.broadcasted_iota(jnp.int32, (tm, ONEHOT_CHUNK), 1))
    ir = ir_ref[...]
    pr = pr_ref[...].astype(BF16)
    zero = jnp.zeros((), BF16)

    def weighted_sum(p):
        tgt = [_slot_targets(ir[:, k:k + 1], ir[:, TOP_K + k:TOP_K + k + 1], p) for k in range(TOP_K)]
        for c in range(N_EXPERTS * REGION_CAP // ONEHOT_CHUNK):
            cols = slice(c * ONEHOT_CHUNK, (c + 1) * ONEHOT_CHUNK)
            wgt = zero
            for k in reversed(range(TOP_K)):
                wgt = jnp.where(col_id == _small_int_bf16(tgt[k] - c * ONEHOT_CHUNK), pr[:, k:k + 1], wgt)
            wgt_ref[:, cols] = wgt
            ysb_ref[cols, :] = _unpack_bf16_pairs(stage_ref[slot, cols, :])
        return _dot(wgt_ref[...], ysb_ref[...])

    _region_copies(off_ref, cnt_ref, i, 0, mk(slot), False)

    @pl.when(i + 1 < n_tiles)
    def _():
        _region_copies(off_ref, cnt_ref, i + 1, 0, mk(1 - slot), True)

    acc_ref[...] = weighted_sum(0)

    def extra_pass(p, carry):
        _region_copies(off_ref, cnt_ref, i, p, mk(slot), True)
        _region_copies(off_ref, cnt_ref, i, p, mk(slot), False)
        acc_ref[...] += weighted_sum(p)
        return carry

    lax.fori_loop(1, np_ref[i], extra_pass, 0)
    out = _layer_norm(DN_ALPHA * x1_ref[...] + acc_ref[...], g2_ref[...], b2_ref[...])
    o_ref[...] = out
    ob_ref[...] = out.astype(BF16)


def _combine(x1, idx_rank, probs, ys, off, cnt8, n_pass, p, tm):
    t = x1.shape[0]
    row = lambda a: a.reshape(1, D_MODEL).astype(F32)
    full = lambda shp: pl.BlockSpec(shp, lambda i, *_: (0,) * len(shp))
    rowb = lambda w: pl.BlockSpec((tm, w), lambda i, *_: (i, 0))
    return pl.pallas_call(
        functools.partial(_combine_kernel, tm=tm, n_tiles=t // tm),
        grid_spec=pltpu.PrefetchScalarGridSpec(
            num_scalar_prefetch=3,
            grid=(t // tm,),
            in_specs=[rowb(D_MODEL), rowb(LANES), rowb(LANES), pl.BlockSpec(memory_space=pl.ANY),
                      full((1, D_MODEL)), full((1, D_MODEL))],
            out_specs=[rowb(D_MODEL), rowb(D_MODEL)],
            scratch_shapes=[pltpu.VMEM((2, N_EXPERTS * REGION_CAP, HALF), jnp.int32),
                            pltpu.VMEM((N_EXPERTS * REGION_CAP, D_MODEL), BF16),
                            pltpu.VMEM((tm, N_EXPERTS * REGION_CAP), BF16),
                            pltpu.VMEM((tm, D_MODEL), F32),
                            pltpu.SemaphoreType.DMA((2,))]),
        out_shape=[jax.ShapeDtypeStruct((t, D_MODEL), F32), jax.ShapeDtypeStruct((t, D_MODEL), BF16)],
        compiler_params=_params(("arbitrary",)),
        name="moe_combine_ln2",
    )(off, cnt8, n_pass, x1, idx_rank, probs, ys, row(p['ln2_g']), row(p['ln2_b']))


def _moe(x1, x1b, idx_rank, irt, probs, tile_counts, p, params, l, bm, tm):
    t = x1.shape[0]
    n_tiles = t // tm
    cnt = tile_counts[:, 0, :N_EXPERTS]
    cnt8 = (cnt + 7) // 8 * 8
    seg = jnp.sum(cnt8, axis=0)
    padded = (seg + bm - 1) // bm * bm
    pend = jnp.cumsum(padded)
    pstart = pend - padded
    off = pstart[None, :] + jnp.cumsum(cnt8, axis=0) - cnt8
    n_pass = jnp.maximum((jnp.max(cnt, axis=1) + REGION_CAP - 1) // REGION_CAP, 1).astype(jnp.int32)
    nb = -(-(t * TOP_K + 7 * n_tiles * N_EXPERTS) // bm) + N_EXPERTS
    n_valid = (pend[-1] // bm).astype(jnp.int32).reshape(1)
    blocks = jnp.arange(nb, dtype=jnp.int32)
    block_e = jnp.minimum(jnp.sum((pend // bm)[None, :] <= blocks[:, None], axis=1), N_EXPERTS - 1).astype(jnp.int32)
    block_rows = jnp.clip(seg[block_e] - (blocks * bm - pstart[block_e]), 0, bm).astype(jnp.int32)
    off = off.astype(jnp.int32).reshape(-1)
    cnt8 = cnt8.astype(jnp.int32).reshape(-1)
    pad = jnp.concatenate([pstart + seg, padded - seg]).astype(jnp.int32)
    xs = _dispatch(x1b, irt, off, cnt8, n_pass, pad, n_valid, nb, bm, tm)
    ys = _experts(xs, block_e, block_rows, n_valid, params, l, bm)
    return _combine(x1, idx_rank, probs, ys, off, cnt8, n_pass, p, tm)


_EXPERT_WEIGHTS = ('w_gate', 'b_gate', 'w_up', 'b_up', 'w_down', 'b_down')


def _layer(x, xb, params, l, b, s, cfg):
    p = {k: v[l] for k, v in params.items() if k not in _EXPERT_WEIGHTS}
    w_in = p['w_in']
    o = IN_OFFS
    pad = jnp.zeros((D_MODEL, LANES - 2 * ML_HEADS), w_in.dtype)
    w_ret = w_in[:, o[0]:o[4]].astype(BF16)
    w_ml = jnp.concatenate([w_in[:, o[4]:o[5]], w_in[:, o[7]:o[8]]], axis=1).astype(BF16)
    w_if = jnp.concatenate([w_in[:, o[5]:o[7]], pad], axis=1).astype(BF16)
    w_gates = w_in[:, o[11]:o[12]].astype(BF16)
    mm = functools.partial(_matmul, tm=cfg['mm_tm'], tn=cfg['mm_tn'])
    ret_in = mm(xb, w_ret, BF16)
    ml_in = mm(xb, w_ml, BF16)
    if_in = mm(xb, w_if, F32)
    gates = mm(xb, w_gates, BF16)
    y_ret = _retention(ret_in, p['ret_gn'], b, s, cfg['seq_rows']).reshape(b * s, MIX_W)
    y_ml = _mlstm(ml_in, if_in, p, b, s, cfg['seq_rows']).reshape(b * s, MIX_W)
    outs, lses = [], []
    wq = ATT_HEADS * ATT_DK
    wv = ATT_HEADS * ATT_DV
    for g in range(ATT_GROUPS):
        assert ATT_DK == 64
        w_att = jnp.concatenate([w_in[:, o[8] + g * wq:o[8] + (g + 1) * wq] * (ATT_DK ** -0.5),
                                 w_in[:, o[9] + g * wq:o[9] + (g + 1) * wq],
                                 w_in[:, o[10] + g * wv:o[10] + (g + 1) * wv]], axis=1).astype(BF16)
        og, lg = _dilated_group(_matmul_strided(xb, w_att, ATT_PATTERNS[g][1], cfg['mm_tm']), g, b, s, cfg['att_nq'])
        outs.append(og)
        lses.append(lg)
    x1, x1b, idx_rank, irt, probs, tile_counts = _merge(x, y_ret, y_ml, outs, lses, gates, p, cfg['merge_tm'])
    return _moe(x1, x1b, idx_rank, irt, probs, tile_counts, p, params, l, cfg['moe_bm'], ROUTE_TM)


CFG = dict(mm_tm=1024, mm_tn=1024, seq_rows=512, att_nq=4, merge_tm=512, moe_bm=512)

_PARAM_NAMES = ('w_in', 'ret_gn', 'ml_conv_w', 'ml_conv_b', 'ml_wq', 'ml_wk', 'ml_wv', 'ml_bi', 'ml_bf', 'ml_gn',
                'ml_skip', 'w_branch', 'w_out', 'ln1_g', 'ln1_b', 'w_router', 'b_router', 'w_gate', 'b_gate',
                'w_up', 'b_up', 'w_down', 'b_down', 'ln2_g', 'ln2_b')


def _forward(x, params, cfg):
    b, s, d = x.shape
    xf = x.reshape(b * s, d).astype(F32)
    xb = xf.astype(BF16)
    for l in range(DEPTH):
        xf, xb = _layer(xf, xb, params, l, b, s, cfg)
    return xf.reshape(b, s, d).astype(x.dtype)


def kernel(x, w_in, ret_gn, ml_conv_w, ml_conv_b, ml_wq, ml_wk, ml_wv, ml_bi, ml_bf, ml_gn, ml_skip, w_branch, w_out, ln1_g, ln1_b, w_router, b_router, w_gate, b_gate, w_up, b_up, w_down, b_down, ln2_g, ln2_b):
    params = dict(zip(_PARAM_NAMES, (w_in, ret_gn, ml_conv_w, ml_conv_b, ml_wq, ml_wk, ml_wv, ml_bi, ml_bf, ml_gn,
                                     ml_skip, w_branch, w_out, ln1_g, ln1_b, w_router, b_router, w_gate, b_gate,
                                     w_up, b_up, w_down, b_down, ln2_g, ln2_b)))
    return _forward(x, params, CFG)
```

```python
import functools

import jax
import jax.numpy as jnp
import numpy as np
from jax import lax
from jax.experimental import pallas as pl
from jax.experimental.pallas import tpu as pltpu

F32 = jnp.float32
BF16 = jnp.bfloat16

D_MODEL = 1024
DEPTH = 2
MIX_W = D_MODEL // 2
N_BRANCHES = 3
RET_HEADS = 4
RET_DV = MIX_W // RET_HEADS
RET_DK = RET_DV // 2
ML_HEADS = 4
ML_DH = MIX_W // ML_HEADS
ML_CONV = 4
ML_QK_BLOCK = 4
ML_HEAD_GROUP = 1
ATT_PATTERNS = ((128, 1), (512, 4), (2048, 16))
ATT_GROUPS = len(ATT_PATTERNS)
ATT_HEADS = 4
ATT_DV = MIX_W // ATT_HEADS
ATT_DK = ATT_DV // 2
N_EXPERTS = 32
TOP_K = 4
D_FF = D_MODEL
SWIGLU_LIMIT = 7.0
SWIGLU_ALPHA = 1.702
DN_ALPHA = (2.0 * DEPTH) ** 0.25
EPS = 1e-5

CHUNK = 128
LANES = 128
NEG = -1e30
VMEM_LIMIT = 56 * 1024 * 1024

IN_SIZES = (RET_HEADS * RET_DK, RET_HEADS * RET_DK, MIX_W, MIX_W,
            MIX_W, ML_HEADS, ML_HEADS, MIX_W,
            ATT_GROUPS * ATT_HEADS * ATT_DK, ATT_GROUPS * ATT_HEADS * ATT_DK, ATT_GROUPS * ATT_HEADS * ATT_DV,
            N_BRANCHES * D_MODEL)
IN_OFFS = tuple(int(v) for v in np.cumsum((0,) + IN_SIZES))


def _params(sem):
    return pltpu.CompilerParams(dimension_semantics=sem, vmem_limit_bytes=VMEM_LIMIT)


def _dot(a, b):
    return jnp.dot(a, b, preferred_element_type=F32)


def _dot_nt(a, b):
    return lax.dot_general(a, b, (((1,), (1,)), ((), ())), preferred_element_type=F32)


def _dot_tn(a, b):
    return lax.dot_general(a, b, (((0,), (0,)), ((), ())), preferred_element_type=F32)


def _sigmoid(x):
    return 1.0 / (1.0 + jnp.exp(-x))


def _head_norm(o, gn):
    mu = jnp.mean(o, axis=-1, keepdims=True)
    oc = o - mu
    var = jnp.mean(oc * oc, axis=-1, keepdims=True)
    return oc * lax.rsqrt(var + EPS) * gn


def _ret_kernel(x_ref, w_ref, dm_ref, qd_ref, kd_ref, cd_ref, gn_ref, o_ref, st_ref, blk_ref, *, nc, nb):
    @pl.when(pl.program_id(0) == 0)
    def _():
        st_ref[...] = jnp.zeros_like(st_ref)

    for bi in range(nb):
        blk_ref[bi] = _dot(x_ref[bi], w_ref[...]).astype(blk_ref.dtype)

    hq = RET_HEADS * RET_DK
    state = [st_ref[j] for j in range(nb * RET_HEADS)]
    for c in range(nc):
        rows = pl.ds(c * CHUNK, CHUNK)
        chains = [(h, bi) for h in range(RET_HEADS) for bi in range(nb)]
        first = []
        for h, bi in chains:
            q = blk_ref[bi, rows, h * RET_DK:(h + 1) * RET_DK]
            k = blk_ref[bi, rows, hq + h * RET_DK:hq + (h + 1) * RET_DK]
            v = blk_ref[bi, rows, 2 * hq + h * RET_DV:2 * hq + (h + 1) * RET_DV]
            st = state[bi * RET_HEADS + h]
            first.append((v, _dot_nt(q, k), _dot(q, st.astype(BF16)),
                          _dot_tn(k, (v.astype(F32) * kd_ref[h]).astype(BF16))))
        for (h, bi), (v, s_qk, q_st, kv) in zip(chains, first):
            o = _dot((s_qk * dm_ref[h]).astype(BF16), v) + q_st * qd_ref[h]
            state[bi * RET_HEADS + h] = cd_ref[h] * state[bi * RET_HEADS + h] + kv
            g = blk_ref[bi, rows, 2 * hq + MIX_W + h * RET_DV:2 * hq + MIX_W + (h + 1) * RET_DV].astype(F32)
            y = _head_norm(o, gn_ref[:, h * RET_DV:(h + 1) * RET_DV]) * (g * _sigmoid(g))
            o_ref[bi, rows, h * RET_DV:(h + 1) * RET_DV] = y.astype(o_ref.dtype)
    for j in range(nb * RET_HEADS):
        st_ref[j] = state[j]


def _retention(xb, w_ret, ret_gn, b, s, rows):
    nc = rows // CHUNK
    scale = RET_DK ** -0.5
    log_gamma = jnp.log1p(-jnp.exp2(-5.0 - jnp.arange(RET_HEADS, dtype=F32)))
    pos = jnp.arange(CHUNK, dtype=F32)
    diff = pos[:, None] - pos[None, :]
    dm = jnp.where(diff >= 0, jnp.exp(log_gamma[:, None, None] * jnp.maximum(diff, 0.0)), 0.0) * scale
    kd = jnp.broadcast_to(jnp.exp(log_gamma[:, None] * (CHUNK - 1.0 - pos))[:, :, None], (RET_HEADS, CHUNK, RET_DV))
    qd = jnp.broadcast_to((jnp.exp(log_gamma[:, None] * (pos + 1.0)) * scale)[:, :, None], (RET_HEADS, CHUNK, RET_DV))
    cd = jnp.broadcast_to(jnp.exp(log_gamma * CHUNK)[:, None, None], (RET_HEADS, 1, RET_DV))
    dmod, w_in = w_ret.shape
    full = lambda shp: pl.BlockSpec(shp, lambda si: (0,) * len(shp))
    return pl.pallas_call(
        functools.partial(_ret_kernel, nc=nc, nb=b),
        grid=(s // rows,),
        in_specs=[pl.BlockSpec((b, rows, dmod), lambda si: (0, si, 0)), full((dmod, w_in)),
                  full((RET_HEADS, CHUNK, CHUNK)), full((RET_HEADS, CHUNK, RET_DV)),
                  full((RET_HEADS, CHUNK, RET_DV)), full((RET_HEADS, 1, RET_DV)), full((1, MIX_W))],
        out_specs=pl.BlockSpec((b, rows, MIX_W), lambda si: (0, si, 0)),
        out_shape=jax.ShapeDtypeStruct((b, s, MIX_W), BF16),
        scratch_shapes=[pltpu.VMEM((b * RET_HEADS, RET_DK, RET_DV), F32), pltpu.VMEM((b, rows, w_in), BF16)],
        compiler_params=_params(("arbitrary",)),
        name="retention",
    )(xb.reshape(b, s, dmod), w_ret, dm, qd, kd, cd, ret_gn.reshape(1, MIX_W).astype(F32))


def _log_sigmoid(x):
    return jnp.minimum(x, 0.0) - jnp.log(1.0 + jnp.exp(-jnp.abs(x)))


def _ml_kernel(x_ref, wml_ref, wif_ref, cw_ref, cb_ref, bq_ref, bk_ref, bv_ref, gb_ref, gn_ref, sk_ref,
               o_ref, c_ref, n_ref, m_ref, halo_ref, ml_ref, if_ref, *, nc, nb):
    rows_total = nc * CHUNK

    @pl.when(pl.program_id(0) == 0)
    def _():
        c_ref[...] = jnp.zeros_like(c_ref)
        n_ref[...] = jnp.zeros_like(n_ref)
        m_ref[...] = jnp.zeros_like(m_ref)
        halo_ref[...] = jnp.zeros_like(halo_ref)

    for bi in range(nb):
        ml_ref[bi] = _dot(x_ref[bi], wml_ref[...]).astype(ml_ref.dtype)
        if_ref[bi] = _dot(x_ref[bi], wif_ref[...])

    x_all, mc_all = [], []
    for bi in range(nb):
        xa = ml_ref[bi, :, 0:MIX_W].astype(F32)
        xf = jnp.concatenate([halo_ref[bi], xa], axis=0)
        acc = jnp.broadcast_to(cb_ref[...], (rows_total, MIX_W))
        for j in range(ML_CONV):
            sh = ML_CONV - 1 - j
            xs = xf if sh == 0 else pltpu.roll(xf, sh, 0)
            acc = acc + xs[8:8 + rows_total] * cw_ref[j:j + 1, :]
        halo_ref[bi] = xa[rows_total - 8:rows_total]
        x_all.append(xa)
        mc_all.append(acc * _sigmoid(acc))

    ri = lax.broadcasted_iota(jnp.int32, (CHUNK, CHUNK), 0)
    ci = lax.broadcasted_iota(jnp.int32, (CHUNK, CHUNK), 1)
    tri = ri >= ci
    tril = tri.astype(F32)
    kscale = ML_DH ** -0.5
    log_kscale = float(np.log(kscale))

    c_st = [[c_ref[bi * ML_HEADS + h] for h in range(ML_HEADS)] for bi in range(nb)]
    n_st = [[n_ref[bi, h:h + 1, :] for h in range(ML_HEADS)] for bi in range(nb)]
    m_st = [[m_ref[bi, h:h + 1, 0:1] for h in range(ML_HEADS)] for bi in range(nb)]

    def run_chains(r0, gates, chains):
        for h, bi in chains:
            g_pre, cum, g_t, cum_t = gates[bi]
            cols = slice(h * ML_DH, (h + 1) * ML_DH)
            mc = mc_all[bi][r0:r0 + CHUNK, cols]
            mcb = mc.astype(BF16)
            q = _dot(mcb, bq_ref[h]).astype(BF16)
            k = _dot(mcb, bk_ref[h]).astype(BF16)
            v = _dot(x_all[bi][r0:r0 + CHUNK, cols].astype(BF16), bv_ref[h]).astype(BF16)
            cum_col = cum[:, ML_HEADS + h:ML_HEADS + h + 1]
            ig_col = g_pre[:, h:h + 1]
            cum_row = cum_t[ML_HEADS + h:ML_HEADS + h + 1, :]
            ig_row = g_t[h:h + 1, :]
            tot = cum[CHUNK - 1:CHUNK, ML_HEADS + h:ML_HEADS + h + 1]
            m_prev, c_prev, n_prev = m_st[bi][h], c_st[bi][h], n_st[bi][h]
            dlog = jnp.where(tri, cum_col + (ig_row - cum_row), NEG)
            m_inter = cum_col + m_prev
            m_q = jnp.maximum(m_inter, jnp.max(dlog, axis=-1, keepdims=True))
            w_qk = jnp.exp(dlog - (m_q - log_kscale)) * _dot_nt(q, k)
            inter = jnp.exp(m_inter - m_q)
            num = _dot(w_qk.astype(BF16), v) + inter * _dot(q, c_prev.astype(BF16))
            den = (jnp.sum(w_qk, axis=-1, keepdims=True)
                   + inter * jnp.sum(q.astype(F32) * n_prev, axis=-1, keepdims=True))
            hh = num * (1.0 / jnp.maximum(jnp.abs(den), jnp.exp(-m_q)))
            a_col = tot - cum_col + ig_col
            a_max = jnp.max(a_col, axis=0, keepdims=True)
            wa = jnp.exp(a_col - a_max) * kscale
            chunk_c = _dot_tn(k, (wa * v.astype(F32)).astype(BF16))
            chunk_n = jnp.sum(wa * k.astype(F32), axis=0, keepdims=True)
            m_new = jnp.maximum(tot + m_prev, a_max)
            s_old = jnp.exp(tot + m_prev - m_new)
            s_new = jnp.exp(a_max - m_new)
            c_st[bi][h] = s_old * c_prev + s_new * chunk_c
            n_st[bi][h] = s_old * n_prev + s_new * chunk_n
            m_st[bi][h] = m_new
            mo = ml_ref[bi, r0:r0 + CHUNK, MIX_W + h * ML_DH:MIX_W + (h + 1) * ML_DH].astype(F32)
            y = _sigmoid(mo) * (_head_norm(hh, gn_ref[:, cols]) + sk_ref[:, cols] * mc)
            o_ref[bi, r0:r0 + CHUNK, cols] = y.astype(o_ref.dtype)

    for c in range(nc):
        r0 = c * CHUNK
        gates = []
        for bi in range(nb):
            g_pre = if_ref[bi, r0:r0 + CHUNK, :] + gb_ref[...]
            cum = jnp.dot(tril, _log_sigmoid(g_pre), preferred_element_type=F32, precision=lax.Precision.HIGHEST)
            gates.append((g_pre, cum, g_pre.T, cum.T))
        for h0 in range(0, ML_HEADS, ML_HEAD_GROUP):
            run_chains(r0, gates, [(h, bi) for h in range(h0, h0 + ML_HEAD_GROUP) for bi in range(nb)])

    for bi in range(nb):
        for h in range(ML_HEADS):
            c_ref[bi * ML_HEADS + h] = c_st[bi][h]
            n_ref[bi, h:h + 1, :] = n_st[bi][h]
            m_ref[bi, h:h + 1, :] = jnp.broadcast_to(m_st[bi][h], (1, LANES))


def _block_diag(w):
    nb = w.shape[0]
    per = nb // ML_HEADS
    wh = w.reshape(ML_HEADS, per, ML_QK_BLOCK, ML_QK_BLOCK)
    eye = jnp.eye(per, dtype=w.dtype)
    bd = jnp.einsum('hncd,nm->hncmd', wh, eye)
    return bd.reshape(ML_HEADS, per * ML_QK_BLOCK, per * ML_QK_BLOCK).astype(BF16)


def _mlstm(xb, w_ml, w_if, p, b, s, rows):
    nc = rows // CHUNK
    dm = xb.shape[1]
    full = lambda shp: pl.BlockSpec(shp, lambda si: (0,) * len(shp))
    gb = jnp.zeros((1, LANES), F32).at[0, 0:ML_HEADS].set(p['ml_bi'].astype(F32)).at[0, ML_HEADS:2 * ML_HEADS].set(p['ml_bf'].astype(F32))
    row = lambda a: a.reshape(1, MIX_W).astype(F32)
    return pl.pallas_call(
        functools.partial(_ml_kernel, nc=nc, nb=b),
        grid=(s // rows,),
        in_specs=[pl.BlockSpec((b, rows, dm), lambda si: (0, si, 0)),
                  full((dm, 2 * MIX_W)), full((dm, LANES)),
                  full((ML_CONV, MIX_W)), full((1, MIX_W)),
                  full((ML_HEADS, ML_DH, ML_DH)), full((ML_HEADS, ML_DH, ML_DH)), full((ML_HEADS, ML_DH, ML_DH)),
                  full((1, LANES)), full((1, MIX_W)), full((1, MIX_W))],
        out_specs=pl.BlockSpec((b, rows, MIX_W), lambda si: (0, si, 0)),
        out_shape=jax.ShapeDtypeStruct((b, s, MIX_W), BF16),
        scratch_shapes=[pltpu.VMEM((b * ML_HEADS, ML_DH, ML_DH), F32), pltpu.VMEM((b, 8, ML_DH), F32),
                        pltpu.VMEM((b, 8, LANES), F32), pltpu.VMEM((b, 8, MIX_W), F32),
                        pltpu.VMEM((b, rows, 2 * MIX_W), BF16), pltpu.VMEM((b, rows, LANES), F32)],
        compiler_params=_params(("arbitrary",)),
        name="mlstm",
    )(xb.reshape(b, s, dm), w_ml, w_if,
      p['ml_conv_w'].astype(F32), row(p['ml_conv_b']),
      _block_diag(p['ml_wq']), _block_diag(p['ml_wk']), _block_diag(p['ml_wv']),
      gb, row(p['ml_gn']), row(p['ml_skip']))


def _att_kernel(x_ref, w_ref, bias_ref, o_ref, lse_ref, qkv_ref, *, nq):
    first = pl.program_id(2) == 0
    span = nq * CHUNK
    wq = ATT_HEADS * ATT_DK

    @pl.when(first)
    def _():
        qkv_ref[0:CHUNK, :] = jnp.zeros((CHUNK, qkv_ref.shape[1]), qkv_ref.dtype)

    qkv_ref[CHUNK:CHUNK + span, :] = _dot(x_ref[...], w_ref[...]).astype(qkv_ref.dtype)
    lane = lax.broadcasted_iota(jnp.int32, (CHUNK, LANES), 1)
    ones = jnp.ones((2 * CHUNK, ATT_DV), BF16)
    for j in range(nq):
        rows = slice(j * CHUNK, (j + 1) * CHUNK)
        kv_rows = slice(j * CHUNK, (j + 2) * CHUNK)
        lse_out = jnp.zeros((CHUNK, LANES), F32)
        for h in range(ATT_HEADS):
            dv = slice(h * ATT_DV, (h + 1) * ATT_DV)
            q = qkv_ref[CHUNK + j * CHUNK:CHUNK + (j + 1) * CHUNK, h * ATT_DK:(h + 1) * ATT_DK]
            kk = qkv_ref[kv_rows, wq + h * ATT_DK:wq + (h + 1) * ATT_DK]
            vv = qkv_ref[kv_rows, 2 * wq + h * ATT_DV:2 * wq + (h + 1) * ATT_DV]
            bias = bias_ref[jnp.where(first, 0, 1), h] if j == 0 else bias_ref[1, h]
            s = _dot_nt(q, kk) + bias
            m = jnp.max(s, axis=-1, keepdims=True)
            p = jnp.exp(s - m).astype(BF16)
            oa = _dot(p, jnp.concatenate([vv, ones], axis=1))
            l = oa[:, ATT_DV:2 * ATT_DV]
            o_ref[rows, dv] = (oa[:, 0:ATT_DV] * (1.0 / l)).astype(o_ref.dtype)
            lse_out = jnp.where(lane == h, m + jnp.log(l), lse_out)
        lse_ref[rows, :] = lse_out
    qkv_ref[0:CHUNK, :] = qkv_ref[span:span + CHUNK, :]


def _att_bias(window, dilation, slopes):
    wb = window // dilation
    qi = jnp.arange(wb)[:, None]
    kj = jnp.arange(2 * wb)[None, :]
    delta = qi + wb - kj
    band = (delta >= 0) & (delta <= wb)
    bias = -slopes[:, None, None] * (dilation * delta).astype(F32)
    later = jnp.where(band[None], bias, NEG)
    first = jnp.where((band & (kj >= wb))[None], bias, NEG)
    return jnp.stack([first, later]).astype(F32)


def _dilated_group(xb, w_qkv, g, b, s, nq):
    window, d = ATT_PATTERNS[g]
    assert window // d == CHUNK
    l_sub = s // d
    assert l_sub % CHUNK == 0
    n_blk = l_sub // CHUNK
    slopes = jnp.exp2(-8.0 * jnp.arange(1, ATT_GROUPS * ATT_HEADS + 1, dtype=F32) / (ATT_GROUPS * ATT_HEADS))
    bias = _att_bias(window, d, slopes.reshape(ATT_GROUPS, ATT_HEADS)[g])
    wq = ATT_HEADS * ATT_DK
    wv = ATT_HEADS * ATT_DV
    per_tok = 2 * wq + wv
    dm = xb.shape[1]
    x = xb.reshape(b, l_sub, d * dm)
    nq = min(nq, n_blk)
    assert n_blk % nq == 0
    span = nq * CHUNK
    o, lse = pl.pallas_call(
        functools.partial(_att_kernel, nq=nq),
        grid=(b, d, n_blk // nq),
        in_specs=[pl.BlockSpec((None, span, dm), lambda bi, r, n: (bi, n, r)),
                  pl.BlockSpec((dm, per_tok), lambda bi, r, n: (0, 0)),
                  pl.BlockSpec((2, ATT_HEADS, CHUNK, 2 * CHUNK), lambda bi, r, n: (0, 0, 0, 0))],
        out_specs=[pl.BlockSpec((None, span, wv), lambda bi, r, n: (bi, n, r)),
                   pl.BlockSpec((None, span, LANES), lambda bi, r, n: (bi, n, r))],
        out_shape=[jax.ShapeDtypeStruct((b, l_sub, d * wv), BF16),
                   jax.ShapeDtypeStruct((b, l_sub, d * LANES), F32)],
        scratch_shapes=[pltpu.VMEM((CHUNK + span, per_tok), BF16)],
        compiler_params=_params(("arbitrary", "arbitrary", "arbitrary")),
        name=f"dilated_attn_g{g}",
    )(x, w_qkv, bias)
    return o.reshape(b * l_sub, d * wv), lse.reshape(b * l_sub, d * LANES)


def _layer_norm(z, g, b):
    mu = jnp.mean(z, axis=-1, keepdims=True)
    zc = z - mu
    var = jnp.mean(zc * zc, axis=-1, keepdims=True)
    return zc * lax.rsqrt(var + EPS) * g + b


def _merge_kernel(x_ref, yr_ref, ym_ref, o0_ref, o1_ref, o2_ref, l0_ref, l1_ref, l2_ref, xb_ref,
                  wg_ref, wb_ref, wo_ref, g1_ref, b1_ref, wrh_ref, wrl_ref, br_ref,
                  x1_ref, x1b_ref, ir_ref, irt_ref, pr_ref, cnt_ref, og_ref, lg_ref, *, tm):
    for g, (src_o, src_l) in enumerate(((o0_ref, l0_ref), (o1_ref, l1_ref), (o2_ref, l2_ref))):
        d = ATT_PATTERNS[g][1]
        n = tm // d
        for r in range(d):
            rows = pl.ds(r, n, stride=d) if d > 1 else pl.ds(0, n)
            for h in range(ATT_HEADS):
                og_ref[g, h, rows, :] = src_o[:, r * MIX_W + h * ATT_DV:r * MIX_W + (h + 1) * ATT_DV].astype(F32)
            lg_ref[g, rows, :] = src_l[:, r * LANES:(r + 1) * LANES]

    l0, l1, l2 = lg_ref[0], lg_ref[1], lg_ref[2]
    lm = jnp.maximum(jnp.maximum(l0, l1), l2)
    e0, e1, e2 = jnp.exp(l0 - lm), jnp.exp(l1 - lm), jnp.exp(l2 - lm)
    inv = 1.0 / (e0 + e1 + e2)
    parts = []
    for h in range(ATT_HEADS):
        dv = slice(h * ATT_DV, (h + 1) * ATT_DV)
        hs = slice(h, h + 1)
        parts.append((e0[:, hs] * og_ref[0, h] + e1[:, hs] * og_ref[1, h] + e2[:, hs] * og_ref[2, h]) * inv[:, hs])
    y_att = jnp.concatenate(parts, axis=-1).astype(BF16)

    xb = xb_ref[...]
    gate = lambda j: _sigmoid(_dot(xb, wg_ref[:, j * D_MODEL:(j + 1) * D_MODEL]))
    merged = gate(0) * _dot(yr_ref[...], wb_ref[0])
    merged = merged + gate(1) * _dot(ym_ref[...], wb_ref[1])
    merged = merged + gate(2) * _dot(y_att, wb_ref[2])
    z = DN_ALPHA * x_ref[...] + _dot(merged.astype(BF16), wo_ref[...])
    x1 = _layer_norm(z, g1_ref[...], b1_ref[...])
    x1_ref[...] = x1
    x1b_ref[...] = x1.astype(BF16)

    xh = x1.astype(BF16)
    xl = (x1 - xh.astype(F32)).astype(BF16)
    logits = _dot(xh, wrh_ref[...]) + _dot(xl, wrh_ref[...]) + _dot(xh, wrl_ref[...]) + br_ref[...]
    lane = lax.broadcasted_iota(jnp.int32, (tm, LANES), 1)
    work = jnp.where(lane < N_EXPERTS, logits, -jnp.inf)
    vals, idxs = [], []
    for _ in range(TOP_K):
        m = jnp.max(work, axis=-1, keepdims=True)
        idx = jnp.min(jnp.where(work == m, lane, LANES), axis=-1, keepdims=True)
        vals.append(m)
        idxs.append(idx)
        work = jnp.where(lane == idx, -jnp.inf, work)
    es = [jnp.exp(v - vals[0]) for v in vals]
    tot = es[0] + es[1] + es[2] + es[3]
    pr = jnp.zeros((tm, LANES), F32)
    for k in range(TOP_K):
        pr = jnp.where(lane == k, es[k] / tot, pr)
    pr_ref[...] = pr

    onehot = jnp.zeros((tm, LANES), F32)
    for k in range(TOP_K):
        onehot = onehot + (lane == idxs[k]).astype(F32)
    ri = lax.broadcasted_iota(jnp.int32, (tm, tm), 0)
    ci = lax.broadcasted_iota(jnp.int32, (tm, tm), 1)
    same_tile = (ri // ROUTE_TM) == (ci // ROUTE_TM)
    before = _dot(((ri > ci) & same_tile).astype(BF16), onehot.astype(BF16))
    ir = jnp.zeros((tm, LANES), F32)
    for k in range(TOP_K):
        rank = jnp.sum(jnp.where(lane == idxs[k], before, 0.0), axis=-1, keepdims=True)
        ir = jnp.where(lane == k, idxs[k].astype(F32), ir)
        ir = jnp.where(lane == TOP_K + k, rank, ir)
    ir_ref[...] = ir.astype(jnp.int32)
    irt_ref[...] = ir.T[0:8, :]
    for j in range(tm // ROUTE_TM):
        cnt_j = jnp.sum(onehot[j * ROUTE_TM:(j + 1) * ROUTE_TM], axis=0, keepdims=True)
        cnt_ref[j] = jnp.broadcast_to(cnt_j, (8, LANES)).astype(jnp.int32)


def _merge(x, xb, y_ret, y_ml, outs, lses, w_gates, p, tm):
    t = x.shape[0]
    tm = min(tm, t)
    rowb = lambda w: pl.BlockSpec((tm, w), lambda i: (i, 0))
    full = lambda shp: pl.BlockSpec(shp, lambda i: (0,) * len(shp))
    wr = jnp.zeros((D_MODEL, LANES), F32).at[:, :N_EXPERTS].set(p['w_router'].astype(F32))
    wrh = wr.astype(BF16)
    wrl = (wr - wrh.astype(F32)).astype(BF16)
    br = jnp.zeros((1, LANES), F32).at[0, :N_EXPERTS].set(p['b_router'].astype(F32))
    row = lambda a: a.reshape(1, D_MODEL).astype(F32)
    return pl.pallas_call(
        functools.partial(_merge_kernel, tm=tm),
        grid=(t // tm,),
        in_specs=[rowb(D_MODEL), rowb(MIX_W), rowb(MIX_W)]
                 + [pl.BlockSpec((tm // d, d * MIX_W), lambda i: (i, 0)) for _, d in ATT_PATTERNS]
                 + [pl.BlockSpec((tm // d, d * LANES), lambda i: (i, 0)) for _, d in ATT_PATTERNS]
                 + [rowb(D_MODEL), full((D_MODEL, N_BRANCHES * D_MODEL)),
                  full((N_BRANCHES, MIX_W, D_MODEL)), full((D_MODEL, D_MODEL)),
                  full((1, D_MODEL)), full((1, D_MODEL)),
                  full((D_MODEL, LANES)), full((D_MODEL, LANES)), full((1, LANES))],
        out_specs=[rowb(D_MODEL), rowb(D_MODEL), rowb(LANES), pl.BlockSpec((8, tm), lambda i: (0, i)), rowb(LANES),
                   pl.BlockSpec((tm // ROUTE_TM, 8, LANES), lambda i: (i, 0, 0))],
        out_shape=[jax.ShapeDtypeStruct((t, D_MODEL), F32),
                   jax.ShapeDtypeStruct((t, D_MODEL), BF16),
                   jax.ShapeDtypeStruct((t, LANES), jnp.int32),
                   jax.ShapeDtypeStruct((8, t), F32),
                   jax.ShapeDtypeStruct((t, LANES), F32),
                   jax.ShapeDtypeStruct((t // ROUTE_TM, 8, LANES), jnp.int32)],
        scratch_shapes=[pltpu.VMEM((ATT_GROUPS, ATT_HEADS, tm, ATT_DV), F32),
                        pltpu.VMEM((ATT_GROUPS, tm, LANES), F32)],
        compiler_params=_params(("arbitrary",)),
        name="merge_ln1_router",
    )(x, y_ret, y_ml, outs[0], outs[1], outs[2], lses[0], lses[1], lses[2], xb, w_gates,
      p['w_branch'].astype(BF16), p['w_out'].astype(BF16), row(p['ln1_g']), row(p['ln1_b']), wrh, wrl, br)


ROUTE_TM = 256
REGION_CAP = 64
REGION_PIECES = (64, 32, 16, 8)
PAD_PIECES = (256, 128, 64, 32, 16, 8)
ONEHOT_CHUNK = 256
HALF = D_MODEL // 2
HI_MASK = -65536


def _pack_bf16_pairs(y):
    lo = lax.shift_right_logical(lax.bitcast_convert_type(y[:, :HALF], jnp.int32), 16)
    hi = lax.bitcast_convert_type(y[:, HALF:], jnp.int32) & HI_MASK
    return lo | hi


def _unpack_bf16_pairs(w):
    lo = lax.bitcast_convert_type(lax.shift_left(w, 16), F32)
    hi = lax.bitcast_convert_type(w & HI_MASK, F32)
    return jnp.concatenate([lo, hi], axis=1).astype(BF16)


def _row_pieces(n, pieces, fn):
    for size in pieces:
        if size == pieces[0]:
            cond, pos = n >= size, 0
        else:
            cond, pos = (n & size) != 0, pl.multiple_of(n & (-2 * size), 8)

        @pl.when(cond)
        def _(pos=pos, size=size):
            fn(pos, size)


def _region_copies(off_ref, cnt_ref, tile, p, make_copy, start):
    for e in range(N_EXPERTS):
        n = jnp.clip(cnt_ref[tile * N_EXPERTS + e] - p * REGION_CAP, 0, REGION_CAP)
        base = off_ref[tile * N_EXPERTS + e] + p * REGION_CAP

        def piece(pos, size, e=e, base=base):
            cp = make_copy(pl.ds(e * REGION_CAP + pos, size), pl.ds(pl.multiple_of(base + pos, 8), size))
            if start:
                cp.start()
            else:
                cp.wait()

        _row_pieces(n, REGION_PIECES, piece)


def _slot_targets(idx, lrank, p):
    lr = lrank - p * REGION_CAP
    return jnp.where((lr >= 0) & (lr < REGION_CAP), idx * REGION_CAP + lr, -1)


def _small_int_bf16(v):
    return v.astype(F32).astype(BF16)


def _dispatch_kernel(off_ref, cnt_ref, np_ref, pad_ref, nv_ref, x_ref, irt_ref, xs_hbm,
                     stage_ref, zero_ref, sem, zsem, *, tm, n_tiles, bm, nb):
    i = pl.program_id(0)
    slot = i % 2
    row_id = _small_int_bf16(lax.broadcasted_iota(jnp.int32, (ONEHOT_CHUNK, tm), 0))
    meta = irt_ref[...].astype(jnp.int32)
    x = x_ref[...]
    mk = lambda s: (lambda src, dst: pltpu.make_async_copy(stage_ref.at[s, src], xs_hbm.at[dst], sem.at[s]))

    def one_pass(p, carry):
        tgt = [_slot_targets(meta[k:k + 1, :], meta[TOP_K + k:TOP_K + k + 1, :], p) for k in range(TOP_K)]
        for c in range(N_EXPERTS * REGION_CAP // ONEHOT_CHUNK):
            hit = row_id == _small_int_bf16(tgt[0] - c * ONEHOT_CHUNK)
            for k in range(1, TOP_K):
                hit = hit | (row_id == _small_int_bf16(tgt[k] - c * ONEHOT_CHUNK))
            sel = jnp.where(hit, jnp.ones((), BF16), jnp.zeros((), BF16))
            stage_ref[slot, c * ONEHOT_CHUNK:(c + 1) * ONEHOT_CHUNK, :] = _pack_bf16_pairs(_dot(sel, x))
        _region_copies(off_ref, cnt_ref, i, p, mk(slot), True)

        @pl.when(p < np_ref[i] - 1)
        def _():
            _region_copies(off_ref, cnt_ref, i, p, mk(slot), False)
        return carry

    lax.fori_loop(0, np_ref[i], one_pass, 0)

    @pl.when(i > 0)
    def _():
        _region_copies(off_ref, cnt_ref, i - 1, np_ref[jnp.maximum(i - 1, 0)] - 1, mk(1 - slot), False)

    @pl.when(i == n_tiles - 1)
    def _():
        _region_copies(off_ref, cnt_ref, i, np_ref[i] - 1, mk(slot), False)
        zero_ref[...] = jnp.zeros_like(zero_ref)
        for phase in (True, False):
            for e in range(N_EXPERTS):
                def piece(pos, size, e=e):
                    cp = pltpu.make_async_copy(zero_ref.at[pl.ds(0, size)],
                                               xs_hbm.at[pl.ds(pl.multiple_of(pad_ref[e] + pos, 8), size)], zsem)
                    cp.start() if phase else cp.wait()
                _row_pieces(pad_ref[N_EXPERTS + e], PAD_PIECES, piece)

        def zero_block(blk, carry):
            for half in range(bm // PAD_PIECES[0]):
                cp = pltpu.make_async_copy(
                    zero_ref, xs_hbm.at[pl.ds(pl.multiple_of(blk * bm + half * PAD_PIECES[0], 8), PAD_PIECES[0])], zsem)
                cp.start()
                cp.wait()
            return carry

        lax.fori_loop(nv_ref[0], nb, zero_block, 0)


def _dispatch(x1b, irt, off, cnt8, n_pass, pad, n_valid, nb, bm, tm):
    t = x1b.shape[0]
    assert bm % PAD_PIECES[0] == 0 and bm <= 2 * PAD_PIECES[0]
    return pl.pallas_call(
        functools.partial(_dispatch_kernel, tm=tm, n_tiles=t // tm, bm=bm, nb=nb),
        grid_spec=pltpu.PrefetchScalarGridSpec(
            num_scalar_prefetch=5,
            grid=(t // tm,),
            in_specs=[pl.BlockSpec((tm, D_MODEL), lambda i, *_: (i, 0)),
                      pl.BlockSpec((8, tm), lambda i, *_: (0, i))],
            out_specs=pl.BlockSpec(memory_space=pl.ANY),
            scratch_shapes=[pltpu.VMEM((2, N_EXPERTS * REGION_CAP, HALF), jnp.int32),
                            pltpu.VMEM((PAD_PIECES[0], HALF), jnp.int32),
                            pltpu.SemaphoreType.DMA((2,)), pltpu.SemaphoreType.DMA(())]),
        out_shape=jax.ShapeDtypeStruct((nb * bm, HALF), jnp.int32),
        compiler_params=_params(("arbitrary",)),
        name="moe_dispatch",
    )(off, cnt8, n_pass, pad, n_valid, x1b, irt)


def _ffn_kernel(be_ref, nr_ref, nv_ref, xs_ref, wg_ref, bg_ref, wu_ref, bu_ref, wd_ref, bd_ref, o_ref,
                wgb_ref, wub_ref, wdb_ref):
    i = pl.program_id(0)

    @pl.when(i < nv_ref[0])
    def _():
        @pl.when((i == 0) | (be_ref[i] != be_ref[jnp.maximum(i - 1, 0)]))
        def _():
            wgb_ref[...] = wg_ref[...].astype(BF16)
            wub_ref[...] = wu_ref[...].astype(BF16)
            wdb_ref[...] = wd_ref[...].astype(BF16)

        rows = lax.broadcasted_iota(jnp.int32, xs_ref.shape, 0)
        x = _unpack_bf16_pairs(jnp.where(rows < nr_ref[i], xs_ref[...], 0))
        gate = jnp.minimum(_dot(x, wgb_ref[...]) + bg_ref[...], SWIGLU_LIMIT)
        up = jnp.clip(_dot(x, wub_ref[...]) + bu_ref[...], -SWIGLU_LIMIT, SWIGLU_LIMIT)
        hid = (up + 1.0) * gate * _sigmoid(SWIGLU_ALPHA * gate)
        y = _dot(hid.astype(BF16), wdb_ref[...]) + bd_ref[...]
        o_ref[...] = _pack_bf16_pairs(y.astype(BF16).astype(F32))

    @pl.when(i >= nv_ref[0])
    def _():
        o_ref[...] = jnp.zeros_like(o_ref)


def _experts(xs, block_e, block_rows, n_valid, params, l, bm):
    n_slots = xs.shape[0]
    nb = n_slots // bm
    blk = lambda i, be, nr, nv: (jnp.minimum(i, nv[0] - 1), 0)
    oblk = lambda i, be, nr, nv: (i, 0)
    wsp = pl.BlockSpec((None, None, D_MODEL, D_FF), lambda i, be, nr, nv: (l, be[i], 0, 0))
    bsp = pl.BlockSpec((None, None, 1, D_FF), lambda i, be, nr, nv: (l, be[i], 0, 0))
    b4 = lambda a: a.reshape(DEPTH, N_EXPERTS, 1, -1)
    return pl.pallas_call(
        _ffn_kernel,
        grid_spec=pltpu.PrefetchScalarGridSpec(
            num_scalar_prefetch=3,
            grid=(nb,),
            in_specs=[pl.BlockSpec((bm, HALF), blk), wsp, bsp, wsp, bsp, wsp, bsp],
            out_specs=pl.BlockSpec((bm, HALF), oblk),
            scratch_shapes=[pltpu.VMEM((D_MODEL, D_FF), BF16), pltpu.VMEM((D_MODEL, D_FF), BF16),
                            pltpu.VMEM((D_FF, D_MODEL), BF16)]),
        out_shape=jax.ShapeDtypeStruct((n_slots, HALF), jnp.int32),
        compiler_params=_params(("arbitrary",)),
        name="moe_experts",
    )(block_e, block_rows, n_valid, xs, params['w_gate'], b4(params['b_gate']), params['w_up'], b4(params['b_up']),
      params['w_down'], b4(params['b_down']))


def _combine_kernel(off_ref, cnt_ref, np_ref, x1_ref, ir_ref, pr_ref, ys_hbm, g2_ref, b2_ref, o_ref, ob_ref,
                    stage_ref, ysb_ref, wgt_ref, acc_ref, sem, *, tm, n_tiles):
    i = pl.program_id(0)
    slot = i % 2
    mk = lambda s: (lambda dst, src: pltpu.make_async_copy(ys_hbm.at[src], stage_ref.at[s, dst], sem.at[s]))

    @pl.when(i == 0)
    def _():
        stage_ref[...] = jnp.zeros_like(stage_ref)
        _region_copies(off_ref, cnt_ref, 0, 0, mk(0), True)

    col_id = _small_int_bf16(lax.broadcasted_iota(jnp.int32, (tm, ONEHOT_CHUNK), 1))
    ir = ir_ref[...]
    pr = pr_ref[...].astype(BF16)
    zero = jnp.zeros((), BF16)

    def weighted_sum(p):
        tgt = [_slot_targets(ir[:, k:k + 1], ir[:, TOP_K + k:TOP_K + k + 1], p) for k in range(TOP_K)]
        for c in range(N_EXPERTS * REGION_CAP // ONEHOT_CHUNK):
            cols = slice(c * ONEHOT_CHUNK, (c + 1) * ONEHOT_CHUNK)
            wgt = zero
            for k in reversed(range(TOP_K)):
                wgt = jnp.where(col_id == _small_int_bf16(tgt[k] - c * ONEHOT_CHUNK), pr[:, k:k + 1], wgt)
            wgt_ref[:, cols] = wgt
            ysb_ref[cols, :] = _unpack_bf16_pairs(stage_ref[slot, cols, :])
        return _dot(wgt_ref[...], ysb_ref[...])

    _region_copies(off_ref, cnt_ref, i, 0, mk(slot), False)

    @pl.when(i + 1 < n_tiles)
    def _():
        _region_copies(off_ref, cnt_ref, i + 1, 0, mk(1 - slot), True)

    acc_ref[...] = weighted_sum(0)

    def extra_pass(p, carry):
        _region_copies(off_ref, cnt_ref, i, p, mk(slot), True)
        _region_copies(off_ref, cnt_ref, i, p, mk(slot), False)
        acc_ref[...] += weighted_sum(p)
        return carry

    lax.fori_loop(1, np_ref[i], extra_pass, 0)
    out = _layer_norm(DN_ALPHA * x1_ref[...] + acc_ref[...], g2_ref[...], b2_ref[...])
    o_ref[...] = out
    ob_ref[...] = out.astype(BF16)


def _combine(x1, idx_rank, probs, ys, off, cnt8, n_pass, p, tm):
    t = x1.shape[0]
    row = lambda a: a.reshape(1, D_MODEL).astype(F32)
    full = lambda shp: pl.BlockSpec(shp, lambda i, *_: (0,) * len(shp))
    rowb = lambda w: pl.BlockSpec((tm, w), lambda i, *_: (i, 0))
    return pl.pallas_call(
        functools.partial(_combine_kernel, tm=tm, n_tiles=t // tm),
        grid_spec=pltpu.PrefetchScalarGridSpec(
            num_scalar_prefetch=3,
            grid=(t // tm,),
            in_specs=[rowb(D_MODEL), rowb(LANES), rowb(LANES), pl.BlockSpec(memory_space=pl.ANY),
                      full((1, D_MODEL)), full((1, D_MODEL))],
            out_specs=[rowb(D_MODEL), rowb(D_MODEL)],
            scratch_shapes=[pltpu.VMEM((2, N_EXPERTS * REGION_CAP, HALF), jnp.int32),
                            pltpu.VMEM((N_EXPERTS * REGION_CAP, D_MODEL), BF16),
                            pltpu.VMEM((tm, N_EXPERTS * REGION_CAP), BF16),
                            pltpu.VMEM((tm, D_MODEL), F32),
                            pltpu.SemaphoreType.DMA((2,))]),
        out_shape=[jax.ShapeDtypeStruct((t, D_MODEL), F32), jax.ShapeDtypeStruct((t, D_MODEL), BF16)],
        compiler_params=_params(("arbitrary",)),
        name="moe_combine_ln2",
    )(off, cnt8, n_pass, x1, idx_rank, probs, ys, row(p['ln2_g']), row(p['ln2_b']))


def _moe(x1, x1b, idx_rank, irt, probs, tile_counts, p, params, l, bm, tm):
    t = x1.shape[0]
    n_tiles = t // tm
    cnt = tile_counts[:, 0, :N_EXPERTS]
    cnt8 = (cnt + 7) // 8 * 8
    seg = jnp.sum(cnt8, axis=0)
    padded = (seg + bm - 1) // bm * bm
    pend = jnp.cumsum(padded)
    pstart = pend - padded
    off = pstart[None, :] + jnp.cumsum(cnt8, axis=0) - cnt8
    n_pass = jnp.maximum((jnp.max(cnt, axis=1) + REGION_CAP - 1) // REGION_CAP, 1).astype(jnp.int32)
    nb = -(-(t * TOP_K + 7 * n_tiles * N_EXPERTS) // bm) + N_EXPERTS
    n_valid = (pend[-1] // bm).astype(jnp.int32).reshape(1)
    blocks = jnp.arange(nb, dtype=jnp.int32)
    block_e = jnp.minimum(jnp.sum((pend // bm)[None, :] <= blocks[:, None], axis=1), N_EXPERTS - 1).astype(jnp.int32)
    block_rows = jnp.clip(seg[block_e] - (blocks * bm - pstart[block_e]), 0, bm).astype(jnp.int32)
    off = off.astype(jnp.int32).reshape(-1)
    cnt8 = cnt8.astype(jnp.int32).reshape(-1)
    pad = jnp.concatenate([pstart + seg, padded - seg]).astype(jnp.int32)
    xs = _dispatch(x1b, irt, off, cnt8, n_pass, pad, n_valid, nb, bm, tm)
    ys = _experts(xs, block_e, block_rows, n_valid, params, l, bm)
    return _combine(x1, idx_rank, probs, ys, off, cnt8, n_pass, p, tm)


_EXPERT_WEIGHTS = ('w_gate', 'b_gate', 'w_up', 'b_up', 'w_down', 'b_down')


def _layer(x, xb, params, l, b, s, cfg):
    p = {k: v[l] for k, v in params.items() if k not in _EXPERT_WEIGHTS}
    w_in = p['w_in']
    o = IN_OFFS
    pad = jnp.zeros((D_MODEL, LANES - 2 * ML_HEADS), w_in.dtype)
    w_ret = w_in[:, o[0]:o[4]].astype(BF16)
    w_ml = jnp.concatenate([w_in[:, o[4]:o[5]], w_in[:, o[7]:o[8]]], axis=1).astype(BF16)
    w_if = jnp.concatenate([w_in[:, o[5]:o[7]], pad], axis=1).astype(BF16)
    w_gates = w_in[:, o[11]:o[12]].astype(BF16)
    y_ret = _retention(xb, w_ret, p['ret_gn'], b, s, cfg['seq_rows']).reshape(b * s, MIX_W)
    y_ml = _mlstm(xb, w_ml, w_if, p, b, s, cfg['seq_rows']).reshape(b * s, MIX_W)
    outs, lses = [], []
    wq = ATT_HEADS * ATT_DK
    wv = ATT_HEADS * ATT_DV
    for g in range(ATT_GROUPS):
        assert ATT_DK == 64
        w_att = jnp.concatenate([w_in[:, o[8] + g * wq:o[8] + (g + 1) * wq] * (ATT_DK ** -0.5),
                                 w_in[:, o[9] + g * wq:o[9] + (g + 1) * wq],
                                 w_in[:, o[10] + g * wv:o[10] + (g + 1) * wv]], axis=1).astype(BF16)
        og, lg = _dilated_group(xb, w_att, g, b, s, cfg['att_nq'])
        outs.append(og)
        lses.append(lg)
    x1, x1b, idx_rank, irt, probs, tile_counts = _merge(x, xb, y_ret, y_ml, outs, lses, w_gates, p, cfg['merge_tm'])
    return _moe(x1, x1b, idx_rank, irt, probs, tile_counts, p, params, l, cfg['moe_bm'], ROUTE_TM)


CFG = dict(seq_rows=512, att_nq=4, merge_tm=512, moe_bm=512)

_PARAM_NAMES = ('w_in', 'ret_gn', 'ml_conv_w', 'ml_conv_b', 'ml_wq', 'ml_wk', 'ml_wv', 'ml_bi', 'ml_bf', 'ml_gn',
                'ml_skip', 'w_branch', 'w_out', 'ln1_g', 'ln1_b', 'w_router', 'b_router', 'w_gate', 'b_gate',
                'w_up', 'b_up', 'w_down', 'b_down', 'ln2_g', 'ln2_b')


def _forward(x, params, cfg):
    b, s, d = x.shape
    xf = x.reshape(b * s, d).astype(F32)
    xb = xf.astype(BF16)
    for l in range(DEPTH):
        xf, xb = _layer(xf, xb, params, l, b, s, cfg)
    return xf.reshape(b, s, d).astype(x.dtype)


def kernel(x, w_in, ret_gn, ml_conv_w, ml_conv_b, ml_wq, ml_wk, ml_wv, ml_bi, ml_bf, ml_gn, ml_skip, w_branch, w_out, ln1_g, ln1_b, w_router, b_router, w_gate, b_gate, w_up, b_up, w_down, b_down, ln2_g, ln2_b):
    params = dict(zip(_PARAM_NAMES, (w_in, ret_gn, ml_conv_w, ml_conv_b, ml_wq, ml_wk, ml_wv, ml_bi, ml_bf, ml_gn,
                                     ml_skip, w_branch, w_out, ln1_g, ln1_b, w_router, b_router, w_gate, b_gate,
                                     w_up, b_up, w_down, b_down, ln2_g, ln2_b)))
    return _forward(x, params, CFG)
```

```python
import functools

import jax
import jax.numpy as jnp
import numpy as np
from jax import lax
from jax.experimental import pallas as pl
from jax.experimental.pallas import tpu as pltpu

F32 = jnp.float32
BF16 = jnp.bfloat16

D_MODEL = 1024
DEPTH = 2
MIX_W = D_MODEL // 2
N_BRANCHES = 3
RET_HEADS = 4
RET_DV = MIX_W // RET_HEADS
RET_DK = RET_DV // 2
ML_HEADS = 4
ML_DH = MIX_W // ML_HEADS
ML_CONV = 4
ML_QK_BLOCK = 4
ML_HEAD_GROUP = 1
ATT_PATTERNS = ((128, 1), (512, 4), (2048, 16))
ATT_GROUPS = len(ATT_PATTERNS)
ATT_HEADS = 4
ATT_DV = MIX_W // ATT_HEADS
ATT_DK = ATT_DV // 2
N_EXPERTS = 32
TOP_K = 4
D_FF = D_MODEL
SWIGLU_LIMIT = 7.0
SWIGLU_ALPHA = 1.702
DN_ALPHA = (2.0 * DEPTH) ** 0.25
EPS = 1e-5

CHUNK = 128
LANES = 128
NEG = -1e30
VMEM_LIMIT = 56 * 1024 * 1024

IN_SIZES = (RET_HEADS * RET_DK, RET_HEADS * RET_DK, MIX_W, MIX_W,
            MIX_W, ML_HEADS, ML_HEADS, MIX_W,
            ATT_GROUPS * ATT_HEADS * ATT_DK, ATT_GROUPS * ATT_HEADS * ATT_DK, ATT_GROUPS * ATT_HEADS * ATT_DV,
            N_BRANCHES * D_MODEL)
IN_OFFS = tuple(int(v) for v in np.cumsum((0,) + IN_SIZES))


def _params(sem):
    return pltpu.CompilerParams(dimension_semantics=sem, vmem_limit_bytes=VMEM_LIMIT)


def _dot(a, b):
    return jnp.dot(a, b, preferred_element_type=F32)


def _dot_nt(a, b):
    return lax.dot_general(a, b, (((1,), (1,)), ((), ())), preferred_element_type=F32)


def _dot_tn(a, b):
    return lax.dot_general(a, b, (((0,), (0,)), ((), ())), preferred_element_type=F32)


def _sigmoid(x):
    return 1.0 / (1.0 + jnp.exp(-x))


def _prep_kernel(x_ref, xb_ref, *rest, dilations):
    group_refs, cols_ref = rest[:-1], rest[-1]
    x = x_ref[...]
    xb_ref[...] = x.astype(xb_ref.dtype)
    tm, dm = x.shape
    for c in range(dm // LANES):
        cols_ref[c] = x[:, c * LANES:(c + 1) * LANES]
    for o_ref, d in zip(group_refs, dilations):
        n = tm // d
        for r in range(d):
            for c in range(dm // LANES):
                o_ref[:, r * dm + c * LANES:r * dm + (c + 1) * LANES] = (
                    cols_ref[c, pl.ds(r, n, stride=d), :].astype(o_ref.dtype))


def _prep(x, tm):
    t, dm = x.shape
    tm = min(tm, t)
    dilations = tuple(sorted({d for _, d in ATT_PATTERNS if d > 1}))
    assert t % tm == 0 and all(tm % (16 * d) == 0 for d in dilations)
    outs = pl.pallas_call(
        functools.partial(_prep_kernel, dilations=dilations),
        grid=(t // tm,),
        in_specs=[pl.BlockSpec((tm, dm), lambda i: (i, 0))],
        out_specs=[pl.BlockSpec((tm, dm), lambda i: (i, 0))]
                  + [pl.BlockSpec((tm // d, d * dm), lambda i: (i, 0)) for d in dilations],
        out_shape=[jax.ShapeDtypeStruct((t, dm), BF16)]
                  + [jax.ShapeDtypeStruct((t // d, d * dm), BF16) for d in dilations],
        scratch_shapes=[pltpu.VMEM((dm // LANES, tm, LANES), F32)],
        compiler_params=_params(("arbitrary",)),
        name="prep_bf16_groups",
    )(x)
    return outs[0], dict(zip(dilations, outs[1:]))


def _head_norm(o, gn):
    mu = jnp.mean(o, axis=-1, keepdims=True)
    oc = o - mu
    var = jnp.mean(oc * oc, axis=-1, keepdims=True)
    return oc * lax.rsqrt(var + EPS) * gn


def _ret_kernel(x_ref, w_ref, dm_ref, qd_ref, kd_ref, cd_ref, gn_ref, o_ref, st_ref, blk_ref, *, nc, nb):
    @pl.when(pl.program_id(0) == 0)
    def _():
        st_ref[...] = jnp.zeros_like(st_ref)

    for bi in range(nb):
        blk_ref[bi] = _dot(x_ref[bi], w_ref[...]).astype(blk_ref.dtype)

    hq = RET_HEADS * RET_DK
    state = [st_ref[j] for j in range(nb * RET_HEADS)]
    for c in range(nc):
        rows = pl.ds(c * CHUNK, CHUNK)
        chains = [(h, bi) for h in range(RET_HEADS) for bi in range(nb)]
        first = []
        for h, bi in chains:
            q = blk_ref[bi, rows, h * RET_DK:(h + 1) * RET_DK]
            k = blk_ref[bi, rows, hq + h * RET_DK:hq + (h + 1) * RET_DK]
            v = blk_ref[bi, rows, 2 * hq + h * RET_DV:2 * hq + (h + 1) * RET_DV]
            st = state[bi * RET_HEADS + h]
            first.append((v, _dot_nt(q, k), _dot(q, st.astype(BF16)),
                          _dot_tn(k, (v.astype(F32) * kd_ref[h]).astype(BF16))))
        for (h, bi), (v, s_qk, q_st, kv) in zip(chains, first):
            o = _dot((s_qk * dm_ref[h]).astype(BF16), v) + q_st * qd_ref[h]
            state[bi * RET_HEADS + h] = cd_ref[h] * state[bi * RET_HEADS + h] + kv
            g = blk_ref[bi, rows, 2 * hq + MIX_W + h * RET_DV:2 * hq + MIX_W + (h + 1) * RET_DV].astype(F32)
            y = _head_norm(o, gn_ref[:, h * RET_DV:(h + 1) * RET_DV]) * (g * _sigmoid(g))
            o_ref[bi, rows, h * RET_DV:(h + 1) * RET_DV] = y.astype(o_ref.dtype)
    for j in range(nb * RET_HEADS):
        st_ref[j] = state[j]


def _retention(xb, w_ret, ret_gn, b, s, rows):
    nc = rows // CHUNK
    scale = RET_DK ** -0.5
    log_gamma = jnp.log1p(-jnp.exp2(-5.0 - jnp.arange(RET_HEADS, dtype=F32)))
    pos = jnp.arange(CHUNK, dtype=F32)
    diff = pos[:, None] - pos[None, :]
    dm = jnp.where(diff >= 0, jnp.exp(log_gamma[:, None, None] * jnp.maximum(diff, 0.0)), 0.0) * scale
    kd = jnp.broadcast_to(jnp.exp(log_gamma[:, None] * (CHUNK - 1.0 - pos))[:, :, None], (RET_HEADS, CHUNK, RET_DV))
    qd = jnp.broadcast_to((jnp.exp(log_gamma[:, None] * (pos + 1.0)) * scale)[:, :, None], (RET_HEADS, CHUNK, RET_DV))
    cd = jnp.broadcast_to(jnp.exp(log_gamma * CHUNK)[:, None, None], (RET_HEADS, 1, RET_DV))
    dmod, w_in = w_ret.shape
    full = lambda shp: pl.BlockSpec(shp, lambda si: (0,) * len(shp))
    return pl.pallas_call(
        functools.partial(_ret_kernel, nc=nc, nb=b),
        grid=(s // rows,),
        in_specs=[pl.BlockSpec((b, rows, dmod), lambda si: (0, si, 0)), full((dmod, w_in)),
                  full((RET_HEADS, CHUNK, CHUNK)), full((RET_HEADS, CHUNK, RET_DV)),
                  full((RET_HEADS, CHUNK, RET_DV)), full((RET_HEADS, 1, RET_DV)), full((1, MIX_W))],
        out_specs=pl.BlockSpec((b, rows, MIX_W), lambda si: (0, si, 0)),
        out_shape=jax.ShapeDtypeStruct((b, s, MIX_W), BF16),
        scratch_shapes=[pltpu.VMEM((b * RET_HEADS, RET_DK, RET_DV), F32), pltpu.VMEM((b, rows, w_in), BF16)],
        compiler_params=_params(("arbitrary",)),
        name="retention",
    )(xb.reshape(b, s, dmod), w_ret, dm, qd, kd, cd, ret_gn.reshape(1, MIX_W).astype(F32))


def _log_sigmoid(x):
    return jnp.minimum(x, 0.0) - jnp.log(1.0 + jnp.exp(-jnp.abs(x)))


def _ml_kernel(x_ref, wml_ref, wif_ref, cw_ref, cb_ref, bq_ref, bk_ref, bv_ref, gb_ref, gn_ref, sk_ref,
               o_ref, c_ref, n_ref, m_ref, halo_ref, ml_ref, if_ref, *, nc, nb):
    rows_total = nc * CHUNK

    @pl.when(pl.program_id(0) == 0)
    def _():
        c_ref[...] = jnp.zeros_like(c_ref)
        n_ref[...] = jnp.zeros_like(n_ref)
        m_ref[...] = jnp.zeros_like(m_ref)
        halo_ref[...] = jnp.zeros_like(halo_ref)

    for bi in range(nb):
        ml_ref[bi] = _dot(x_ref[bi], wml_ref[...]).astype(ml_ref.dtype)
        if_ref[bi] = _dot(x_ref[bi], wif_ref[...])

    x_all, mc_all = [], []
    for bi in range(nb):
        xa = ml_ref[bi, :, 0:MIX_W].astype(F32)
        xf = jnp.concatenate([halo_ref[bi], xa], axis=0)
        acc = jnp.broadcast_to(cb_ref[...], (rows_total, MIX_W))
        for j in range(ML_CONV):
            sh = ML_CONV - 1 - j
            xs = xf if sh == 0 else pltpu.roll(xf, sh, 0)
            acc = acc + xs[8:8 + rows_total] * cw_ref[j:j + 1, :]
        halo_ref[bi] = xa[rows_total - 8:rows_total]
        x_all.append(xa)
        mc_all.append(acc * _sigmoid(acc))

    ri = lax.broadcasted_iota(jnp.int32, (CHUNK, CHUNK), 0)
    ci = lax.broadcasted_iota(jnp.int32, (CHUNK, CHUNK), 1)
    tri = ri >= ci
    tril = tri.astype(F32)
    kscale = ML_DH ** -0.5
    log_kscale = float(np.log(kscale))

    c_st = [[c_ref[bi * ML_HEADS + h] for h in range(ML_HEADS)] for bi in range(nb)]
    n_st = [[n_ref[bi, h:h + 1, :] for h in range(ML_HEADS)] for bi in range(nb)]
    m_st = [[m_ref[bi, h:h + 1, 0:1] for h in range(ML_HEADS)] for bi in range(nb)]

    def run_chains(r0, gates, chains):
        for h, bi in chains:
            g_pre, cum, g_t, cum_t = gates[bi]
            cols = slice(h * ML_DH, (h + 1) * ML_DH)
            mc = mc_all[bi][r0:r0 + CHUNK, cols]
            mcb = mc.astype(BF16)
            q = _dot(mcb, bq_ref[h]).astype(BF16)
            k = _dot(mcb, bk_ref[h]).astype(BF16)
            v = _dot(x_all[bi][r0:r0 + CHUNK, cols].astype(BF16), bv_ref[h]).astype(BF16)
            cum_col = cum[:, ML_HEADS + h:ML_HEADS + h + 1]
            ig_col = g_pre[:, h:h + 1]
            cum_row = cum_t[ML_HEADS + h:ML_HEADS + h + 1, :]
            ig_row = g_t[h:h + 1, :]
            tot = cum[CHUNK - 1:CHUNK, ML_HEADS + h:ML_HEADS + h + 1]
            m_prev, c_prev, n_prev = m_st[bi][h], c_st[bi][h], n_st[bi][h]
            dlog = jnp.where(tri, cum_col + (ig_row - cum_row), NEG)
            m_inter = cum_col + m_prev
            m_q = jnp.maximum(m_inter, jnp.max(dlog, axis=-1, keepdims=True))
            w_qk = jnp.exp(dlog - (m_q - log_kscale)) * _dot_nt(q, k)
            inter = jnp.exp(m_inter - m_q)
            num = _dot(w_qk.astype(BF16), v) + inter * _dot(q, c_prev.astype(BF16))
            den = (jnp.sum(w_qk, axis=-1, keepdims=True)
                   + inter * jnp.sum(q.astype(F32) * n_prev, axis=-1, keepdims=True))
            hh = num * (1.0 / jnp.maximum(jnp.abs(den), jnp.exp(-m_q)))
            a_col = tot - cum_col + ig_col
            a_max = jnp.max(a_col, axis=0, keepdims=True)
            wa = jnp.exp(a_col - a_max) * kscale
            chunk_c = _dot_tn(k, (wa * v.astype(F32)).astype(BF16))
            chunk_n = jnp.sum(wa * k.astype(F32), axis=0, keepdims=True)
            m_new = jnp.maximum(tot + m_prev, a_max)
            s_old = jnp.exp(tot + m_prev - m_new)
            s_new = jnp.exp(a_max - m_new)
            c_st[bi][h] = s_old * c_prev + s_new * chunk_c
            n_st[bi][h] = s_old * n_prev + s_new * chunk_n
            m_st[bi][h] = m_new
            mo = ml_ref[bi, r0:r0 + CHUNK, MIX_W + h * ML_DH:MIX_W + (h + 1) * ML_DH].astype(F32)
            y = _sigmoid(mo) * (_head_norm(hh, gn_ref[:, cols]) + sk_ref[:, cols] * mc)
            o_ref[bi, r0:r0 + CHUNK, cols] = y.astype(o_ref.dtype)

    for c in range(nc):
        r0 = c * CHUNK
        gates = []
        for bi in range(nb):
            g_pre = if_ref[bi, r0:r0 + CHUNK, :] + gb_ref[...]
            cum = jnp.dot(tril, _log_sigmoid(g_pre), preferred_element_type=F32, precision=lax.Precision.HIGHEST)
            gates.append((g_pre, cum, g_pre.T, cum.T))
        for h0 in range(0, ML_HEADS, ML_HEAD_GROUP):
            run_chains(r0, gates, [(h, bi) for h in range(h0, h0 + ML_HEAD_GROUP) for bi in range(nb)])

    for bi in range(nb):
        for h in range(ML_HEADS):
            c_ref[bi * ML_HEADS + h] = c_st[bi][h]
            n_ref[bi, h:h + 1, :] = n_st[bi][h]
            m_ref[bi, h:h + 1, :] = jnp.broadcast_to(m_st[bi][h], (1, LANES))


def _block_diag(w):
    nb = w.shape[0]
    per = nb // ML_HEADS
    wh = w.reshape(ML_HEADS, per, ML_QK_BLOCK, ML_QK_BLOCK)
    eye = jnp.eye(per, dtype=w.dtype)
    bd = jnp.einsum('hncd,nm->hncmd', wh, eye)
    return bd.reshape(ML_HEADS, per * ML_QK_BLOCK, per * ML_QK_BLOCK).astype(BF16)


def _mlstm(xb, w_ml, w_if, p, b, s, rows):
    nc = rows // CHUNK
    dm = xb.shape[1]
    full = lambda shp: pl.BlockSpec(shp, lambda si: (0,) * len(shp))
    gb = jnp.zeros((1, LANES), F32).at[0, 0:ML_HEADS].set(p['ml_bi'].astype(F32)).at[0, ML_HEADS:2 * ML_HEADS].set(p['ml_bf'].astype(F32))
    row = lambda a: a.reshape(1, MIX_W).astype(F32)
    return pl.pallas_call(
        functools.partial(_ml_kernel, nc=nc, nb=b),
        grid=(s // rows,),
        in_specs=[pl.BlockSpec((b, rows, dm), lambda si: (0, si, 0)),
                  full((dm, 2 * MIX_W)), full((dm, LANES)),
                  full((ML_CONV, MIX_W)), full((1, MIX_W)),
                  full((ML_HEADS, ML_DH, ML_DH)), full((ML_HEADS, ML_DH, ML_DH)), full((ML_HEADS, ML_DH, ML_DH)),
                  full((1, LANES)), full((1, MIX_W)), full((1, MIX_W))],
        out_specs=pl.BlockSpec((b, rows, MIX_W), lambda si: (0, si, 0)),
        out_shape=jax.ShapeDtypeStruct((b, s, MIX_W), BF16),
        scratch_shapes=[pltpu.VMEM((b * ML_HEADS, ML_DH, ML_DH), F32), pltpu.VMEM((b, 8, ML_DH), F32),
                        pltpu.VMEM((b, 8, LANES), F32), pltpu.VMEM((b, 8, MIX_W), F32),
                        pltpu.VMEM((b, rows, 2 * MIX_W), BF16), pltpu.VMEM((b, rows, LANES), F32)],
        compiler_params=_params(("arbitrary",)),
        name="mlstm",
    )(xb.reshape(b, s, dm), w_ml, w_if,
      p['ml_conv_w'].astype(F32), row(p['ml_conv_b']),
      _block_diag(p['ml_wq']), _block_diag(p['ml_wk']), _block_diag(p['ml_wv']),
      gb, row(p['ml_gn']), row(p['ml_skip']))


def _att_kernel(x_ref, w_ref, bias_ref, o_ref, lse_ref, qkv_ref, *, nq):
    first = pl.program_id(2) == 0
    span = nq * CHUNK
    wq = ATT_HEADS * ATT_DK

    @pl.when(first)
    def _():
        qkv_ref[0:CHUNK, :] = jnp.zeros((CHUNK, qkv_ref.shape[1]), qkv_ref.dtype)

    qkv_ref[CHUNK:CHUNK + span, :] = _dot(x_ref[...], w_ref[...]).astype(qkv_ref.dtype)
    lane = lax.broadcasted_iota(jnp.int32, (CHUNK, LANES), 1)
    ones = jnp.ones((2 * CHUNK, ATT_DV), BF16)
    for j in range(nq):
        rows = slice(j * CHUNK, (j + 1) * CHUNK)
        kv_rows = slice(j * CHUNK, (j + 2) * CHUNK)
        lse_out = jnp.zeros((CHUNK, LANES), F32)
        for h in range(ATT_HEADS):
            dv = slice(h * ATT_DV, (h + 1) * ATT_DV)
            q = qkv_ref[CHUNK + j * CHUNK:CHUNK + (j + 1) * CHUNK, h * ATT_DK:(h + 1) * ATT_DK]
            kk = qkv_ref[kv_rows, wq + h * ATT_DK:wq + (h + 1) * ATT_DK]
            vv = qkv_ref[kv_rows, 2 * wq + h * ATT_DV:2 * wq + (h + 1) * ATT_DV]
            bias = bias_ref[jnp.where(first, 0, 1), h] if j == 0 else bias_ref[1, h]
            s = _dot_nt(q, kk) + bias
            m = jnp.max(s, axis=-1, keepdims=True)
            p = jnp.exp(s - m).astype(BF16)
            oa = _dot(p, jnp.concatenate([vv, ones], axis=1))
            l = oa[:, ATT_DV:2 * ATT_DV]
            o_ref[rows, dv] = (oa[:, 0:ATT_DV] * (1.0 / l)).astype(o_ref.dtype)
            lse_out = jnp.where(lane == h, m + jnp.log(l), lse_out)
        lse_ref[rows, :] = lse_out
    qkv_ref[0:CHUNK, :] = qkv_ref[span:span + CHUNK, :]


def _att_bias(window, dilation, slopes):
    wb = window // dilation
    qi = jnp.arange(wb)[:, None]
    kj = jnp.arange(2 * wb)[None, :]
    delta = qi + wb - kj
    band = (delta >= 0) & (delta <= wb)
    bias = -slopes[:, None, None] * (dilation * delta).astype(F32)
    later = jnp.where(band[None], bias, NEG)
    first = jnp.where((band & (kj >= wb))[None], bias, NEG)
    return jnp.stack([first, later]).astype(F32)


def _dilated_group(xg, w_qkv, g, b, s, nq):
    window, d = ATT_PATTERNS[g]
    assert window // d == CHUNK
    l_sub = s // d
    assert l_sub % CHUNK == 0
    n_blk = l_sub // CHUNK
    slopes = jnp.exp2(-8.0 * jnp.arange(1, ATT_GROUPS * ATT_HEADS + 1, dtype=F32) / (ATT_GROUPS * ATT_HEADS))
    bias = _att_bias(window, d, slopes.reshape(ATT_GROUPS, ATT_HEADS)[g])
    wq = ATT_HEADS * ATT_DK
    wv = ATT_HEADS * ATT_DV
    per_tok = 2 * wq + wv
    dm = w_qkv.shape[0]
    assert xg.shape == (b * l_sub, d * dm)
    x = xg.reshape(b, l_sub, d * dm)
    nq = min(nq, n_blk)
    assert n_blk % nq == 0
    span = nq * CHUNK
    o, lse = pl.pallas_call(
        functools.partial(_att_kernel, nq=nq),
        grid=(b, d, n_blk // nq),
        in_specs=[pl.BlockSpec((None, span, dm), lambda bi, r, n: (bi, n, r)),
                  pl.BlockSpec((dm, per_tok), lambda bi, r, n: (0, 0)),
                  pl.BlockSpec((2, ATT_HEADS, CHUNK, 2 * CHUNK), lambda bi, r, n: (0, 0, 0, 0))],
        out_specs=[pl.BlockSpec((None, span, wv), lambda bi, r, n: (bi, n, r)),
                   pl.BlockSpec((None, span, LANES), lambda bi, r, n: (bi, n, r))],
        out_shape=[jax.ShapeDtypeStruct((b, l_sub, d * wv), BF16),
                   jax.ShapeDtypeStruct((b, l_sub, d * LANES), F32)],
        scratch_shapes=[pltpu.VMEM((CHUNK + span, per_tok), BF16)],
        compiler_params=_params(("arbitrary", "arbitrary", "arbitrary")),
        name=f"dilated_attn_g{g}",
    )(x, w_qkv, bias)
    return o.reshape(b * l_sub, d * wv), lse.reshape(b * l_sub, d * LANES)


def _layer_norm(z, g, b):
    mu = jnp.mean(z, axis=-1, keepdims=True)
    zc = z - mu
    var = jnp.mean(zc * zc, axis=-1, keepdims=True)
    return zc * lax.rsqrt(var + EPS) * g + b


def _merge_kernel(x_ref, yr_ref, ym_ref, o0_ref, o1_ref, o2_ref, l0_ref, l1_ref, l2_ref, xb_ref,
                  wg_ref, wb_ref, wo_ref, g1_ref, b1_ref, wrh_ref, wrl_ref, br_ref,
                  x1_ref, x1b_ref, ir_ref, irt_ref, pr_ref, cnt_ref, og_ref, lg_ref, *, tm):
    for g, (src_o, src_l) in enumerate(((o0_ref, l0_ref), (o1_ref, l1_ref), (o2_ref, l2_ref))):
        d = ATT_PATTERNS[g][1]
        n = tm // d
        for r in range(d):
            rows = pl.ds(r, n, stride=d) if d > 1 else pl.ds(0, n)
            for h in range(ATT_HEADS):
                og_ref[g, h, rows, :] = src_o[:, r * MIX_W + h * ATT_DV:r * MIX_W + (h + 1) * ATT_DV].astype(F32)
            lg_ref[g, rows, :] = src_l[:, r * LANES:(r + 1) * LANES]

    l0, l1, l2 = lg_ref[0], lg_ref[1], lg_ref[2]
    lm = jnp.maximum(jnp.maximum(l0, l1), l2)
    e0, e1, e2 = jnp.exp(l0 - lm), jnp.exp(l1 - lm), jnp.exp(l2 - lm)
    inv = 1.0 / (e0 + e1 + e2)
    parts = []
    for h in range(ATT_HEADS):
        dv = slice(h * ATT_DV, (h + 1) * ATT_DV)
        hs = slice(h, h + 1)
        parts.append((e0[:, hs] * og_ref[0, h] + e1[:, hs] * og_ref[1, h] + e2[:, hs] * og_ref[2, h]) * inv[:, hs])
    y_att = jnp.concatenate(parts, axis=-1).astype(BF16)

    xb = xb_ref[...]
    gate = lambda j: _sigmoid(_dot(xb, wg_ref[:, j * D_MODEL:(j + 1) * D_MODEL]))
    merged = gate(0) * _dot(yr_ref[...], wb_ref[0])
    merged = merged + gate(1) * _dot(ym_ref[...], wb_ref[1])
    merged = merged + gate(2) * _dot(y_att, wb_ref[2])
    z = DN_ALPHA * x_ref[...] + _dot(merged.astype(BF16), wo_ref[...])
    x1 = _layer_norm(z, g1_ref[...], b1_ref[...])
    x1_ref[...] = x1
    x1b_ref[...] = x1.astype(BF16)

    xh = x1.astype(BF16)
    xl = (x1 - xh.astype(F32)).astype(BF16)
    logits = _dot(xh, wrh_ref[...]) + _dot(xl, wrh_ref[...]) + _dot(xh, wrl_ref[...]) + br_ref[...]
    lane = lax.broadcasted_iota(jnp.int32, (tm, LANES), 1)
    work = jnp.where(lane < N_EXPERTS, logits, -jnp.inf)
    vals, idxs = [], []
    for _ in range(TOP_K):
        m = jnp.max(work, axis=-1, keepdims=True)
        idx = jnp.min(jnp.where(work == m, lane, LANES), axis=-1, keepdims=True)
        vals.append(m)
        idxs.append(idx)
        work = jnp.where(lane == idx, -jnp.inf, work)
    es = [jnp.exp(v - vals[0]) for v in vals]
    tot = es[0] + es[1] + es[2] + es[3]
    pr = jnp.zeros((tm, LANES), F32)
    for k in range(TOP_K):
        pr = jnp.where(lane == k, es[k] / tot, pr)
    pr_ref[...] = pr

    onehot = jnp.zeros((tm, LANES), F32)
    for k in range(TOP_K):
        onehot = onehot + (lane == idxs[k]).astype(F32)
    ri = lax.broadcasted_iota(jnp.int32, (tm, tm), 0)
    ci = lax.broadcasted_iota(jnp.int32, (tm, tm), 1)
    same_tile = (ri // ROUTE_TM) == (ci // ROUTE_TM)
    before = _dot(((ri > ci) & same_tile).astype(BF16), onehot.astype(BF16))
    ir = jnp.zeros((tm, LANES), F32)
    for k in range(TOP_K):
        rank = jnp.sum(jnp.where(lane == idxs[k], before, 0.0), axis=-1, keepdims=True)
        ir = jnp.where(lane == k, idxs[k].astype(F32), ir)
        ir = jnp.where(lane == TOP_K + k, rank, ir)
    ir_ref[...] = ir.astype(jnp.int32)
    irt_ref[...] = ir.T[0:8, :]
    for j in range(tm // ROUTE_TM):
        cnt_j = jnp.sum(onehot[j * ROUTE_TM:(j + 1) * ROUTE_TM], axis=0, keepdims=True)
        cnt_ref[j] = jnp.broadcast_to(cnt_j, (8, LANES)).astype(jnp.int32)


def _merge(x, xb, y_ret, y_ml, outs, lses, w_gates, p, tm):
    t = x.shape[0]
    tm = min(tm, t)
    rowb = lambda w: pl.BlockSpec((tm, w), lambda i: (i, 0))
    full = lambda shp: pl.BlockSpec(shp, lambda i: (0,) * len(shp))
    wr = jnp.zeros((D_MODEL, LANES), F32).at[:, :N_EXPERTS].set(p['w_router'].astype(F32))
    wrh = wr.astype(BF16)
    wrl = (wr - wrh.astype(F32)).astype(BF16)
    br = jnp.zeros((1, LANES), F32).at[0, :N_EXPERTS].set(p['b_router'].astype(F32))
    row = lambda a: a.reshape(1, D_MODEL).astype(F32)
    return pl.pallas_call(
        functools.partial(_merge_kernel, tm=tm),
        grid=(t // tm,),
        in_specs=[rowb(D_MODEL), rowb(MIX_W), rowb(MIX_W)]
                 + [pl.BlockSpec((tm // d, d * MIX_W), lambda i: (i, 0)) for _, d in ATT_PATTERNS]
                 + [pl.BlockSpec((tm // d, d * LANES), lambda i: (i, 0)) for _, d in ATT_PATTERNS]
                 + [rowb(D_MODEL), full((D_MODEL, N_BRANCHES * D_MODEL)),
                  full((N_BRANCHES, MIX_W, D_MODEL)), full((D_MODEL, D_MODEL)),
                  full((1, D_MODEL)), full((1, D_MODEL)),
                  full((D_MODEL, LANES)), full((D_MODEL, LANES)), full((1, LANES))],
        out_specs=[rowb(D_MODEL), rowb(D_MODEL), rowb(LANES), pl.BlockSpec((8, tm), lambda i: (0, i)), rowb(LANES),
                   pl.BlockSpec((tm // ROUTE_TM, 8, LANES), lambda i: (i, 0, 0))],
        out_shape=[jax.ShapeDtypeStruct((t, D_MODEL), F32),
                   jax.ShapeDtypeStruct((t, D_MODEL), BF16),
                   jax.ShapeDtypeStruct((t, LANES), jnp.int32),
                   jax.ShapeDtypeStruct((8, t), F32),
                   jax.ShapeDtypeStruct((t, LANES), F32),
                   jax.ShapeDtypeStruct((t // ROUTE_TM, 8, LANES), jnp.int32)],
        scratch_shapes=[pltpu.VMEM((ATT_GROUPS, ATT_HEADS, tm, ATT_DV), F32),
                        pltpu.VMEM((ATT_GROUPS, tm, LANES), F32)],
        compiler_params=_params(("arbitrary",)),
        name="merge_ln1_router",
    )(x, y_ret, y_ml, outs[0], outs[1], outs[2], lses[0], lses[1], lses[2], xb, w_gates,
      p['w_branch'].astype(BF16), p['w_out'].astype(BF16), row(p['ln1_g']), row(p['ln1_b']), wrh, wrl, br)


ROUTE_TM = 256
REGION_CAP = 64
REGION_PIECES = (64, 32, 16, 8)
PAD_PIECES = (256, 128, 64, 32, 16, 8)
ONEHOT_CHUNK = 256
HALF = D_MODEL // 2
HI_MASK = -65536


def _pack_bf16_pairs(y):
    lo = lax.shift_right_logical(lax.bitcast_convert_type(y[:, :HALF], jnp.int32), 16)
    hi = lax.bitcast_convert_type(y[:, HALF:], jnp.int32) & HI_MASK
    return lo | hi


def _unpack_bf16_pairs(w):
    lo = lax.bitcast_convert_type(lax.shift_left(w, 16), F32)
    hi = lax.bitcast_convert_type(w & HI_MASK, F32)
    return jnp.concatenate([lo, hi], axis=1).astype(BF16)


def _row_pieces(n, pieces, fn):
    for size in pieces:
        if size == pieces[0]:
            cond, pos = n >= size, 0
        else:
            cond, pos = (n & size) != 0, pl.multiple_of(n & (-2 * size), 8)

        @pl.when(cond)
        def _(pos=pos, size=size):
            fn(pos, size)


def _region_copies(off_ref, cnt_ref, tile, p, make_copy, start):
    for e in range(N_EXPERTS):
        n = jnp.clip(cnt_ref[tile * N_EXPERTS + e] - p * REGION_CAP, 0, REGION_CAP)
        base = off_ref[tile * N_EXPERTS + e] + p * REGION_CAP

        def piece(pos, size, e=e, base=base):
            cp = make_copy(pl.ds(e * REGION_CAP + pos, size), pl.ds(pl.multiple_of(base + pos, 8), size))
            if start:
                cp.start()
            else:
                cp.wait()

        _row_pieces(n, REGION_PIECES, piece)


def _slot_targets(idx, lrank, p):
    lr = lrank - p * REGION_CAP
    return jnp.where((lr >= 0) & (lr < REGION_CAP), idx * REGION_CAP + lr, -1)


def _small_int_bf16(v):
    return v.astype(F32).astype(BF16)


def _dispatch_kernel(off_ref, cnt_ref, np_ref, pad_ref, nv_ref, x_ref, irt_ref, xs_hbm,
                     stage_ref, zero_ref, sem, zsem, *, tm, n_tiles, bm, nb):
    i = pl.program_id(0)
    slot = i % 2
    row_id = _small_int_bf16(lax.broadcasted_iota(jnp.int32, (ONEHOT_CHUNK, tm), 0))
    meta = irt_ref[...].astype(jnp.int32)
    x = x_ref[...]
    mk = lambda s: (lambda src, dst: pltpu.make_async_copy(stage_ref.at[s, src], xs_hbm.at[dst], sem.at[s]))

    def one_pass(p, carry):
        tgt = [_slot_targets(meta[k:k + 1, :], meta[TOP_K + k:TOP_K + k + 1, :], p) for k in range(TOP_K)]
        for c in range(N_EXPERTS * REGION_CAP // ONEHOT_CHUNK):
            hit = row_id == _small_int_bf16(tgt[0] - c * ONEHOT_CHUNK)
            for k in range(1, TOP_K):
                hit = hit | (row_id == _small_int_bf16(tgt[k] - c * ONEHOT_CHUNK))
            sel = jnp.where(hit, jnp.ones((), BF16), jnp.zeros((), BF16))
            stage_ref[slot, c * ONEHOT_CHUNK:(c + 1) * ONEHOT_CHUNK, :] = _pack_bf16_pairs(_dot(sel, x))
        _region_copies(off_ref, cnt_ref, i, p, mk(slot), True)

        @pl.when(p < np_ref[i] - 1)
        def _():
            _region_copies(off_ref, cnt_ref, i, p, mk(slot), False)
        return carry

    lax.fori_loop(0, np_ref[i], one_pass, 0)

    @pl.when(i > 0)
    def _():
        _region_copies(off_ref, cnt_ref, i - 1, np_ref[jnp.maximum(i - 1, 0)] - 1, mk(1 - slot), False)

    @pl.when(i == n_tiles - 1)
    def _():
        _region_copies(off_ref, cnt_ref, i, np_ref[i] - 1, mk(slot), False)
        zero_ref[...] = jnp.zeros_like(zero_ref)
        for phase in (True, False):
            for e in range(N_EXPERTS):
                def piece(pos, size, e=e):
                    cp = pltpu.make_async_copy(zero_ref.at[pl.ds(0, size)],
                                               xs_hbm.at[pl.ds(pl.multiple_of(pad_ref[e] + pos, 8), size)], zsem)
                    cp.start() if phase else cp.wait()
                _row_pieces(pad_ref[N_EXPERTS + e], PAD_PIECES, piece)

        def zero_block(blk, carry):
            for half in range(bm // PAD_PIECES[0]):
                cp = pltpu.make_async_copy(
                    zero_ref, xs_hbm.at[pl.ds(pl.multiple_of(blk * bm + half * PAD_PIECES[0], 8), PAD_PIECES[0])], zsem)
                cp.start()
                cp.wait()
            return carry

        lax.fori_loop(nv_ref[0], nb, zero_block, 0)


def _dispatch(x1b, irt, off, cnt8, n_pass, pad, n_valid, nb, bm, tm):
    t = x1b.shape[0]
    assert bm % PAD_PIECES[0] == 0 and bm <= 2 * PAD_PIECES[0]
    return pl.pallas_call(
        functools.partial(_dispatch_kernel, tm=tm, n_tiles=t // tm, bm=bm, nb=nb),
        grid_spec=pltpu.PrefetchScalarGridSpec(
            num_scalar_prefetch=5,
            grid=(t // tm,),
            in_specs=[pl.BlockSpec((tm, D_MODEL), lambda i, *_: (i, 0)),
                      pl.BlockSpec((8, tm), lambda i, *_: (0, i))],
            out_specs=pl.BlockSpec(memory_space=pl.ANY),
            scratch_shapes=[pltpu.VMEM((2, N_EXPERTS * REGION_CAP, HALF), jnp.int32),
                            pltpu.VMEM((PAD_PIECES[0], HALF), jnp.int32),
                            pltpu.SemaphoreType.DMA((2,)), pltpu.SemaphoreType.DMA(())]),
        out_shape=jax.ShapeDtypeStruct((nb * bm, HALF), jnp.int32),
        compiler_params=_params(("arbitrary",)),
        name="moe_dispatch",
    )(off, cnt8, n_pass, pad, n_valid, x1b, irt)


def _ffn_kernel(be_ref, nr_ref, nv_ref, xs_ref, wg_ref, bg_ref, wu_ref, bu_ref, wd_ref, bd_ref, o_ref,
                wgb_ref, wub_ref, wdb_ref):
    i = pl.program_id(0)

    @pl.when(i < nv_ref[0])
    def _():
        @pl.when((i == 0) | (be_ref[i] != be_ref[jnp.maximum(i - 1, 0)]))
        def _():
            wgb_ref[...] = wg_ref[...].astype(BF16)
            wub_ref[...] = wu_ref[...].astype(BF16)
            wdb_ref[...] = wd_ref[...].astype(BF16)

        rows = lax.broadcasted_iota(jnp.int32, xs_ref.shape, 0)
        x = _unpack_bf16_pairs(jnp.where(rows < nr_ref[i], xs_ref[...], 0))
        gate = jnp.minimum(_dot(x, wgb_ref[...]) + bg_ref[...], SWIGLU_LIMIT)
        up = jnp.clip(_dot(x, wub_ref[...]) + bu_ref[...], -SWIGLU_LIMIT, SWIGLU_LIMIT)
        hid = (up + 1.0) * gate * _sigmoid(SWIGLU_ALPHA * gate)
        y = _dot(hid.astype(BF16), wdb_ref[...]) + bd_ref[...]
        o_ref[...] = _pack_bf16_pairs(y.astype(BF16).astype(F32))

    @pl.when(i >= nv_ref[0])
    def _():
        o_ref[...] = jnp.zeros_like(o_ref)


def _experts(xs, block_e, block_rows, n_valid, params, l, bm):
    n_slots = xs.shape[0]
    nb = n_slots // bm
    blk = lambda i, be, nr, nv: (jnp.minimum(i, nv[0] - 1), 0)
    oblk = lambda i, be, nr, nv: (i, 0)
    wsp = pl.BlockSpec((None, None, D_MODEL, D_FF), lambda i, be, nr, nv: (l, be[i], 0, 0))
    bsp = pl.BlockSpec((None, None, 1, D_FF), lambda i, be, nr, nv: (l, be[i], 0, 0))
    b4 = lambda a: a.reshape(DEPTH, N_EXPERTS, 1, -1)
    return pl.pallas_call(
        _ffn_kernel,
        grid_spec=pltpu.PrefetchScalarGridSpec(
            num_scalar_prefetch=3,
            grid=(nb,),
            in_specs=[pl.BlockSpec((bm, HALF), blk), wsp, bsp, wsp, bsp, wsp, bsp],
            out_specs=pl.BlockSpec((bm, HALF), oblk),
            scratch_shapes=[pltpu.VMEM((D_MODEL, D_FF), BF16), pltpu.VMEM((D_MODEL, D_FF), BF16),
                            pltpu.VMEM((D_FF, D_MODEL), BF16)]),
        out_shape=jax.ShapeDtypeStruct((n_slots, HALF), jnp.int32),
        compiler_params=_params(("arbitrary",)),
        name="moe_experts",
    )(block_e, block_rows, n_valid, xs, params['w_gate'], b4(params['b_gate']), params['w_up'], b4(params['b_up']),
      params['w_down'], b4(params['b_down']))


def _combine_kernel(off_ref, cnt_ref, np_ref, x1_ref, ir_ref, pr_ref, ys_hbm, g2_ref, b2_ref, o_ref,
                    stage_ref, ysb_ref, wgt_ref, acc_ref, sem, *, tm, n_tiles):
    i = pl.program_id(0)
    slot = i % 2
    mk = lambda s: (lambda dst, src: pltpu.make_async_copy(ys_hbm.at[src], stage_ref.at[s, dst], sem.at[s]))

    @pl.when(i == 0)
    def _():
        stage_ref[...] = jnp.zeros_like(stage_ref)
        _region_copies(off_ref, cnt_ref, 0, 0, mk(0), True)

    col_id = _small_int_bf16(lax.broadcasted_iota(jnp.int32, (tm, ONEHOT_CHUNK), 1))
    ir = ir_ref[...]
    pr = pr_ref[...].astype(BF16)
    zero = jnp.zeros((), BF16)

    def weighted_sum(p):
        tgt = [_slot_targets(ir[:, k:k + 1], ir[:, TOP_K + k:TOP_K + k + 1], p) for k in range(TOP_K)]
        for c in range(N_EXPERTS * REGION_CAP // ONEHOT_CHUNK):
            cols = slice(c * ONEHOT_CHUNK, (c + 1) * ONEHOT_CHUNK)
            wgt = zero
            for k in reversed(range(TOP_K)):
                wgt = jnp.where(col_id == _small_int_bf16(tgt[k] - c * ONEHOT_CHUNK), pr[:, k:k + 1], wgt)
            wgt_ref[:, cols] = wgt
            ysb_ref[cols, :] = _unpack_bf16_pairs(stage_ref[slot, cols, :])
        return _dot(wgt_ref[...], ysb_ref[...])

    _region_copies(off_ref, cnt_ref, i, 0, mk(slot), False)

    @pl.when(i + 1 < n_tiles)
    def _():
        _region_copies(off_ref, cnt_ref, i + 1, 0, mk(1 - slot), True)

    acc_ref[...] = weighted_sum(0)

    def extra_pass(p, carry):
        _region_copies(off_ref, cnt_ref, i, p, mk(slot), True)
        _region_copies(off_ref, cnt_ref, i, p, mk(slot), False)
        acc_ref[...] += weighted_sum(p)
        return carry

    lax.fori_loop(1, np_ref[i], extra_pass, 0)
    o_ref[...] = _layer_norm(DN_ALPHA * x1_ref[...] + acc_ref[...], g2_ref[...], b2_ref[...])


def _combine(x1, idx_rank, probs, ys, off, cnt8, n_pass, p, tm):
    t = x1.shape[0]
    row = lambda a: a.reshape(1, D_MODEL).astype(F32)
    full = lambda shp: pl.BlockSpec(shp, lambda i, *_: (0,) * len(shp))
    rowb = lambda w: pl.BlockSpec((tm, w), lambda i, *_: (i, 0))
    return pl.pallas_call(
        functools.partial(_combine_kernel, tm=tm, n_tiles=t // tm),
        grid_spec=pltpu.PrefetchScalarGridSpec(
            num_scalar_prefetch=3,
            grid=(t // tm,),
            in_specs=[rowb(D_MODEL), rowb(LANES), rowb(LANES), pl.BlockSpec(memory_space=pl.ANY),
                      full((1, D_MODEL)), full((1, D_MODEL))],
            out_specs=rowb(D_MODEL),
            scratch_shapes=[pltpu.VMEM((2, N_EXPERTS * REGION_CAP, HALF), jnp.int32),
                            pltpu.VMEM((N_EXPERTS * REGION_CAP, D_MODEL), BF16),
                            pltpu.VMEM((tm, N_EXPERTS * REGION_CAP), BF16),
                            pltpu.VMEM((tm, D_MODEL), F32),
                            pltpu.SemaphoreType.DMA((2,))]),
        out_shape=jax.ShapeDtypeStruct((t, D_MODEL), F32),
        compiler_params=_params(("arbitrary",)),
        name="moe_combine_ln2",
    )(off, cnt8, n_pass, x1, idx_rank, probs, ys, row(p['ln2_g']), row(p['ln2_b']))


def _moe(x1, x1b, idx_rank, irt, probs, tile_counts, p, params, l, bm, tm):
    t = x1.shape[0]
    n_tiles = t // tm
    cnt = tile_counts[:, 0, :N_EXPERTS]
    cnt8 = (cnt + 7) // 8 * 8
    seg = jnp.sum(cnt8, axis=0)
    padded = (seg + bm - 1) // bm * bm
    pend = jnp.cumsum(padded)
    pstart = pend - padded
    off = pstart[None, :] + jnp.cumsum(cnt8, axis=0) - cnt8
    n_pass = jnp.maximum((jnp.max(cnt, axis=1) + REGION_CAP - 1) // REGION_CAP, 1).astype(jnp.int32)
    nb = -(-(t * TOP_K + 7 * n_tiles * N_EXPERTS) // bm) + N_EXPERTS
    n_valid = (pend[-1] // bm).astype(jnp.int32).reshape(1)
    blocks = jnp.arange(nb, dtype=jnp.int32)
    block_e = jnp.minimum(jnp.sum((pend // bm)[None, :] <= blocks[:, None], axis=1), N_EXPERTS - 1).astype(jnp.int32)
    block_rows = jnp.clip(seg[block_e] - (blocks * bm - pstart[block_e]), 0, bm).astype(jnp.int32)
    off = off.astype(jnp.int32).reshape(-1)
    cnt8 = cnt8.astype(jnp.int32).reshape(-1)
    pad = jnp.concatenate([pstart + seg, padded - seg]).astype(jnp.int32)
    xs = _dispatch(x1b, irt, off, cnt8, n_pass, pad, n_valid, nb, bm, tm)
    ys = _experts(xs, block_e, block_rows, n_valid, params, l, bm)
    return _combine(x1, idx_rank, probs, ys, off, cnt8, n_pass, p, tm)


_EXPERT_WEIGHTS = ('w_gate', 'b_gate', 'w_up', 'b_up', 'w_down', 'b_down')


def _layer(x, params, l, b, s, cfg):
    p = {k: v[l] for k, v in params.items() if k not in _EXPERT_WEIGHTS}
    xb, x_groups = _prep(x, cfg['prep_tm'])
    w_in = p['w_in']
    o = IN_OFFS
    pad = jnp.zeros((D_MODEL, LANES - 2 * ML_HEADS), w_in.dtype)
    w_ret = w_in[:, o[0]:o[4]].astype(BF16)
    w_ml = jnp.concatenate([w_in[:, o[4]:o[5]], w_in[:, o[7]:o[8]]], axis=1).astype(BF16)
    w_if = jnp.concatenate([w_in[:, o[5]:o[7]], pad], axis=1).astype(BF16)
    w_gates = w_in[:, o[11]:o[12]].astype(BF16)
    y_ret = _retention(xb, w_ret, p['ret_gn'], b, s, cfg['seq_rows']).reshape(b * s, MIX_W)
    y_ml = _mlstm(xb, w_ml, w_if, p, b, s, cfg['seq_rows']).reshape(b * s, MIX_W)
    outs, lses = [], []
    wq = ATT_HEADS * ATT_DK
    wv = ATT_HEADS * ATT_DV
    for g in range(ATT_GROUPS):
        assert ATT_DK == 64
        w_att = jnp.concatenate([w_in[:, o[8] + g * wq:o[8] + (g + 1) * wq] * (ATT_DK ** -0.5),
                                 w_in[:, o[9] + g * wq:o[9] + (g + 1) * wq],
                                 w_in[:, o[10] + g * wv:o[10] + (g + 1) * wv]], axis=1).astype(BF16)
        d = ATT_PATTERNS[g][1]
        og, lg = _dilated_group(xb if d == 1 else x_groups[d], w_att, g, b, s, cfg['att_nq'])
        outs.append(og)
        lses.append(lg)
    x1, x1b, idx_rank, irt, probs, tile_counts = _merge(x, xb, y_ret, y_ml, outs, lses, w_gates, p, cfg['merge_tm'])
    return _moe(x1, x1b, idx_rank, irt, probs, tile_counts, p, params, l, cfg['moe_bm'], ROUTE_TM)


CFG = dict(prep_tm=1024, seq_rows=512, att_nq=4, merge_tm=512, moe_bm=512)

_PARAM_NAMES = ('w_in', 'ret_gn', 'ml_conv_w', 'ml_conv_b', 'ml_wq', 'ml_wk', 'ml_wv', 'ml_bi', 'ml_bf', 'ml_gn',
                'ml_skip', 'w_branch', 'w_out', 'ln1_g', 'ln1_b', 'w_router', 'b_router', 'w_gate', 'b_gate',
                'w_up', 'b_up', 'w_down', 'b_down', 'ln2_g', 'ln2_b')


def _forward(x, params, cfg):
    b, s, d = x.shape
    xf = x.reshape(b * s, d).astype(F32)
    for l in range(DEPTH):
        xf = _layer(xf, params, l, b, s, cfg)
    return xf.reshape(b, s, d).astype(x.dtype)


def kernel(x, w_in, ret_gn, ml_conv_w, ml_conv_b, ml_wq, ml_wk, ml_wv, ml_bi, ml_bf, ml_gn, ml_skip, w_branch, w_out, ln1_g, ln1_b, w_router, b_router, w_gate, b_gate, w_up, b_up, w_down, b_down, ln2_g, ln2_b):
    params = dict(zip(_PARAM_NAMES, (w_in, ret_gn, ml_conv_w, ml_conv_b, ml_wq, ml_wk, ml_wv, ml_bi, ml_bf, ml_gn,
                                     ml_skip, w_branch, w_out, ln1_g, ln1_b, w_router, b_router, w_gate, b_gate,
                                     w_up, b_up, w_down, b_down, ln2_g, ln2_b)))
    return _forward(x, params, CFG)
```

```python
import functools

import jax
import jax.numpy as jnp
import numpy as np
from jax import lax
from jax.experimental import pallas as pl
from jax.experimental.pallas import tpu as pltpu

F32 = jnp.float32
BF16 = jnp.bfloat16

D_MODEL = 1024
DEPTH = 2
MIX_W = D_MODEL // 2
N_BRANCHES = 3
RET_HEADS = 4
RET_DV = MIX_W // RET_HEADS
RET_DK = RET_DV // 2
ML_HEADS = 4
ML_DH = MIX_W // ML_HEADS
ML_CONV = 4
ML_QK_BLOCK = 4
ML_HEAD_GROUP = 1
ATT_PATTERNS = ((128, 1), (512, 4), (2048, 16))
ATT_GROUPS = len(ATT_PATTERNS)
ATT_HEADS = 4
ATT_DV = MIX_W // ATT_HEADS
ATT_DK = ATT_DV // 2
N_EXPERTS = 32
TOP_K = 4
D_FF = D_MODEL
SWIGLU_LIMIT = 7.0
SWIGLU_ALPHA = 1.702
DN_ALPHA = (2.0 * DEPTH) ** 0.25
EPS = 1e-5

CHUNK = 128
LANES = 128
NEG = -1e30
VMEM_LIMIT = 56 * 1024 * 1024

IN_SIZES = (RET_HEADS * RET_DK, RET_HEADS * RET_DK, MIX_W, MIX_W,
            MIX_W, ML_HEADS, ML_HEADS, MIX_W,
            ATT_GROUPS * ATT_HEADS * ATT_DK, ATT_GROUPS * ATT_HEADS * ATT_DK, ATT_GROUPS * ATT_HEADS * ATT_DV,
            N_BRANCHES * D_MODEL)
IN_OFFS = tuple(int(v) for v in np.cumsum((0,) + IN_SIZES))


def _params(sem):
    return pltpu.CompilerParams(dimension_semantics=sem, vmem_limit_bytes=VMEM_LIMIT)


def _dot(a, b):
    return jnp.dot(a, b, preferred_element_type=F32)


def _dot_nt(a, b):
    return lax.dot_general(a, b, (((1,), (1,)), ((), ())), preferred_element_type=F32)


def _dot_tn(a, b):
    return lax.dot_general(a, b, (((0,), (0,)), ((), ())), preferred_element_type=F32)


def _sigmoid(x):
    return 1.0 / (1.0 + jnp.exp(-x))


def _prep_kernel(x_ref, xb_ref, *rest, dilations):
    group_refs, cols_ref = rest[:-1], rest[-1]
    x = x_ref[...]
    xb_ref[...] = x.astype(xb_ref.dtype)
    tm, dm = x.shape
    for c in range(dm // LANES):
        cols_ref[c] = x[:, c * LANES:(c + 1) * LANES]
    for o_ref, d in zip(group_refs, dilations):
        n = tm // d
        for r in range(d):
            for c in range(dm // LANES):
                o_ref[:, r * dm + c * LANES:r * dm + (c + 1) * LANES] = (
                    cols_ref[c, pl.ds(r, n, stride=d), :].astype(o_ref.dtype))


def _prep(x, tm):
    t, dm = x.shape
    tm = min(tm, t)
    dilations = tuple(sorted({d for _, d in ATT_PATTERNS if d > 1}))
    assert t % tm == 0 and all(tm % (16 * d) == 0 for d in dilations)
    outs = pl.pallas_call(
        functools.partial(_prep_kernel, dilations=dilations),
        grid=(t // tm,),
        in_specs=[pl.BlockSpec((tm, dm), lambda i: (i, 0))],
        out_specs=[pl.BlockSpec((tm, dm), lambda i: (i, 0))]
                  + [pl.BlockSpec((tm // d, d * dm), lambda i: (i, 0)) for d in dilations],
        out_shape=[jax.ShapeDtypeStruct((t, dm), BF16)]
                  + [jax.ShapeDtypeStruct((t // d, d * dm), BF16) for d in dilations],
        scratch_shapes=[pltpu.VMEM((dm // LANES, tm, LANES), F32)],
        compiler_params=_params(("arbitrary",)),
        name="prep_bf16_groups",
    )(x)
    return outs[0], dict(zip(dilations, outs[1:]))


def _head_norm(o, gn):
    mu = jnp.mean(o, axis=-1, keepdims=True)
    oc = o - mu
    var = jnp.mean(oc * oc, axis=-1, keepdims=True)
    return oc * lax.rsqrt(var + EPS) * gn


def _ret_kernel(x_ref, w_ref, dm_ref, qd_ref, kd_ref, cd_ref, gn_ref, o_ref, st_ref, blk_ref, *, nc, nb):
    @pl.when(pl.program_id(0) == 0)
    def _():
        st_ref[...] = jnp.zeros_like(st_ref)

    for bi in range(nb):
        blk_ref[bi] = _dot(x_ref[bi], w_ref[...]).astype(blk_ref.dtype)

    hq = RET_HEADS * RET_DK
    state = [st_ref[j] for j in range(nb * RET_HEADS)]
    for c in range(nc):
        rows = pl.ds(c * CHUNK, CHUNK)
        chains = [(h, bi) for h in range(RET_HEADS) for bi in range(nb)]
        first = []
        for h, bi in chains:
            q = blk_ref[bi, rows, h * RET_DK:(h + 1) * RET_DK]
            k = blk_ref[bi, rows, hq + h * RET_DK:hq + (h + 1) * RET_DK]
            v = blk_ref[bi, rows, 2 * hq + h * RET_DV:2 * hq + (h + 1) * RET_DV]
            st = state[bi * RET_HEADS + h]
            first.append((v, _dot_nt(q, k), _dot(q, st.astype(BF16)),
                          _dot_tn(k, (v.astype(F32) * kd_ref[h]).astype(BF16))))
        for (h, bi), (v, s_qk, q_st, kv) in zip(chains, first):
            o = _dot((s_qk * dm_ref[h]).astype(BF16), v) + q_st * qd_ref[h]
            state[bi * RET_HEADS + h] = cd_ref[h] * state[bi * RET_HEADS + h] + kv
            g = blk_ref[bi, rows, 2 * hq + MIX_W + h * RET_DV:2 * hq + MIX_W + (h + 1) * RET_DV].astype(F32)
            y = _head_norm(o, gn_ref[:, h * RET_DV:(h + 1) * RET_DV]) * (g * _sigmoid(g))
            o_ref[bi, rows, h * RET_DV:(h + 1) * RET_DV] = y.astype(o_ref.dtype)
    for j in range(nb * RET_HEADS):
        st_ref[j] = state[j]


def _retention(xb, w_ret, ret_gn, b, s, rows):
    nc = rows // CHUNK
    scale = RET_DK ** -0.5
    log_gamma = jnp.log1p(-jnp.exp2(-5.0 - jnp.arange(RET_HEADS, dtype=F32)))
    pos = jnp.arange(CHUNK, dtype=F32)
    diff = pos[:, None] - pos[None, :]
    dm = jnp.where(diff >= 0, jnp.exp(log_gamma[:, None, None] * jnp.maximum(diff, 0.0)), 0.0) * scale
    kd = jnp.broadcast_to(jnp.exp(log_gamma[:, None] * (CHUNK - 1.0 - pos))[:, :, None], (RET_HEADS, CHUNK, RET_DV))
    qd = jnp.broadcast_to((jnp.exp(log_gamma[:, None] * (pos + 1.0)) * scale)[:, :, None], (RET_HEADS, CHUNK, RET_DV))
    cd = jnp.broadcast_to(jnp.exp(log_gamma * CHUNK)[:, None, None], (RET_HEADS, 1, RET_DV))
    dmod, w_in = w_ret.shape
    full = lambda shp: pl.BlockSpec(shp, lambda si: (0,) * len(shp))
    return pl.pallas_call(
        functools.partial(_ret_kernel, nc=nc, nb=b),
        grid=(s // rows,),
        in_specs=[pl.BlockSpec((b, rows, dmod), lambda si: (0, si, 0)), full((dmod, w_in)),
                  full((RET_HEADS, CHUNK, CHUNK)), full((RET_HEADS, CHUNK, RET_DV)),
                  full((RET_HEADS, CHUNK, RET_DV)), full((RET_HEADS, 1, RET_DV)), full((1, MIX_W))],
        out_specs=pl.BlockSpec((b, rows, MIX_W), lambda si: (0, si, 0)),
        out_shape=jax.ShapeDtypeStruct((b, s, MIX_W), BF16),
        scratch_shapes=[pltpu.VMEM((b * RET_HEADS, RET_DK, RET_DV), F32), pltpu.VMEM((b, rows, w_in), BF16)],
        compiler_params=_params(("arbitrary",)),
        name="retention",
    )(xb.reshape(b, s, dmod), w_ret, dm, qd, kd, cd, ret_gn.reshape(1, MIX_W).astype(F32))


def _log_sigmoid(x):
    return jnp.minimum(x, 0.0) - jnp.log(1.0 + jnp.exp(-jnp.abs(x)))


def _ml_kernel(x_ref, wml_ref, wif_ref, cw_ref, cb_ref, bq_ref, bk_ref, bv_ref, gb_ref, gn_ref, sk_ref,
               o_ref, c_ref, n_ref, m_ref, halo_ref, ml_ref, if_ref, qkv_ref, *, nc, nb):
    rows_total = nc * CHUNK

    @pl.when(pl.program_id(0) == 0)
    def _():
        c_ref[...] = jnp.zeros_like(c_ref)
        n_ref[...] = jnp.zeros_like(n_ref)
        m_ref[...] = jnp.zeros_like(m_ref)
        halo_ref[...] = jnp.zeros_like(halo_ref)

    for bi in range(nb):
        ml_ref[bi] = _dot(x_ref[bi], wml_ref[...]).astype(ml_ref.dtype)
        if_ref[bi] = _dot(x_ref[bi], wif_ref[...])

    x_all, mc_all = [], []
    for bi in range(nb):
        xa = ml_ref[bi, :, 0:MIX_W].astype(F32)
        xf = jnp.concatenate([halo_ref[bi], xa], axis=0)
        acc = jnp.broadcast_to(cb_ref[...], (rows_total, MIX_W))
        for j in range(ML_CONV):
            sh = ML_CONV - 1 - j
            xs = xf if sh == 0 else pltpu.roll(xf, sh, 0)
            acc = acc + xs[8:8 + rows_total] * cw_ref[j:j + 1, :]
        halo_ref[bi] = xa[rows_total - 8:rows_total]
        x_all.append(xa)
        mc_all.append(acc * _sigmoid(acc))

    ri = lax.broadcasted_iota(jnp.int32, (CHUNK, CHUNK), 0)
    ci = lax.broadcasted_iota(jnp.int32, (CHUNK, CHUNK), 1)
    tri = ri >= ci
    tril = tri.astype(F32)
    kscale = ML_DH ** -0.5
    log_kscale = float(np.log(kscale))

    c_st = [[c_ref[bi * ML_HEADS + h] for h in range(ML_HEADS)] for bi in range(nb)]
    n_st = [[n_ref[bi, h:h + 1, :] for h in range(ML_HEADS)] for bi in range(nb)]
    m_st = [[m_ref[bi, h:h + 1, 0:1] for h in range(ML_HEADS)] for bi in range(nb)]

    for bi in range(nb):
        mcb = mc_all[bi].astype(BF16)
        xvb = x_all[bi].astype(BF16)
        for h in range(ML_HEADS):
            cols = slice(h * ML_DH, (h + 1) * ML_DH)
            qkv_ref[bi, 0, :, cols] = _dot(mcb[:, cols], bq_ref[h]).astype(BF16)
            qkv_ref[bi, 1, :, cols] = _dot(mcb[:, cols], bk_ref[h]).astype(BF16)
            qkv_ref[bi, 2, :, cols] = _dot(xvb[:, cols], bv_ref[h]).astype(BF16)

    def run_chains(r0, gates, chains):
        for h, bi in chains:
            g_pre, cum, g_t, cum_t = gates[bi]
            cols = slice(h * ML_DH, (h + 1) * ML_DH)
            mc = mc_all[bi][r0:r0 + CHUNK, cols]
            q = qkv_ref[bi, 0, r0:r0 + CHUNK, cols]
            k = qkv_ref[bi, 1, r0:r0 + CHUNK, cols]
            v = qkv_ref[bi, 2, r0:r0 + CHUNK, cols]
            cum_col = cum[:, ML_HEADS + h:ML_HEADS + h + 1]
            ig_col = g_pre[:, h:h + 1]
            cum_row = cum_t[ML_HEADS + h:ML_HEADS + h + 1, :]
            ig_row = g_t[h:h + 1, :]
            tot = cum[CHUNK - 1:CHUNK, ML_HEADS + h:ML_HEADS + h + 1]
            m_prev, c_prev, n_prev = m_st[bi][h], c_st[bi][h], n_st[bi][h]
            dlog = jnp.where(tri, cum_col + (ig_row - cum_row), NEG)
            m_inter = cum_col + m_prev
            m_q = jnp.maximum(m_inter, jnp.max(dlog, axis=-1, keepdims=True))
            w_qk = jnp.exp(dlog - (m_q - log_kscale)) * _dot_nt(q, k)
            inter = jnp.exp(m_inter - m_q)
            num = _dot(w_qk.astype(BF16), v) + inter * _dot(q, c_prev.astype(BF16))
            den = (jnp.sum(w_qk, axis=-1, keepdims=True)
                   + inter * jnp.sum(q.astype(F32) * n_prev, axis=-1, keepdims=True))
            hh = num * (1.0 / jnp.maximum(jnp.abs(den), jnp.exp(-m_q)))
            a_col = tot - cum_col + ig_col
            a_max = jnp.max(a_col, axis=0, keepdims=True)
            wa = jnp.exp(a_col - a_max) * kscale
            chunk_c = _dot_tn(k, (wa * v.astype(F32)).astype(BF16))
            chunk_n = jnp.sum(wa * k.astype(F32), axis=0, keepdims=True)
            m_new = jnp.maximum(tot + m_prev, a_max)
            s_old = jnp.exp(tot + m_prev - m_new)
            s_new = jnp.exp(a_max - m_new)
            c_st[bi][h] = s_old * c_prev + s_new * chunk_c
            n_st[bi][h] = s_old * n_prev + s_new * chunk_n
            m_st[bi][h] = m_new
            mo = ml_ref[bi, r0:r0 + CHUNK, MIX_W + h * ML_DH:MIX_W + (h + 1) * ML_DH].astype(F32)
            y = _sigmoid(mo) * (_head_norm(hh, gn_ref[:, cols]) + sk_ref[:, cols] * mc)
            o_ref[bi, r0:r0 + CHUNK, cols] = y.astype(o_ref.dtype)

    for c in range(nc):
        r0 = c * CHUNK
        gates = []
        for bi in range(nb):
            g_pre = if_ref[bi, r0:r0 + CHUNK, :] + gb_ref[...]
            cum = jnp.dot(tril, _log_sigmoid(g_pre), preferred_element_type=F32, precision=lax.Precision.HIGHEST)
            gates.append((g_pre, cum, g_pre.T, cum.T))
        for h0 in range(0, ML_HEADS, ML_HEAD_GROUP):
            run_chains(r0, gates, [(h, bi) for h in range(h0, h0 + ML_HEAD_GROUP) for bi in range(nb)])

    for bi in range(nb):
        for h in range(ML_HEADS):
            c_ref[bi * ML_HEADS + h] = c_st[bi][h]
            n_ref[bi, h:h + 1, :] = n_st[bi][h]
            m_ref[bi, h:h + 1, :] = jnp.broadcast_to(m_st[bi][h], (1, LANES))


def _block_diag(w):
    nb = w.shape[0]
    per = nb // ML_HEADS
    wh = w.reshape(ML_HEADS, per, ML_QK_BLOCK, ML_QK_BLOCK)
    eye = jnp.eye(per, dtype=w.dtype)
    bd = jnp.einsum('hncd,nm->hncmd', wh, eye)
    return bd.reshape(ML_HEADS, per * ML_QK_BLOCK, per * ML_QK_BLOCK).astype(BF16)


def _mlstm(xb, w_ml, w_if, p, b, s, rows):
    nc = rows // CHUNK
    dm = xb.shape[1]
    full = lambda shp: pl.BlockSpec(shp, lambda si: (0,) * len(shp))
    gb = jnp.zeros((1, LANES), F32).at[0, 0:ML_HEADS].set(p['ml_bi'].astype(F32)).at[0, ML_HEADS:2 * ML_HEADS].set(p['ml_bf'].astype(F32))
    row = lambda a: a.reshape(1, MIX_W).astype(F32)
    return pl.pallas_call(
        functools.partial(_ml_kernel, nc=nc, nb=b),
        grid=(s // rows,),
        in_specs=[pl.BlockSpec((b, rows, dm), lambda si: (0, si, 0)),
                  full((dm, 2 * MIX_W)), full((dm, LANES)),
                  full((ML_CONV, MIX_W)), full((1, MIX_W)),
                  full((ML_HEADS, ML_DH, ML_DH)), full((ML_HEADS, ML_DH, ML_DH)), full((ML_HEADS, ML_DH, ML_DH)),
                  full((1, LANES)), full((1, MIX_W)), full((1, MIX_W))],
        out_specs=pl.BlockSpec((b, rows, MIX_W), lambda si: (0, si, 0)),
        out_shape=jax.ShapeDtypeStruct((b, s, MIX_W), BF16),
        scratch_shapes=[pltpu.VMEM((b * ML_HEADS, ML_DH, ML_DH), F32), pltpu.VMEM((b, 8, ML_DH), F32),
                        pltpu.VMEM((b, 8, LANES), F32), pltpu.VMEM((b, 8, MIX_W), F32),
                        pltpu.VMEM((b, rows, 2 * MIX_W), BF16), pltpu.VMEM((b, rows, LANES), F32),
                        pltpu.VMEM((b, 3, rows, MIX_W), BF16)],
        compiler_params=_params(("arbitrary",)),
        name="mlstm",
    )(xb.reshape(b, s, dm), w_ml, w_if,
      p['ml_conv_w'].astype(F32), row(p['ml_conv_b']),
      _block_diag(p['ml_wq']), _block_diag(p['ml_wk']), _block_diag(p['ml_wv']),
      gb, row(p['ml_gn']), row(p['ml_skip']))


def _att_kernel(x_ref, w_ref, bias_ref, o_ref, lse_ref, qkv_ref, *, nq):
    first = pl.program_id(2) == 0
    span = nq * CHUNK
    wq = ATT_HEADS * ATT_DK

    @pl.when(first)
    def _():
        qkv_ref[0:CHUNK, :] = jnp.zeros((CHUNK, qkv_ref.shape[1]), qkv_ref.dtype)

    qkv_ref[CHUNK:CHUNK + span, :] = _dot(x_ref[...], w_ref[...]).astype(qkv_ref.dtype)
    lane = lax.broadcasted_iota(jnp.int32, (CHUNK, LANES), 1)
    ones = jnp.ones((2 * CHUNK, ATT_DV), BF16)
    for j in range(nq):
        rows = slice(j * CHUNK, (j + 1) * CHUNK)
        kv_rows = slice(j * CHUNK, (j + 2) * CHUNK)
        lse_out = jnp.zeros((CHUNK, LANES), F32)
        for h in range(ATT_HEADS):
            dv = slice(h * ATT_DV, (h + 1) * ATT_DV)
            q = qkv_ref[CHUNK + j * CHUNK:CHUNK + (j + 1) * CHUNK, h * ATT_DK:(h + 1) * ATT_DK]
            kk = qkv_ref[kv_rows, wq + h * ATT_DK:wq + (h + 1) * ATT_DK]
            vv = qkv_ref[kv_rows, 2 * wq + h * ATT_DV:2 * wq + (h + 1) * ATT_DV]
            bias = bias_ref[jnp.where(first, 0, 1), h] if j == 0 else bias_ref[1, h]
            s = _dot_nt(q, kk) + bias
            m = jnp.max(s, axis=-1, keepdims=True)
            p = jnp.exp(s - m).astype(BF16)
            oa = _dot(p, jnp.concatenate([vv, ones], axis=1))
            l = oa[:, ATT_DV:2 * ATT_DV]
            o_ref[rows, dv] = (oa[:, 0:ATT_DV] * (1.0 / l)).astype(o_ref.dtype)
            lse_out = jnp.where(lane == h, m + jnp.log(l), lse_out)
        lse_ref[rows, :] = lse_out
    qkv_ref[0:CHUNK, :] = qkv_ref[span:span + CHUNK, :]


def _att_bias(window, dilation, slopes):
    wb = window // dilation
    qi = jnp.arange(wb)[:, None]
    kj = jnp.arange(2 * wb)[None, :]
    delta = qi + wb - kj
    band = (delta >= 0) & (delta <= wb)
    bias = -slopes[:, None, None] * (dilation * delta).astype(F32)
    later = jnp.where(band[None], bias, NEG)
    first = jnp.where((band & (kj >= wb))[None], bias, NEG)
    return jnp.stack([first, later]).astype(F32)


def _dilated_group(xg, w_qkv, g, b, s, nq):
    window, d = ATT_PATTERNS[g]
    assert window // d == CHUNK
    l_sub = s // d
    assert l_sub % CHUNK == 0
    n_blk = l_sub // CHUNK
    slopes = jnp.exp2(-8.0 * jnp.arange(1, ATT_GROUPS * ATT_HEADS + 1, dtype=F32) / (ATT_GROUPS * ATT_HEADS))
    bias = _att_bias(window, d, slopes.reshape(ATT_GROUPS, ATT_HEADS)[g])
    wq = ATT_HEADS * ATT_DK
    wv = ATT_HEADS * ATT_DV
    per_tok = 2 * wq + wv
    dm = w_qkv.shape[0]
    assert xg.shape == (b * l_sub, d * dm)
    x = xg.reshape(b, l_sub, d * dm)
    nq = min(nq, n_blk)
    assert n_blk % nq == 0
    span = nq * CHUNK
    o, lse = pl.pallas_call(
        functools.partial(_att_kernel, nq=nq),
        grid=(b, d, n_blk // nq),
        in_specs=[pl.BlockSpec((None, span, dm), lambda bi, r, n: (bi, n, r)),
                  pl.BlockSpec((dm, per_tok), lambda bi, r, n: (0, 0)),
                  pl.BlockSpec((2, ATT_HEADS, CHUNK, 2 * CHUNK), lambda bi, r, n: (0, 0, 0, 0))],
        out_specs=[pl.BlockSpec((None, span, wv), lambda bi, r, n: (bi, n, r)),
                   pl.BlockSpec((None, span, LANES), lambda bi, r, n: (bi, n, r))],
        out_shape=[jax.ShapeDtypeStruct((b, l_sub, d * wv), BF16),
                   jax.ShapeDtypeStruct((b, l_sub, d * LANES), F32)],
        scratch_shapes=[pltpu.VMEM((CHUNK + span, per_tok), BF16)],
        compiler_params=_params(("arbitrary", "arbitrary", "arbitrary")),
        name=f"dilated_attn_g{g}",
    )(x, w_qkv, bias)
    return o.reshape(b * l_sub, d * wv), lse.reshape(b * l_sub, d * LANES)


def _layer_norm(z, g, b):
    mu = jnp.mean(z, axis=-1, keepdims=True)
    zc = z - mu
    var = jnp.mean(zc * zc, axis=-1, keepdims=True)
    return zc * lax.rsqrt(var + EPS) * g + b


def _merge_kernel(x_ref, yr_ref, ym_ref, o0_ref, o1_ref, o2_ref, l0_ref, l1_ref, l2_ref, xb_ref,
                  wg_ref, wb_ref, wo_ref, g1_ref, b1_ref, wrh_ref, wrl_ref, br_ref,
                  x1_ref, x1b_ref, ir_ref, irt_ref, pr_ref, cnt_ref, og_ref, lg_ref, *, tm):
    for g, (src_o, src_l) in enumerate(((o0_ref, l0_ref), (o1_ref, l1_ref), (o2_ref, l2_ref))):
        d = ATT_PATTERNS[g][1]
        n = tm // d
        for r in range(d):
            rows = pl.ds(r, n, stride=d) if d > 1 else pl.ds(0, n)
            for h in range(ATT_HEADS):
                og_ref[g, h, rows, :] = src_o[:, r * MIX_W + h * ATT_DV:r * MIX_W + (h + 1) * ATT_DV].astype(F32)
            lg_ref[g, rows, :] = src_l[:, r * LANES:(r + 1) * LANES]

    l0, l1, l2 = lg_ref[0], lg_ref[1], lg_ref[2]
    lm = jnp.maximum(jnp.maximum(l0, l1), l2)
    e0, e1, e2 = jnp.exp(l0 - lm), jnp.exp(l1 - lm), jnp.exp(l2 - lm)
    inv = 1.0 / (e0 + e1 + e2)
    parts = []
    for h in range(ATT_HEADS):
        dv = slice(h * ATT_DV, (h + 1) * ATT_DV)
        hs = slice(h, h + 1)
        parts.append((e0[:, hs] * og_ref[0, h] + e1[:, hs] * og_ref[1, h] + e2[:, hs] * og_ref[2, h]) * inv[:, hs])
    y_att = jnp.concatenate(parts, axis=-1).astype(BF16)

    xb = xb_ref[...]
    gate = lambda j: _sigmoid(_dot(xb, wg_ref[:, j * D_MODEL:(j + 1) * D_MODEL]))
    merged = gate(0) * _dot(yr_ref[...], wb_ref[0])
    merged = merged + gate(1) * _dot(ym_ref[...], wb_ref[1])
    merged = merged + gate(2) * _dot(y_att, wb_ref[2])
    z = DN_ALPHA * x_ref[...] + _dot(merged.astype(BF16), wo_ref[...])
    x1 = _layer_norm(z, g1_ref[...], b1_ref[...])
    x1_ref[...] = x1
    x1b_ref[...] = x1.astype(BF16)

    xh = x1.astype(BF16)
    xl = (x1 - xh.astype(F32)).astype(BF16)
    logits = _dot(xh, wrh_ref[...]) + _dot(xl, wrh_ref[...]) + _dot(xh, wrl_ref[...]) + br_ref[...]
    lane = lax.broadcasted_iota(jnp.int32, (tm, LANES), 1)
    work = jnp.where(lane < N_EXPERTS, logits, -jnp.inf)
    vals, idxs = [], []
    for _ in range(TOP_K):
        m = jnp.max(work, axis=-1, keepdims=True)
        idx = jnp.min(jnp.where(work == m, lane, LANES), axis=-1, keepdims=True)
        vals.append(m)
        idxs.append(idx)
        work = jnp.where(lane == idx, -jnp.inf, work)
    es = [jnp.exp(v - vals[0]) for v in vals]
    tot = es[0] + es[1] + es[2] + es[3]
    pr = jnp.zeros((tm, LANES), F32)
    for k in range(TOP_K):
        pr = jnp.where(lane == k, es[k] / tot, pr)
    pr_ref[...] = pr

    onehot = jnp.zeros((tm, LANES), F32)
    for k in range(TOP_K):
        onehot = onehot + (lane == idxs[k]).astype(F32)
    ri = lax.broadcasted_iota(jnp.int32, (tm, tm), 0)
    ci = lax.broadcasted_iota(jnp.int32, (tm, tm), 1)
    same_tile = (ri // ROUTE_TM) == (ci // ROUTE_TM)
    before = _dot(((ri > ci) & same_tile).astype(BF16), onehot.astype(BF16))
    ir = jnp.zeros((tm, LANES), F32)
    for k in range(TOP_K):
        rank = jnp.sum(jnp.where(lane == idxs[k], before, 0.0), axis=-1, keepdims=True)
        ir = jnp.where(lane == k, idxs[k].astype(F32), ir)
        ir = jnp.where(lane == TOP_K + k, rank, ir)
    ir_ref[...] = ir.astype(jnp.int32)
    irt_ref[...] = ir.T[0:8, :]
    for j in range(tm // ROUTE_TM):
        cnt_j = jnp.sum(onehot[j * ROUTE_TM:(j + 1) * ROUTE_TM], axis=0, keepdims=True)
        cnt_ref[j] = jnp.broadcast_to(cnt_j, (8, LANES)).astype(jnp.int32)


def _merge(x, xb, y_ret, y_ml, outs, lses, w_gates, p, tm):
    t = x.shape[0]
    tm = min(tm, t)
    rowb = lambda w: pl.BlockSpec((tm, w), lambda i: (i, 0))
    full = lambda shp: pl.BlockSpec(shp, lambda i: (0,) * len(shp))
    wr = jnp.zeros((D_MODEL, LANES), F32).at[:, :N_EXPERTS].set(p['w_router'].astype(F32))
    wrh = wr.astype(BF16)
    wrl = (wr - wrh.astype(F32)).astype(BF16)
    br = jnp.zeros((1, LANES), F32).at[0, :N_EXPERTS].set(p['b_router'].astype(F32))
    row = lambda a: a.reshape(1, D_MODEL).astype(F32)
    return pl.pallas_call(
        functools.partial(_merge_kernel, tm=tm),
        grid=(t // tm,),
        in_specs=[rowb(D_MODEL), rowb(MIX_W), rowb(MIX_W)]
                 + [pl.BlockSpec((tm // d, d * MIX_W), lambda i: (i, 0)) for _, d in ATT_PATTERNS]
                 + [pl.BlockSpec((tm // d, d * LANES), lambda i: (i, 0)) for _, d in ATT_PATTERNS]
                 + [rowb(D_MODEL), full((D_MODEL, N_BRANCHES * D_MODEL)),
                  full((N_BRANCHES, MIX_W, D_MODEL)), full((D_MODEL, D_MODEL)),
                  full((1, D_MODEL)), full((1, D_MODEL)),
                  full((D_MODEL, LANES)), full((D_MODEL, LANES)), full((1, LANES))],
        out_specs=[rowb(D_MODEL), rowb(D_MODEL), rowb(LANES), pl.BlockSpec((8, tm), lambda i: (0, i)), rowb(LANES),
                   pl.BlockSpec((tm // ROUTE_TM, 8, LANES), lambda i: (i, 0, 0))],
        out_shape=[jax.ShapeDtypeStruct((t, D_MODEL), F32),
                   jax.ShapeDtypeStruct((t, D_MODEL), BF16),
                   jax.ShapeDtypeStruct((t, LANES), jnp.int32),
                   jax.ShapeDtypeStruct((8, t), F32),
                   jax.ShapeDtypeStruct((t, LANES), F32),
                   jax.ShapeDtypeStruct((t // ROUTE_TM, 8, LANES), jnp.int32)],
        scratch_shapes=[pltpu.VMEM((ATT_GROUPS, ATT_HEADS, tm, ATT_DV), F32),
                        pltpu.VMEM((ATT_GROUPS, tm, LANES), F32)],
        compiler_params=_params(("arbitrary",)),
        name="merge_ln1_router",
    )(x, y_ret, y_ml, outs[0], outs[1], outs[2], lses[0], lses[1], lses[2], xb, w_gates,
      p['w_branch'].astype(BF16), p['w_out'].astype(BF16), row(p['ln1_g']), row(p['ln1_b']), wrh, wrl, br)


ROUTE_TM = 256
REGION_CAP = 64
REGION_PIECES = (64, 32, 16, 8)
PAD_PIECES = (256, 128, 64, 32, 16, 8)
ONEHOT_CHUNK = 256
HALF = D_MODEL // 2
HI_MASK = -65536


def _pack_bf16_pairs(y):
    lo = lax.shift_right_logical(lax.bitcast_convert_type(y[:, :HALF], jnp.int32), 16)
    hi = lax.bitcast_convert_type(y[:, HALF:], jnp.int32) & HI_MASK
    return lo | hi


def _unpack_bf16_pairs(w):
    lo = lax.bitcast_convert_type(lax.shift_left(w, 16), F32)
    hi = lax.bitcast_convert_type(w & HI_MASK, F32)
    return jnp.concatenate([lo, hi], axis=1).astype(BF16)


def _row_pieces(n, pieces, fn):
    for size in pieces:
        if size == pieces[0]:
            cond, pos = n >= size, 0
        else:
            cond, pos = (n & size) != 0, pl.multiple_of(n & (-2 * size), 8)

        @pl.when(cond)
        def _(pos=pos, size=size):
            fn(pos, size)


def _region_copies(off_ref, cnt_ref, tile, p, make_copy, start):
    for e in range(N_EXPERTS):
        n = jnp.clip(cnt_ref[tile * N_EXPERTS + e] - p * REGION_CAP, 0, REGION_CAP)
        base = off_ref[tile * N_EXPERTS + e] + p * REGION_CAP

        def piece(pos, size, e=e, base=base):
            cp = make_copy(pl.ds(e * REGION_CAP + pos, size), pl.ds(pl.multiple_of(base + pos, 8), size))
            if start:
                cp.start()
            else:
                cp.wait()

        _row_pieces(n, REGION_PIECES, piece)


def _slot_targets(idx, lrank, p):
    lr = lrank - p * REGION_CAP
    return jnp.where((lr >= 0) & (lr < REGION_CAP), idx * REGION_CAP + lr, -1)


def _small_int_bf16(v):
    return v.astype(F32).astype(BF16)


def _dispatch_kernel(off_ref, cnt_ref, np_ref, pad_ref, nv_ref, x_ref, irt_ref, xs_hbm,
                     stage_ref, zero_ref, sem, zsem, *, tm, n_tiles, bm, nb):
    i = pl.program_id(0)
    slot = i % 2
    row_id = _small_int_bf16(lax.broadcasted_iota(jnp.int32, (ONEHOT_CHUNK, tm), 0))
    meta = irt_ref[...].astype(jnp.int32)
    x = x_ref[...]
    mk = lambda s: (lambda src, dst: pltpu.make_async_copy(stage_ref.at[s, src], xs_hbm.at[dst], sem.at[s]))

    def one_pass(p, carry):
        tgt = [_slot_targets(meta[k:k + 1, :], meta[TOP_K + k:TOP_K + k + 1, :], p) for k in range(TOP_K)]
        for c in range(N_EXPERTS * REGION_CAP // ONEHOT_CHUNK):
            hit = row_id == _small_int_bf16(tgt[0] - c * ONEHOT_CHUNK)
            for k in range(1, TOP_K):
                hit = hit | (row_id == _small_int_bf16(tgt[k] - c * ONEHOT_CHUNK))
            sel = jnp.where(hit, jnp.ones((), BF16), jnp.zeros((), BF16))
            stage_ref[slot, c * ONEHOT_CHUNK:(c + 1) * ONEHOT_CHUNK, :] = _pack_bf16_pairs(_dot(sel, x))
        _region_copies(off_ref, cnt_ref, i, p, mk(slot), True)

        @pl.when(p < np_ref[i] - 1)
        def _():
            _region_copies(off_ref, cnt_ref, i, p, mk(slot), False)
        return carry

    lax.fori_loop(0, np_ref[i], one_pass, 0)

    @pl.when(i > 0)
    def _():
        _region_copies(off_ref, cnt_ref, i - 1, np_ref[jnp.maximum(i - 1, 0)] - 1, mk(1 - slot), False)

    @pl.when(i == n_tiles - 1)
    def _():
        _region_copies(off_ref, cnt_ref, i, np_ref[i] - 1, mk(slot), False)
        zero_ref[...] = jnp.zeros_like(zero_ref)
        for phase in (True, False):
            for e in range(N_EXPERTS):
                def piece(pos, size, e=e):
                    cp = pltpu.make_async_copy(zero_ref.at[pl.ds(0, size)],
                                               xs_hbm.at[pl.ds(pl.multiple_of(pad_ref[e] + pos, 8), size)], zsem)
                    cp.start() if phase else cp.wait()
                _row_pieces(pad_ref[N_EXPERTS + e], PAD_PIECES, piece)

        def zero_block(blk, carry):
            for half in range(bm // PAD_PIECES[0]):
                cp = pltpu.make_async_copy(
                    zero_ref, xs_hbm.at[pl.ds(pl.multiple_of(blk * bm + half * PAD_PIECES[0], 8), PAD_PIECES[0])], zsem)
                cp.start()
                cp.wait()
            return carry

        lax.fori_loop(nv_ref[0], nb, zero_block, 0)


def _dispatch(x1b, irt, off, cnt8, n_pass, pad, n_valid, nb, bm, tm):
    t = x1b.shape[0]
    assert bm % PAD_PIECES[0] == 0 and bm <= 2 * PAD_PIECES[0]
    return pl.pallas_call(
        functools.partial(_dispatch_kernel, tm=tm, n_tiles=t // tm, bm=bm, nb=nb),
        grid_spec=pltpu.PrefetchScalarGridSpec(
            num_scalar_prefetch=5,
            grid=(t // tm,),
            in_specs=[pl.BlockSpec((tm, D_MODEL), lambda i, *_: (i, 0)),
                      pl.BlockSpec((8, tm), lambda i, *_: (0, i))],
            out_specs=pl.BlockSpec(memory_space=pl.ANY),
            scratch_shapes=[pltpu.VMEM((2, N_EXPERTS * REGION_CAP, HALF), jnp.int32),
                            pltpu.VMEM((PAD_PIECES[0], HALF), jnp.int32),
                            pltpu.SemaphoreType.DMA((2,)), pltpu.SemaphoreType.DMA(())]),
        out_shape=jax.ShapeDtypeStruct((nb * bm, HALF), jnp.int32),
        compiler_params=_params(("arbitrary",)),
        name="moe_dispatch",
    )(off, cnt8, n_pass, pad, n_valid, x1b, irt)


def _ffn_kernel(be_ref, nr_ref, nv_ref, xs_ref, wg_ref, bg_ref, wu_ref, bu_ref, wd_ref, bd_ref, o_ref,
                wgb_ref, wub_ref, wdb_ref):
    i = pl.program_id(0)

    @pl.when(i < nv_ref[0])
    def _():
        @pl.when((i == 0) | (be_ref[i] != be_ref[jnp.maximum(i - 1, 0)]))
        def _():
            wgb_ref[...] = wg_ref[...].astype(BF16)
            wub_ref[...] = wu_ref[...].astype(BF16)
            wdb_ref[...] = wd_ref[...].astype(BF16)

        rows = lax.broadcasted_iota(jnp.int32, xs_ref.shape, 0)
        x = _unpack_bf16_pairs(jnp.where(rows < nr_ref[i], xs_ref[...], 0))
        gate = jnp.minimum(_dot(x, wgb_ref[...]) + bg_ref[...], SWIGLU_LIMIT)
        up = jnp.clip(_dot(x, wub_ref[...]) + bu_ref[...], -SWIGLU_LIMIT, SWIGLU_LIMIT)
        hid = (up + 1.0) * gate * _sigmoid(SWIGLU_ALPHA * gate)
        y = _dot(hid.astype(BF16), wdb_ref[...]) + bd_ref[...]
        o_ref[...] = _pack_bf16_pairs(y.astype(BF16).astype(F32))

    @pl.when(i >= nv_ref[0])
    def _():
        o_ref[...] = jnp.zeros_like(o_ref)


def _experts(xs, block_e, block_rows, n_valid, params, l, bm):
    n_slots = xs.shape[0]
    nb = n_slots // bm
    blk = lambda i, be, nr, nv: (jnp.minimum(i, nv[0] - 1), 0)
    oblk = lambda i, be, nr, nv: (i, 0)
    wsp = pl.BlockSpec((None, None, D_MODEL, D_FF), lambda i, be, nr, nv: (l, be[i], 0, 0))
    bsp = pl.BlockSpec((None, None, 1, D_FF), lambda i, be, nr, nv: (l, be[i], 0, 0))
    b4 = lambda a: a.reshape(DEPTH, N_EXPERTS, 1, -1)
    return pl.pallas_call(
        _ffn_kernel,
        grid_spec=pltpu.PrefetchScalarGridSpec(
            num_scalar_prefetch=3,
            grid=(nb,),
            in_specs=[pl.BlockSpec((bm, HALF), blk), wsp, bsp, wsp, bsp, wsp, bsp],
            out_specs=pl.BlockSpec((bm, HALF), oblk),
            scratch_shapes=[pltpu.VMEM((D_MODEL, D_FF), BF16), pltpu.VMEM((D_MODEL, D_FF), BF16),
                            pltpu.VMEM((D_FF, D_MODEL), BF16)]),
        out_shape=jax.ShapeDtypeStruct((n_slots, HALF), jnp.int32),
        compiler_params=_params(("arbitrary",)),
        name="moe_experts",
    )(block_e, block_rows, n_valid, xs, params['w_gate'], b4(params['b_gate']), params['w_up'], b4(params['b_up']),
      params['w_down'], b4(params['b_down']))


def _combine_kernel(off_ref, cnt_ref, np_ref, x1_ref, ir_ref, pr_ref, ys_hbm, g2_ref, b2_ref, o_ref,
                    stage_ref, ysb_ref, wgt_ref, acc_ref, sem, *, tm, n_tiles):
    i = pl.program_id(0)
    slot = i % 2
    mk = lambda s: (lambda dst, src: pltpu.make_async_copy(ys_hbm.at[src], stage_ref.at[s, dst], sem.at[s]))

    @pl.when(i == 0)
    def _():
        stage_ref[...] = jnp.zeros_like(stage_ref)
        _region_copies(off_ref, cnt_ref, 0, 0, mk(0), True)

    col_id = _small_int_bf16(lax.broadcasted_iota(jnp.int32, (tm, ONEHOT_CHUNK), 1))
    ir = ir_ref[...]
    pr = pr_ref[...].astype(BF16)
    zero = jnp.zeros((), BF16)

    def weighted_sum(p):
        tgt = [_slot_targets(ir[:, k:k + 1], ir[:, TOP_K + k:TOP_K + k + 1], p) for k in range(TOP_K)]
        for c in range(N_EXPERTS * REGION_CAP // ONEHOT_CHUNK):
            cols = slice(c * ONEHOT_CHUNK, (c + 1) * ONEHOT_CHUNK)
            wgt = zero
            for k in reversed(range(TOP_K)):
                wgt = jnp.where(col_id == _small_int_bf16(tgt[k] - c * ONEHOT_CHUNK), pr[:, k:k + 1], wgt)
            wgt_ref[:, cols] = wgt
            ysb_ref[cols, :] = _unpack_bf16_pairs(stage_ref[slot, cols, :])
        return _dot(wgt_ref[...], ysb_ref[...])

    _region_copies(off_ref, cnt_ref, i, 0, mk(slot), False)

    @pl.when(i + 1 < n_tiles)
    def _():
        _region_copies(off_ref, cnt_ref, i + 1, 0, mk(1 - slot), True)

    acc_ref[...] = weighted_sum(0)

    def extra_pass(p, carry):
        _region_copies(off_ref, cnt_ref, i, p, mk(slot), True)
        _region_copies(off_ref, cnt_ref, i, p, mk(slot), False)
        acc_ref[...] += weighted_sum(p)
        return carry

    lax.fori_loop(1, np_ref[i], extra_pass, 0)
    o_ref[...] = _layer_norm(DN_ALPHA * x1_ref[...] + acc_ref[...], g2_ref[...], b2_ref[...])


def _combine(x1, idx_rank, probs, ys, off, cnt8, n_pass, p, tm):
    t = x1.shape[0]
    row = lambda a: a.reshape(1, D_MODEL).astype(F32)
    full = lambda shp: pl.BlockSpec(shp, lambda i, *_: (0,) * len(shp))
    rowb = lambda w: pl.BlockSpec((tm, w), lambda i, *_: (i, 0))
    return pl.pallas_call(
        functools.partial(_combine_kernel, tm=tm, n_tiles=t // tm),
        grid_spec=pltpu.PrefetchScalarGridSpec(
            num_scalar_prefetch=3,
            grid=(t // tm,),
            in_specs=[rowb(D_MODEL), rowb(LANES), rowb(LANES), pl.BlockSpec(memory_space=pl.ANY),
                      full((1, D_MODEL)), full((1, D_MODEL))],
            out_specs=rowb(D_MODEL),
            scratch_shapes=[pltpu.VMEM((2, N_EXPERTS * REGION_CAP, HALF), jnp.int32),
                            pltpu.VMEM((N_EXPERTS * REGION_CAP, D_MODEL), BF16),
                            pltpu.VMEM((tm, N_EXPERTS * REGION_CAP), BF16),
                            pltpu.VMEM((tm, D_MODEL), F32),
                            pltpu.SemaphoreType.DMA((2,))]),
        out_shape=jax.ShapeDtypeStruct((t, D_MODEL), F32),
        compiler_params=_params(("arbitrary",)),
        name="moe_combine_ln2",
    )(off, cnt8, n_pass, x1, idx_rank, probs, ys, row(p['ln2_g']), row(p['ln2_b']))


def _moe(x1, x1b, idx_rank, irt, probs, tile_counts, p, params, l, bm, tm):
    t = x1.shape[0]
    n_tiles = t // tm
    cnt = tile_counts[:, 0, :N_EXPERTS]
    cnt8 = (cnt + 7) // 8 * 8
    seg = jnp.sum(cnt8, axis=0)
    padded = (seg + bm - 1) // bm * bm
    pend = jnp.cumsum(padded)
    pstart = pend - padded
    off = pstart[None, :] + jnp.cumsum(cnt8, axis=0) - cnt8
    n_pass = jnp.maximum((jnp.max(cnt, axis=1) + REGION_CAP - 1) // REGION_CAP, 1).astype(jnp.int32)
    nb = -(-(t * TOP_K + 7 * n_tiles * N_EXPERTS) // bm) + N_EXPERTS
    n_valid = (pend[-1] // bm).astype(jnp.int32).reshape(1)
    blocks = jnp.arange(nb, dtype=jnp.int32)
    block_e = jnp.minimum(jnp.sum((pend // bm)[None, :] <= blocks[:, None], axis=1), N_EXPERTS - 1).astype(jnp.int32)
    block_rows = jnp.clip(seg[block_e] - (blocks * bm - pstart[block_e]), 0, bm).astype(jnp.int32)
    off = off.astype(jnp.int32).reshape(-1)
    cnt8 = cnt8.astype(jnp.int32).reshape(-1)
    pad = jnp.concatenate([pstart + seg, padded - seg]).astype(jnp.int32)
    xs = _dispatch(x1b, irt, off, cnt8, n_pass, pad, n_valid, nb, bm, tm)
    ys = _experts(xs, block_e, block_rows, n_valid, params, l, bm)
    return _combine(x1, idx_rank, probs, ys, off, cnt8, n_pass, p, tm)


_EXPERT_WEIGHTS = ('w_gate', 'b_gate', 'w_up', 'b_up', 'w_down', 'b_down')


def _layer(x, params, l, b, s, cfg):
    p = {k: v[l] for k, v in params.items() if k not in _EXPERT_WEIGHTS}
    xb, x_groups = _prep(x, cfg['prep_tm'])
    w_in = p['w_in']
    o = IN_OFFS
    pad = jnp.zeros((D_MODEL, LANES - 2 * ML_HEADS), w_in.dtype)
    w_ret = w_in[:, o[0]:o[4]].astype(BF16)
    w_ml = jnp.concatenate([w_in[:, o[4]:o[5]], w_in[:, o[7]:o[8]]], axis=1).astype(BF16)
    w_if = jnp.concatenate([w_in[:, o[5]:o[7]], pad], axis=1).astype(BF16)
    w_gates = w_in[:, o[11]:o[12]].astype(BF16)
    y_ret = _retention(xb, w_ret, p['ret_gn'], b, s, cfg['seq_rows']).reshape(b * s, MIX_W)
    y_ml = _mlstm(xb, w_ml, w_if, p, b, s, cfg['seq_rows']).reshape(b * s, MIX_W)
    outs, lses = [], []
    wq = ATT_HEADS * ATT_DK
    wv = ATT_HEADS * ATT_DV
    for g in range(ATT_GROUPS):
        assert ATT_DK == 64
        w_att = jnp.concatenate([w_in[:, o[8] + g * wq:o[8] + (g + 1) * wq] * (ATT_DK ** -0.5),
                                 w_in[:, o[9] + g * wq:o[9] + (g + 1) * wq],
                                 w_in[:, o[10] + g * wv:o[10] + (g + 1) * wv]], axis=1).astype(BF16)
        d = ATT_PATTERNS[g][1]
        og, lg = _dilated_group(xb if d == 1 else x_groups[d], w_att, g, b, s, cfg['att_nq'])
        outs.append(og)
        lses.append(lg)
    x1, x1b, idx_rank, irt, probs, tile_counts = _merge(x, xb, y_ret, y_ml, outs, lses, w_gates, p, cfg['merge_tm'])
    return _moe(x1, x1b, idx_rank, irt, probs, tile_counts, p, params, l, cfg['moe_bm'], ROUTE_TM)


CFG = dict(prep_tm=1024, seq_rows=512, att_nq=4, merge_tm=512, moe_bm=512)

_PARAM_NAMES = ('w_in', 'ret_gn', 'ml_conv_w', 'ml_conv_b', 'ml_wq', 'ml_wk', 'ml_wv', 'ml_bi', 'ml_bf', 'ml_gn',
                'ml_skip', 'w_branch', 'w_out', 'ln1_g', 'ln1_b', 'w_router', 'b_router', 'w_gate', 'b_gate',
                'w_up', 'b_up', 'w_down', 'b_down', 'ln2_g', 'ln2_b')


def _forward(x, params, cfg):
    b, s, d = x.shape
    xf = x.reshape(b * s, d).astype(F32)
    for l in range(DEPTH):
        xf = _layer(xf, params, l, b, s, cfg)
    return xf.reshape(b, s, d).astype(x.dtype)


def kernel(x, w_in, ret_gn, ml_conv_w, ml_conv_b, ml_wq, ml_wk, ml_wv, ml_bi, ml_bf, ml_gn, ml_skip, w_branch, w_out, ln1_g, ln1_b, w_router, b_router, w_gate, b_gate, w_up, b_up, w_down, b_down, ln2_g, ln2_b):
    params = dict(zip(_PARAM_NAMES, (w_in, ret_gn, ml_conv_w, ml_conv_b, ml_wq, ml_wk, ml_wv, ml_bi, ml_bf, ml_gn,
                                     ml_skip, w_branch, w_out, ln1_g, ln1_b, w_router, b_router, w_gate, b_gate,
                                     w_up, b_up, w_down, b_down, ln2_g, ln2_b)))
    return _forward(x, params, CFG)
```

```python
import functools

import jax
import jax.numpy as jnp
import numpy as np
from jax import lax
from jax.experimental import pallas as pl
from jax.experimental.pallas import tpu as pltpu

F32 = jnp.float32
BF16 = jnp.bfloat16

D_MODEL = 1024
DEPTH = 2
MIX_W = D_MODEL // 2
N_BRANCHES = 3
RET_HEADS = 4
RET_DV = MIX_W // RET_HEADS
RET_DK = RET_DV // 2
ML_HEADS = 4
ML_DH = MIX_W // ML_HEADS
ML_CONV = 4
ML_QK_BLOCK = 4
ML_HEAD_GROUP = 1
ATT_PATTERNS = ((128, 1), (512, 4), (2048, 16))
ATT_GROUPS = len(ATT_PATTERNS)
ATT_HEADS = 4
ATT_DV = MIX_W // ATT_HEADS
ATT_DK = ATT_DV // 2
N_EXPERTS = 32
TOP_K = 4
D_FF = D_MODEL
SWIGLU_LIMIT = 7.0
SWIGLU_ALPHA = 1.702
DN_ALPHA = (2.0 * DEPTH) ** 0.25
EPS = 1e-5

CHUNK = 128
LANES = 128
NEG = -1e30
VMEM_LIMIT = 56 * 1024 * 1024

IN_SIZES = (RET_HEADS * RET_DK, RET_HEADS * RET_DK, MIX_W, MIX_W,
            MIX_W, ML_HEADS, ML_HEADS, MIX_W,
            ATT_GROUPS * ATT_HEADS * ATT_DK, ATT_GROUPS * ATT_HEADS * ATT_DK, ATT_GROUPS * ATT_HEADS * ATT_DV,
            N_BRANCHES * D_MODEL)
IN_OFFS = tuple(int(v) for v in np.cumsum((0,) + IN_SIZES))


def _params(sem):
    return pltpu.CompilerParams(dimension_semantics=sem, vmem_limit_bytes=VMEM_LIMIT)


def _dot(a, b):
    return jnp.dot(a, b, preferred_element_type=F32)


def _dot_nt(a, b):
    return lax.dot_general(a, b, (((1,), (1,)), ((), ())), preferred_element_type=F32)


def _dot_tn(a, b):
    return lax.dot_general(a, b, (((0,), (0,)), ((), ())), preferred_element_type=F32)


def _sigmoid(x):
    return 1.0 / (1.0 + jnp.exp(-x))


def _prep_kernel(x_ref, xb_ref, *rest, dilations):
    group_refs, cols_ref = rest[:-1], rest[-1]
    x = x_ref[...]
    xb_ref[...] = x.astype(xb_ref.dtype)
    tm, dm = x.shape
    for c in range(dm // LANES):
        cols_ref[c] = x[:, c * LANES:(c + 1) * LANES]
    for o_ref, d in zip(group_refs, dilations):
        n = tm // d
        for r in range(d):
            for c in range(dm // LANES):
                o_ref[:, r * dm + c * LANES:r * dm + (c + 1) * LANES] = (
                    cols_ref[c, pl.ds(r, n, stride=d), :].astype(o_ref.dtype))


def _prep(x, tm):
    t, dm = x.shape
    tm = min(tm, t)
    dilations = tuple(sorted({d for _, d in ATT_PATTERNS if d > 1}))
    assert t % tm == 0 and all(tm % (16 * d) == 0 for d in dilations)
    outs = pl.pallas_call(
        functools.partial(_prep_kernel, dilations=dilations),
        grid=(t // tm,),
        in_specs=[pl.BlockSpec((tm, dm), lambda i: (i, 0))],
        out_specs=[pl.BlockSpec((tm, dm), lambda i: (i, 0))]
                  + [pl.BlockSpec((tm // d, d * dm), lambda i: (i, 0)) for d in dilations],
        out_shape=[jax.ShapeDtypeStruct((t, dm), BF16)]
                  + [jax.ShapeDtypeStruct((t // d, d * dm), BF16) for d in dilations],
        scratch_shapes=[pltpu.VMEM((dm // LANES, tm, LANES), F32)],
        compiler_params=_params(("arbitrary",)),
        name="prep_bf16_groups",
    )(x)
    return outs[0], dict(zip(dilations, outs[1:]))


def _head_norm(o, gn):
    mu = jnp.mean(o, axis=-1, keepdims=True)
    oc = o - mu
    var = jnp.mean(oc * oc, axis=-1, keepdims=True)
    return oc * lax.rsqrt(var + EPS) * gn


def _ret_kernel(x_ref, w_ref, dm_ref, qd_ref, kd_ref, cd_ref, gn_ref, o_ref, st_ref, blk_ref, *, nc, nb):
    @pl.when(pl.program_id(0) == 0)
    def _():
        st_ref[...] = jnp.zeros_like(st_ref)

    for bi in range(nb):
        blk_ref[bi] = _dot(x_ref[bi], w_ref[...]).astype(blk_ref.dtype)

    hq = RET_HEADS * RET_DK
    state = [st_ref[j] for j in range(nb * RET_HEADS)]
    for c in range(nc):
        rows = pl.ds(c * CHUNK, CHUNK)
        chains = [(h, bi) for h in range(RET_HEADS) for bi in range(nb)]
        first = []
        for h, bi in chains:
            q = blk_ref[bi, rows, h * RET_DK:(h + 1) * RET_DK]
            k = blk_ref[bi, rows, hq + h * RET_DK:hq + (h + 1) * RET_DK]
            v = blk_ref[bi, rows, 2 * hq + h * RET_DV:2 * hq + (h + 1) * RET_DV]
            st = state[bi * RET_HEADS + h]
            first.append((v, _dot_nt(q, k), _dot(q, st.astype(BF16)),
                          _dot_tn(k, (v.astype(F32) * kd_ref[h]).astype(BF16))))
        for (h, bi), (v, s_qk, q_st, kv) in zip(chains, first):
            o = _dot((s_qk * dm_ref[h]).astype(BF16), v) + q_st * qd_ref[h]
            state[bi * RET_HEADS + h] = cd_ref[h] * state[bi * RET_HEADS + h] + kv
            g = blk_ref[bi, rows, 2 * hq + MIX_W + h * RET_DV:2 * hq + MIX_W + (h + 1) * RET_DV].astype(F32)
            y = _head_norm(o, gn_ref[:, h * RET_DV:(h + 1) * RET_DV]) * (g * _sigmoid(g))
            o_ref[bi, rows, h * RET_DV:(h + 1) * RET_DV] = y.astype(o_ref.dtype)
    for j in range(nb * RET_HEADS):
        st_ref[j] = state[j]


def _retention(xb, w_ret, ret_gn, b, s, rows):
    nc = rows // CHUNK
    scale = RET_DK ** -0.5
    log_gamma = jnp.log1p(-jnp.exp2(-5.0 - jnp.arange(RET_HEADS, dtype=F32)))
    pos = jnp.arange(CHUNK, dtype=F32)
    diff = pos[:, None] - pos[None, :]
    dm = jnp.where(diff >= 0, jnp.exp(log_gamma[:, None, None] * jnp.maximum(diff, 0.0)), 0.0) * scale
    kd = jnp.broadcast_to(jnp.exp(log_gamma[:, None] * (CHUNK - 1.0 - pos))[:, :, None], (RET_HEADS, CHUNK, RET_DV))
    qd = jnp.broadcast_to((jnp.exp(log_gamma[:, None] * (pos + 1.0)) * scale)[:, :, None], (RET_HEADS, CHUNK, RET_DV))
    cd = jnp.broadcast_to(jnp.exp(log_gamma * CHUNK)[:, None, None], (RET_HEADS, 1, RET_DV))
    dmod, w_in = w_ret.shape
    full = lambda shp: pl.BlockSpec(shp, lambda si: (0,) * len(shp))
    return pl.pallas_call(
        functools.partial(_ret_kernel, nc=nc, nb=b),
        grid=(s // rows,),
        in_specs=[pl.BlockSpec((b, rows, dmod), lambda si: (0, si, 0)), full((dmod, w_in)),
                  full((RET_HEADS, CHUNK, CHUNK)), full((RET_HEADS, CHUNK, RET_DV)),
                  full((RET_HEADS, CHUNK, RET_DV)), full((RET_HEADS, 1, RET_DV)), full((1, MIX_W))],
        out_specs=pl.BlockSpec((b, rows, MIX_W), lambda si: (0, si, 0)),
        out_shape=jax.ShapeDtypeStruct((b, s, MIX_W), BF16),
        scratch_shapes=[pltpu.VMEM((b * RET_HEADS, RET_DK, RET_DV), F32), pltpu.VMEM((b, rows, w_in), BF16)],
        compiler_params=_params(("arbitrary",)),
        name="retention",
    )(xb.reshape(b, s, dmod), w_ret, dm, qd, kd, cd, ret_gn.reshape(1, MIX_W).astype(F32))


def _log_sigmoid(x):
    return jnp.minimum(x, 0.0) - jnp.log(1.0 + jnp.exp(-jnp.abs(x)))


def _ml_kernel(x_ref, wml_ref, wif_ref, cw_ref, cb_ref, bq_ref, bk_ref, bv_ref, gb_ref, gn_ref, sk_ref,
               o_ref, c_ref, n_ref, m_ref, halo_ref, ml_ref, if_ref, qkv_ref, *, nc, nb):
    rows_total = nc * CHUNK

    @pl.when(pl.program_id(0) == 0)
    def _():
        c_ref[...] = jnp.zeros_like(c_ref)
        n_ref[...] = jnp.zeros_like(n_ref)
        m_ref[...] = jnp.zeros_like(m_ref)
        halo_ref[...] = jnp.zeros_like(halo_ref)

    for bi in range(nb):
        ml_ref[bi] = _dot(x_ref[bi], wml_ref[...]).astype(ml_ref.dtype)
        if_ref[bi] = _dot(x_ref[bi], wif_ref[...])

    x_all, mc_all = [], []
    for bi in range(nb):
        xa = ml_ref[bi, :, 0:MIX_W].astype(F32)
        xf = jnp.concatenate([halo_ref[bi], xa], axis=0)
        acc = jnp.broadcast_to(cb_ref[...], (rows_total, MIX_W))
        for j in range(ML_CONV):
            sh = ML_CONV - 1 - j
            xs = xf if sh == 0 else pltpu.roll(xf, sh, 0)
            acc = acc + xs[8:8 + rows_total] * cw_ref[j:j + 1, :]
        halo_ref[bi] = xa[rows_total - 8:rows_total]
        x_all.append(xa)
        mc_all.append(acc * _sigmoid(acc))

    ri = lax.broadcasted_iota(jnp.int32, (CHUNK, CHUNK), 0)
    ci = lax.broadcasted_iota(jnp.int32, (CHUNK, CHUNK), 1)
    tri = ri >= ci
    tril = tri.astype(F32)
    kscale = ML_DH ** -0.5
    log_kscale = float(np.log(kscale))

    c_st = [[c_ref[bi * ML_HEADS + h] for h in range(ML_HEADS)] for bi in range(nb)]
    n_st = [[n_ref[bi, h:h + 1, :] for h in range(ML_HEADS)] for bi in range(nb)]
    m_st = [[m_ref[bi, h:h + 1, 0:1] for h in range(ML_HEADS)] for bi in range(nb)]

    for bi in range(nb):
        mcb = mc_all[bi].astype(BF16)
        xvb = x_all[bi].astype(BF16)
        for h in range(ML_HEADS):
            cols = slice(h * ML_DH, (h + 1) * ML_DH)
            qkv_ref[bi, 0, :, cols] = _dot(mcb[:, cols], bq_ref[h]).astype(BF16)
            qkv_ref[bi, 1, :, cols] = _dot(mcb[:, cols], bk_ref[h]).astype(BF16)
            qkv_ref[bi, 2, :, cols] = _dot(xvb[:, cols], bv_ref[h]).astype(BF16)

    def run_chains(r0, gates, chains):
        for h, bi in chains:
            g_pre, cum, g_t, cum_t = gates[bi]
            cols = slice(h * ML_DH, (h + 1) * ML_DH)
            mc = mc_all[bi][r0:r0 + CHUNK, cols]
            q = qkv_ref[bi, 0, r0:r0 + CHUNK, cols]
            k = qkv_ref[bi, 1, r0:r0 + CHUNK, cols]
            v = qkv_ref[bi, 2, r0:r0 + CHUNK, cols]
            cum_col = cum[:, ML_HEADS + h:ML_HEADS + h + 1]
            ig_col = g_pre[:, h:h + 1]
            cum_row = cum_t[ML_HEADS + h:ML_HEADS + h + 1, :]
            ig_row = g_t[h:h + 1, :]
            tot = cum[CHUNK - 1:CHUNK, ML_HEADS + h:ML_HEADS + h + 1]
            m_prev, c_prev, n_prev = m_st[bi][h], c_st[bi][h], n_st[bi][h]
            dlog = jnp.where(tri, cum_col + (ig_row - cum_row), NEG)
            m_inter = cum_col + m_prev
            m_q = jnp.maximum(m_inter, jnp.max(dlog, axis=-1, keepdims=True))
            w_qk = jnp.exp(dlog - (m_q - log_kscale)) * _dot_nt(q, k)
            inter = jnp.exp(m_inter - m_q)
            num = _dot(w_qk.astype(BF16), v) + inter * _dot(q, c_prev.astype(BF16))
            den = (jnp.sum(w_qk, axis=-1, keepdims=True)
                   + inter * jnp.sum(q.astype(F32) * n_prev, axis=-1, keepdims=True))
            hh = num * (1.0 / jnp.maximum(jnp.abs(den), jnp.exp(-m_q)))
            a_col = tot - cum_col + ig_col
            a_max = jnp.max(a_col, axis=0, keepdims=True)
            wa = jnp.exp(a_col - a_max) * kscale
            chunk_c = _dot_tn(k, (wa * v.astype(F32)).astype(BF16))
            chunk_n = jnp.sum(wa * k.astype(F32), axis=0, keepdims=True)
            m_new = jnp.maximum(tot + m_prev, a_max)
            s_old = jnp.exp(tot + m_prev - m_new)
            s_new = jnp.exp(a_max - m_new)
            c_st[bi][h] = s_old * c_prev + s_new * chunk_c
            n_st[bi][h] = s_old * n_prev + s_new * chunk_n
            m_st[bi][h] = m_new
            mo = ml_ref[bi, r0:r0 + CHUNK, MIX_W + h * ML_DH:MIX_W + (h + 1) * ML_DH].astype(F32)
            y = _sigmoid(mo) * (_head_norm(hh, gn_ref[:, cols]) + sk_ref[:, cols] * mc)
            o_ref[bi, r0:r0 + CHUNK, cols] = y.astype(o_ref.dtype)

    for c in range(nc):
        r0 = c * CHUNK
        gates = []
        for bi in range(nb):
            g_pre = if_ref[bi, r0:r0 + CHUNK, :] + gb_ref[...]
            cum = jnp.dot(tril, _log_sigmoid(g_pre), preferred_element_type=F32, precision=lax.Precision.HIGHEST)
            gates.append((g_pre, cum, g_pre.T, cum.T))
        for h0 in range(0, ML_HEADS, ML_HEAD_GROUP):
            run_chains(r0, gates, [(h, bi) for h in range(h0, h0 + ML_HEAD_GROUP) for bi in range(nb)])

    for bi in range(nb):
        for h in range(ML_HEADS):
            c_ref[bi * ML_HEADS + h] = c_st[bi][h]
            n_ref[bi, h:h + 1, :] = n_st[bi][h]
            m_ref[bi, h:h + 1, :] = jnp.broadcast_to(m_st[bi][h], (1, LANES))


def _block_diag(w):
    nb = w.shape[0]
    per = nb // ML_HEADS
    wh = w.reshape(ML_HEADS, per, ML_QK_BLOCK, ML_QK_BLOCK)
    eye = jnp.eye(per, dtype=w.dtype)
    bd = jnp.einsum('hncd,nm->hncmd', wh, eye)
    return bd.reshape(ML_HEADS, per * ML_QK_BLOCK, per * ML_QK_BLOCK).astype(BF16)


def _mlstm(xb, w_ml, w_if, p, b, s, rows):
    nc = rows // CHUNK
    dm = xb.shape[1]
    full = lambda shp: pl.BlockSpec(shp, lambda si: (0,) * len(shp))
    gb = jnp.zeros((1, LANES), F32).at[0, 0:ML_HEADS].set(p['ml_bi'].astype(F32)).at[0, ML_HEADS:2 * ML_HEADS].set(p['ml_bf'].astype(F32))
    row = lambda a: a.reshape(1, MIX_W).astype(F32)
    return pl.pallas_call(
        functools.partial(_ml_kernel, nc=nc, nb=b),
        grid=(s // rows,),
        in_specs=[pl.BlockSpec((b, rows, dm), lambda si: (0, si, 0)),
                  full((dm, 2 * MIX_W)), full((dm, LANES)),
                  full((ML_CONV, MIX_W)), full((1, MIX_W)),
                  full((ML_HEADS, ML_DH, ML_DH)), full((ML_HEADS, ML_DH, ML_DH)), full((ML_HEADS, ML_DH, ML_DH)),
                  full((1, LANES)), full((1, MIX_W)), full((1, MIX_W))],
        out_specs=pl.BlockSpec((b, rows, MIX_W), lambda si: (0, si, 0)),
        out_shape=jax.ShapeDtypeStruct((b, s, MIX_W), BF16),
        scratch_shapes=[pltpu.VMEM((b * ML_HEADS, ML_DH, ML_DH), F32), pltpu.VMEM((b, 8, ML_DH), F32),
                        pltpu.VMEM((b, 8, LANES), F32), pltpu.VMEM((b, 8, MIX_W), F32),
                        pltpu.VMEM((b, rows, 2 * MIX_W), BF16), pltpu.VMEM((b, rows, LANES), F32),
                        pltpu.VMEM((b, 3, rows, MIX_W), BF16)],
        compiler_params=_params(("arbitrary",)),
        name="mlstm",
    )(xb.reshape(b, s, dm), w_ml, w_if,
      p['ml_conv_w'].astype(F32), row(p['ml_conv_b']),
      _block_diag(p['ml_wq']), _block_diag(p['ml_wk']), _block_diag(p['ml_wv']),
      gb, row(p['ml_gn']), row(p['ml_skip']))


def _att_kernel(x_ref, w_ref, bias_ref, o_ref, lse_ref, qkv_ref, *, nq):
    first = pl.program_id(2) == 0
    span = nq * CHUNK
    wq = ATT_HEADS * ATT_DK

    @pl.when(first)
    def _():
        qkv_ref[0:CHUNK, :] = jnp.zeros((CHUNK, qkv_ref.shape[1]), qkv_ref.dtype)

    qkv_ref[CHUNK:CHUNK + span, :] = _dot(x_ref[...], w_ref[...]).astype(qkv_ref.dtype)
    lane = lax.broadcasted_iota(jnp.int32, (CHUNK, LANES), 1)
    ones = jnp.ones((2 * CHUNK, ATT_DV), BF16)
    for j in range(nq):
        rows = slice(j * CHUNK, (j + 1) * CHUNK)
        kv_rows = slice(j * CHUNK, (j + 2) * CHUNK)
        lse_out = jnp.zeros((CHUNK, LANES), F32)
        for h in range(ATT_HEADS):
            dv = slice(h * ATT_DV, (h + 1) * ATT_DV)
            q = qkv_ref[CHUNK + j * CHUNK:CHUNK + (j + 1) * CHUNK, h * ATT_DK:(h + 1) * ATT_DK]
            kk = qkv_ref[kv_rows, wq + h * ATT_DK:wq + (h + 1) * ATT_DK]
            vv = qkv_ref[kv_rows, 2 * wq + h * ATT_DV:2 * wq + (h + 1) * ATT_DV]
            bias = bias_ref[jnp.where(first, 0, 1), h] if j == 0 else bias_ref[1, h]
            s = _dot_nt(q, kk) + bias
            m = jnp.max(s, axis=-1, keepdims=True)
            p = jnp.exp(s - m).astype(BF16)
            oa = _dot(p, jnp.concatenate([vv, ones], axis=1))
            l = oa[:, ATT_DV:2 * ATT_DV]
            o_ref[rows, dv] = (oa[:, 0:ATT_DV] * (1.0 / l)).astype(o_ref.dtype)
            lse_out = jnp.where(lane == h, m + jnp.log(l), lse_out)
        lse_ref[rows, :] = lse_out
    qkv_ref[0:CHUNK, :] = qkv_ref[span:span + CHUNK, :]


def _att_bias(window, dilation, slopes):
    wb = window // dilation
    qi = jnp.arange(wb)[:, None]
    kj = jnp.arange(2 * wb)[None, :]
    delta = qi + wb - kj
    band = (delta >= 0) & (delta <= wb)
    bias = -slopes[:, None, None] * (dilation * delta).astype(F32)
    later = jnp.where(band[None], bias, NEG)
    first = jnp.where((band & (kj >= wb))[None], bias, NEG)
    return jnp.stack([first, later]).astype(F32)


def _dilated_group(xg, w_qkv, g, b, s, nq):
    window, d = ATT_PATTERNS[g]
    assert window // d == CHUNK
    l_sub = s // d
    assert l_sub % CHUNK == 0
    n_blk = l_sub // CHUNK
    slopes = jnp.exp2(-8.0 * jnp.arange(1, ATT_GROUPS * ATT_HEADS + 1, dtype=F32) / (ATT_GROUPS * ATT_HEADS))
    bias = _att_bias(window, d, slopes.reshape(ATT_GROUPS, ATT_HEADS)[g])
    wq = ATT_HEADS * ATT_DK
    wv = ATT_HEADS * ATT_DV
    per_tok = 2 * wq + wv
    dm = w_qkv.shape[0]
    assert xg.shape == (b * l_sub, d * dm)
    x = xg.reshape(b, l_sub, d * dm)
    nq = min(nq, n_blk)
    assert n_blk % nq == 0
    span = nq * CHUNK
    o, lse = pl.pallas_call(
        functools.partial(_att_kernel, nq=nq),
        grid=(b, d, n_blk // nq),
        in_specs=[pl.BlockSpec((None, span, dm), lambda bi, r, n: (bi, n, r)),
                  pl.BlockSpec((dm, per_tok), lambda bi, r, n: (0, 0)),
                  pl.BlockSpec((2, ATT_HEADS, CHUNK, 2 * CHUNK), lambda bi, r, n: (0, 0, 0, 0))],
        out_specs=[pl.BlockSpec((None, span, wv), lambda bi, r, n: (bi, n, r)),
                   pl.BlockSpec((None, span, LANES), lambda bi, r, n: (bi, n, r))],
        out_shape=[jax.ShapeDtypeStruct((b, l_sub, d * wv), BF16),
                   jax.ShapeDtypeStruct((b, l_sub, d * LANES), F32)],
        scratch_shapes=[pltpu.VMEM((CHUNK + span, per_tok), BF16)],
        compiler_params=_params(("arbitrary", "arbitrary", "arbitrary")),
        name=f"dilated_attn_g{g}",
    )(x, w_qkv, bias)
    return o.reshape(b * l_sub, d * wv), lse.reshape(b * l_sub, d * LANES)


def _layer_norm(z, g, b):
    mu = jnp.mean(z, axis=-1, keepdims=True)
    zc = z - mu
    var = jnp.mean(zc * zc, axis=-1, keepdims=True)
    return zc * lax.rsqrt(var + EPS) * g + b


def _merge_kernel(x_ref, yr_ref, ym_ref, o0_ref, o1_ref, o2_ref, l0_ref, l1_ref, l2_ref, xb_ref,
                  wg_ref, wb_ref, wo_ref, g1_ref, b1_ref, wrh_ref, wrl_ref, br_ref,
                  x1_ref, x1b_ref, ir_ref, irt_ref, pr_ref, cnt_ref, og_ref, lg_ref, *, tm):
    for g, (src_o, src_l) in enumerate(((o0_ref, l0_ref), (o1_ref, l1_ref), (o2_ref, l2_ref))):
        d = ATT_PATTERNS[g][1]
        n = tm // d
        for r in range(d):
            rows = pl.ds(r, n, stride=d) if d > 1 else pl.ds(0, n)
            for h in range(ATT_HEADS):
                og_ref[g, h, rows, :] = src_o[:, r * MIX_W + h * ATT_DV:r * MIX_W + (h + 1) * ATT_DV].astype(F32)
            lg_ref[g, rows, :] = src_l[:, r * LANES:(r + 1) * LANES]

    l0, l1, l2 = lg_ref[0], lg_ref[1], lg_ref[2]
    lm = jnp.maximum(jnp.maximum(l0, l1), l2)
    e0, e1, e2 = jnp.exp(l0 - lm), jnp.exp(l1 - lm), jnp.exp(l2 - lm)
    inv = 1.0 / (e0 + e1 + e2)
    parts = []
    for h in range(ATT_HEADS):
        dv = slice(h * ATT_DV, (h + 1) * ATT_DV)
        hs = slice(h, h + 1)
        parts.append((e0[:, hs] * og_ref[0, h] + e1[:, hs] * og_ref[1, h] + e2[:, hs] * og_ref[2, h]) * inv[:, hs])
    y_att = jnp.concatenate(parts, axis=-1).astype(BF16)

    xb = xb_ref[...]
    gate = lambda j: _sigmoid(_dot(xb, wg_ref[:, j * D_MODEL:(j + 1) * D_MODEL]))
    merged = gate(0) * _dot(yr_ref[...], wb_ref[0])
    merged = merged + gate(1) * _dot(ym_ref[...], wb_ref[1])
    merged = merged + gate(2) * _dot(y_att, wb_ref[2])
    z = DN_ALPHA * x_ref[...] + _dot(merged.astype(BF16), wo_ref[...])
    x1 = _layer_norm(z, g1_ref[...], b1_ref[...])
    x1_ref[...] = x1
    x1b_ref[...] = x1.astype(BF16)

    xh = x1.astype(BF16)
    xl = (x1 - xh.astype(F32)).astype(BF16)
    logits = _dot(xh, wrh_ref[...]) + _dot(xl, wrh_ref[...]) + _dot(xh, wrl_ref[...]) + br_ref[...]
    lane = lax.broadcasted_iota(jnp.int32, (tm, LANES), 1)
    work = jnp.where(lane < N_EXPERTS, logits, -jnp.inf)
    vals, idxs = [], []
    for _ in range(TOP_K):
        m = jnp.max(work, axis=-1, keepdims=True)
        idx = jnp.min(jnp.where(work == m, lane, LANES), axis=-1, keepdims=True)
        vals.append(m)
        idxs.append(idx)
        work = jnp.where(lane == idx, -jnp.inf, work)
    es = [jnp.exp(v - vals[0]) for v in vals]
    tot = es[0] + es[1] + es[2] + es[3]
    pr = jnp.zeros((tm, LANES), F32)
    for k in range(TOP_K):
        pr = jnp.where(lane == k, es[k] / tot, pr)
    pr_ref[...] = pr

    onehot = jnp.zeros((tm, LANES), F32)
    for k in range(TOP_K):
        onehot = onehot + (lane == idxs[k]).astype(F32)
    ri = lax.broadcasted_iota(jnp.int32, (tm, tm), 0)
    ci = lax.broadcasted_iota(jnp.int32, (tm, tm), 1)
    same_tile = (ri // ROUTE_TM) == (ci // ROUTE_TM)
    before = _dot(((ri > ci) & same_tile).astype(BF16), onehot.astype(BF16))
    ir = jnp.zeros((tm, LANES), F32)
    for k in range(TOP_K):
        rank = jnp.sum(jnp.where(lane == idxs[k], before, 0.0), axis=-1, keepdims=True)
        ir = jnp.where(lane == k, idxs[k].astype(F32), ir)
        ir = jnp.where(lane == TOP_K + k, rank, ir)
    ir_ref[...] = ir.astype(jnp.int32)
    irt_ref[...] = ir.T[0:8, :]
    for j in range(tm // ROUTE_TM):
        cnt_j = jnp.sum(onehot[j * ROUTE_TM:(j + 1) * ROUTE_TM], axis=0, keepdims=True)
        cnt_ref[j] = jnp.broadcast_to(cnt_j, (8, LANES)).astype(jnp.int32)


def _merge(x, xb, y_ret, y_ml, outs, lses, w_gates, p, tm):
    t = x.shape[0]
    tm = min(tm, t)
    rowb = lambda w: pl.BlockSpec((tm, w), lambda i: (i, 0))
    full = lambda shp: pl.BlockSpec(shp, lambda i: (0,) * len(shp))
    wr = jnp.zeros((D_MODEL, LANES), F32).at[:, :N_EXPERTS].set(p['w_router'].astype(F32))
    wrh = wr.astype(BF16)
    wrl = (wr - wrh.astype(F32)).astype(BF16)
    br = jnp.zeros((1, LANES), F32).at[0, :N_EXPERTS].set(p['b_router'].astype(F32))
    row = lambda a: a.reshape(1, D_MODEL).astype(F32)
    return pl.pallas_call(
        functools.partial(_merge_kernel, tm=tm),
        grid=(t // tm,),
        in_specs=[rowb(D_MODEL), rowb(MIX_W), rowb(MIX_W)]
                 + [pl.BlockSpec((tm // d, d * MIX_W), lambda i: (i, 0)) for _, d in ATT_PATTERNS]
                 + [pl.BlockSpec((tm // d, d * LANES), lambda i: (i, 0)) for _, d in ATT_PATTERNS]
                 + [rowb(D_MODEL), full((D_MODEL, N_BRANCHES * D_MODEL)),
                  full((N_BRANCHES, MIX_W, D_MODEL)), full((D_MODEL, D_MODEL)),
                  full((1, D_MODEL)), full((1, D_MODEL)),
                  full((D_MODEL, LANES)), full((D_MODEL, LANES)), full((1, LANES))],
        out_specs=[rowb(D_MODEL), rowb(D_MODEL), rowb(LANES), pl.BlockSpec((8, tm), lambda i: (0, i)), rowb(LANES),
                   pl.BlockSpec((tm // ROUTE_TM, 8, LANES), lambda i: (i, 0, 0))],
        out_shape=[jax.ShapeDtypeStruct((t, D_MODEL), F32),
                   jax.ShapeDtypeStruct((t, D_MODEL), BF16),
                   jax.ShapeDtypeStruct((t, LANES), jnp.int32),
                   jax.ShapeDtypeStruct((8, t), F32),
                   jax.ShapeDtypeStruct((t, LANES), F32),
                   jax.ShapeDtypeStruct((t // ROUTE_TM, 8, LANES), jnp.int32)],
        scratch_shapes=[pltpu.VMEM((ATT_GROUPS, ATT_HEADS, tm, ATT_DV), F32),
                        pltpu.VMEM((ATT_GROUPS, tm, LANES), F32)],
        compiler_params=_params(("arbitrary",)),
        name="merge_ln1_router",
    )(x, y_ret, y_ml, outs[0], outs[1], outs[2], lses[0], lses[1], lses[2], xb, w_gates,
      p['w_branch'].astype(BF16), p['w_out'].astype(BF16), row(p['ln1_g']), row(p['ln1_b']), wrh, wrl, br)


ROUTE_TM = 256
REGION_CAP = 64
SEGMENT_SLACK = REGION_CAP - 8
PAD_PIECES = (512, 256, 128, 64, 32, 16, 8)
ONEHOT_CHUNK = 256
HALF = D_MODEL // 2
HI_MASK = -65536


def _pack_bf16_pairs(y):
    lo = lax.shift_right_logical(lax.bitcast_convert_type(y[:, :HALF], jnp.int32), 16)
    hi = lax.bitcast_convert_type(y[:, HALF:], jnp.int32) & HI_MASK
    return lo | hi


def _unpack_bf16_pairs(w):
    lo = lax.bitcast_convert_type(lax.shift_left(w, 16), F32)
    hi = lax.bitcast_convert_type(w & HI_MASK, F32)
    return jnp.concatenate([lo, hi], axis=1).astype(BF16)


def _row_pieces(n, pieces, fn):
    for size in pieces:
        if size == pieces[0]:
            cond, pos = n >= size, 0
        else:
            cond, pos = (n & size) != 0, pl.multiple_of(n & (-2 * size), 8)

        @pl.when(cond)
        def _(pos=pos, size=size):
            fn(pos, size)


def _region_copies(off_ref, tile, p, make_copy, start):
    for e in range(N_EXPERTS):
        base = off_ref[tile * N_EXPERTS + e] + p * REGION_CAP
        cp = make_copy(pl.ds(e * REGION_CAP, REGION_CAP), pl.ds(pl.multiple_of(base, 8), REGION_CAP))
        if start:
            cp.start()
        else:
            cp.wait()


def _slot_targets(idx, lrank, p):
    lr = lrank - p * REGION_CAP
    return jnp.where((lr >= 0) & (lr < REGION_CAP), idx * REGION_CAP + lr, -1)


def _small_int_bf16(v):
    return v.astype(F32).astype(BF16)


def _dispatch_kernel(off_ref, np_ref, pad_ref, nv_ref, x_ref, irt_ref, xs_hbm,
                     stage_ref, zero_ref, sem, zsem, *, tm, n_tiles, bm, nb):
    i = pl.program_id(0)
    slot = i % 2
    row_id = _small_int_bf16(lax.broadcasted_iota(jnp.int32, (ONEHOT_CHUNK, tm), 0))
    meta = irt_ref[...].astype(jnp.int32)
    x = x_ref[...]
    mk = lambda s: (lambda src, dst: pltpu.make_async_copy(stage_ref.at[s, src], xs_hbm.at[dst], sem.at[s]))

    def build(p):
        tgt = [_slot_targets(meta[k:k + 1, :], meta[TOP_K + k:TOP_K + k + 1, :], p) for k in range(TOP_K)]
        for c in range(N_EXPERTS * REGION_CAP // ONEHOT_CHUNK):
            hit = row_id == _small_int_bf16(tgt[0] - c * ONEHOT_CHUNK)
            for k in range(1, TOP_K):
                hit = hit | (row_id == _small_int_bf16(tgt[k] - c * ONEHOT_CHUNK))
            sel = jnp.where(hit, jnp.ones((), BF16), jnp.zeros((), BF16))
            stage_ref[slot, c * ONEHOT_CHUNK:(c + 1) * ONEHOT_CHUNK, :] = _pack_bf16_pairs(_dot(sel, x))

    build(0)

    @pl.when(i > 0)
    def _():
        _region_copies(off_ref, i - 1, np_ref[jnp.maximum(i - 1, 0)] - 1, mk(1 - slot), False)

    _region_copies(off_ref, i, 0, mk(slot), True)

    def extra_pass(p, carry):
        _region_copies(off_ref, i, p - 1, mk(slot), False)
        build(p)
        _region_copies(off_ref, i, p, mk(slot), True)
        return carry

    lax.fori_loop(1, np_ref[i], extra_pass, 0)

    @pl.when(i == n_tiles - 1)
    def _():
        _region_copies(off_ref, i, np_ref[i] - 1, mk(slot), False)
        zero_ref[...] = jnp.zeros_like(zero_ref)
        for phase in (True, False):
            for e in range(N_EXPERTS):
                def piece(pos, size, e=e):
                    cp = pltpu.make_async_copy(zero_ref.at[pl.ds(0, size)],
                                               xs_hbm.at[pl.ds(pl.multiple_of(pad_ref[e] + pos, 8), size)], zsem)
                    cp.start() if phase else cp.wait()
                _row_pieces(pad_ref[N_EXPERTS + e], PAD_PIECES, piece)

        def zero_block(blk, carry):
            for half in range(bm // PAD_PIECES[0]):
                cp = pltpu.make_async_copy(
                    zero_ref, xs_hbm.at[pl.ds(pl.multiple_of(blk * bm + half * PAD_PIECES[0], 8), PAD_PIECES[0])], zsem)
                cp.start()
                cp.wait()
            return carry

        lax.fori_loop(nv_ref[0], nb, zero_block, 0)


def _dispatch(x1b, irt, off, n_pass, pad, n_valid, nb, bm, tm):
    t = x1b.shape[0]
    assert bm % PAD_PIECES[0] == 0 and bm + SEGMENT_SLACK < 2 * PAD_PIECES[0]
    return pl.pallas_call(
        functools.partial(_dispatch_kernel, tm=tm, n_tiles=t // tm, bm=bm, nb=nb),
        grid_spec=pltpu.PrefetchScalarGridSpec(
            num_scalar_prefetch=4,
            grid=(t // tm,),
            in_specs=[pl.BlockSpec((tm, D_MODEL), lambda i, *_: (i, 0)),
                      pl.BlockSpec((8, tm), lambda i, *_: (0, i))],
            out_specs=pl.BlockSpec(memory_space=pl.ANY),
            scratch_shapes=[pltpu.VMEM((2, N_EXPERTS * REGION_CAP, HALF), jnp.int32),
                            pltpu.VMEM((PAD_PIECES[0], HALF), jnp.int32),
                            pltpu.SemaphoreType.DMA((2,)), pltpu.SemaphoreType.DMA(())]),
        out_shape=jax.ShapeDtypeStruct((nb * bm, HALF), jnp.int32),
        compiler_params=_params(("arbitrary",)),
        name="moe_dispatch",
    )(off, n_pass, pad, n_valid, x1b, irt)


def _ffn_kernel(be_ref, nr_ref, nv_ref, xs_ref, wg_ref, bg_ref, wu_ref, bu_ref, wd_ref, bd_ref, o_ref,
                wgb_ref, wub_ref, wdb_ref):
    i = pl.program_id(0)

    @pl.when(i < nv_ref[0])
    def _():
        @pl.when((i == 0) | (be_ref[i] != be_ref[jnp.maximum(i - 1, 0)]))
        def _():
            wgb_ref[...] = wg_ref[...].astype(BF16)
            wub_ref[...] = wu_ref[...].astype(BF16)
            wdb_ref[...] = wd_ref[...].astype(BF16)

        rows = lax.broadcasted_iota(jnp.int32, xs_ref.shape, 0)
        x = _unpack_bf16_pairs(jnp.where(rows < nr_ref[i], xs_ref[...], 0))
        gate = jnp.minimum(_dot(x, wgb_ref[...]) + bg_ref[...], SWIGLU_LIMIT)
        up = jnp.clip(_dot(x, wub_ref[...]) + bu_ref[...], -SWIGLU_LIMIT, SWIGLU_LIMIT)
        hid = (up + 1.0) * gate * _sigmoid(SWIGLU_ALPHA * gate)
        y = _dot(hid.astype(BF16), wdb_ref[...]) + bd_ref[...]
        o_ref[...] = _pack_bf16_pairs(y.astype(BF16).astype(F32))

    @pl.when(i >= nv_ref[0])
    def _():
        o_ref[...] = jnp.zeros_like(o_ref)


def _experts(xs, block_e, block_rows, n_valid, params, l, bm):
    n_slots = xs.shape[0]
    nb = n_slots // bm
    blk = lambda i, be, nr, nv: (jnp.minimum(i, nv[0] - 1), 0)
    oblk = lambda i, be, nr, nv: (i, 0)
    wsp = pl.BlockSpec((None, None, D_MODEL, D_FF), lambda i, be, nr, nv: (l, be[i], 0, 0))
    bsp = pl.BlockSpec((None, None, 1, D_FF), lambda i, be, nr, nv: (l, be[i], 0, 0))
    b4 = lambda a: a.reshape(DEPTH, N_EXPERTS, 1, -1)
    return pl.pallas_call(
        _ffn_kernel,
        grid_spec=pltpu.PrefetchScalarGridSpec(
            num_scalar_prefetch=3,
            grid=(nb,),
            in_specs=[pl.BlockSpec((bm, HALF), blk), wsp, bsp, wsp, bsp, wsp, bsp],
            out_specs=pl.BlockSpec((bm, HALF), oblk),
            scratch_shapes=[pltpu.VMEM((D_MODEL, D_FF), BF16), pltpu.VMEM((D_MODEL, D_FF), BF16),
                            pltpu.VMEM((D_FF, D_MODEL), BF16)]),
        out_shape=jax.ShapeDtypeStruct((n_slots, HALF), jnp.int32),
        compiler_params=_params(("arbitrary",)),
        name="moe_experts",
    )(block_e, block_rows, n_valid, xs, params['w_gate'], b4(params['b_gate']), params['w_up'], b4(params['b_up']),
      params['w_down'], b4(params['b_down']))


def _combine_kernel(off_ref, np_ref, x1_ref, ir_ref, pr_ref, ys_hbm, g2_ref, b2_ref, o_ref,
                    stage_ref, ysb_ref, wgt_ref, acc_ref, sem, *, tm, n_tiles):
    i = pl.program_id(0)
    slot = i % 2
    mk = lambda s: (lambda dst, src: pltpu.make_async_copy(ys_hbm.at[src], stage_ref.at[s, dst], sem.at[s]))

    @pl.when(i == 0)
    def _():
        _region_copies(off_ref, 0, 0, mk(0), True)

    col_id = _small_int_bf16(lax.broadcasted_iota(jnp.int32, (tm, ONEHOT_CHUNK), 1))
    ir = ir_ref[...]
    pr = pr_ref[...].astype(BF16)
    zero = jnp.zeros((), BF16)

    def weighted_sum(p):
        tgt = [_slot_targets(ir[:, k:k + 1], ir[:, TOP_K + k:TOP_K + k + 1], p) for k in range(TOP_K)]
        for c in range(N_EXPERTS * REGION_CAP // ONEHOT_CHUNK):
            cols = slice(c * ONEHOT_CHUNK, (c + 1) * ONEHOT_CHUNK)
            wgt = zero
            for k in reversed(range(TOP_K)):
                wgt = jnp.where(col_id == _small_int_bf16(tgt[k] - c * ONEHOT_CHUNK), pr[:, k:k + 1], wgt)
            wgt_ref[:, cols] = wgt
            ysb_ref[cols, :] = _unpack_bf16_pairs(stage_ref[slot, cols, :])
        return _dot(wgt_ref[...], ysb_ref[...])

    _region_copies(off_ref, i, 0, mk(slot), False)

    @pl.when(i + 1 < n_tiles)
    def _():
        _region_copies(off_ref, i + 1, 0, mk(1 - slot), True)

    acc_ref[...] = weighted_sum(0)

    def extra_pass(p, carry):
        _region_copies(off_ref, i, p, mk(slot), True)
        _region_copies(off_ref, i, p, mk(slot), False)
        acc_ref[...] += weighted_sum(p)
        return carry

    lax.fori_loop(1, np_ref[i], extra_pass, 0)
    o_ref[...] = _layer_norm(DN_ALPHA * x1_ref[...] + acc_ref[...], g2_ref[...], b2_ref[...])


def _combine(x1, idx_rank, probs, ys, off, n_pass, p, tm):
    t = x1.shape[0]
    row = lambda a: a.reshape(1, D_MODEL).astype(F32)
    full = lambda shp: pl.BlockSpec(shp, lambda i, *_: (0,) * len(shp))
    rowb = lambda w: pl.BlockSpec((tm, w), lambda i, *_: (i, 0))
    return pl.pallas_call(
        functools.partial(_combine_kernel, tm=tm, n_tiles=t // tm),
        grid_spec=pltpu.PrefetchScalarGridSpec(
            num_scalar_prefetch=2,
            grid=(t // tm,),
            in_specs=[rowb(D_MODEL), rowb(LANES), rowb(LANES), pl.BlockSpec(memory_space=pl.ANY),
                      full((1, D_MODEL)), full((1, D_MODEL))],
            out_specs=rowb(D_MODEL),
            scratch_shapes=[pltpu.VMEM((2, N_EXPERTS * REGION_CAP, HALF), jnp.int32),
                            pltpu.VMEM((N_EXPERTS * REGION_CAP, D_MODEL), BF16),
                            pltpu.VMEM((tm, N_EXPERTS * REGION_CAP), BF16),
                            pltpu.VMEM((tm, D_MODEL), F32),
                            pltpu.SemaphoreType.DMA((2,))]),
        out_shape=jax.ShapeDtypeStruct((t, D_MODEL), F32),
        compiler_params=_params(("arbitrary",)),
        name="moe_combine_ln2",
    )(off, n_pass, x1, idx_rank, probs, ys, row(p['ln2_g']), row(p['ln2_b']))


def _moe(x1, x1b, idx_rank, irt, probs, tile_counts, p, params, l, bm, tm):
    t = x1.shape[0]
    n_tiles = t // tm
    cnt = tile_counts[:, 0, :N_EXPERTS]
    cnt8 = (cnt + 7) // 8 * 8
    seg = jnp.sum(cnt8, axis=0)
    padded = (seg + SEGMENT_SLACK + bm - 1) // bm * bm
    pend = jnp.cumsum(padded)
    pstart = pend - padded
    off = pstart[None, :] + jnp.cumsum(cnt8, axis=0) - cnt8
    n_pass = jnp.maximum((jnp.max(cnt, axis=1) + REGION_CAP - 1) // REGION_CAP, 1).astype(jnp.int32)
    nb = -(-(t * TOP_K + 7 * n_tiles * N_EXPERTS + SEGMENT_SLACK * N_EXPERTS) // bm) + N_EXPERTS
    n_valid = (pend[-1] // bm).astype(jnp.int32).reshape(1)
    blocks = jnp.arange(nb, dtype=jnp.int32)
    block_e = jnp.minimum(jnp.sum((pend // bm)[None, :] <= blocks[:, None], axis=1), N_EXPERTS - 1).astype(jnp.int32)
    block_rows = jnp.clip(seg[block_e] - (blocks * bm - pstart[block_e]), 0, bm).astype(jnp.int32)
    off = off.astype(jnp.int32).reshape(-1)
    pad = jnp.concatenate([pstart + seg, padded - seg]).astype(jnp.int32)
    xs = _dispatch(x1b, irt, off, n_pass, pad, n_valid, nb, bm, tm)
    ys = _experts(xs, block_e, block_rows, n_valid, params, l, bm)
    return _combine(x1, idx_rank, probs, ys, off, n_pass, p, tm)


_EXPERT_WEIGHTS = ('w_gate', 'b_gate', 'w_up', 'b_up', 'w_down', 'b_down')


def _layer(x, params, l, b, s, cfg):
    p = {k: v[l] for k, v in params.items() if k not in _EXPERT_WEIGHTS}
    xb, x_groups = _prep(x, cfg['prep_tm'])
    w_in = p['w_in']
    o = IN_OFFS
    pad = jnp.zeros((D_MODEL, LANES - 2 * ML_HEADS), w_in.dtype)
    w_ret = w_in[:, o[0]:o[4]].astype(BF16)
    w_ml = jnp.concatenate([w_in[:, o[4]:o[5]], w_in[:, o[7]:o[8]]], axis=1).astype(BF16)
    w_if = jnp.concatenate([w_in[:, o[5]:o[7]], pad], axis=1).astype(BF16)
    w_gates = w_in[:, o[11]:o[12]].astype(BF16)
    y_ret = _retention(xb, w_ret, p['ret_gn'], b, s, cfg['seq_rows']).reshape(b * s, MIX_W)
    y_ml = _mlstm(xb, w_ml, w_if, p, b, s, cfg['seq_rows']).reshape(b * s, MIX_W)
    outs, lses = [], []
    wq = ATT_HEADS * ATT_DK
    wv = ATT_HEADS * ATT_DV
    for g in range(ATT_GROUPS):
        assert ATT_DK == 64
        w_att = jnp.concatenate([w_in[:, o[8] + g * wq:o[8] + (g + 1) * wq] * (ATT_DK ** -0.5),
                                 w_in[:, o[9] + g * wq:o[9] + (g + 1) * wq],
                                 w_in[:, o[10] + g * wv:o[10] + (g + 1) * wv]], axis=1).astype(BF16)
        d = ATT_PATTERNS[g][1]
        og, lg = _dilated_group(xb if d == 1 else x_groups[d], w_att, g, b, s, cfg['att_nq'])
        outs.append(og)
        lses.append(lg)
    x1, x1b, idx_rank, irt, probs, tile_counts = _merge(x, xb, y_ret, y_ml, outs, lses, w_gates, p, cfg['merge_tm'])
    return _moe(x1, x1b, idx_rank, irt, probs, tile_counts, p, params, l, cfg['moe_bm'], ROUTE_TM)


CFG = dict(prep_tm=1024, seq_rows=512, att_nq=4, merge_tm=512, moe_bm=512)

_PARAM_NAMES = ('w_in', 'ret_gn', 'ml_conv_w', 'ml_conv_b', 'ml_wq', 'ml_wk', 'ml_wv', 'ml_bi', 'ml_bf', 'ml_gn',
                'ml_skip', 'w_branch', 'w_out', 'ln1_g', 'ln1_b', 'w_router', 'b_router', 'w_gate', 'b_gate',
                'w_up', 'b_up', 'w_down', 'b_down', 'ln2_g', 'ln2_b')


def _forward(x, params, cfg):
    b, s, d = x.shape
    xf = x.reshape(b * s, d).astype(F32)
    for l in range(DEPTH):
        xf = _layer(xf, params, l, b, s, cfg)
    return xf.reshape(b, s, d).astype(x.dtype)


def kernel(x, w_in, ret_gn, ml_conv_w, ml_conv_b, ml_wq, ml_wk, ml_wv, ml_bi, ml_bf, ml_gn, ml_skip, w_branch, w_out, ln1_g, ln1_b, w_router, b_router, w_gate, b_gate, w_up, b_up, w_down, b_down, ln2_g, ln2_b):
    params = dict(zip(_PARAM_NAMES, (w_in, ret_gn, ml_conv_w, ml_conv_b, ml_wq, ml_wk, ml_wv, ml_bi, ml_bf, ml_gn,
                                     ml_skip, w_branch, w_out, ln1_g, ln1_b, w_router, b_router, w_gate, b_gate,
                                     w_up, b_up, w_down, b_down, ln2_g, ln2_b)))
    return _forward(x, params, CFG)
```

```python
import functools

import jax
import jax.numpy as jnp
import numpy as np
from jax import lax
from jax.experimental import pallas as pl
from jax.experimental.pallas import tpu as pltpu

F32 = jnp.float32
BF16 = jnp.bfloat16

D_MODEL = 1024
DEPTH = 2
MIX_W = D_MODEL // 2
N_BRANCHES = 3
RET_HEADS = 4
RET_DV = MIX_W // RET_HEADS
RET_DK = RET_DV // 2
ML_HEADS = 4
ML_DH = MIX_W // ML_HEADS
ML_CONV = 4
ML_QK_BLOCK = 4
ML_HEAD_GROUP = 1
ATT_PATTERNS = ((128, 1), (512, 4), (2048, 16))
ATT_GROUPS = len(ATT_PATTERNS)
ATT_HEADS = 4
ATT_DV = MIX_W // ATT_HEADS
ATT_DK = ATT_DV // 2
N_EXPERTS = 32
TOP_K = 4
D_FF = D_MODEL
SWIGLU_LIMIT = 7.0
SWIGLU_ALPHA = 1.702
DN_ALPHA = (2.0 * DEPTH) ** 0.25
EPS = 1e-5

CHUNK = 128
LANES = 128
NEG = -1e30
VMEM_LIMIT = 56 * 1024 * 1024

IN_SIZES = (RET_HEADS * RET_DK, RET_HEADS * RET_DK, MIX_W, MIX_W,
            MIX_W, ML_HEADS, ML_HEADS, MIX_W,
            ATT_GROUPS * ATT_HEADS * ATT_DK, ATT_GROUPS * ATT_HEADS * ATT_DK, ATT_GROUPS * ATT_HEADS * ATT_DV,
            N_BRANCHES * D_MODEL)
IN_OFFS = tuple(int(v) for v in np.cumsum((0,) + IN_SIZES))


def _params(sem):
    return pltpu.CompilerParams(dimension_semantics=sem, vmem_limit_bytes=VMEM_LIMIT)


def _dot(a, b):
    return jnp.dot(a, b, preferred_element_type=F32)


def _dot_nt(a, b):
    return lax.dot_general(a, b, (((1,), (1,)), ((), ())), preferred_element_type=F32)


def _dot_tn(a, b):
    return lax.dot_general(a, b, (((0,), (0,)), ((), ())), preferred_element_type=F32)


def _sigmoid(x):
    return 1.0 / (1.0 + jnp.exp(-x))


def _prep_kernel(x_ref, xb_ref, *rest, dilations):
    group_refs, cols_ref = rest[:-1], rest[-1]
    x = x_ref[...]
    xb_ref[...] = x.astype(xb_ref.dtype)
    tm, dm = x.shape
    for c in range(dm // LANES):
        cols_ref[c] = x[:, c * LANES:(c + 1) * LANES]
    for o_ref, d in zip(group_refs, dilations):
        n = tm // d
        for r in range(d):
            for c in range(dm // LANES):
                o_ref[:, r * dm + c * LANES:r * dm + (c + 1) * LANES] = (
                    cols_ref[c, pl.ds(r, n, stride=d), :].astype(o_ref.dtype))


def _prep(x, tm):
    t, dm = x.shape
    tm = min(tm, t)
    dilations = tuple(sorted({d for _, d in ATT_PATTERNS if d > 1}))
    assert t % tm == 0 and all(tm % (16 * d) == 0 for d in dilations)
    outs = pl.pallas_call(
        functools.partial(_prep_kernel, dilations=dilations),
        grid=(t // tm,),
        in_specs=[pl.BlockSpec((tm, dm), lambda i: (i, 0))],
        out_specs=[pl.BlockSpec((tm, dm), lambda i: (i, 0))]
                  + [pl.BlockSpec((tm // d, d * dm), lambda i: (i, 0)) for d in dilations],
        out_shape=[jax.ShapeDtypeStruct((t, dm), BF16)]
                  + [jax.ShapeDtypeStruct((t // d, d * dm), BF16) for d in dilations],
        scratch_shapes=[pltpu.VMEM((dm // LANES, tm, LANES), F32)],
        compiler_params=_params(("arbitrary",)),
        name="prep_bf16_groups",
    )(x)
    return outs[0], dict(zip(dilations, outs[1:]))


def _head_norm(o, gn):
    mu = jnp.mean(o, axis=-1, keepdims=True)
    oc = o - mu
    var = jnp.mean(oc * oc, axis=-1, keepdims=True)
    return oc * lax.rsqrt(var + EPS) * gn


def _ret_kernel(x_ref, w_ref, dm_ref, qd_ref, kd_ref, cd_ref, gn_ref, o_ref, st_ref, blk_ref, *, nc, nb):
    @pl.when(pl.program_id(0) == 0)
    def _():
        st_ref[...] = jnp.zeros_like(st_ref)

    for bi in range(nb):
        blk_ref[bi] = _dot(x_ref[bi], w_ref[...]).astype(blk_ref.dtype)

    hq = RET_HEADS * RET_DK
    state = [st_ref[j] for j in range(nb * RET_HEADS)]
    for c in range(nc):
        rows = pl.ds(c * CHUNK, CHUNK)
        chains = [(h, bi) for h in range(RET_HEADS) for bi in range(nb)]
        first = []
        for h, bi in chains:
            q = blk_ref[bi, rows, h * RET_DK:(h + 1) * RET_DK]
            k = blk_ref[bi, rows, hq + h * RET_DK:hq + (h + 1) * RET_DK]
            v = blk_ref[bi, rows, 2 * hq + h * RET_DV:2 * hq + (h + 1) * RET_DV]
            st = state[bi * RET_HEADS + h]
            first.append((v, _dot_nt(q, k), _dot(q, st.astype(BF16)),
                          _dot_tn(k, (v.astype(F32) * kd_ref[h]).astype(BF16))))
        for (h, bi), (v, s_qk, q_st, kv) in zip(chains, first):
            o = _dot((s_qk * dm_ref[h]).astype(BF16), v) + q_st * qd_ref[h]
            state[bi * RET_HEADS + h] = cd_ref[h] * state[bi * RET_HEADS + h] + kv
            g = blk_ref[bi, rows, 2 * hq + MIX_W + h * RET_DV:2 * hq + MIX_W + (h + 1) * RET_DV].astype(F32)
            y = _head_norm(o, gn_ref[:, h * RET_DV:(h + 1) * RET_DV]) * (g * _sigmoid(g))
            o_ref[bi, rows, h * RET_DV:(h + 1) * RET_DV] = y.astype(o_ref.dtype)
    for j in range(nb * RET_HEADS):
        st_ref[j] = state[j]


def _retention(xb, w_ret, ret_gn, b, s, rows):
    nc = rows // CHUNK
    scale = RET_DK ** -0.5
    log_gamma = jnp.log1p(-jnp.exp2(-5.0 - jnp.arange(RET_HEADS, dtype=F32)))
    pos = jnp.arange(CHUNK, dtype=F32)
    diff = pos[:, None] - pos[None, :]
    dm = jnp.where(diff >= 0, jnp.exp(log_gamma[:, None, None] * jnp.maximum(diff, 0.0)), 0.0) * scale
    kd = jnp.broadcast_to(jnp.exp(log_gamma[:, None] * (CHUNK - 1.0 - pos))[:, :, None], (RET_HEADS, CHUNK, RET_DV))
    qd = jnp.broadcast_to((jnp.exp(log_gamma[:, None] * (pos + 1.0)) * scale)[:, :, None], (RET_HEADS, CHUNK, RET_DV))
    cd = jnp.broadcast_to(jnp.exp(log_gamma * CHUNK)[:, None, None], (RET_HEADS, 1, RET_DV))
    dmod, w_in = w_ret.shape
    full = lambda shp: pl.BlockSpec(shp, lambda si: (0,) * len(shp))
    return pl.pallas_call(
        functools.partial(_ret_kernel, nc=nc, nb=b),
        grid=(s // rows,),
        in_specs=[pl.BlockSpec((b, rows, dmod), lambda si: (0, si, 0)), full((dmod, w_in)),
                  full((RET_HEADS, CHUNK, CHUNK)), full((RET_HEADS, CHUNK, RET_DV)),
                  full((RET_HEADS, CHUNK, RET_DV)), full((RET_HEADS, 1, RET_DV)), full((1, MIX_W))],
        out_specs=pl.BlockSpec((b, rows, MIX_W), lambda si: (0, si, 0)),
        out_shape=jax.ShapeDtypeStruct((b, s, MIX_W), BF16),
        scratch_shapes=[pltpu.VMEM((b * RET_HEADS, RET_DK, RET_DV), F32), pltpu.VMEM((b, rows, w_in), BF16)],
        compiler_params=_params(("arbitrary",)),
        name="retention",
    )(xb.reshape(b, s, dmod), w_ret, dm, qd, kd, cd, ret_gn.reshape(1, MIX_W).astype(F32))


def _log_sigmoid(x):
    return jnp.minimum(x, 0.0) - jnp.log(1.0 + jnp.exp(-jnp.abs(x)))


def _ml_kernel(x_ref, wml_ref, wif_ref, cw_ref, cb_ref, bq_ref, bk_ref, bv_ref, gb_ref, gn_ref, sk_ref,
               o_ref, c_ref, n_ref, m_ref, halo_ref, ml_ref, if_ref, qkv_ref, *, nc, nb):
    rows_total = nc * CHUNK

    @pl.when(pl.program_id(0) == 0)
    def _():
        c_ref[...] = jnp.zeros_like(c_ref)
        n_ref[...] = jnp.zeros_like(n_ref)
        m_ref[...] = jnp.zeros_like(m_ref)
        halo_ref[...] = jnp.zeros_like(halo_ref)

    for bi in range(nb):
        ml_ref[bi] = _dot(x_ref[bi], wml_ref[...]).astype(ml_ref.dtype)
        if_ref[bi] = _dot(x_ref[bi], wif_ref[...])

    x_all, mc_all = [], []
    for bi in range(nb):
        xa = ml_ref[bi, :, 0:MIX_W].astype(F32)
        xf = jnp.concatenate([halo_ref[bi], xa], axis=0)
        acc = jnp.broadcast_to(cb_ref[...], (rows_total, MIX_W))
        for j in range(ML_CONV):
            sh = ML_CONV - 1 - j
            xs = xf if sh == 0 else pltpu.roll(xf, sh, 0)
            acc = acc + xs[8:8 + rows_total] * cw_ref[j:j + 1, :]
        halo_ref[bi] = xa[rows_total - 8:rows_total]
        x_all.append(xa)
        mc_all.append(acc * _sigmoid(acc))

    ri = lax.broadcasted_iota(jnp.int32, (CHUNK, CHUNK), 0)
    ci = lax.broadcasted_iota(jnp.int32, (CHUNK, CHUNK), 1)
    tri = ri >= ci
    tril = tri.astype(F32)
    kscale = ML_DH ** -0.5
    log_kscale = float(np.log(kscale))

    c_st = [[c_ref[bi * ML_HEADS + h] for h in range(ML_HEADS)] for bi in range(nb)]
    n_st = [[n_ref[bi, h:h + 1, :] for h in range(ML_HEADS)] for bi in range(nb)]
    m_st = [[m_ref[bi, h:h + 1, 0:1] for h in range(ML_HEADS)] for bi in range(nb)]

    for bi in range(nb):
        mcb = mc_all[bi].astype(BF16)
        xvb = x_all[bi].astype(BF16)
        for h in range(ML_HEADS):
            cols = slice(h * ML_DH, (h + 1) * ML_DH)
            qkv_ref[bi, 0, :, cols] = _dot(mcb[:, cols], bq_ref[h]).astype(BF16)
            qkv_ref[bi, 1, :, cols] = _dot(mcb[:, cols], bk_ref[h]).astype(BF16)
            qkv_ref[bi, 2, :, cols] = _dot(xvb[:, cols], bv_ref[h]).astype(BF16)

    def run_chains(r0, gates, chains):
        for h, bi in chains:
            g_pre, cum, g_t, cum_t = gates[bi]
            cols = slice(h * ML_DH, (h + 1) * ML_DH)
            mc = mc_all[bi][r0:r0 + CHUNK, cols]
            q = qkv_ref[bi, 0, r0:r0 + CHUNK, cols]
            k = qkv_ref[bi, 1, r0:r0 + CHUNK, cols]
            v = qkv_ref[bi, 2, r0:r0 + CHUNK, cols]
            cum_col = cum[:, ML_HEADS + h:ML_HEADS + h + 1]
            ig_col = g_pre[:, h:h + 1]
            cum_row = cum_t[ML_HEADS + h:ML_HEADS + h + 1, :]
            ig_row = g_t[h:h + 1, :]
            tot = cum[CHUNK - 1:CHUNK, ML_HEADS + h:ML_HEADS + h + 1]
            m_prev, c_prev, n_prev = m_st[bi][h], c_st[bi][h], n_st[bi][h]
            dlog = jnp.where(tri, cum_col + (ig_row - cum_row), NEG)
            m_inter = cum_col + m_prev
            m_q = jnp.maximum(m_inter, jnp.max(dlog, axis=-1, keepdims=True))
            w_qk = jnp.exp(dlog - (m_q - log_kscale)) * _dot_nt(q, k)
            inter = jnp.exp(m_inter - m_q)
            num = _dot(w_qk.astype(BF16), v) + inter * _dot(q, c_prev.astype(BF16))
            den = (jnp.sum(w_qk, axis=-1, keepdims=True)
                   + inter * jnp.sum(q.astype(F32) * n_prev, axis=-1, keepdims=True))
            hh = num * (1.0 / jnp.maximum(jnp.abs(den), jnp.exp(-m_q)))
            a_col = tot - cum_col + ig_col
            a_max = jnp.max(a_col, axis=0, keepdims=True)
            wa = jnp.exp(a_col - a_max) * kscale
            chunk_c = _dot_tn(k, (wa * v.astype(F32)).astype(BF16))
            chunk_n = jnp.sum(wa * k.astype(F32), axis=0, keepdims=True)
            m_new = jnp.maximum(tot + m_prev, a_max)
            s_old = jnp.exp(tot + m_prev - m_new)
            s_new = jnp.exp(a_max - m_new)
            c_st[bi][h] = s_old * c_prev + s_new * chunk_c
            n_st[bi][h] = s_old * n_prev + s_new * chunk_n
            m_st[bi][h] = m_new
            mo = ml_ref[bi, r0:r0 + CHUNK, MIX_W + h * ML_DH:MIX_W + (h + 1) * ML_DH].astype(F32)
            y = _sigmoid(mo) * (_head_norm(hh, gn_ref[:, cols]) + sk_ref[:, cols] * mc)
            o_ref[bi, r0:r0 + CHUNK, cols] = y.astype(o_ref.dtype)

    for c in range(nc):
        r0 = c * CHUNK
        gates = []
        for bi in range(nb):
            g_pre = if_ref[bi, r0:r0 + CHUNK, :] + gb_ref[...]
            cum = jnp.dot(tril, _log_sigmoid(g_pre), preferred_element_type=F32, precision=lax.Precision.HIGHEST)
            gates.append((g_pre, cum, g_pre.T, cum.T))
        for h0 in range(0, ML_HEADS, ML_HEAD_GROUP):
            run_chains(r0, gates, [(h, bi) for h in range(h0, h0 + ML_HEAD_GROUP) for bi in range(nb)])

    for bi in range(nb):
        for h in range(ML_HEADS):
            c_ref[bi * ML_HEADS + h] = c_st[bi][h]
            n_ref[bi, h:h + 1, :] = n_st[bi][h]
            m_ref[bi, h:h + 1, :] = jnp.broadcast_to(m_st[bi][h], (1, LANES))


def _block_diag(w):
    nb = w.shape[0]
    per = nb // ML_HEADS
    wh = w.reshape(ML_HEADS, per, ML_QK_BLOCK, ML_QK_BLOCK)
    eye = jnp.eye(per, dtype=w.dtype)
    bd = jnp.einsum('hncd,nm->hncmd', wh, eye)
    return bd.reshape(ML_HEADS, per * ML_QK_BLOCK, per * ML_QK_BLOCK).astype(BF16)


def _mlstm(xb, w_ml, w_if, p, b, s, rows):
    nc = rows // CHUNK
    dm = xb.shape[1]
    full = lambda shp: pl.BlockSpec(shp, lambda si: (0,) * len(shp))
    gb = jnp.zeros((1, LANES), F32).at[0, 0:ML_HEADS].set(p['ml_bi'].astype(F32)).at[0, ML_HEADS:2 * ML_HEADS].set(p['ml_bf'].astype(F32))
    row = lambda a: a.reshape(1, MIX_W).astype(F32)
    return pl.pallas_call(
        functools.partial(_ml_kernel, nc=nc, nb=b),
        grid=(s // rows,),
        in_specs=[pl.BlockSpec((b, rows, dm), lambda si: (0, si, 0)),
                  full((dm, 2 * MIX_W)), full((dm, LANES)),
                  full((ML_CONV, MIX_W)), full((1, MIX_W)),
                  full((ML_HEADS, ML_DH, ML_DH)), full((ML_HEADS, ML_DH, ML_DH)), full((ML_HEADS, ML_DH, ML_DH)),
                  full((1, LANES)), full((1, MIX_W)), full((1, MIX_W))],
        out_specs=pl.BlockSpec((b, rows, MIX_W), lambda si: (0, si, 0)),
        out_shape=jax.ShapeDtypeStruct((b, s, MIX_W), BF16),
        scratch_shapes=[pltpu.VMEM((b * ML_HEADS, ML_DH, ML_DH), F32), pltpu.VMEM((b, 8, ML_DH), F32),
                        pltpu.VMEM((b, 8, LANES), F32), pltpu.VMEM((b, 8, MIX_W), F32),
                        pltpu.VMEM((b, rows, 2 * MIX_W), BF16), pltpu.VMEM((b, rows, LANES), F32),
                        pltpu.VMEM((b, 3, rows, MIX_W), BF16)],
        compiler_params=_params(("arbitrary",)),
        name="mlstm",
    )(xb.reshape(b, s, dm), w_ml, w_if,
      p['ml_conv_w'].astype(F32), row(p['ml_conv_b']),
      _block_diag(p['ml_wq']), _block_diag(p['ml_wk']), _block_diag(p['ml_wv']),
      gb, row(p['ml_gn']), row(p['ml_skip']))


def _att_kernel(x_ref, w_ref, bias_ref, o_ref, lse_ref, qkv_ref, *, nq):
    first = pl.program_id(2) == 0
    span = nq * CHUNK
    wq = ATT_HEADS * ATT_DK

    @pl.when(first)
    def _():
        qkv_ref[0:CHUNK, :] = jnp.zeros((CHUNK, qkv_ref.shape[1]), qkv_ref.dtype)

    qkv_ref[CHUNK:CHUNK + span, :] = _dot(x_ref[...], w_ref[...]).astype(qkv_ref.dtype)
    lane = lax.broadcasted_iota(jnp.int32, (CHUNK, LANES), 1)
    ones = jnp.ones((2 * CHUNK, ATT_DV), BF16)
    for j in range(nq):
        rows = slice(j * CHUNK, (j + 1) * CHUNK)
        kv_rows = slice(j * CHUNK, (j + 2) * CHUNK)
        lse_out = jnp.zeros((CHUNK, LANES), F32)
        for h in range(ATT_HEADS):
            dv = slice(h * ATT_DV, (h + 1) * ATT_DV)
            q = qkv_ref[CHUNK + j * CHUNK:CHUNK + (j + 1) * CHUNK, h * ATT_DK:(h + 1) * ATT_DK]
            kk = qkv_ref[kv_rows, wq + h * ATT_DK:wq + (h + 1) * ATT_DK]
            vv = qkv_ref[kv_rows, 2 * wq + h * ATT_DV:2 * wq + (h + 1) * ATT_DV]
            bias = bias_ref[jnp.where(first, 0, 1), h] if j == 0 else bias_ref[1, h]
            s = _dot_nt(q, kk) + bias
            m = jnp.max(s, axis=-1, keepdims=True)
            p = jnp.exp(s - m).astype(BF16)
            oa = _dot(p, jnp.concatenate([vv, ones], axis=1))
            l = oa[:, ATT_DV:2 * ATT_DV]
            o_ref[rows, dv] = (oa[:, 0:ATT_DV] * (1.0 / l)).astype(o_ref.dtype)
            lse_out = jnp.where(lane == h, m + jnp.log(l), lse_out)
        lse_ref[rows, :] = lse_out
    qkv_ref[0:CHUNK, :] = qkv_ref[span:span + CHUNK, :]


def _att_bias(window, dilation, slopes):
    wb = window // dilation
    qi = jnp.arange(wb)[:, None]
    kj = jnp.arange(2 * wb)[None, :]
    delta = qi + wb - kj
    band = (delta >= 0) & (delta <= wb)
    bias = -slopes[:, None, None] * (dilation * delta).astype(F32)
    later = jnp.where(band[None], bias, NEG)
    first = jnp.where((band & (kj >= wb))[None], bias, NEG)
    return jnp.stack([first, later]).astype(F32)


def _dilated_group(xg, w_qkv, g, b, s, nq):
    window, d = ATT_PATTERNS[g]
    assert window // d == CHUNK
    l_sub = s // d
    assert l_sub % CHUNK == 0
    n_blk = l_sub // CHUNK
    slopes = jnp.exp2(-8.0 * jnp.arange(1, ATT_GROUPS * ATT_HEADS + 1, dtype=F32) / (ATT_GROUPS * ATT_HEADS))
    bias = _att_bias(window, d, slopes.reshape(ATT_GROUPS, ATT_HEADS)[g])
    wq = ATT_HEADS * ATT_DK
    wv = ATT_HEADS * ATT_DV
    per_tok = 2 * wq + wv
    dm = w_qkv.shape[0]
    assert xg.shape == (b * l_sub, d * dm)
    x = xg.reshape(b, l_sub, d * dm)
    nq = min(nq, n_blk)
    assert n_blk % nq == 0
    span = nq * CHUNK
    o, lse = pl.pallas_call(
        functools.partial(_att_kernel, nq=nq),
        grid=(b, d, n_blk // nq),
        in_specs=[pl.BlockSpec((None, span, dm), lambda bi, r, n: (bi, n, r)),
                  pl.BlockSpec((dm, per_tok), lambda bi, r, n: (0, 0)),
                  pl.BlockSpec((2, ATT_HEADS, CHUNK, 2 * CHUNK), lambda bi, r, n: (0, 0, 0, 0))],
        out_specs=[pl.BlockSpec((None, span, wv), lambda bi, r, n: (bi, n, r)),
                   pl.BlockSpec((None, span, LANES), lambda bi, r, n: (bi, n, r))],
        out_shape=[jax.ShapeDtypeStruct((b, l_sub, d * wv), BF16),
                   jax.ShapeDtypeStruct((b, l_sub, d * LANES), F32)],
        scratch_shapes=[pltpu.VMEM((CHUNK + span, per_tok), BF16)],
        compiler_params=_params(("arbitrary", "arbitrary", "arbitrary")),
        name=f"dilated_attn_g{g}",
    )(x, w_qkv, bias)
    return o.reshape(b * l_sub, d * wv), lse.reshape(b * l_sub, d * LANES)


def _layer_norm(z, g, b):
    mu = jnp.mean(z, axis=-1, keepdims=True)
    zc = z - mu
    var = jnp.mean(zc * zc, axis=-1, keepdims=True)
    return zc * lax.rsqrt(var + EPS) * g + b


def _merge_kernel(x_ref, yr_ref, ym_ref, o0_ref, o1_ref, o2_ref, l0_ref, l1_ref, l2_ref, xb_ref,
                  wg_ref, wb_ref, wo_ref, g1_ref, b1_ref, wrh_ref, wrl_ref, br_ref,
                  x1_ref, x1b_ref, ir_ref, irt_ref, pr_ref, cnt_ref, og_ref, lg_ref, *, tm):
    for g, (src_o, src_l) in enumerate(((o0_ref, l0_ref), (o1_ref, l1_ref), (o2_ref, l2_ref))):
        d = ATT_PATTERNS[g][1]
        n = tm // d
        for r in range(d):
            rows = pl.ds(r, n, stride=d) if d > 1 else pl.ds(0, n)
            for h in range(ATT_HEADS):
                og_ref[g, h, rows, :] = src_o[:, r * MIX_W + h * ATT_DV:r * MIX_W + (h + 1) * ATT_DV].astype(F32)
            lg_ref[g, rows, :] = src_l[:, r * LANES:(r + 1) * LANES]

    l0, l1, l2 = lg_ref[0], lg_ref[1], lg_ref[2]
    lm = jnp.maximum(jnp.maximum(l0, l1), l2)
    e0, e1, e2 = jnp.exp(l0 - lm), jnp.exp(l1 - lm), jnp.exp(l2 - lm)
    inv = 1.0 / (e0 + e1 + e2)
    parts = []
    for h in range(ATT_HEADS):
        dv = slice(h * ATT_DV, (h + 1) * ATT_DV)
        hs = slice(h, h + 1)
        parts.append((e0[:, hs] * og_ref[0, h] + e1[:, hs] * og_ref[1, h] + e2[:, hs] * og_ref[2, h]) * inv[:, hs])
    y_att = jnp.concatenate(parts, axis=-1).astype(BF16)

    xb = xb_ref[...]
    gate = lambda j: _sigmoid(_dot(xb, wg_ref[:, j * D_MODEL:(j + 1) * D_MODEL]))
    merged = gate(0) * _dot(yr_ref[...], wb_ref[0])
    merged = merged + gate(1) * _dot(ym_ref[...], wb_ref[1])
    merged = merged + gate(2) * _dot(y_att, wb_ref[2])
    z = DN_ALPHA * x_ref[...] + _dot(merged.astype(BF16), wo_ref[...])
    x1 = _layer_norm(z, g1_ref[...], b1_ref[...])
    x1_ref[...] = x1
    x1b_ref[...] = x1.astype(BF16)

    xh = x1.astype(BF16)
    xl = (x1 - xh.astype(F32)).astype(BF16)
    logits = _dot(xh, wrh_ref[...]) + _dot(xl, wrh_ref[...]) + _dot(xh, wrl_ref[...]) + br_ref[...]
    lane = lax.broadcasted_iota(jnp.int32, (tm, LANES), 1)
    work = jnp.where(lane < N_EXPERTS, logits, -jnp.inf)
    vals, idxs = [], []
    for _ in range(TOP_K):
        m = jnp.max(work, axis=-1, keepdims=True)
        idx = jnp.min(jnp.where(work == m, lane, LANES), axis=-1, keepdims=True)
        vals.append(m)
        idxs.append(idx)
        work = jnp.where(lane == idx, -jnp.inf, work)
    es = [jnp.exp(v - vals[0]) for v in vals]
    tot = es[0] + es[1] + es[2] + es[3]
    pr = jnp.zeros((tm, LANES), F32)
    for k in range(TOP_K):
        pr = jnp.where(lane == k, es[k] / tot, pr)
    pr_ref[...] = pr

    onehot = jnp.zeros((tm, LANES), F32)
    for k in range(TOP_K):
        onehot = onehot + (lane == idxs[k]).astype(F32)
    ri = lax.broadcasted_iota(jnp.int32, (tm, tm), 0)
    ci = lax.broadcasted_iota(jnp.int32, (tm, tm), 1)
    same_tile = (ri // ROUTE_TM) == (ci // ROUTE_TM)
    before = _dot(((ri > ci) & same_tile).astype(BF16), onehot.astype(BF16))
    ir = jnp.zeros((tm, LANES), F32)
    for k in range(TOP_K):
        rank = jnp.sum(jnp.where(lane == idxs[k], before, 0.0), axis=-1, keepdims=True)
        ir = jnp.where(lane == k, idxs[k].astype(F32), ir)
        ir = jnp.where(lane == TOP_K + k, rank, ir)
    ir_ref[...] = ir.astype(jnp.int32)
    irt_ref[...] = ir.T[0:8, :]
    for j in range(tm // ROUTE_TM):
        cnt_j = jnp.sum(onehot[j * ROUTE_TM:(j + 1) * ROUTE_TM], axis=0, keepdims=True)
        cnt_ref[j] = jnp.broadcast_to(cnt_j, (8, LANES)).astype(jnp.int32)


def _merge(x, xb, y_ret, y_ml, outs, lses, w_gates, p, tm):
    t = x.shape[0]
    tm = min(tm, t)
    rowb = lambda w: pl.BlockSpec((tm, w), lambda i: (i, 0))
    full = lambda shp: pl.BlockSpec(shp, lambda i: (0,) * len(shp))
    wr = jnp.zeros((D_MODEL, LANES), F32).at[:, :N_EXPERTS].set(p['w_router'].astype(F32))
    wrh = wr.astype(BF16)
    wrl = (wr - wrh.astype(F32)).astype(BF16)
    br = jnp.zeros((1, LANES), F32).at[0, :N_EXPERTS].set(p['b_router'].astype(F32))
    row = lambda a: a.reshape(1, D_MODEL).astype(F32)
    return pl.pallas_call(
        functools.partial(_merge_kernel, tm=tm),
        grid=(t // tm,),
        in_specs=[rowb(D_MODEL), rowb(MIX_W), rowb(MIX_W)]
                 + [pl.BlockSpec((tm // d, d * MIX_W), lambda i: (i, 0)) for _, d in ATT_PATTERNS]
                 + [pl.BlockSpec((tm // d, d * LANES), lambda i: (i, 0)) for _, d in ATT_PATTERNS]
                 + [rowb(D_MODEL), full((D_MODEL, N_BRANCHES * D_MODEL)),
                  full((N_BRANCHES, MIX_W, D_MODEL)), full((D_MODEL, D_MODEL)),
                  full((1, D_MODEL)), full((1, D_MODEL)),
                  full((D_MODEL, LANES)), full((D_MODEL, LANES)), full((1, LANES))],
        out_specs=[rowb(D_MODEL), rowb(D_MODEL), rowb(LANES), pl.BlockSpec((8, tm), lambda i: (0, i)), rowb(LANES),
                   pl.BlockSpec((tm // ROUTE_TM, 8, LANES), lambda i: (i, 0, 0))],
        out_shape=[jax.ShapeDtypeStruct((t, D_MODEL), F32),
                   jax.ShapeDtypeStruct((t, D_MODEL), BF16),
                   jax.ShapeDtypeStruct((t, LANES), jnp.int32),
                   jax.ShapeDtypeStruct((8, t), F32),
                   jax.ShapeDtypeStruct((t, LANES), F32),
                   jax.ShapeDtypeStruct((t // ROUTE_TM, 8, LANES), jnp.int32)],
        scratch_shapes=[pltpu.VMEM((ATT_GROUPS, ATT_HEADS, tm, ATT_DV), F32),
                        pltpu.VMEM((ATT_GROUPS, tm, LANES), F32)],
        compiler_params=_params(("arbitrary",)),
        name="merge_ln1_router",
    )(x, y_ret, y_ml, outs[0], outs[1], outs[2], lses[0], lses[1], lses[2], xb, w_gates,
      p['w_branch'].astype(BF16), p['w_out'].astype(BF16), row(p['ln1_g']), row(p['ln1_b']), wrh, wrl, br)


ROUTE_TM = 256
REGION_CAP = 64
SEGMENT_SLACK = REGION_CAP - 8
PAD_PIECES = (512, 256, 128, 64, 32, 16, 8)
ONEHOT_CHUNK = 256
HALF = D_MODEL // 2
HI_MASK = -65536


def _pack_bf16_pairs(y):
    lo = lax.shift_right_logical(lax.bitcast_convert_type(y[:, :HALF], jnp.int32), 16)
    hi = lax.bitcast_convert_type(y[:, HALF:], jnp.int32) & HI_MASK
    return lo | hi


def _unpack_bf16_pairs(w):
    lo = lax.bitcast_convert_type(lax.shift_left(w, 16), F32)
    hi = lax.bitcast_convert_type(w & HI_MASK, F32)
    return jnp.concatenate([lo, hi], axis=1).astype(BF16)


def _row_pieces(n, pieces, fn):
    for size in pieces:
        if size == pieces[0]:
            cond, pos = n >= size, 0
        else:
            cond, pos = (n & size) != 0, pl.multiple_of(n & (-2 * size), 8)

        @pl.when(cond)
        def _(pos=pos, size=size):
            fn(pos, size)


def _region_copies(off_ref, tile, p, make_copy, start):
    for e in range(N_EXPERTS):
        base = off_ref[tile * N_EXPERTS + e] + p * REGION_CAP
        cp = make_copy(pl.ds(e * REGION_CAP, REGION_CAP), pl.ds(pl.multiple_of(base, 8), REGION_CAP))
        if start:
            cp.start()
        else:
            cp.wait()


def _slot_targets(idx, lrank, p):
    lr = lrank - p * REGION_CAP
    return jnp.where((lr >= 0) & (lr < REGION_CAP), idx * REGION_CAP + lr, -1)


def _small_int_bf16(v):
    return v.astype(F32).astype(BF16)


def _dispatch_kernel(off_ref, np_ref, pad_ref, nv_ref, x_ref, irt_ref, xs_hbm,
                     stage_ref, zero_ref, sem, zsem, *, tm, n_tiles, bm, nb):
    i = pl.program_id(0)
    slot = i % 2
    row_id = _small_int_bf16(lax.broadcasted_iota(jnp.int32, (ONEHOT_CHUNK, tm), 0))
    meta = irt_ref[...].astype(jnp.int32)
    x = x_ref[...]
    mk = lambda s: (lambda src, dst: pltpu.make_async_copy(stage_ref.at[s, src], xs_hbm.at[dst], sem.at[s]))

    def build(p):
        tgt = [_slot_targets(meta[k:k + 1, :], meta[TOP_K + k:TOP_K + k + 1, :], p) for k in range(TOP_K)]
        for c in range(N_EXPERTS * REGION_CAP // ONEHOT_CHUNK):
            hit = row_id == _small_int_bf16(tgt[0] - c * ONEHOT_CHUNK)
            for k in range(1, TOP_K):
                hit = hit | (row_id == _small_int_bf16(tgt[k] - c * ONEHOT_CHUNK))
            sel = jnp.where(hit, jnp.ones((), BF16), jnp.zeros((), BF16))
            stage_ref[slot, c * ONEHOT_CHUNK:(c + 1) * ONEHOT_CHUNK, :] = _pack_bf16_pairs(_dot(sel, x))

    build(0)

    @pl.when(i > 0)
    def _():
        _region_copies(off_ref, i - 1, np_ref[jnp.maximum(i - 1, 0)] - 1, mk(1 - slot), False)

    _region_copies(off_ref, i, 0, mk(slot), True)

    def extra_pass(p, carry):
        _region_copies(off_ref, i, p - 1, mk(slot), False)
        build(p)
        _region_copies(off_ref, i, p, mk(slot), True)
        return carry

    lax.fori_loop(1, np_ref[i], extra_pass, 0)

    @pl.when(i == n_tiles - 1)
    def _():
        _region_copies(off_ref, i, np_ref[i] - 1, mk(slot), False)
        zero_ref[...] = jnp.zeros_like(zero_ref)
        for phase in (True, False):
            for e in range(N_EXPERTS):
                def piece(pos, size, e=e):
                    cp = pltpu.make_async_copy(zero_ref.at[pl.ds(0, size)],
                                               xs_hbm.at[pl.ds(pl.multiple_of(pad_ref[e] + pos, 8), size)], zsem)
                    cp.start() if phase else cp.wait()
                _row_pieces(pad_ref[N_EXPERTS + e], PAD_PIECES, piece)

        def zero_block(blk, carry):
            for half in range(bm // PAD_PIECES[0]):
                cp = pltpu.make_async_copy(
                    zero_ref, xs_hbm.at[pl.ds(pl.multiple_of(blk * bm + half * PAD_PIECES[0], 8), PAD_PIECES[0])], zsem)
                cp.start()
                cp.wait()
            return carry

        lax.fori_loop(nv_ref[0], nb, zero_block, 0)


def _dispatch(x1b, irt, off, n_pass, pad, n_valid, nb, bm, tm):
    t = x1b.shape[0]
    assert bm % PAD_PIECES[0] == 0 and bm + SEGMENT_SLACK < 2 * PAD_PIECES[0]
    return pl.pallas_call(
        functools.partial(_dispatch_kernel, tm=tm, n_tiles=t // tm, bm=bm, nb=nb),
        grid_spec=pltpu.PrefetchScalarGridSpec(
            num_scalar_prefetch=4,
            grid=(t // tm,),
            in_specs=[pl.BlockSpec((tm, D_MODEL), lambda i, *_: (i, 0)),
                      pl.BlockSpec((8, tm), lambda i, *_: (0, i))],
            out_specs=pl.BlockSpec(memory_space=pl.ANY),
            scratch_shapes=[pltpu.VMEM((2, N_EXPERTS * REGION_CAP, HALF), jnp.int32),
                            pltpu.VMEM((PAD_PIECES[0], HALF), jnp.int32),
                            pltpu.SemaphoreType.DMA((2,)), pltpu.SemaphoreType.DMA(())]),
        out_shape=jax.ShapeDtypeStruct((nb * bm, HALF), jnp.int32),
        compiler_params=_params(("arbitrary",)),
        name="moe_dispatch",
    )(off, n_pass, pad, n_valid, x1b, irt)


def _ffn_kernel(be_ref, nr_ref, nv_ref, sg_ref, nx_ref, xs_ref, wg_hbm, bg_ref, wu_hbm, bu_ref, wd_hbm, bd_ref, o_ref,
                wf_ref, wgb_ref, wub_ref, wdb_ref, sem, *, layer):
    i = pl.program_id(0)

    def fetch(e, s, start):
        for j, w_hbm in enumerate((wg_hbm, wu_hbm, wd_hbm)):
            cp = pltpu.make_async_copy(w_hbm.at[layer, e], wf_ref.at[s, j], sem.at[s])
            if start:
                cp.start()
            else:
                cp.wait()

    @pl.when(i < nv_ref[0])
    def _():
        seg = sg_ref[i]

        @pl.when((i == 0) | (seg != sg_ref[jnp.maximum(i - 1, 0)]))
        def _():
            s = seg % 2

            @pl.when(seg == 0)
            def _():
                fetch(be_ref[i], s, True)

            fetch(be_ref[i], s, False)
            wgb_ref[...] = wf_ref[s, 0].astype(BF16)
            wub_ref[...] = wf_ref[s, 1].astype(BF16)
            wdb_ref[...] = wf_ref[s, 2].astype(BF16)

            @pl.when(nx_ref[i] >= 0)
            def _():
                fetch(nx_ref[i], 1 - s, True)

        rows = lax.broadcasted_iota(jnp.int32, xs_ref.shape, 0)
        x = _unpack_bf16_pairs(jnp.where(rows < nr_ref[i], xs_ref[...], 0))
        gate = jnp.minimum(_dot(x, wgb_ref[...]) + bg_ref[...], SWIGLU_LIMIT)
        up = jnp.clip(_dot(x, wub_ref[...]) + bu_ref[...], -SWIGLU_LIMIT, SWIGLU_LIMIT)
        hid = (up + 1.0) * gate * _sigmoid(SWIGLU_ALPHA * gate)
        y = _dot(hid.astype(BF16), wdb_ref[...]) + bd_ref[...]
        o_ref[...] = _pack_bf16_pairs(y.astype(BF16).astype(F32))

    @pl.when(i >= nv_ref[0])
    def _():
        o_ref[...] = jnp.zeros_like(o_ref)


def _experts(xs, block_e, block_rows, n_valid, seg_idx, next_e, params, l, bm):
    n_slots = xs.shape[0]
    nb = n_slots // bm
    assert D_FF == D_MODEL
    blk = lambda i, be, nr, nv, *_: (jnp.minimum(i, nv[0] - 1), 0)
    oblk = lambda i, *_: (i, 0)
    wsp = pl.BlockSpec(memory_space=pl.ANY)
    bsp = pl.BlockSpec((None, None, 1, D_FF), lambda i, be, *_: (l, be[i], 0, 0))
    b4 = lambda a: a.reshape(DEPTH, N_EXPERTS, 1, -1)
    return pl.pallas_call(
        functools.partial(_ffn_kernel, layer=l),
        grid_spec=pltpu.PrefetchScalarGridSpec(
            num_scalar_prefetch=5,
            grid=(nb,),
            in_specs=[pl.BlockSpec((bm, HALF), blk), wsp, bsp, wsp, bsp, wsp, bsp],
            out_specs=pl.BlockSpec((bm, HALF), oblk),
            scratch_shapes=[pltpu.VMEM((2, 3, D_MODEL, D_FF), F32),
                            pltpu.VMEM((D_MODEL, D_FF), BF16), pltpu.VMEM((D_MODEL, D_FF), BF16),
                            pltpu.VMEM((D_FF, D_MODEL), BF16), pltpu.SemaphoreType.DMA((2,))]),
        out_shape=jax.ShapeDtypeStruct((n_slots, HALF), jnp.int32),
        compiler_params=_params(("arbitrary",)),
        name="moe_experts",
    )(block_e, block_rows, n_valid, seg_idx, next_e, xs, params['w_gate'], b4(params['b_gate']),
      params['w_up'], b4(params['b_up']), params['w_down'], b4(params['b_down']))


def _combine_kernel(off_ref, np_ref, x1_ref, ir_ref, pr_ref, ys_hbm, g2_ref, b2_ref, o_ref,
                    stage_ref, ysb_ref, wgt_ref, acc_ref, sem, *, tm, n_tiles):
    i = pl.program_id(0)
    slot = i % 2
    mk = lambda s: (lambda dst, src: pltpu.make_async_copy(ys_hbm.at[src], stage_ref.at[s, dst], sem.at[s]))

    @pl.when(i == 0)
    def _():
        _region_copies(off_ref, 0, 0, mk(0), True)

    col_id = _small_int_bf16(lax.broadcasted_iota(jnp.int32, (tm, ONEHOT_CHUNK), 1))
    ir = ir_ref[...]
    pr = pr_ref[...].astype(BF16)
    zero = jnp.zeros((), BF16)

    def weighted_sum(p):
        tgt = [_slot_targets(ir[:, k:k + 1], ir[:, TOP_K + k:TOP_K + k + 1], p) for k in range(TOP_K)]
        for c in range(N_EXPERTS * REGION_CAP // ONEHOT_CHUNK):
            cols = slice(c * ONEHOT_CHUNK, (c + 1) * ONEHOT_CHUNK)
            wgt = zero
            for k in reversed(range(TOP_K)):
                wgt = jnp.where(col_id == _small_int_bf16(tgt[k] - c * ONEHOT_CHUNK), pr[:, k:k + 1], wgt)
            wgt_ref[:, cols] = wgt
            ysb_ref[cols, :] = _unpack_bf16_pairs(stage_ref[slot, cols, :])
        return _dot(wgt_ref[...], ysb_ref[...])

    _region_copies(off_ref, i, 0, mk(slot), False)

    @pl.when(i + 1 < n_tiles)
    def _():
        _region_copies(off_ref, i + 1, 0, mk(1 - slot), True)

    acc_ref[...] = weighted_sum(0)

    def extra_pass(p, carry):
        _region_copies(off_ref, i, p, mk(slot), True)
        _region_copies(off_ref, i, p, mk(slot), False)
        acc_ref[...] += weighted_sum(p)
        return carry

    lax.fori_loop(1, np_ref[i], extra_pass, 0)
    o_ref[...] = _layer_norm(DN_ALPHA * x1_ref[...] + acc_ref[...], g2_ref[...], b2_ref[...])


def _combine(x1, idx_rank, probs, ys, off, n_pass, p, tm):
    t = x1.shape[0]
    row = lambda a: a.reshape(1, D_MODEL).astype(F32)
    full = lambda shp: pl.BlockSpec(shp, lambda i, *_: (0,) * len(shp))
    rowb = lambda w: pl.BlockSpec((tm, w), lambda i, *_: (i, 0))
    return pl.pallas_call(
        functools.partial(_combine_kernel, tm=tm, n_tiles=t // tm),
        grid_spec=pltpu.PrefetchScalarGridSpec(
            num_scalar_prefetch=2,
            grid=(t // tm,),
            in_specs=[rowb(D_MODEL), rowb(LANES), rowb(LANES), pl.BlockSpec(memory_space=pl.ANY),
                      full((1, D_MODEL)), full((1, D_MODEL))],
            out_specs=rowb(D_MODEL),
            scratch_shapes=[pltpu.VMEM((2, N_EXPERTS * REGION_CAP, HALF), jnp.int32),
                            pltpu.VMEM((N_EXPERTS * REGION_CAP, D_MODEL), BF16),
                            pltpu.VMEM((tm, N_EXPERTS * REGION_CAP), BF16),
                            pltpu.VMEM((tm, D_MODEL), F32),
                            pltpu.SemaphoreType.DMA((2,))]),
        out_shape=jax.ShapeDtypeStruct((t, D_MODEL), F32),
        compiler_params=_params(("arbitrary",)),
        name="moe_combine_ln2",
    )(off, n_pass, x1, idx_rank, probs, ys, row(p['ln2_g']), row(p['ln2_b']))


def _moe(x1, x1b, idx_rank, irt, probs, tile_counts, p, params, l, bm, tm):
    t = x1.shape[0]
    n_tiles = t // tm
    cnt = tile_counts[:, 0, :N_EXPERTS]
    cnt8 = (cnt + 7) // 8 * 8
    seg = jnp.sum(cnt8, axis=0)
    padded = (seg + SEGMENT_SLACK + bm - 1) // bm * bm
    pend = jnp.cumsum(padded)
    pstart = pend - padded
    off = pstart[None, :] + jnp.cumsum(cnt8, axis=0) - cnt8
    n_pass = jnp.maximum((jnp.max(cnt, axis=1) + REGION_CAP - 1) // REGION_CAP, 1).astype(jnp.int32)
    nb = -(-(t * TOP_K + 7 * n_tiles * N_EXPERTS + SEGMENT_SLACK * N_EXPERTS) // bm) + N_EXPERTS
    n_valid = (pend[-1] // bm).astype(jnp.int32).reshape(1)
    blocks = jnp.arange(nb, dtype=jnp.int32)
    block_e = jnp.minimum(jnp.sum((pend // bm)[None, :] <= blocks[:, None], axis=1), N_EXPERTS - 1).astype(jnp.int32)
    block_rows = jnp.clip(seg[block_e] - (blocks * bm - pstart[block_e]), 0, bm).astype(jnp.int32)
    off = off.astype(jnp.int32).reshape(-1)
    pad = jnp.concatenate([pstart + seg, padded - seg]).astype(jnp.int32)
    first = jnp.concatenate([jnp.ones((1,), jnp.int32), (block_e[1:] != block_e[:-1]).astype(jnp.int32)])
    seg_idx = (jnp.cumsum(first) - 1).astype(jnp.int32)
    next_first = (pend // bm)[block_e]
    next_e = jnp.where(next_first < n_valid[0], block_e[jnp.minimum(next_first, nb - 1)], -1).astype(jnp.int32)
    xs = _dispatch(x1b, irt, off, n_pass, pad, n_valid, nb, bm, tm)
    ys = _experts(xs, block_e, block_rows, n_valid, seg_idx, next_e, params, l, bm)
    return _combine(x1, idx_rank, probs, ys, off, n_pass, p, tm)


_EXPERT_WEIGHTS = ('w_gate', 'b_gate', 'w_up', 'b_up', 'w_down', 'b_down')


def _layer(x, params, l, b, s, cfg):
    p = {k: v[l] for k, v in params.items() if k not in _EXPERT_WEIGHTS}
    xb, x_groups = _prep(x, cfg['prep_tm'])
    w_in = p['w_in']
    o = IN_OFFS
    pad = jnp.zeros((D_MODEL, LANES - 2 * ML_HEADS), w_in.dtype)
    w_ret = w_in[:, o[0]:o[4]].astype(BF16)
    w_ml = jnp.concatenate([w_in[:, o[4]:o[5]], w_in[:, o[7]:o[8]]], axis=1).astype(BF16)
    w_if = jnp.concatenate([w_in[:, o[5]:o[7]], pad], axis=1).astype(BF16)
    w_gates = w_in[:, o[11]:o[12]].astype(BF16)
    y_ret = _retention(xb, w_ret, p['ret_gn'], b, s, cfg['seq_rows']).reshape(b * s, MIX_W)
    y_ml = _mlstm(xb, w_ml, w_if, p, b, s, cfg['seq_rows']).reshape(b * s, MIX_W)
    outs, lses = [], []
    wq = ATT_HEADS * ATT_DK
    wv = ATT_HEADS * ATT_DV
    for g in range(ATT_GROUPS):
        assert ATT_DK == 64
        w_att = jnp.concatenate([w_in[:, o[8] + g * wq:o[8] + (g + 1) * wq] * (ATT_DK ** -0.5),
                                 w_in[:, o[9] + g * wq:o[9] + (g + 1) * wq],
                                 w_in[:, o[10] + g * wv:o[10] + (g + 1) * wv]], axis=1).astype(BF16)
        d = ATT_PATTERNS[g][1]
        og, lg = _dilated_group(xb if d == 1 else x_groups[d], w_att, g, b, s, cfg['att_nq'])
        outs.append(og)
        lses.append(lg)
    x1, x1b, idx_rank, irt, probs, tile_counts = _merge(x, xb, y_ret, y_ml, outs, lses, w_gates, p, cfg['merge_tm'])
    return _moe(x1, x1b, idx_rank, irt, probs, tile_counts, p, params, l, cfg['moe_bm'], ROUTE_TM)


CFG = dict(prep_tm=1024, seq_rows=512, att_nq=4, merge_tm=512, moe_bm=512)

_PARAM_NAMES = ('w_in', 'ret_gn', 'ml_conv_w', 'ml_conv_b', 'ml_wq', 'ml_wk', 'ml_wv', 'ml_bi', 'ml_bf', 'ml_gn',
                'ml_skip', 'w_branch', 'w_out', 'ln1_g', 'ln1_b', 'w_router', 'b_router', 'w_gate', 'b_gate',
                'w_up', 'b_up', 'w_down', 'b_down', 'ln2_g', 'ln2_b')


def _forward(x, params, cfg):
    b, s, d = x.shape
    xf = x.reshape(b * s, d).astype(F32)
    for l in range(DEPTH):
        xf = _layer(xf, params, l, b, s, cfg)
    return xf.reshape(b, s, d).astype(x.dtype)


def kernel(x, w_in, ret_gn, ml_conv_w, ml_conv_b, ml_wq, ml_wk, ml_wv, ml_bi, ml_bf, ml_gn, ml_skip, w_branch, w_out, ln1_g, ln1_b, w_router, b_router, w_gate, b_gate, w_up, b_up, w_down, b_down, ln2_g, ln2_b):
    params = dict(zip(_PARAM_NAMES, (w_in, ret_gn, ml_conv_w, ml_conv_b, ml_wq, ml_wk, ml_wv, ml_bi, ml_bf, ml_gn,
                                     ml_skip, w_branch, w_out, ln1_g, ln1_b, w_router, b_router, w_gate, b_gate,
                                     w_up, b_up, w_down, b_down, ln2_g, ln2_b)))
    return _forward(x, params, CFG)
```

```python
import functools

import jax
import jax.numpy as jnp
import numpy as np
from jax import lax
from jax.experimental import pallas as pl
from jax.experimental.pallas import tpu as pltpu

F32 = jnp.float32
BF16 = jnp.bfloat16

D_MODEL = 1024
DEPTH = 2
MIX_W = D_MODEL // 2
N_BRANCHES = 3
RET_HEADS = 4
RET_DV = MIX_W // RET_HEADS
RET_DK = RET_DV // 2
ML_HEADS = 4
ML_DH = MIX_W // ML_HEADS
ML_CONV = 4
ML_QK_BLOCK = 4
ML_HEAD_GROUP = 1
ATT_PATTERNS = ((128, 1), (512, 4), (2048, 16))
ATT_GROUPS = len(ATT_PATTERNS)
ATT_HEADS = 4
ATT_DV = MIX_W // ATT_HEADS
ATT_DK = ATT_DV // 2
N_EXPERTS = 32
TOP_K = 4
D_FF = D_MODEL
SWIGLU_LIMIT = 7.0
SWIGLU_ALPHA = 1.702
DN_ALPHA = (2.0 * DEPTH) ** 0.25
EPS = 1e-5

CHUNK = 128
LANES = 128
NEG = -1e30
VMEM_LIMIT = 56 * 1024 * 1024

IN_SIZES = (RET_HEADS * RET_DK, RET_HEADS * RET_DK, MIX_W, MIX_W,
            MIX_W, ML_HEADS, ML_HEADS, MIX_W,
            ATT_GROUPS * ATT_HEADS * ATT_DK, ATT_GROUPS * ATT_HEADS * ATT_DK, ATT_GROUPS * ATT_HEADS * ATT_DV,
            N_BRANCHES * D_MODEL)
IN_OFFS = tuple(int(v) for v in np.cumsum((0,) + IN_SIZES))


def _params(sem):
    return pltpu.CompilerParams(dimension_semantics=sem, vmem_limit_bytes=VMEM_LIMIT)


def _dot(a, b):
    return jnp.dot(a, b, preferred_element_type=F32)


def _dot_nt(a, b):
    return lax.dot_general(a, b, (((1,), (1,)), ((), ())), preferred_element_type=F32)


def _dot_tn(a, b):
    return lax.dot_general(a, b, (((0,), (0,)), ((), ())), preferred_element_type=F32)


def _sigmoid(x):
    return 1.0 / (1.0 + jnp.exp(-x))


def _prep_kernel(x_ref, xb_ref, *rest, dilations):
    group_refs, cols_ref = rest[:-1], rest[-1]
    x = x_ref[...]
    xb_ref[...] = x.astype(xb_ref.dtype)
    tm, dm = x.shape
    for c in range(dm // LANES):
        cols_ref[c] = x[:, c * LANES:(c + 1) * LANES]
    for o_ref, d in zip(group_refs, dilations):
        n = tm // d
        for r in range(d):
            for c in range(dm // LANES):
                o_ref[:, r * dm + c * LANES:r * dm + (c + 1) * LANES] = (
                    cols_ref[c, pl.ds(r, n, stride=d), :].astype(o_ref.dtype))


def _prep(x, tm):
    t, dm = x.shape
    tm = min(tm, t)
    dilations = tuple(sorted({d for _, d in ATT_PATTERNS if d > 1}))
    assert t % tm == 0 and all(tm % (16 * d) == 0 for d in dilations)
    outs = pl.pallas_call(
        functools.partial(_prep_kernel, dilations=dilations),
        grid=(t // tm,),
        in_specs=[pl.BlockSpec((tm, dm), lambda i: (i, 0))],
        out_specs=[pl.BlockSpec((tm, dm), lambda i: (i, 0))]
                  + [pl.BlockSpec((tm // d, d * dm), lambda i: (i, 0)) for d in dilations],
        out_shape=[jax.ShapeDtypeStruct((t, dm), BF16)]
                  + [jax.ShapeDtypeStruct((t // d, d * dm), BF16) for d in dilations],
        scratch_shapes=[pltpu.VMEM((dm // LANES, tm, LANES), F32)],
        compiler_params=_params(("arbitrary",)),
        name="prep_bf16_groups",
    )(x)
    return outs[0], dict(zip(dilations, outs[1:]))


def _head_norm(o, gn):
    mu = jnp.mean(o, axis=-1, keepdims=True)
    oc = o - mu
    var = jnp.mean(oc * oc, axis=-1, keepdims=True)
    return oc * lax.rsqrt(var + EPS) * gn


def _ret_kernel(x_ref, w_ref, dm_ref, qd_ref, kd_ref, cd_ref, gn_ref, o_ref, st_ref, blk_ref, *, nc, nb):
    @pl.when(pl.program_id(0) == 0)
    def _():
        st_ref[...] = jnp.zeros_like(st_ref)

    for bi in range(nb):
        blk_ref[bi] = _dot(x_ref[bi], w_ref[...]).astype(blk_ref.dtype)

    hq = RET_HEADS * RET_DK
    state = [st_ref[j] for j in range(nb * RET_HEADS)]
    for c in range(nc):
        rows = pl.ds(c * CHUNK, CHUNK)
        chains = [(h, bi) for h in range(RET_HEADS) for bi in range(nb)]
        first = []
        for h, bi in chains:
            q = blk_ref[bi, rows, h * RET_DK:(h + 1) * RET_DK]
            k = blk_ref[bi, rows, hq + h * RET_DK:hq + (h + 1) * RET_DK]
            v = blk_ref[bi, rows, 2 * hq + h * RET_DV:2 * hq + (h + 1) * RET_DV]
            st = state[bi * RET_HEADS + h]
            first.append((v, _dot_nt(q, k), _dot(q, st.astype(BF16)),
                          _dot_tn(k, (v.astype(F32) * kd_ref[h]).astype(BF16))))
        for (h, bi), (v, s_qk, q_st, kv) in zip(chains, first):
            o = _dot((s_qk * dm_ref[h]).astype(BF16), v) + q_st * qd_ref[h]
            state[bi * RET_HEADS + h] = cd_ref[h] * state[bi * RET_HEADS + h] + kv
            g = blk_ref[bi, rows, 2 * hq + MIX_W + h * RET_DV:2 * hq + MIX_W + (h + 1) * RET_DV].astype(F32)
            y = _head_norm(o, gn_ref[:, h * RET_DV:(h + 1) * RET_DV]) * (g * _sigmoid(g))
            o_ref[bi, rows, h * RET_DV:(h + 1) * RET_DV] = y.astype(o_ref.dtype)
    for j in range(nb * RET_HEADS):
        st_ref[j] = state[j]


def _retention(xb, w_ret, ret_gn, b, s, rows):
    nc = rows // CHUNK
    scale = RET_DK ** -0.5
    log_gamma = jnp.log1p(-jnp.exp2(-5.0 - jnp.arange(RET_HEADS, dtype=F32)))
    pos = jnp.arange(CHUNK, dtype=F32)
    diff = pos[:, None] - pos[None, :]
    dm = jnp.where(diff >= 0, jnp.exp(log_gamma[:, None, None] * jnp.maximum(diff, 0.0)), 0.0) * scale
    kd = jnp.broadcast_to(jnp.exp(log_gamma[:, None] * (CHUNK - 1.0 - pos))[:, :, None], (RET_HEADS, CHUNK, RET_DV))
    qd = jnp.broadcast_to((jnp.exp(log_gamma[:, None] * (pos + 1.0)) * scale)[:, :, None], (RET_HEADS, CHUNK, RET_DV))
    cd = jnp.broadcast_to(jnp.exp(log_gamma * CHUNK)[:, None, None], (RET_HEADS, 1, RET_DV))
    dmod, w_in = w_ret.shape
    full = lambda shp: pl.BlockSpec(shp, lambda si: (0,) * len(shp))
    return pl.pallas_call(
        functools.partial(_ret_kernel, nc=nc, nb=b),
        grid=(s // rows,),
        in_specs=[pl.BlockSpec((b, rows, dmod), lambda si: (0, si, 0)), full((dmod, w_in)),
                  full((RET_HEADS, CHUNK, CHUNK)), full((RET_HEADS, CHUNK, RET_DV)),
                  full((RET_HEADS, CHUNK, RET_DV)), full((RET_HEADS, 1, RET_DV)), full((1, MIX_W))],
        out_specs=pl.BlockSpec((b, rows, MIX_W), lambda si: (0, si, 0)),
        out_shape=jax.ShapeDtypeStruct((b, s, MIX_W), BF16),
        scratch_shapes=[pltpu.VMEM((b * RET_HEADS, RET_DK, RET_DV), F32), pltpu.VMEM((b, rows, w_in), BF16)],
        compiler_params=_params(("arbitrary",)),
        name="retention",
    )(xb.reshape(b, s, dmod), w_ret, dm, qd, kd, cd, ret_gn.reshape(1, MIX_W).astype(F32))


def _log_sigmoid(x):
    return jnp.minimum(x, 0.0) - jnp.log(1.0 + jnp.exp(-jnp.abs(x)))


def _ml_kernel(x_ref, wml_ref, wif_ref, cw_ref, cb_ref, bq_ref, bk_ref, bv_ref, gb_ref, gn_ref, sk_ref,
               o_ref, c_ref, n_ref, m_ref, halo_ref, ml_ref, if_ref, qkv_ref, *, nc, nb):
    rows_total = nc * CHUNK

    @pl.when(pl.program_id(0) == 0)
    def _():
        c_ref[...] = jnp.zeros_like(c_ref)
        n_ref[...] = jnp.zeros_like(n_ref)
        m_ref[...] = jnp.zeros_like(m_ref)
        halo_ref[...] = jnp.zeros_like(halo_ref)

    for bi in range(nb):
        ml_ref[bi] = _dot(x_ref[bi], wml_ref[...]).astype(ml_ref.dtype)
        if_ref[bi] = _dot(x_ref[bi], wif_ref[...])

    x_all, mc_all = [], []
    for bi in range(nb):
        xa = ml_ref[bi, :, 0:MIX_W].astype(F32)
        xf = jnp.concatenate([halo_ref[bi], xa], axis=0)
        acc = jnp.broadcast_to(cb_ref[...], (rows_total, MIX_W))
        for j in range(ML_CONV):
            sh = ML_CONV - 1 - j
            xs = xf if sh == 0 else pltpu.roll(xf, sh, 0)
            acc = acc + xs[8:8 + rows_total] * cw_ref[j:j + 1, :]
        halo_ref[bi] = xa[rows_total - 8:rows_total]
        x_all.append(xa)
        mc_all.append(acc * _sigmoid(acc))

    ri = lax.broadcasted_iota(jnp.int32, (CHUNK, CHUNK), 0)
    ci = lax.broadcasted_iota(jnp.int32, (CHUNK, CHUNK), 1)
    tri = ri >= ci
    tril = tri.astype(F32)
    kscale = ML_DH ** -0.5
    log_kscale = float(np.log(kscale))

    c_st = [[c_ref[bi * ML_HEADS + h] for h in range(ML_HEADS)] for bi in range(nb)]
    n_st = [[n_ref[bi, h:h + 1, :] for h in range(ML_HEADS)] for bi in range(nb)]
    m_st = [[m_ref[bi, h:h + 1, 0:1] for h in range(ML_HEADS)] for bi in range(nb)]

    for bi in range(nb):
        mcb = mc_all[bi].astype(BF16)
        xvb = x_all[bi].astype(BF16)
        for h in range(ML_HEADS):
            cols = slice(h * ML_DH, (h + 1) * ML_DH)
            qkv_ref[bi, 0, :, cols] = _dot(mcb[:, cols], bq_ref[h]).astype(BF16)
            qkv_ref[bi, 1, :, cols] = _dot(mcb[:, cols], bk_ref[h]).astype(BF16)
            qkv_ref[bi, 2, :, cols] = _dot(xvb[:, cols], bv_ref[h]).astype(BF16)

    def run_chains(r0, gates, chains):
        for h, bi in chains:
            g_pre, cum, g_t, cum_t = gates[bi]
            cols = slice(h * ML_DH, (h + 1) * ML_DH)
            mc = mc_all[bi][r0:r0 + CHUNK, cols]
            q = qkv_ref[bi, 0, r0:r0 + CHUNK, cols]
            k = qkv_ref[bi, 1, r0:r0 + CHUNK, cols]
            v = qkv_ref[bi, 2, r0:r0 + CHUNK, cols]
            cum_col = cum[:, ML_HEADS + h:ML_HEADS + h + 1]
            ig_col = g_pre[:, h:h + 1]
            cum_row = cum_t[ML_HEADS + h:ML_HEADS + h + 1, :]
            ig_row = g_t[h:h + 1, :]
            tot = cum[CHUNK - 1:CHUNK, ML_HEADS + h:ML_HEADS + h + 1]
            m_prev, c_prev, n_prev = m_st[bi][h], c_st[bi][h], n_st[bi][h]
            dlog = jnp.where(tri, cum_col + (ig_row - cum_row), NEG)
            m_inter = cum_col + m_prev
            m_q = jnp.maximum(m_inter, jnp.max(dlog, axis=-1, keepdims=True))
            w_qk = jnp.exp(dlog - (m_q - log_kscale)) * _dot_nt(q, k)
            inter = jnp.exp(m_inter - m_q)
            num = _dot(w_qk.astype(BF16), v) + inter * _dot(q, c_prev.astype(BF16))
            den = (jnp.sum(w_qk, axis=-1, keepdims=True)
                   + inter * jnp.sum(q.astype(F32) * n_prev, axis=-1, keepdims=True))
            hh = num * (1.0 / jnp.maximum(jnp.abs(den), jnp.exp(-m_q)))
            a_col = tot - cum_col + ig_col
            a_max = jnp.max(a_col, axis=0, keepdims=True)
            wa = jnp.exp(a_col - a_max) * kscale
            chunk_c = _dot_tn(k, (wa * v.astype(F32)).astype(BF16))
            chunk_n = jnp.sum(wa * k.astype(F32), axis=0, keepdims=True)
            m_new = jnp.maximum(tot + m_prev, a_max)
            s_old = jnp.exp(tot + m_prev - m_new)
            s_new = jnp.exp(a_max - m_new)
            c_st[bi][h] = s_old * c_prev + s_new * chunk_c
            n_st[bi][h] = s_old * n_prev + s_new * chunk_n
            m_st[bi][h] = m_new
            mo = ml_ref[bi, r0:r0 + CHUNK, MIX_W + h * ML_DH:MIX_W + (h + 1) * ML_DH].astype(F32)
            y = _sigmoid(mo) * (_head_norm(hh, gn_ref[:, cols]) + sk_ref[:, cols] * mc)
            o_ref[bi, r0:r0 + CHUNK, cols] = y.astype(o_ref.dtype)

    for c in range(nc):
        r0 = c * CHUNK
        gates = []
        for bi in range(nb):
            g_pre = if_ref[bi, r0:r0 + CHUNK, :] + gb_ref[...]
            cum = jnp.dot(tril, _log_sigmoid(g_pre), preferred_element_type=F32, precision=lax.Precision.HIGHEST)
            gates.append((g_pre, cum, g_pre.T, cum.T))
        for h0 in range(0, ML_HEADS, ML_HEAD_GROUP):
            run_chains(r0, gates, [(h, bi) for h in range(h0, h0 + ML_HEAD_GROUP) for bi in range(nb)])

    for bi in range(nb):
        for h in range(ML_HEADS):
            c_ref[bi * ML_HEADS + h] = c_st[bi][h]
            n_ref[bi, h:h + 1, :] = n_st[bi][h]
            m_ref[bi, h:h + 1, :] = jnp.broadcast_to(m_st[bi][h], (1, LANES))


def _block_diag(w):
    nb = w.shape[0]
    per = nb // ML_HEADS
    wh = w.reshape(ML_HEADS, per, ML_QK_BLOCK, ML_QK_BLOCK)
    eye = jnp.eye(per, dtype=w.dtype)
    bd = jnp.einsum('hncd,nm->hncmd', wh, eye)
    return bd.reshape(ML_HEADS, per * ML_QK_BLOCK, per * ML_QK_BLOCK).astype(BF16)


def _mlstm(xb, w_ml, w_if, p, b, s, rows):
    nc = rows // CHUNK
    dm = xb.shape[1]
    full = lambda shp: pl.BlockSpec(shp, lambda si: (0,) * len(shp))
    gb = jnp.zeros((1, LANES), F32).at[0, 0:ML_HEADS].set(p['ml_bi'].astype(F32)).at[0, ML_HEADS:2 * ML_HEADS].set(p['ml_bf'].astype(F32))
    row = lambda a: a.reshape(1, MIX_W).astype(F32)
    return pl.pallas_call(
        functools.partial(_ml_kernel, nc=nc, nb=b),
        grid=(s // rows,),
        in_specs=[pl.BlockSpec((b, rows, dm), lambda si: (0, si, 0)),
                  full((dm, 2 * MIX_W)), full((dm, LANES)),
                  full((ML_CONV, MIX_W)), full((1, MIX_W)),
                  full((ML_HEADS, ML_DH, ML_DH)), full((ML_HEADS, ML_DH, ML_DH)), full((ML_HEADS, ML_DH, ML_DH)),
                  full((1, LANES)), full((1, MIX_W)), full((1, MIX_W))],
        out_specs=pl.BlockSpec((b, rows, MIX_W), lambda si: (0, si, 0)),
        out_shape=jax.ShapeDtypeStruct((b, s, MIX_W), BF16),
        scratch_shapes=[pltpu.VMEM((b * ML_HEADS, ML_DH, ML_DH), F32), pltpu.VMEM((b, 8, ML_DH), F32),
                        pltpu.VMEM((b, 8, LANES), F32), pltpu.VMEM((b, 8, MIX_W), F32),
                        pltpu.VMEM((b, rows, 2 * MIX_W), BF16), pltpu.VMEM((b, rows, LANES), F32),
                        pltpu.VMEM((b, 3, rows, MIX_W), BF16)],
        compiler_params=_params(("arbitrary",)),
        name="mlstm",
    )(xb.reshape(b, s, dm), w_ml, w_if,
      p['ml_conv_w'].astype(F32), row(p['ml_conv_b']),
      _block_diag(p['ml_wq']), _block_diag(p['ml_wk']), _block_diag(p['ml_wv']),
      gb, row(p['ml_gn']), row(p['ml_skip']))


def _att_kernel(x_ref, w_ref, bias_ref, o_ref, lse_ref, qkv_ref, *, nq):
    first = pl.program_id(2) == 0
    span = nq * CHUNK
    wq = ATT_HEADS * ATT_DK

    @pl.when(first)
    def _():
        qkv_ref[0:CHUNK, :] = jnp.zeros((CHUNK, qkv_ref.shape[1]), qkv_ref.dtype)

    qkv_ref[CHUNK:CHUNK + span, :] = _dot(x_ref[...], w_ref[...]).astype(qkv_ref.dtype)
    lane = lax.broadcasted_iota(jnp.int32, (CHUNK, LANES), 1)
    ones = jnp.ones((2 * CHUNK, ATT_DV), BF16)
    for j in range(nq):
        rows = slice(j * CHUNK, (j + 1) * CHUNK)
        kv_rows = slice(j * CHUNK, (j + 2) * CHUNK)
        lse_out = jnp.zeros((CHUNK, LANES), F32)
        for h in range(ATT_HEADS):
            dv = slice(h * ATT_DV, (h + 1) * ATT_DV)
            q = qkv_ref[CHUNK + j * CHUNK:CHUNK + (j + 1) * CHUNK, h * ATT_DK:(h + 1) * ATT_DK]
            kk = qkv_ref[kv_rows, wq + h * ATT_DK:wq + (h + 1) * ATT_DK]
            vv = qkv_ref[kv_rows, 2 * wq + h * ATT_DV:2 * wq + (h + 1) * ATT_DV]
            bias = bias_ref[jnp.where(first, 0, 1), h] if j == 0 else bias_ref[1, h]
            s = _dot_nt(q, kk) + bias
            m = jnp.max(s, axis=-1, keepdims=True)
            p = jnp.exp(s - m).astype(BF16)
            oa = _dot(p, jnp.concatenate([vv, ones], axis=1))
            l = oa[:, ATT_DV:2 * ATT_DV]
            o_ref[rows, dv] = (oa[:, 0:ATT_DV] * (1.0 / l)).astype(o_ref.dtype)
            lse_out = jnp.where(lane == h, m + jnp.log(l), lse_out)
        lse_ref[rows, :] = lse_out
    qkv_ref[0:CHUNK, :] = qkv_ref[span:span + CHUNK, :]


def _att_bias(window, dilation, slopes):
    wb = window // dilation
    qi = jnp.arange(wb)[:, None]
    kj = jnp.arange(2 * wb)[None, :]
    delta = qi + wb - kj
    band = (delta >= 0) & (delta <= wb)
    bias = -slopes[:, None, None] * (dilation * delta).astype(F32)
    later = jnp.where(band[None], bias, NEG)
    first = jnp.where((band & (kj >= wb))[None], bias, NEG)
    return jnp.stack([first, later]).astype(F32)


def _dilated_group(xg, w_qkv, g, b, s, nq):
    window, d = ATT_PATTERNS[g]
    assert window // d == CHUNK
    l_sub = s // d
    assert l_sub % CHUNK == 0
    n_blk = l_sub // CHUNK
    slopes = jnp.exp2(-8.0 * jnp.arange(1, ATT_GROUPS * ATT_HEADS + 1, dtype=F32) / (ATT_GROUPS * ATT_HEADS))
    bias = _att_bias(window, d, slopes.reshape(ATT_GROUPS, ATT_HEADS)[g])
    wq = ATT_HEADS * ATT_DK
    wv = ATT_HEADS * ATT_DV
    per_tok = 2 * wq + wv
    dm = w_qkv.shape[0]
    assert xg.shape == (b * l_sub, d * dm)
    x = xg.reshape(b, l_sub, d * dm)
    nq = min(nq, n_blk)
    assert n_blk % nq == 0
    span = nq * CHUNK
    o, lse = pl.pallas_call(
        functools.partial(_att_kernel, nq=nq),
        grid=(b, d, n_blk // nq),
        in_specs=[pl.BlockSpec((None, span, dm), lambda bi, r, n: (bi, n, r)),
                  pl.BlockSpec((dm, per_tok), lambda bi, r, n: (0, 0)),
                  pl.BlockSpec((2, ATT_HEADS, CHUNK, 2 * CHUNK), lambda bi, r, n: (0, 0, 0, 0))],
        out_specs=[pl.BlockSpec((None, span, wv), lambda bi, r, n: (bi, n, r)),
                   pl.BlockSpec((None, span, LANES), lambda bi, r, n: (bi, n, r))],
        out_shape=[jax.ShapeDtypeStruct((b, l_sub, d * wv), BF16),
                   jax.ShapeDtypeStruct((b, l_sub, d * LANES), F32)],
        scratch_shapes=[pltpu.VMEM((CHUNK + span, per_tok), BF16)],
        compiler_params=_params(("arbitrary", "arbitrary", "arbitrary")),
        name=f"dilated_attn_g{g}",
    )(x, w_qkv, bias)
    return o.reshape(b * l_sub, d * wv), lse.reshape(b * l_sub, d * LANES)


def _layer_norm(z, g, b):
    mu = jnp.mean(z, axis=-1, keepdims=True)
    zc = z - mu
    var = jnp.mean(zc * zc, axis=-1, keepdims=True)
    return zc * lax.rsqrt(var + EPS) * g + b


def _merge_kernel(x_ref, yr_ref, ym_ref, o0_ref, o1_ref, o2_ref, l0_ref, l1_ref, l2_ref, xb_ref,
                  wg_ref, wb_ref, wo_ref, g1_ref, b1_ref, wrh_ref, wrl_ref, br_ref,
                  x1_ref, x1b_ref, ir_ref, irt_ref, pr_ref, cnt_ref, og_ref, lg_ref, *, tm):
    for g, (src_o, src_l) in enumerate(((o0_ref, l0_ref), (o1_ref, l1_ref), (o2_ref, l2_ref))):
        d = ATT_PATTERNS[g][1]
        n = tm // d
        for r in range(d):
            rows = pl.ds(r, n, stride=d) if d > 1 else pl.ds(0, n)
            for h in range(ATT_HEADS):
                og_ref[g, h, rows, :] = src_o[:, r * MIX_W + h * ATT_DV:r * MIX_W + (h + 1) * ATT_DV].astype(F32)
            lg_ref[g, rows, :] = src_l[:, r * LANES:(r + 1) * LANES]

    l0, l1, l2 = lg_ref[0], lg_ref[1], lg_ref[2]
    lm = jnp.maximum(jnp.maximum(l0, l1), l2)
    e0, e1, e2 = jnp.exp(l0 - lm), jnp.exp(l1 - lm), jnp.exp(l2 - lm)
    inv = 1.0 / (e0 + e1 + e2)
    parts = []
    for h in range(ATT_HEADS):
        dv = slice(h * ATT_DV, (h + 1) * ATT_DV)
        hs = slice(h, h + 1)
        parts.append((e0[:, hs] * og_ref[0, h] + e1[:, hs] * og_ref[1, h] + e2[:, hs] * og_ref[2, h]) * inv[:, hs])
    y_att = jnp.concatenate(parts, axis=-1).astype(BF16)

    xb = xb_ref[...]
    gate = lambda j: _sigmoid(_dot(xb, wg_ref[:, j * D_MODEL:(j + 1) * D_MODEL]))
    merged = gate(0) * _dot(yr_ref[...], wb_ref[0])
    merged = merged + gate(1) * _dot(ym_ref[...], wb_ref[1])
    merged = merged + gate(2) * _dot(y_att, wb_ref[2])
    z = DN_ALPHA * x_ref[...] + _dot(merged.astype(BF16), wo_ref[...])
    x1 = _layer_norm(z, g1_ref[...], b1_ref[...])
    x1_ref[...] = x1
    x1b_ref[...] = x1.astype(BF16)

    xh = x1.astype(BF16)
    xl = (x1 - xh.astype(F32)).astype(BF16)
    logits = _dot(xh, wrh_ref[...]) + _dot(xl, wrh_ref[...]) + _dot(xh, wrl_ref[...]) + br_ref[...]
    lane = lax.broadcasted_iota(jnp.int32, (tm, LANES), 1)
    work = jnp.where(lane < N_EXPERTS, logits, -jnp.inf)
    vals, idxs = [], []
    for _ in range(TOP_K):
        m = jnp.max(work, axis=-1, keepdims=True)
        idx = jnp.min(jnp.where(work == m, lane, LANES), axis=-1, keepdims=True)
        vals.append(m)
        idxs.append(idx)
        work = jnp.where(lane == idx, -jnp.inf, work)
    es = [jnp.exp(v - vals[0]) for v in vals]
    tot = es[0] + es[1] + es[2] + es[3]
    pr = jnp.zeros((tm, LANES), F32)
    for k in range(TOP_K):
        pr = jnp.where(lane == k, es[k] / tot, pr)
    pr_ref[...] = pr

    onehot = jnp.zeros((tm, LANES), F32)
    for k in range(TOP_K):
        onehot = onehot + (lane == idxs[k]).astype(F32)
    ri = lax.broadcasted_iota(jnp.int32, (tm, tm), 0)
    ci = lax.broadcasted_iota(jnp.int32, (tm, tm), 1)
    same_tile = (ri // ROUTE_TM) == (ci // ROUTE_TM)
    before = _dot(((ri > ci) & same_tile).astype(BF16), onehot.astype(BF16))
    ir = jnp.zeros((tm, LANES), F32)
    for k in range(TOP_K):
        rank = jnp.sum(jnp.where(lane == idxs[k], before, 0.0), axis=-1, keepdims=True)
        ir = jnp.where(lane == k, idxs[k].astype(F32), ir)
        ir = jnp.where(lane == TOP_K + k, rank, ir)
    ir_ref[...] = ir.astype(jnp.int32)
    irt_ref[...] = ir.T[0:8, :]
    for j in range(tm // ROUTE_TM):
        cnt_j = jnp.sum(onehot[j * ROUTE_TM:(j + 1) * ROUTE_TM], axis=0, keepdims=True)
        cnt_ref[j] = jnp.broadcast_to(cnt_j, (8, LANES)).astype(jnp.int32)


def _merge(x, xb, y_ret, y_ml, outs, lses, w_gates, p, tm):
    t = x.shape[0]
    tm = min(tm, t)
    rowb = lambda w: pl.BlockSpec((tm, w), lambda i: (i, 0))
    full = lambda shp: pl.BlockSpec(shp, lambda i: (0,) * len(shp))
    wr = jnp.zeros((D_MODEL, LANES), F32).at[:, :N_EXPERTS].set(p['w_router'].astype(F32))
    wrh = wr.astype(BF16)
    wrl = (wr - wrh.astype(F32)).astype(BF16)
    br = jnp.zeros((1, LANES), F32).at[0, :N_EXPERTS].set(p['b_router'].astype(F32))
    row = lambda a: a.reshape(1, D_MODEL).astype(F32)
    return pl.pallas_call(
        functools.partial(_merge_kernel, tm=tm),
        grid=(t // tm,),
        in_specs=[rowb(D_MODEL), rowb(MIX_W), rowb(MIX_W)]
                 + [pl.BlockSpec((tm // d, d * MIX_W), lambda i: (i, 0)) for _, d in ATT_PATTERNS]
                 + [pl.BlockSpec((tm // d, d * LANES), lambda i: (i, 0)) for _, d in ATT_PATTERNS]
                 + [rowb(D_MODEL), full((D_MODEL, N_BRANCHES * D_MODEL)),
                  full((N_BRANCHES, MIX_W, D_MODEL)), full((D_MODEL, D_MODEL)),
                  full((1, D_MODEL)), full((1, D_MODEL)),
                  full((D_MODEL, LANES)), full((D_MODEL, LANES)), full((1, LANES))],
        out_specs=[rowb(D_MODEL), rowb(D_MODEL), rowb(LANES), pl.BlockSpec((8, tm), lambda i: (0, i)), rowb(LANES),
                   pl.BlockSpec((tm // ROUTE_TM, 8, LANES), lambda i: (i, 0, 0))],
        out_shape=[jax.ShapeDtypeStruct((t, D_MODEL), F32),
                   jax.ShapeDtypeStruct((t, D_MODEL), BF16),
                   jax.ShapeDtypeStruct((t, LANES), jnp.int32),
                   jax.ShapeDtypeStruct((8, t), F32),
                   jax.ShapeDtypeStruct((t, LANES), F32),
                   jax.ShapeDtypeStruct((t // ROUTE_TM, 8, LANES), jnp.int32)],
        scratch_shapes=[pltpu.VMEM((ATT_GROUPS, ATT_HEADS, tm, ATT_DV), F32),
                        pltpu.VMEM((ATT_GROUPS, tm, LANES), F32)],
        compiler_params=_params(("arbitrary",)),
        name="merge_ln1_router",
    )(x, y_ret, y_ml, outs[0], outs[1], outs[2], lses[0], lses[1], lses[2], xb, w_gates,
      p['w_branch'].astype(BF16), p['w_out'].astype(BF16), row(p['ln1_g']), row(p['ln1_b']), wrh, wrl, br)


ROUTE_TM = 256
REGION_CAP = 64
SEGMENT_SLACK = REGION_CAP
PAD_PIECES = (512, 256, 128, 64, 32, 16, 8)
ONEHOT_CHUNK = 256
HALF = D_MODEL // 2
HI_MASK = -65536


def _pack_bf16_pairs(y):
    lo = lax.shift_right_logical(lax.bitcast_convert_type(y[:, :HALF], jnp.int32), 16)
    hi = lax.bitcast_convert_type(y[:, HALF:], jnp.int32) & HI_MASK
    return lo | hi


def _unpack_bf16_pairs(w):
    lo = lax.bitcast_convert_type(lax.shift_left(w, 16), F32)
    hi = lax.bitcast_convert_type(w & HI_MASK, F32)
    return jnp.concatenate([lo, hi], axis=1).astype(BF16)


def _row_pieces(n, pieces, fn):
    for size in pieces:
        if size == pieces[0]:
            cond, pos = n >= size, 0
        else:
            cond, pos = (n & size) != 0, pl.multiple_of(n & (-2 * size), 8)

        @pl.when(cond)
        def _(pos=pos, size=size):
            fn(pos, size)


def _region_copies(off_ref, cnt_ref, tile, p, make_copy, start):
    for e in range(N_EXPERTS):
        base = off_ref[tile * N_EXPERTS + e] + p * REGION_CAP
        cp = make_copy(pl.ds(e * REGION_CAP, REGION_CAP), pl.ds(pl.multiple_of(base, 8), REGION_CAP))
        act = cp.start if start else cp.wait
        if cnt_ref is None:
            act()
        else:
            pl.when(cnt_ref[tile * N_EXPERTS + e] > p * REGION_CAP)(act)


def _slot_targets(idx, lrank, p):
    lr = lrank - p * REGION_CAP
    return jnp.where((lr >= 0) & (lr < REGION_CAP), idx * REGION_CAP + lr, -1)


def _small_int_bf16(v):
    return v.astype(F32).astype(BF16)


def _dispatch_kernel(off_ref, cnt_ref, np_ref, pad_ref, nv_ref, x_ref, irt_ref, xs_hbm,
                     stage_ref, zero_ref, sem, zsem, *, tm, n_tiles, bm, nb):
    i = pl.program_id(0)
    slot = i % 2
    row_id = _small_int_bf16(lax.broadcasted_iota(jnp.int32, (ONEHOT_CHUNK, tm), 0))
    meta = irt_ref[...].astype(jnp.int32)
    x = x_ref[...]
    mk = lambda s: (lambda src, dst: pltpu.make_async_copy(stage_ref.at[s, src], xs_hbm.at[dst], sem.at[s]))

    def build(p):
        tgt = [_slot_targets(meta[k:k + 1, :], meta[TOP_K + k:TOP_K + k + 1, :], p) for k in range(TOP_K)]
        for c in range(N_EXPERTS * REGION_CAP // ONEHOT_CHUNK):
            hit = row_id == _small_int_bf16(tgt[0] - c * ONEHOT_CHUNK)
            for k in range(1, TOP_K):
                hit = hit | (row_id == _small_int_bf16(tgt[k] - c * ONEHOT_CHUNK))
            sel = jnp.where(hit, jnp.ones((), BF16), jnp.zeros((), BF16))
            stage_ref[slot, c * ONEHOT_CHUNK:(c + 1) * ONEHOT_CHUNK, :] = _pack_bf16_pairs(_dot(sel, x))

    build(0)

    @pl.when((i > 0) & (np_ref[jnp.maximum(i - 1, 0)] == 1))
    def _():
        _region_copies(off_ref, None, i - 1, 0, mk(1 - slot), False)

    _region_copies(off_ref, None, i, 0, mk(slot), True)

    def extra_pass(p, carry):
        @pl.when(p == 1)
        def _():
            _region_copies(off_ref, None, i, 0, mk(slot), False)

        @pl.when(p > 1)
        def _():
            _region_copies(off_ref, cnt_ref, i, p - 1, mk(slot), False)

        build(p)
        _region_copies(off_ref, cnt_ref, i, p, mk(slot), True)
        return carry

    lax.fori_loop(1, np_ref[i], extra_pass, 0)

    @pl.when(np_ref[i] > 1)
    def _():
        _region_copies(off_ref, cnt_ref, i, np_ref[i] - 1, mk(slot), False)

    @pl.when(i == n_tiles - 1)
    def _():
        @pl.when(np_ref[i] == 1)
        def _():
            _region_copies(off_ref, None, i, 0, mk(slot), False)

        zero_ref[...] = jnp.zeros_like(zero_ref)
        for phase in (True, False):
            for e in range(N_EXPERTS):
                def piece(pos, size, e=e):
                    cp = pltpu.make_async_copy(zero_ref.at[pl.ds(0, size)],
                                               xs_hbm.at[pl.ds(pl.multiple_of(pad_ref[e] + pos, 8), size)], zsem)
                    cp.start() if phase else cp.wait()
                _row_pieces(pad_ref[N_EXPERTS + e], PAD_PIECES, piece)

        def zero_block(blk, carry):
            for half in range(bm // PAD_PIECES[0]):
                cp = pltpu.make_async_copy(
                    zero_ref, xs_hbm.at[pl.ds(pl.multiple_of(blk * bm + half * PAD_PIECES[0], 8), PAD_PIECES[0])], zsem)
                cp.start()
                cp.wait()
            return carry

        lax.fori_loop(nv_ref[0], nb, zero_block, 0)


def _dispatch(x1b, irt, off, cnt8, n_pass, pad, n_valid, nb, bm, tm):
    t = x1b.shape[0]
    assert bm % PAD_PIECES[0] == 0 and bm + SEGMENT_SLACK < 2 * PAD_PIECES[0]
    return pl.pallas_call(
        functools.partial(_dispatch_kernel, tm=tm, n_tiles=t // tm, bm=bm, nb=nb),
        grid_spec=pltpu.PrefetchScalarGridSpec(
            num_scalar_prefetch=5,
            grid=(t // tm,),
            in_specs=[pl.BlockSpec((tm, D_MODEL), lambda i, *_: (i, 0)),
                      pl.BlockSpec((8, tm), lambda i, *_: (0, i))],
            out_specs=pl.BlockSpec(memory_space=pl.ANY),
            scratch_shapes=[pltpu.VMEM((2, N_EXPERTS * REGION_CAP, HALF), jnp.int32),
                            pltpu.VMEM((PAD_PIECES[0], HALF), jnp.int32),
                            pltpu.SemaphoreType.DMA((2,)), pltpu.SemaphoreType.DMA(())]),
        out_shape=jax.ShapeDtypeStruct((nb * bm, HALF), jnp.int32),
        compiler_params=_params(("arbitrary",)),
        name="moe_dispatch",
    )(off, cnt8, n_pass, pad, n_valid, x1b, irt)


def _ffn_kernel(be_ref, nr_ref, nv_ref, sg_ref, nx_ref, xs_ref, wg_hbm, bg_ref, wu_hbm, bu_ref, wd_hbm, bd_ref, o_ref,
                wf_ref, wgb_ref, wub_ref, wdb_ref, sem, *, layer):
    i = pl.program_id(0)

    def fetch(e, s, start):
        for j, w_hbm in enumerate((wg_hbm, wu_hbm, wd_hbm)):
            cp = pltpu.make_async_copy(w_hbm.at[layer, e], wf_ref.at[s, j], sem.at[s])
            if start:
                cp.start()
            else:
                cp.wait()

    @pl.when(i < nv_ref[0])
    def _():
        seg = sg_ref[i]

        @pl.when((i == 0) | (seg != sg_ref[jnp.maximum(i - 1, 0)]))
        def _():
            s = seg % 2

            @pl.when(seg == 0)
            def _():
                fetch(be_ref[i], s, True)

            fetch(be_ref[i], s, False)
            wgb_ref[...] = wf_ref[s, 0].astype(BF16)
            wub_ref[...] = wf_ref[s, 1].astype(BF16)
            wdb_ref[...] = wf_ref[s, 2].astype(BF16)

            @pl.when(nx_ref[i] >= 0)
            def _():
                fetch(nx_ref[i], 1 - s, True)

        rows = lax.broadcasted_iota(jnp.int32, xs_ref.shape, 0)
        x = _unpack_bf16_pairs(jnp.where(rows < nr_ref[i], xs_ref[...], 0))
        gate = jnp.minimum(_dot(x, wgb_ref[...]) + bg_ref[...], SWIGLU_LIMIT)
        up = jnp.clip(_dot(x, wub_ref[...]) + bu_ref[...], -SWIGLU_LIMIT, SWIGLU_LIMIT)
        hid = (up + 1.0) * gate * _sigmoid(SWIGLU_ALPHA * gate)
        y = _dot(hid.astype(BF16), wdb_ref[...]) + bd_ref[...]
        o_ref[...] = _pack_bf16_pairs(y.astype(BF16).astype(F32))

    @pl.when(i >= nv_ref[0])
    def _():
        o_ref[...] = jnp.zeros_like(o_ref)


def _experts(xs, block_e, block_rows, n_valid, seg_idx, next_e, params, l, bm):
    n_slots = xs.shape[0]
    nb = n_slots // bm
    assert D_FF == D_MODEL
    blk = lambda i, be, nr, nv, *_: (jnp.minimum(i, nv[0] - 1), 0)
    oblk = lambda i, *_: (i, 0)
    wsp = pl.BlockSpec(memory_space=pl.ANY)
    bsp = pl.BlockSpec((None, None, 1, D_FF), lambda i, be, *_: (l, be[i], 0, 0))
    b4 = lambda a: a.reshape(DEPTH, N_EXPERTS, 1, -1)
    return pl.pallas_call(
        functools.partial(_ffn_kernel, layer=l),
        grid_spec=pltpu.PrefetchScalarGridSpec(
            num_scalar_prefetch=5,
            grid=(nb,),
            in_specs=[pl.BlockSpec((bm, HALF), blk), wsp, bsp, wsp, bsp, wsp, bsp],
            out_specs=pl.BlockSpec((bm, HALF), oblk),
            scratch_shapes=[pltpu.VMEM((2, 3, D_MODEL, D_FF), F32),
                            pltpu.VMEM((D_MODEL, D_FF), BF16), pltpu.VMEM((D_MODEL, D_FF), BF16),
                            pltpu.VMEM((D_FF, D_MODEL), BF16), pltpu.SemaphoreType.DMA((2,))]),
        out_shape=jax.ShapeDtypeStruct((n_slots, HALF), jnp.int32),
        compiler_params=_params(("arbitrary",)),
        name="moe_experts",
    )(block_e, block_rows, n_valid, seg_idx, next_e, xs, params['w_gate'], b4(params['b_gate']),
      params['w_up'], b4(params['b_up']), params['w_down'], b4(params['b_down']))


def _combine_kernel(off_ref, cnt_ref, np_ref, x1_ref, ir_ref, pr_ref, ys_hbm, g2_ref, b2_ref, o_ref,
                    stage_ref, ysb_ref, wgt_ref, acc_ref, sem, *, tm, n_tiles):
    i = pl.program_id(0)
    slot = i % 2
    mk = lambda s: (lambda dst, src: pltpu.make_async_copy(ys_hbm.at[src], stage_ref.at[s, dst], sem.at[s]))

    @pl.when(i == 0)
    def _():
        _region_copies(off_ref, None, 0, 0, mk(0), True)

    col_id = _small_int_bf16(lax.broadcasted_iota(jnp.int32, (tm, ONEHOT_CHUNK), 1))
    ir = ir_ref[...]
    pr = pr_ref[...].astype(BF16)
    zero = jnp.zeros((), BF16)

    def weighted_sum(p):
        tgt = [_slot_targets(ir[:, k:k + 1], ir[:, TOP_K + k:TOP_K + k + 1], p) for k in range(TOP_K)]
        for c in range(N_EXPERTS * REGION_CAP // ONEHOT_CHUNK):
            cols = slice(c * ONEHOT_CHUNK, (c + 1) * ONEHOT_CHUNK)
            wgt = zero
            for k in reversed(range(TOP_K)):
                wgt = jnp.where(col_id == _small_int_bf16(tgt[k] - c * ONEHOT_CHUNK), pr[:, k:k + 1], wgt)
            wgt_ref[:, cols] = wgt
            ysb_ref[cols, :] = _unpack_bf16_pairs(stage_ref[slot, cols, :])
        return _dot(wgt_ref[...], ysb_ref[...])

    _region_copies(off_ref, None, i, 0, mk(slot), False)

    @pl.when(i + 1 < n_tiles)
    def _():
        _region_copies(off_ref, None, i + 1, 0, mk(1 - slot), True)

    acc_ref[...] = weighted_sum(0)

    def extra_pass(p, carry):
        _region_copies(off_ref, cnt_ref, i, p, mk(slot), True)
        _region_copies(off_ref, cnt_ref, i, p, mk(slot), False)
        acc_ref[...] += weighted_sum(p)
        return carry

    lax.fori_loop(1, np_ref[i], extra_pass, 0)
    o_ref[...] = _layer_norm(DN_ALPHA * x1_ref[...] + acc_ref[...], g2_ref[...], b2_ref[...])


def _combine(x1, idx_rank, probs, ys, off, cnt8, n_pass, p, tm):
    t = x1.shape[0]
    row = lambda a: a.reshape(1, D_MODEL).astype(F32)
    full = lambda shp: pl.BlockSpec(shp, lambda i, *_: (0,) * len(shp))
    rowb = lambda w: pl.BlockSpec((tm, w), lambda i, *_: (i, 0))
    return pl.pallas_call(
        functools.partial(_combine_kernel, tm=tm, n_tiles=t // tm),
        grid_spec=pltpu.PrefetchScalarGridSpec(
            num_scalar_prefetch=3,
            grid=(t // tm,),
            in_specs=[rowb(D_MODEL), rowb(LANES), rowb(LANES), pl.BlockSpec(memory_space=pl.ANY),
                      full((1, D_MODEL)), full((1, D_MODEL))],
            out_specs=rowb(D_MODEL),
            scratch_shapes=[pltpu.VMEM((2, N_EXPERTS * REGION_CAP, HALF), jnp.int32),
                            pltpu.VMEM((N_EXPERTS * REGION_CAP, D_MODEL), BF16),
                            pltpu.VMEM((tm, N_EXPERTS * REGION_CAP), BF16),
                            pltpu.VMEM((tm, D_MODEL), F32),
                            pltpu.SemaphoreType.DMA((2,))]),
        out_shape=jax.ShapeDtypeStruct((t, D_MODEL), F32),
        compiler_params=_params(("arbitrary",)),
        name="moe_combine_ln2",
    )(off, cnt8, n_pass, x1, idx_rank, probs, ys, row(p['ln2_g']), row(p['ln2_b']))


def _moe(x1, x1b, idx_rank, irt, probs, tile_counts, p, params, l, bm, tm):
    t = x1.shape[0]
    n_tiles = t // tm
    cnt = tile_counts[:, 0, :N_EXPERTS]
    cnt8 = (cnt + 7) // 8 * 8
    seg = jnp.sum(cnt8, axis=0)
    padded = (seg + SEGMENT_SLACK + bm - 1) // bm * bm
    pend = jnp.cumsum(padded)
    pstart = pend - padded
    off = pstart[None, :] + jnp.cumsum(cnt8, axis=0) - cnt8
    n_pass = jnp.maximum((jnp.max(cnt, axis=1) + REGION_CAP - 1) // REGION_CAP, 1).astype(jnp.int32)
    nb = -(-(t * TOP_K + 7 * n_tiles * N_EXPERTS + SEGMENT_SLACK * N_EXPERTS) // bm) + N_EXPERTS
    n_valid = (pend[-1] // bm).astype(jnp.int32).reshape(1)
    blocks = jnp.arange(nb, dtype=jnp.int32)
    block_e = jnp.minimum(jnp.sum((pend // bm)[None, :] <= blocks[:, None], axis=1), N_EXPERTS - 1).astype(jnp.int32)
    block_rows = jnp.clip(seg[block_e] - (blocks * bm - pstart[block_e]), 0, bm).astype(jnp.int32)
    off = off.astype(jnp.int32).reshape(-1)
    cnt8 = cnt8.astype(jnp.int32).reshape(-1)
    pad = jnp.concatenate([pstart + seg, padded - seg]).astype(jnp.int32)
    first = jnp.concatenate([jnp.ones((1,), jnp.int32), (block_e[1:] != block_e[:-1]).astype(jnp.int32)])
    seg_idx = (jnp.cumsum(first) - 1).astype(jnp.int32)
    next_first = (pend // bm)[block_e]
    next_e = jnp.where(next_first < n_valid[0], block_e[jnp.minimum(next_first, nb - 1)], -1).astype(jnp.int32)
    xs = _dispatch(x1b, irt, off, cnt8, n_pass, pad, n_valid, nb, bm, tm)
    ys = _experts(xs, block_e, block_rows, n_valid, seg_idx, next_e, params, l, bm)
    return _combine(x1, idx_rank, probs, ys, off, cnt8, n_pass, p, tm)


_EXPERT_WEIGHTS = ('w_gate', 'b_gate', 'w_up', 'b_up', 'w_down', 'b_down')


def _layer(x, params, l, b, s, cfg):
    p = {k: v[l] for k, v in params.items() if k not in _EXPERT_WEIGHTS}
    xb, x_groups = _prep(x, cfg['prep_tm'])
    w_in = p['w_in']
    o = IN_OFFS
    pad = jnp.zeros((D_MODEL, LANES - 2 * ML_HEADS), w_in.dtype)
    w_ret = w_in[:, o[0]:o[4]].astype(BF16)
    w_ml = jnp.concatenate([w_in[:, o[4]:o[5]], w_in[:, o[7]:o[8]]], axis=1).astype(BF16)
    w_if = jnp.concatenate([w_in[:, o[5]:o[7]], pad], axis=1).astype(BF16)
    w_gates = w_in[:, o[11]:o[12]].astype(BF16)
    y_ret = _retention(xb, w_ret, p['ret_gn'], b, s, cfg['seq_rows']).reshape(b * s, MIX_W)
    y_ml = _mlstm(xb, w_ml, w_if, p, b, s, cfg['seq_rows']).reshape(b * s, MIX_W)
    outs, lses = [], []
    wq = ATT_HEADS * ATT_DK
    wv = ATT_HEADS * ATT_DV
    for g in range(ATT_GROUPS):
        assert ATT_DK == 64
        w_att = jnp.concatenate([w_in[:, o[8] + g * wq:o[8] + (g + 1) * wq] * (ATT_DK ** -0.5),
                                 w_in[:, o[9] + g * wq:o[9] + (g + 1) * wq],
                                 w_in[:, o[10] + g * wv:o[10] + (g + 1) * wv]], axis=1).astype(BF16)
        d = ATT_PATTERNS[g][1]
        og, lg = _dilated_group(xb if d == 1 else x_groups[d], w_att, g, b, s, cfg['att_nq'])
        outs.append(og)
        lses.append(lg)
    x1, x1b, idx_rank, irt, probs, tile_counts = _merge(x, xb, y_ret, y_ml, outs, lses, w_gates, p, cfg['merge_tm'])
    return _moe(x1, x1b, idx_rank, irt, probs, tile_counts, p, params, l, cfg['moe_bm'], ROUTE_TM)


CFG = dict(prep_tm=1024, seq_rows=512, att_nq=4, merge_tm=512, moe_bm=512)

_PARAM_NAMES = ('w_in', 'ret_gn', 'ml_conv_w', 'ml_conv_b', 'ml_wq', 'ml_wk', 'ml_wv', 'ml_bi', 'ml_bf', 'ml_gn',
                'ml_skip', 'w_branch', 'w_out', 'ln1_g', 'ln1_b', 'w_router', 'b_router', 'w_gate', 'b_gate',
                'w_up', 'b_up', 'w_down', 'b_down', 'ln2_g', 'ln2_b')


def _forward(x, params, cfg):
    b, s, d = x.shape
    xf = x.reshape(b * s, d).astype(F32)
    for l in range(DEPTH):
        xf = _layer(xf, params, l, b, s, cfg)
    return xf.reshape(b, s, d).astype(x.dtype)


def kernel(x, w_in, ret_gn, ml_conv_w, ml_conv_b, ml_wq, ml_wk, ml_wv, ml_bi, ml_bf, ml_gn, ml_skip, w_branch, w_out, ln1_g, ln1_b, w_router, b_router, w_gate, b_gate, w_up, b_up, w_down, b_down, ln2_g, ln2_b):
    params = dict(zip(_PARAM_NAMES, (w_in, ret_gn, ml_conv_w, ml_conv_b, ml_wq, ml_wk, ml_wv, ml_bi, ml_bf, ml_gn,
                                     ml_skip, w_branch, w_out, ln1_g, ln1_b, w_router, b_router, w_gate, b_gate,
                                     w_up, b_up, w_down, b_down, ln2_g, ln2_b)))
    return _forward(x, params, CFG)
```

```python
import functools

import jax
import jax.numpy as jnp
import numpy as np
from jax import lax
from jax.experimental import pallas as pl
from jax.experimental.pallas import tpu as pltpu

F32 = jnp.float32
BF16 = jnp.bfloat16

D_MODEL = 1024
DEPTH = 2
MIX_W = D_MODEL // 2
N_BRANCHES = 3
RET_HEADS = 4
RET_DV = MIX_W // RET_HEADS
RET_DK = RET_DV // 2
ML_HEADS = 4
ML_DH = MIX_W // ML_HEADS
ML_CONV = 4
ML_QK_BLOCK = 4
ML_HEAD_GROUP = 1
ATT_PATTERNS = ((128, 1), (512, 4), (2048, 16))
ATT_GROUPS = len(ATT_PATTERNS)
ATT_HEADS = 4
ATT_DV = MIX_W // ATT_HEADS
ATT_DK = ATT_DV // 2
N_EXPERTS = 32
TOP_K = 4
D_FF = D_MODEL
SWIGLU_LIMIT = 7.0
SWIGLU_ALPHA = 1.702
DN_ALPHA = (2.0 * DEPTH) ** 0.25
EPS = 1e-5

CHUNK = 128
LANES = 128
NEG = -1e30
VMEM_LIMIT = 56 * 1024 * 1024

IN_SIZES = (RET_HEADS * RET_DK, RET_HEADS * RET_DK, MIX_W, MIX_W,
            MIX_W, ML_HEADS, ML_HEADS, MIX_W,
            ATT_GROUPS * ATT_HEADS * ATT_DK, ATT_GROUPS * ATT_HEADS * ATT_DK, ATT_GROUPS * ATT_HEADS * ATT_DV,
            N_BRANCHES * D_MODEL)
IN_OFFS = tuple(int(v) for v in np.cumsum((0,) + IN_SIZES))


def _params(sem):
    return pltpu.CompilerParams(dimension_semantics=sem, vmem_limit_bytes=VMEM_LIMIT)


def _dot(a, b):
    return jnp.dot(a, b, preferred_element_type=F32)


def _dot_nt(a, b):
    return lax.dot_general(a, b, (((1,), (1,)), ((), ())), preferred_element_type=F32)


def _dot_tn(a, b):
    return lax.dot_general(a, b, (((0,), (0,)), ((), ())), preferred_element_type=F32)


def _sigmoid(x):
    return 1.0 / (1.0 + jnp.exp(-x))


def _prep_kernel(x_ref, xb_ref, *rest, dilations):
    group_refs, cols_ref = rest[:-1], rest[-1]
    x = x_ref[...]
    xb_ref[...] = x.astype(xb_ref.dtype)
    tm, dm = x.shape
    for c in range(dm // LANES):
        cols_ref[c] = x[:, c * LANES:(c + 1) * LANES]
    for o_ref, d in zip(group_refs, dilations):
        n = tm // d
        for r in range(d):
            for c in range(dm // LANES):
                o_ref[:, r * dm + c * LANES:r * dm + (c + 1) * LANES] = (
                    cols_ref[c, pl.ds(r, n, stride=d), :].astype(o_ref.dtype))


def _prep(x, tm):
    t, dm = x.shape
    tm = min(tm, t)
    dilations = tuple(sorted({d for _, d in ATT_PATTERNS if d > 1}))
    assert t % tm == 0 and all(tm % (16 * d) == 0 for d in dilations)
    outs = pl.pallas_call(
        functools.partial(_prep_kernel, dilations=dilations),
        grid=(t // tm,),
        in_specs=[pl.BlockSpec((tm, dm), lambda i: (i, 0))],
        out_specs=[pl.BlockSpec((tm, dm), lambda i: (i, 0))]
                  + [pl.BlockSpec((tm // d, d * dm), lambda i: (i, 0)) for d in dilations],
        out_shape=[jax.ShapeDtypeStruct((t, dm), BF16)]
                  + [jax.ShapeDtypeStruct((t // d, d * dm), BF16) for d in dilations],
        scratch_shapes=[pltpu.VMEM((dm // LANES, tm, LANES), F32)],
        compiler_params=_params(("arbitrary",)),
        name="prep_bf16_groups",
    )(x)
    return outs[0], dict(zip(dilations, outs[1:]))


def _head_norm(o, gn):
    mu = jnp.mean(o, axis=-1, keepdims=True)
    oc = o - mu
    var = jnp.mean(oc * oc, axis=-1, keepdims=True)
    return oc * lax.rsqrt(var + EPS) * gn


def _ret_kernel(x_ref, w_ref, dm_ref, qd_ref, kd_ref, cd_ref, gn_ref, o_ref, st_ref, blk_ref, *, nc, nb):
    @pl.when(pl.program_id(0) == 0)
    def _():
        st_ref[...] = jnp.zeros_like(st_ref)

    for bi in range(nb):
        blk_ref[bi] = _dot(x_ref[bi], w_ref[...]).astype(blk_ref.dtype)

    hq = RET_HEADS * RET_DK
    state = [st_ref[j] for j in range(nb * RET_HEADS)]
    for c in range(nc):
        rows = pl.ds(c * CHUNK, CHUNK)
        chains = [(h, bi) for h in range(RET_HEADS) for bi in range(nb)]
        first = []
        for h, bi in chains:
            q = blk_ref[bi, rows, h * RET_DK:(h + 1) * RET_DK]
            k = blk_ref[bi, rows, hq + h * RET_DK:hq + (h + 1) * RET_DK]
            v = blk_ref[bi, rows, 2 * hq + h * RET_DV:2 * hq + (h + 1) * RET_DV]
            st = state[bi * RET_HEADS + h]
            first.append((v, _dot_nt(q, k), _dot(q, st.astype(BF16)),
                          _dot_tn(k, (v.astype(F32) * kd_ref[h]).astype(BF16))))
        for (h, bi), (v, s_qk, q_st, kv) in zip(chains, first):
            o = _dot((s_qk * dm_ref[h]).astype(BF16), v) + q_st * qd_ref[h]
            state[bi * RET_HEADS + h] = cd_ref[h] * state[bi * RET_HEADS + h] + kv
            g = blk_ref[bi, rows, 2 * hq + MIX_W + h * RET_DV:2 * hq + MIX_W + (h + 1) * RET_DV].astype(F32)
            y = _head_norm(o, gn_ref[:, h * RET_DV:(h + 1) * RET_DV]) * (g * _sigmoid(g))
            o_ref[bi, rows, h * RET_DV:(h + 1) * RET_DV] = y.astype(o_ref.dtype)
    for j in range(nb * RET_HEADS):
        st_ref[j] = state[j]


def _retention(xb, w_ret, ret_gn, b, s, rows):
    nc = rows // CHUNK
    scale = RET_DK ** -0.5
    log_gamma = jnp.log1p(-jnp.exp2(-5.0 - jnp.arange(RET_HEADS, dtype=F32)))
    pos = jnp.arange(CHUNK, dtype=F32)
    diff = pos[:, None] - pos[None, :]
    dm = jnp.where(diff >= 0, jnp.exp(log_gamma[:, None, None] * jnp.maximum(diff, 0.0)), 0.0) * scale
    kd = jnp.broadcast_to(jnp.exp(log_gamma[:, None] * (CHUNK - 1.0 - pos))[:, :, None], (RET_HEADS, CHUNK, RET_DV))
    qd = jnp.broadcast_to((jnp.exp(log_gamma[:, None] * (pos + 1.0)) * scale)[:, :, None], (RET_HEADS, CHUNK, RET_DV))
    cd = jnp.broadcast_to(jnp.exp(log_gamma * CHUNK)[:, None, None], (RET_HEADS, 1, RET_DV))
    dmod, w_in = w_ret.shape
    full = lambda shp: pl.BlockSpec(shp, lambda si: (0,) * len(shp))
    return pl.pallas_call(
        functools.partial(_ret_kernel, nc=nc, nb=b),
        grid=(s // rows,),
        in_specs=[pl.BlockSpec((b, rows, dmod), lambda si: (0, si, 0)), full((dmod, w_in)),
                  full((RET_HEADS, CHUNK, CHUNK)), full((RET_HEADS, CHUNK, RET_DV)),
                  full((RET_HEADS, CHUNK, RET_DV)), full((RET_HEADS, 1, RET_DV)), full((1, MIX_W))],
        out_specs=pl.BlockSpec((b, rows, MIX_W), lambda si: (0, si, 0)),
        out_shape=jax.ShapeDtypeStruct((b, s, MIX_W), BF16),
        scratch_shapes=[pltpu.VMEM((b * RET_HEADS, RET_DK, RET_DV), F32), pltpu.VMEM((b, rows, w_in), BF16)],
        compiler_params=_params(("arbitrary",)),
        name="retention",
    )(xb.reshape(b, s, dmod), w_ret, dm, qd, kd, cd, ret_gn.reshape(1, MIX_W).astype(F32))


def _log_sigmoid(x):
    return jnp.minimum(x, 0.0) - jnp.log(1.0 + jnp.exp(-jnp.abs(x)))


def _ml_kernel(x_ref, wml_ref, wif_ref, cw_ref, cb_ref, bq_ref, bk_ref, bv_ref, gb_ref, gn_ref, sk_ref,
               o_ref, c_ref, n_ref, m_ref, halo_ref, ml_ref, if_ref, qkv_ref, *, nc, nb):
    rows_total = nc * CHUNK

    @pl.when(pl.program_id(0) == 0)
    def _():
        c_ref[...] = jnp.zeros_like(c_ref)
        n_ref[...] = jnp.zeros_like(n_ref)
        m_ref[...] = jnp.zeros_like(m_ref)
        halo_ref[...] = jnp.zeros_like(halo_ref)

    for bi in range(nb):
        ml_ref[bi] = _dot(x_ref[bi], wml_ref[...]).astype(ml_ref.dtype)
        if_ref[bi] = _dot(x_ref[bi], wif_ref[...])

    x_all, mc_all = [], []
    for bi in range(nb):
        xa = ml_ref[bi, :, 0:MIX_W].astype(F32)
        xf = jnp.concatenate([halo_ref[bi], xa], axis=0)
        acc = jnp.broadcast_to(cb_ref[...], (rows_total, MIX_W))
        for j in range(ML_CONV):
            sh = ML_CONV - 1 - j
            xs = xf if sh == 0 else pltpu.roll(xf, sh, 0)
            acc = acc + xs[8:8 + rows_total] * cw_ref[j:j + 1, :]
        halo_ref[bi] = xa[rows_total - 8:rows_total]
        x_all.append(xa)
        mc_all.append(acc * _sigmoid(acc))

    ri = lax.broadcasted_iota(jnp.int32, (CHUNK, CHUNK), 0)
    ci = lax.broadcasted_iota(jnp.int32, (CHUNK, CHUNK), 1)
    tri = ri >= ci
    tril = tri.astype(F32)
    kscale = ML_DH ** -0.5
    log_kscale = float(np.log(kscale))

    c_st = [[c_ref[bi * ML_HEADS + h] for h in range(ML_HEADS)] for bi in range(nb)]
    n_st = [[n_ref[bi, h:h + 1, :] for h in range(ML_HEADS)] for bi in range(nb)]
    m_st = [[m_ref[bi, h:h + 1, 0:1] for h in range(ML_HEADS)] for bi in range(nb)]

    for bi in range(nb):
        mcb = mc_all[bi].astype(BF16)
        xvb = x_all[bi].astype(BF16)
        for h in range(ML_HEADS):
            cols = slice(h * ML_DH, (h + 1) * ML_DH)
            qkv_ref[bi, 0, :, cols] = _dot(mcb[:, cols], bq_ref[h]).astype(BF16)
            qkv_ref[bi, 1, :, cols] = _dot(mcb[:, cols], bk_ref[h]).astype(BF16)
            qkv_ref[bi, 2, :, cols] = _dot(xvb[:, cols], bv_ref[h]).astype(BF16)

    def run_chains(r0, gates, chains):
        for h, bi in chains:
            g_pre, cum, g_t, cum_t = gates[bi]
            cols = slice(h * ML_DH, (h + 1) * ML_DH)
            mc = mc_all[bi][r0:r0 + CHUNK, cols]
            q = qkv_ref[bi, 0, r0:r0 + CHUNK, cols]
            k = qkv_ref[bi, 1, r0:r0 + CHUNK, cols]
            v = qkv_ref[bi, 2, r0:r0 + CHUNK, cols]
            cum_col = cum[:, ML_HEADS + h:ML_HEADS + h + 1]
            ig_col = g_pre[:, h:h + 1]
            cum_row = cum_t[ML_HEADS + h:ML_HEADS + h + 1, :]
            ig_row = g_t[h:h + 1, :]
            tot = cum[CHUNK - 1:CHUNK, ML_HEADS + h:ML_HEADS + h + 1]
            m_prev, c_prev, n_prev = m_st[bi][h], c_st[bi][h], n_st[bi][h]
            dlog = jnp.where(tri, cum_col + (ig_row - cum_row), NEG)
            m_inter = cum_col + m_prev
            m_q = jnp.maximum(m_inter, jnp.max(dlog, axis=-1, keepdims=True))
            w_qk = jnp.exp(dlog - (m_q - log_kscale)) * _dot_nt(q, k)
            inter = jnp.exp(m_inter - m_q)
            num = _dot(w_qk.astype(BF16), v) + inter * _dot(q, c_prev.astype(BF16))
            den = (jnp.sum(w_qk, axis=-1, keepdims=True)
                   + inter * jnp.sum(q.astype(F32) * n_prev, axis=-1, keepdims=True))
            hh = num * (1.0 / jnp.maximum(jnp.abs(den), jnp.exp(-m_q)))
            a_col = tot - cum_col + ig_col
            a_max = jnp.max(a_col, axis=0, keepdims=True)
            wa = jnp.exp(a_col - a_max) * kscale
            chunk_c = _dot_tn(k, (wa * v.astype(F32)).astype(BF16))
            chunk_n = jnp.sum(wa * k.astype(F32), axis=0, keepdims=True)
            m_new = jnp.maximum(tot + m_prev, a_max)
            s_old = jnp.exp(tot + m_prev - m_new)
            s_new = jnp.exp(a_max - m_new)
            c_st[bi][h] = s_old * c_prev + s_new * chunk_c
            n_st[bi][h] = s_old * n_prev + s_new * chunk_n
            m_st[bi][h] = m_new
            mo = ml_ref[bi, r0:r0 + CHUNK, MIX_W + h * ML_DH:MIX_W + (h + 1) * ML_DH].astype(F32)
            y = _sigmoid(mo) * (_head_norm(hh, gn_ref[:, cols]) + sk_ref[:, cols] * mc)
            o_ref[bi, r0:r0 + CHUNK, cols] = y.astype(o_ref.dtype)

    for c in range(nc):
        r0 = c * CHUNK
        gates = []
        for bi in range(nb):
            g_pre = if_ref[bi, r0:r0 + CHUNK, :] + gb_ref[...]
            cum = jnp.dot(tril, _log_sigmoid(g_pre), preferred_element_type=F32, precision=lax.Precision.HIGHEST)
            gates.append((g_pre, cum, g_pre.T, cum.T))
        for h0 in range(0, ML_HEADS, ML_HEAD_GROUP):
            run_chains(r0, gates, [(h, bi) for h in range(h0, h0 + ML_HEAD_GROUP) for bi in range(nb)])

    for bi in range(nb):
        for h in range(ML_HEADS):
            c_ref[bi * ML_HEADS + h] = c_st[bi][h]
            n_ref[bi, h:h + 1, :] = n_st[bi][h]
            m_ref[bi, h:h + 1, :] = jnp.broadcast_to(m_st[bi][h], (1, LANES))


def _block_diag(w):
    nb = w.shape[0]
    per = nb // ML_HEADS
    wh = w.reshape(ML_HEADS, per, ML_QK_BLOCK, ML_QK_BLOCK)
    eye = jnp.eye(per, dtype=w.dtype)
    bd = jnp.einsum('hncd,nm->hncmd', wh, eye)
    return bd.reshape(ML_HEADS, per * ML_QK_BLOCK, per * ML_QK_BLOCK).astype(BF16)


def _mlstm(xb, w_ml, w_if, p, b, s, rows):
    nc = rows // CHUNK
    dm = xb.shape[1]
    full = lambda shp: pl.BlockSpec(shp, lambda si: (0,) * len(shp))
    gb = jnp.zeros((1, LANES), F32).at[0, 0:ML_HEADS].set(p['ml_bi'].astype(F32)).at[0, ML_HEADS:2 * ML_HEADS].set(p['ml_bf'].astype(F32))
    row = lambda a: a.reshape(1, MIX_W).astype(F32)
    return pl.pallas_call(
        functools.partial(_ml_kernel, nc=nc, nb=b),
        grid=(s // rows,),
        in_specs=[pl.BlockSpec((b, rows, dm), lambda si: (0, si, 0)),
                  full((dm, 2 * MIX_W)), full((dm, LANES)),
                  full((ML_CONV, MIX_W)), full((1, MIX_W)),
                  full((ML_HEADS, ML_DH, ML_DH)), full((ML_HEADS, ML_DH, ML_DH)), full((ML_HEADS, ML_DH, ML_DH)),
                  full((1, LANES)), full((1, MIX_W)), full((1, MIX_W))],
        out_specs=pl.BlockSpec((b, rows, MIX_W), lambda si: (0, si, 0)),
        out_shape=jax.ShapeDtypeStruct((b, s, MIX_W), BF16),
        scratch_shapes=[pltpu.VMEM((b * ML_HEADS, ML_DH, ML_DH), F32), pltpu.VMEM((b, 8, ML_DH), F32),
                        pltpu.VMEM((b, 8, LANES), F32), pltpu.VMEM((b, 8, MIX_W), F32),
                        pltpu.VMEM((b, rows, 2 * MIX_W), BF16), pltpu.VMEM((b, rows, LANES), F32),
                        pltpu.VMEM((b, 3, rows, MIX_W), BF16)],
        compiler_params=_params(("arbitrary",)),
        name="mlstm",
    )(xb.reshape(b, s, dm), w_ml, w_if,
      p['ml_conv_w'].astype(F32), row(p['ml_conv_b']),
      _block_diag(p['ml_wq']), _block_diag(p['ml_wk']), _block_diag(p['ml_wv']),
      gb, row(p['ml_gn']), row(p['ml_skip']))


def _att_kernel(x_ref, w_ref, bias_ref, o_ref, lse_ref, qkv_ref, *, nq):
    first = pl.program_id(2) == 0
    span = nq * CHUNK
    wq = ATT_HEADS * ATT_DK

    @pl.when(first)
    def _():
        qkv_ref[0:CHUNK, :] = jnp.zeros((CHUNK, qkv_ref.shape[1]), qkv_ref.dtype)

    qkv_ref[CHUNK:CHUNK + span, :] = _dot(x_ref[...], w_ref[...]).astype(qkv_ref.dtype)
    lane = lax.broadcasted_iota(jnp.int32, (CHUNK, LANES), 1)
    ones = jnp.ones((2 * CHUNK, ATT_DV), BF16)
    for j in range(nq):
        rows = slice(j * CHUNK, (j + 1) * CHUNK)
        kv_rows = slice(j * CHUNK, (j + 2) * CHUNK)
        lse_out = jnp.zeros((CHUNK, LANES), F32)
        for h in range(ATT_HEADS):
            dv = slice(h * ATT_DV, (h + 1) * ATT_DV)
            q = qkv_ref[CHUNK + j * CHUNK:CHUNK + (j + 1) * CHUNK, h * ATT_DK:(h + 1) * ATT_DK]
            kk = qkv_ref[kv_rows, wq + h * ATT_DK:wq + (h + 1) * ATT_DK]
            vv = qkv_ref[kv_rows, 2 * wq + h * ATT_DV:2 * wq + (h + 1) * ATT_DV]
            bias = bias_ref[jnp.where(first, 0, 1), h] if j == 0 else bias_ref[1, h]
            s = _dot_nt(q, kk) + bias
            m = jnp.max(s, axis=-1, keepdims=True)
            p = jnp.exp(s - m).astype(BF16)
            oa = _dot(p, jnp.concatenate([vv, ones], axis=1))
            l = oa[:, ATT_DV:2 * ATT_DV]
            o_ref[rows, dv] = (oa[:, 0:ATT_DV] * (1.0 / l)).astype(o_ref.dtype)
            lse_out = jnp.where(lane == h, m + jnp.log(l), lse_out)
        lse_ref[rows, :] = lse_out
    qkv_ref[0:CHUNK, :] = qkv_ref[span:span + CHUNK, :]


def _att_bias(window, dilation, slopes):
    wb = window // dilation
    qi = jnp.arange(wb)[:, None]
    kj = jnp.arange(2 * wb)[None, :]
    delta = qi + wb - kj
    band = (delta >= 0) & (delta <= wb)
    bias = -slopes[:, None, None] * (dilation * delta).astype(F32)
    later = jnp.where(band[None], bias, NEG)
    first = jnp.where((band & (kj >= wb))[None], bias, NEG)
    return jnp.stack([first, later]).astype(F32)


def _dilated_group(xg, w_qkv, g, b, s, nq):
    window, d = ATT_PATTERNS[g]
    assert window // d == CHUNK
    l_sub = s // d
    assert l_sub % CHUNK == 0
    n_blk = l_sub // CHUNK
    slopes = jnp.exp2(-8.0 * jnp.arange(1, ATT_GROUPS * ATT_HEADS + 1, dtype=F32) / (ATT_GROUPS * ATT_HEADS))
    bias = _att_bias(window, d, slopes.reshape(ATT_GROUPS, ATT_HEADS)[g])
    wq = ATT_HEADS * ATT_DK
    wv = ATT_HEADS * ATT_DV
    per_tok = 2 * wq + wv
    dm = w_qkv.shape[0]
    assert xg.shape == (b * l_sub, d * dm)
    x = xg.reshape(b, l_sub, d * dm)
    nq = min(nq, n_blk)
    assert n_blk % nq == 0
    span = nq * CHUNK
    o, lse = pl.pallas_call(
        functools.partial(_att_kernel, nq=nq),
        grid=(b, d, n_blk // nq),
        in_specs=[pl.BlockSpec((None, span, dm), lambda bi, r, n: (bi, n, r)),
                  pl.BlockSpec((dm, per_tok), lambda bi, r, n: (0, 0)),
                  pl.BlockSpec((2, ATT_HEADS, CHUNK, 2 * CHUNK), lambda bi, r, n: (0, 0, 0, 0))],
        out_specs=[pl.BlockSpec((None, span, wv), lambda bi, r, n: (bi, n, r)),
                   pl.BlockSpec((None, span, LANES), lambda bi, r, n: (bi, n, r))],
        out_shape=[jax.ShapeDtypeStruct((b, l_sub, d * wv), BF16),
                   jax.ShapeDtypeStruct((b, l_sub, d * LANES), F32)],
        scratch_shapes=[pltpu.VMEM((CHUNK + span, per_tok), BF16)],
        compiler_params=_params(("arbitrary", "arbitrary", "arbitrary")),
        name=f"dilated_attn_g{g}",
    )(x, w_qkv, bias)
    return o.reshape(b * l_sub, d * wv), lse.reshape(b * l_sub, d * LANES)


def _layer_norm(z, g, b):
    mu = jnp.mean(z, axis=-1, keepdims=True)
    zc = z - mu
    var = jnp.mean(zc * zc, axis=-1, keepdims=True)
    return zc * lax.rsqrt(var + EPS) * g + b


def _merge_kernel(x_ref, yr_ref, ym_ref, o0_ref, o1_ref, o2_ref, l0_ref, l1_ref, l2_ref, xb_ref,
                  wg_ref, wb_ref, wo_ref, g1_ref, b1_ref, wrh_ref, wrl_ref, br_ref,
                  x1_ref, x1b_ref, ir_ref, irt_ref, pr_ref, cnt_ref, og_ref, lg_ref, *, tm):
    for g, (src_o, src_l) in enumerate(((o0_ref, l0_ref), (o1_ref, l1_ref), (o2_ref, l2_ref))):
        d = ATT_PATTERNS[g][1]
        n = tm // d
        for r in range(d):
            rows = pl.ds(r, n, stride=d) if d > 1 else pl.ds(0, n)
            for h in range(ATT_HEADS):
                og_ref[g, h, rows, :] = src_o[:, r * MIX_W + h * ATT_DV:r * MIX_W + (h + 1) * ATT_DV].astype(F32)
            lg_ref[g, rows, :] = src_l[:, r * LANES:(r + 1) * LANES]

    l0, l1, l2 = lg_ref[0], lg_ref[1], lg_ref[2]
    lm = jnp.maximum(jnp.maximum(l0, l1), l2)
    e0, e1, e2 = jnp.exp(l0 - lm), jnp.exp(l1 - lm), jnp.exp(l2 - lm)
    inv = 1.0 / (e0 + e1 + e2)
    parts = []
    for h in range(ATT_HEADS):
        dv = slice(h * ATT_DV, (h + 1) * ATT_DV)
        hs = slice(h, h + 1)
        parts.append((e0[:, hs] * og_ref[0, h] + e1[:, hs] * og_ref[1, h] + e2[:, hs] * og_ref[2, h]) * inv[:, hs])
    y_att = jnp.concatenate(parts, axis=-1).astype(BF16)

    xb = xb_ref[...]
    gate = lambda j: _sigmoid(_dot(xb, wg_ref[:, j * D_MODEL:(j + 1) * D_MODEL]))
    merged = gate(0) * _dot(yr_ref[...], wb_ref[0])
    merged = merged + gate(1) * _dot(ym_ref[...], wb_ref[1])
    merged = merged + gate(2) * _dot(y_att, wb_ref[2])
    z = DN_ALPHA * x_ref[...] + _dot(merged.astype(BF16), wo_ref[...])
    x1 = _layer_norm(z, g1_ref[...], b1_ref[...])
    x1_ref[...] = x1
    x1b_ref[...] = x1.astype(BF16)

    xh = x1.astype(BF16)
    xl = (x1 - xh.astype(F32)).astype(BF16)
    logits = _dot(xh, wrh_ref[...]) + _dot(xl, wrh_ref[...]) + _dot(xh, wrl_ref[...]) + br_ref[...]
    lane = lax.broadcasted_iota(jnp.int32, (tm, LANES), 1)
    work = jnp.where(lane < N_EXPERTS, logits, -jnp.inf)
    vals, idxs = [], []
    for _ in range(TOP_K):
        m = jnp.max(work, axis=-1, keepdims=True)
        idx = jnp.min(jnp.where(work == m, lane, LANES), axis=-1, keepdims=True)
        vals.append(m)
        idxs.append(idx)
        work = jnp.where(lane == idx, -jnp.inf, work)
    es = [jnp.exp(v - vals[0]) for v in vals]
    tot = es[0] + es[1] + es[2] + es[3]
    pr = jnp.zeros((tm, LANES), F32)
    for k in range(TOP_K):
        pr = jnp.where(lane == k, es[k] / tot, pr)
    pr_ref[...] = pr

    onehot = jnp.zeros((tm, LANES), F32)
    for k in range(TOP_K):
        onehot = onehot + (lane == idxs[k]).astype(F32)
    ri = lax.broadcasted_iota(jnp.int32, (tm, tm), 0)
    ci = lax.broadcasted_iota(jnp.int32, (tm, tm), 1)
    same_tile = (ri // ROUTE_TM) == (ci // ROUTE_TM)
    before = _dot(((ri > ci) & same_tile).astype(BF16), onehot.astype(BF16))
    ir = jnp.zeros((tm, LANES), F32)
    for k in range(TOP_K):
        rank = jnp.sum(jnp.where(lane == idxs[k], before, 0.0), axis=-1, keepdims=True)
        ir = jnp.where(lane == k, idxs[k].astype(F32), ir)
        ir = jnp.where(lane == TOP_K + k, rank, ir)
    ir_ref[...] = ir.astype(jnp.int32)
    irt_ref[...] = ir.T[0:8, :]
    for j in range(tm // ROUTE_TM):
        cnt_j = jnp.sum(onehot[j * ROUTE_TM:(j + 1) * ROUTE_TM], axis=0, keepdims=True)
        cnt_ref[j] = jnp.broadcast_to(cnt_j, (8, LANES)).astype(jnp.int32)


def _merge(x, xb, y_ret, y_ml, outs, lses, w_gates, p, tm):
    t = x.shape[0]
    tm = min(tm, t)
    rowb = lambda w: pl.BlockSpec((tm, w), lambda i: (i, 0))
    full = lambda shp: pl.BlockSpec(shp, lambda i: (0,) * len(shp))
    wr = jnp.zeros((D_MODEL, LANES), F32).at[:, :N_EXPERTS].set(p['w_router'].astype(F32))
    wrh = wr.astype(BF16)
    wrl = (wr - wrh.astype(F32)).astype(BF16)
    br = jnp.zeros((1, LANES), F32).at[0, :N_EXPERTS].set(p['b_router'].astype(F32))
    row = lambda a: a.reshape(1, D_MODEL).astype(F32)
    return pl.pallas_call(
        functools.partial(_merge_kernel, tm=tm),
        grid=(t // tm,),
        in_specs=[rowb(D_MODEL), rowb(MIX_W), rowb(MIX_W)]
                 + [pl.BlockSpec((tm // d, d * MIX_W), lambda i: (i, 0)) for _, d in ATT_PATTERNS]
                 + [pl.BlockSpec((tm // d, d * LANES), lambda i: (i, 0)) for _, d in ATT_PATTERNS]
                 + [rowb(D_MODEL), full((D_MODEL, N_BRANCHES * D_MODEL)),
                  full((N_BRANCHES, MIX_W, D_MODEL)), full((D_MODEL, D_MODEL)),
                  full((1, D_MODEL)), full((1, D_MODEL)),
                  full((D_MODEL, LANES)), full((D_MODEL, LANES)), full((1, LANES))],
        out_specs=[rowb(D_MODEL), rowb(D_MODEL), rowb(LANES), pl.BlockSpec((8, tm), lambda i: (0, i)), rowb(LANES),
                   pl.BlockSpec((tm // ROUTE_TM, 8, LANES), lambda i: (i, 0, 0))],
        out_shape=[jax.ShapeDtypeStruct((t, D_MODEL), F32),
                   jax.ShapeDtypeStruct((t, D_MODEL), BF16),
                   jax.ShapeDtypeStruct((t, LANES), jnp.int32),
                   jax.ShapeDtypeStruct((8, t), F32),
                   jax.ShapeDtypeStruct((t, LANES), F32),
                   jax.ShapeDtypeStruct((t // ROUTE_TM, 8, LANES), jnp.int32)],
        scratch_shapes=[pltpu.VMEM((ATT_GROUPS, ATT_HEADS, tm, ATT_DV), F32),
                        pltpu.VMEM((ATT_GROUPS, tm, LANES), F32)],
        compiler_params=_params(("arbitrary",)),
        name="merge_ln1_router",
    )(x, y_ret, y_ml, outs[0], outs[1], outs[2], lses[0], lses[1], lses[2], xb, w_gates,
      p['w_branch'].astype(BF16), p['w_out'].astype(BF16), row(p['ln1_g']), row(p['ln1_b']), wrh, wrl, br)


ROUTE_TM = 256
REGION_CAP = 64
SEGMENT_SLACK = REGION_CAP
PAD_PIECES = (512, 256, 128, 64, 32, 16, 8)
ONEHOT_CHUNK = 256
HALF = D_MODEL // 2
HI_MASK = -65536


def _pack_bf16_pairs(y):
    lo = lax.shift_right_logical(lax.bitcast_convert_type(y[:, :HALF], jnp.int32), 16)
    hi = lax.bitcast_convert_type(y[:, HALF:], jnp.int32) & HI_MASK
    return lo | hi


def _unpack_bf16_pairs(w):
    lo = lax.bitcast_convert_type(lax.shift_left(w, 16), F32)
    hi = lax.bitcast_convert_type(w & HI_MASK, F32)
    return jnp.concatenate([lo, hi], axis=1).astype(BF16)


def _row_pieces(n, pieces, fn):
    for size in pieces:
        if size == pieces[0]:
            cond, pos = n >= size, 0
        else:
            cond, pos = (n & size) != 0, pl.multiple_of(n & (-2 * size), 8)

        @pl.when(cond)
        def _(pos=pos, size=size):
            fn(pos, size)


def _region_copies(off_ref, cnt_ref, tile, p, make_copy, start):
    for e in range(N_EXPERTS):
        base = off_ref[tile * N_EXPERTS + e] + p * REGION_CAP
        cp = make_copy(pl.ds(e * REGION_CAP, REGION_CAP), pl.ds(pl.multiple_of(base, 8), REGION_CAP))
        act = cp.start if start else cp.wait
        if cnt_ref is None:
            act()
        else:
            pl.when(cnt_ref[tile * N_EXPERTS + e] > p * REGION_CAP)(act)


def _slot_targets(idx, lrank, p):
    lr = lrank - p * REGION_CAP
    return jnp.where((lr >= 0) & (lr < REGION_CAP), idx * REGION_CAP + lr, -1)


def _small_int_bf16(v):
    return v.astype(F32).astype(BF16)


def _dispatch_kernel(off_ref, cnt_ref, np_ref, pad_ref, nv_ref, x_ref, irt_ref, xs_hbm,
                     stage_ref, zero_ref, sem, zsem, *, tm, n_tiles, bm, nb):
    i = pl.program_id(0)
    slot = i % 2
    row_id = _small_int_bf16(lax.broadcasted_iota(jnp.int32, (ONEHOT_CHUNK, tm), 0))
    meta = irt_ref[...].astype(jnp.int32)
    x = x_ref[...]
    mk = lambda s: (lambda src, dst: pltpu.make_async_copy(stage_ref.at[s, src], xs_hbm.at[dst], sem.at[s]))

    def build(p):
        tgt = [_slot_targets(meta[k:k + 1, :], meta[TOP_K + k:TOP_K + k + 1, :], p) for k in range(TOP_K)]
        for c in range(N_EXPERTS * REGION_CAP // ONEHOT_CHUNK):
            hit = row_id == _small_int_bf16(tgt[0] - c * ONEHOT_CHUNK)
            for k in range(1, TOP_K):
                hit = hit | (row_id == _small_int_bf16(tgt[k] - c * ONEHOT_CHUNK))
            sel = jnp.where(hit, jnp.ones((), BF16), jnp.zeros((), BF16))
            stage_ref[slot, c * ONEHOT_CHUNK:(c + 1) * ONEHOT_CHUNK, :] = _pack_bf16_pairs(_dot(sel, x))

    build(0)

    @pl.when((i > 0) & (np_ref[jnp.maximum(i - 1, 0)] == 1))
    def _():
        _region_copies(off_ref, None, i - 1, 0, mk(1 - slot), False)

    _region_copies(off_ref, None, i, 0, mk(slot), True)

    def extra_pass(p, carry):
        @pl.when(p == 1)
        def _():
            _region_copies(off_ref, None, i, 0, mk(slot), False)

        @pl.when(p > 1)
        def _():
            _region_copies(off_ref, cnt_ref, i, p - 1, mk(slot), False)

        build(p)
        _region_copies(off_ref, cnt_ref, i, p, mk(slot), True)
        return carry

    lax.fori_loop(1, np_ref[i], extra_pass, 0)

    @pl.when(np_ref[i] > 1)
    def _():
        _region_copies(off_ref, cnt_ref, i, np_ref[i] - 1, mk(slot), False)

    @pl.when(i == n_tiles - 1)
    def _():
        @pl.when(np_ref[i] == 1)
        def _():
            _region_copies(off_ref, None, i, 0, mk(slot), False)

        zero_ref[...] = jnp.zeros_like(zero_ref)
        for phase in (True, False):
            for e in range(N_EXPERTS):
                def piece(pos, size, e=e):
                    cp = pltpu.make_async_copy(zero_ref.at[pl.ds(0, size)],
                                               xs_hbm.at[pl.ds(pl.multiple_of(pad_ref[e] + pos, 8), size)], zsem)
                    cp.start() if phase else cp.wait()
                _row_pieces(pad_ref[N_EXPERTS + e], PAD_PIECES, piece)

        def zero_block(blk, carry):
            for half in range(bm // PAD_PIECES[0]):
                cp = pltpu.make_async_copy(
                    zero_ref, xs_hbm.at[pl.ds(pl.multiple_of(blk * bm + half * PAD_PIECES[0], 8), PAD_PIECES[0])], zsem)
                cp.start()
                cp.wait()
            return carry

        lax.fori_loop(nv_ref[0], nb, zero_block, 0)


def _dispatch(x1b, irt, off, cnt8, n_pass, pad, n_valid, nb, bm, tm):
    t = x1b.shape[0]
    assert bm % PAD_PIECES[0] == 0 and bm + SEGMENT_SLACK < 2 * PAD_PIECES[0]
    return pl.pallas_call(
        functools.partial(_dispatch_kernel, tm=tm, n_tiles=t // tm, bm=bm, nb=nb),
        grid_spec=pltpu.PrefetchScalarGridSpec(
            num_scalar_prefetch=5,
            grid=(t // tm,),
            in_specs=[pl.BlockSpec((tm, D_MODEL), lambda i, *_: (i, 0)),
                      pl.BlockSpec((8, tm), lambda i, *_: (0, i))],
            out_specs=pl.BlockSpec(memory_space=pl.ANY),
            scratch_shapes=[pltpu.VMEM((2, N_EXPERTS * REGION_CAP, HALF), jnp.int32),
                            pltpu.VMEM((PAD_PIECES[0], HALF), jnp.int32),
                            pltpu.SemaphoreType.DMA((2,)), pltpu.SemaphoreType.DMA(())]),
        out_shape=jax.ShapeDtypeStruct((nb * bm, HALF), jnp.int32),
        compiler_params=_params(("arbitrary",)),
        name="moe_dispatch",
    )(off, cnt8, n_pass, pad, n_valid, x1b, irt)


def _ffn_kernel(be_ref, nr_ref, nv_ref, sg_ref, nx_ref, xs_ref, wg_hbm, bg_ref, wu_hbm, bu_ref, wd_hbm, bd_ref, o_ref,
                wf_ref, wgb_ref, wub_ref, wdb_ref, sem, *, layer):
    i = pl.program_id(0)

    def fetch(e, s, start):
        for j, w_hbm in enumerate((wg_hbm, wu_hbm, wd_hbm)):
            cp = pltpu.make_async_copy(w_hbm.at[layer, e], wf_ref.at[s, j], sem.at[s])
            if start:
                cp.start()
            else:
                cp.wait()

    @pl.when(i < nv_ref[0])
    def _():
        seg = sg_ref[i]

        @pl.when((i == 0) | (seg != sg_ref[jnp.maximum(i - 1, 0)]))
        def _():
            s = seg % 2

            @pl.when(seg == 0)
            def _():
                fetch(be_ref[i], s, True)

            fetch(be_ref[i], s, False)
            wgb_ref[...] = wf_ref[s, 0].astype(BF16)
            wub_ref[...] = wf_ref[s, 1].astype(BF16)
            wdb_ref[...] = wf_ref[s, 2].astype(BF16)

            @pl.when(nx_ref[i] >= 0)
            def _():
                fetch(nx_ref[i], 1 - s, True)

        rows = lax.broadcasted_iota(jnp.int32, xs_ref.shape, 0)
        x = _unpack_bf16_pairs(jnp.where(rows < nr_ref[i], xs_ref[...], 0))
        gate = jnp.minimum(_dot(x, wgb_ref[...]) + bg_ref[...], SWIGLU_LIMIT)
        up = jnp.clip(_dot(x, wub_ref[...]) + bu_ref[...], -SWIGLU_LIMIT, SWIGLU_LIMIT)
        hid = (up + 1.0) * gate * _sigmoid(SWIGLU_ALPHA * gate)
        y = _dot(hid.astype(BF16), wdb_ref[...]) + bd_ref[...]
        o_ref[...] = _pack_bf16_pairs(y.astype(BF16).astype(F32))

    @pl.when(i >= nv_ref[0])
    def _():
        o_ref[...] = jnp.zeros_like(o_ref)


def _experts(xs, block_e, block_rows, n_valid, seg_idx, next_e, params, l, bm):
    n_slots = xs.shape[0]
    nb = n_slots // bm
    assert D_FF == D_MODEL
    blk = lambda i, be, nr, nv, *_: (jnp.minimum(i, nv[0] - 1), 0)
    oblk = lambda i, *_: (i, 0)
    wsp = pl.BlockSpec(memory_space=pl.ANY)
    bsp = pl.BlockSpec((None, None, 1, D_FF), lambda i, be, *_: (l, be[i], 0, 0))
    b4 = lambda a: a.reshape(DEPTH, N_EXPERTS, 1, -1)
    return pl.pallas_call(
        functools.partial(_ffn_kernel, layer=l),
        grid_spec=pltpu.PrefetchScalarGridSpec(
            num_scalar_prefetch=5,
            grid=(nb,),
            in_specs=[pl.BlockSpec((bm, HALF), blk), wsp, bsp, wsp, bsp, wsp, bsp],
            out_specs=pl.BlockSpec((bm, HALF), oblk),
            scratch_shapes=[pltpu.VMEM((2, 3, D_MODEL, D_FF), F32),
                            pltpu.VMEM((D_MODEL, D_FF), BF16), pltpu.VMEM((D_MODEL, D_FF), BF16),
                            pltpu.VMEM((D_FF, D_MODEL), BF16), pltpu.SemaphoreType.DMA((2,))]),
        out_shape=jax.ShapeDtypeStruct((n_slots, HALF), jnp.int32),
        compiler_params=_params(("arbitrary",)),
        name="moe_experts",
    )(block_e, block_rows, n_valid, seg_idx, next_e, xs, params['w_gate'], b4(params['b_gate']),
      params['w_up'], b4(params['b_up']), params['w_down'], b4(params['b_down']))


def _combine_kernel(off_ref, cnt_ref, np_ref, x1_ref, ir_ref, pr_ref, ys_hbm, g2_ref, b2_ref, o_ref,
                    stage_ref, ysb_ref, wgt_ref, acc_ref, sem, *, tm, n_tiles):
    i = pl.program_id(0)
    slot = i % 2
    mk = lambda s: (lambda dst, src: pltpu.make_async_copy(ys_hbm.at[src], stage_ref.at[s, dst], sem.at[s]))

    @pl.when(i == 0)
    def _():
        _region_copies(off_ref, None, 0, 0, mk(0), True)

    col_id = _small_int_bf16(lax.broadcasted_iota(jnp.int32, (tm, ONEHOT_CHUNK), 1))
    ir = ir_ref[...]
    pr = pr_ref[...].astype(BF16)
    zero = jnp.zeros((), BF16)

    def weighted_sum(p):
        tgt = [_slot_targets(ir[:, k:k + 1], ir[:, TOP_K + k:TOP_K + k + 1], p) for k in range(TOP_K)]
        for c in range(N_EXPERTS * REGION_CAP // ONEHOT_CHUNK):
            cols = slice(c * ONEHOT_CHUNK, (c + 1) * ONEHOT_CHUNK)
            wgt = zero
            for k in reversed(range(TOP_K)):
                wgt = jnp.where(col_id == _small_int_bf16(tgt[k] - c * ONEHOT_CHUNK), pr[:, k:k + 1], wgt)
            wgt_ref[:, cols] = wgt
            ysb_ref[cols, :] = _unpack_bf16_pairs(stage_ref[slot, cols, :])
        return _dot(wgt_ref[...], ysb_ref[...])

    _region_copies(off_ref, None, i, 0, mk(slot), False)

    @pl.when(i + 1 < n_tiles)
    def _():
        _region_copies(off_ref, None, i + 1, 0, mk(1 - slot), True)

    acc_ref[...] = weighted_sum(0)

    def extra_pass(p, carry):
        _region_copies(off_ref, cnt_ref, i, p, mk(slot), True)
        _region_copies(off_ref, cnt_ref, i, p, mk(slot), False)
        acc_ref[...] += weighted_sum(p)
        return carry

    lax.fori_loop(1, np_ref[i], extra_pass, 0)
    o_ref[...] = _layer_norm(DN_ALPHA * x1_ref[...] + acc_ref[...], g2_ref[...], b2_ref[...])


def _combine(x1, idx_rank, probs, ys, off, cnt8, n_pass, p, tm):
    t = x1.shape[0]
    row = lambda a: a.reshape(1, D_MODEL).astype(F32)
    full = lambda shp: pl.BlockSpec(shp, lambda i, *_: (0,) * len(shp))
    rowb = lambda w: pl.BlockSpec((tm, w), lambda i, *_: (i, 0))
    return pl.pallas_call(
        functools.partial(_combine_kernel, tm=tm, n_tiles=t // tm),
        grid_spec=pltpu.PrefetchScalarGridSpec(
            num_scalar_prefetch=3,
            grid=(t // tm,),
            in_specs=[rowb(D_MODEL), rowb(LANES), rowb(LANES), pl.BlockSpec(memory_space=pl.ANY),
                      full((1, D_MODEL)), full((1, D_MODEL))],
            out_specs=rowb(D_MODEL),
            scratch_shapes=[pltpu.VMEM((2, N_EXPERTS * REGION_CAP, HALF), jnp.int32),
                            pltpu.VMEM((N_EXPERTS * REGION_CAP, D_MODEL), BF16),
                            pltpu.VMEM((tm, N_EXPERTS * REGION_CAP), BF16),
                            pltpu.VMEM((tm, D_MODEL), F32),
                            pltpu.SemaphoreType.DMA((2,))]),
        out_shape=jax.ShapeDtypeStruct((t, D_MODEL), F32),
        compiler_params=_params(("arbitrary",)),
        name="moe_combine_ln2",
    )(off, cnt8, n_pass, x1, idx_rank, probs, ys, row(p['ln2_g']), row(p['ln2_b']))


def _moe(x1, x1b, idx_rank, irt, probs, tile_counts, p, params, l, bm, tm):
    t = x1.shape[0]
    n_tiles = t // tm
    cnt = tile_counts[:, 0, :N_EXPERTS]
    cnt8 = (cnt + 7) // 8 * 8
    seg = jnp.sum(cnt8, axis=0)
    padded = (seg + SEGMENT_SLACK + bm - 1) // bm * bm
    pend = jnp.cumsum(padded)
    pstart = pend - padded
    off = pstart[None, :] + jnp.cumsum(cnt8, axis=0) - cnt8
    n_pass = jnp.maximum((jnp.max(cnt, axis=1) + REGION_CAP - 1) // REGION_CAP, 1).astype(jnp.int32)
    nb = -(-(t * TOP_K + 7 * n_tiles * N_EXPERTS + SEGMENT_SLACK * N_EXPERTS) // bm) + N_EXPERTS
    n_valid = (pend[-1] // bm).astype(jnp.int32).reshape(1)
    blocks = jnp.arange(nb, dtype=jnp.int32)
    block_e = jnp.minimum(jnp.sum((pend // bm)[None, :] <= blocks[:, None], axis=1), N_EXPERTS - 1).astype(jnp.int32)
    block_rows = jnp.clip(seg[block_e] - (blocks * bm - pstart[block_e]), 0, bm).astype(jnp.int32)
    off = off.astype(jnp.int32).reshape(-1)
    cnt8 = cnt8.astype(jnp.int32).reshape(-1)
    pad = jnp.concatenate([pstart + seg, padded - seg]).astype(jnp.int32)
    first = jnp.concatenate([jnp.ones((1,), jnp.int32), (block_e[1:] != block_e[:-1]).astype(jnp.int32)])
    seg_idx = (jnp.cumsum(first) - 1).astype(jnp.int32)
    next_first = (pend // bm)[block_e]
    next_e = jnp.where(next_first < n_valid[0], block_e[jnp.minimum(next_first, nb - 1)], -1).astype(jnp.int32)
    xs = _dispatch(x1b, irt, off, cnt8, n_pass, pad, n_valid, nb, bm, tm)
    ys = _experts(xs, block_e, block_rows, n_valid, seg_idx, next_e, params, l, bm)
    return _combine(x1, idx_rank, probs, ys, off, cnt8, n_pass, p, tm)


_EXPERT_WEIGHTS = ('w_gate', 'b_gate', 'w_up', 'b_up', 'w_down', 'b_down')


def _layer(x, params, l, b, s, cfg):
    p = {k: v[l] for k, v in params.items() if k not in _EXPERT_WEIGHTS}
    xb, x_groups = _prep(x, cfg['prep_tm'])
    w_in = p['w_in'].astype(BF16)
    o = IN_OFFS
    pad = jnp.zeros((D_MODEL, LANES - 2 * ML_HEADS), w_in.dtype)
    w_ret = w_in[:, o[0]:o[4]]
    w_ml = jnp.concatenate([w_in[:, o[4]:o[5]], w_in[:, o[7]:o[8]]], axis=1)
    w_if = jnp.concatenate([w_in[:, o[5]:o[7]], pad], axis=1)
    w_gates = w_in[:, o[11]:o[12]]
    y_ret = _retention(xb, w_ret, p['ret_gn'], b, s, cfg['seq_rows']).reshape(b * s, MIX_W)
    y_ml = _mlstm(xb, w_ml, w_if, p, b, s, cfg['seq_rows']).reshape(b * s, MIX_W)
    outs, lses = [], []
    wq = ATT_HEADS * ATT_DK
    wv = ATT_HEADS * ATT_DV
    for g in range(ATT_GROUPS):
        assert ATT_DK == 64
        w_att = jnp.concatenate([w_in[:, o[8] + g * wq:o[8] + (g + 1) * wq] * (ATT_DK ** -0.5),
                                 w_in[:, o[9] + g * wq:o[9] + (g + 1) * wq],
                                 w_in[:, o[10] + g * wv:o[10] + (g + 1) * wv]], axis=1)
        d = ATT_PATTERNS[g][1]
        og, lg = _dilated_group(xb if d == 1 else x_groups[d], w_att, g, b, s, cfg['att_nq'])
        outs.append(og)
        lses.append(lg)
    x1, x1b, idx_rank, irt, probs, tile_counts = _merge(x, xb, y_ret, y_ml, outs, lses, w_gates, p, cfg['merge_tm'])
    return _moe(x1, x1b, idx_rank, irt, probs, tile_counts, p, params, l, cfg['moe_bm'], ROUTE_TM)


CFG = dict(prep_tm=1024, seq_rows=512, att_nq=4, merge_tm=512, moe_bm=512)

_PARAM_NAMES = ('w_in', 'ret_gn', 'ml_conv_w', 'ml_conv_b', 'ml_wq', 'ml_wk', 'ml_wv', 'ml_bi', 'ml_bf', 'ml_gn',
                'ml_skip', 'w_branch', 'w_out', 'ln1_g', 'ln1_b', 'w_router', 'b_router', 'w_gate', 'b_gate',
                'w_up', 'b_up', 'w_down', 'b_down', 'ln2_g', 'ln2_b')


def _forward(x, params, cfg):
    b, s, d = x.shape
    xf = x.reshape(b * s, d).astype(F32)
    for l in range(DEPTH):
        xf = _layer(xf, params, l, b, s, cfg)
    return xf.reshape(b, s, d).astype(x.dtype)


def kernel(x, w_in, ret_gn, ml_conv_w, ml_conv_b, ml_wq, ml_wk, ml_wv, ml_bi, ml_bf, ml_gn, ml_skip, w_branch, w_out, ln1_g, ln1_b, w_router, b_router, w_gate, b_gate, w_up, b_up, w_down, b_down, ln2_g, ln2_b):
    params = dict(zip(_PARAM_NAMES, (w_in, ret_gn, ml_conv_w, ml_conv_b, ml_wq, ml_wk, ml_wv, ml_bi, ml_bf, ml_gn,
                                     ml_skip, w_branch, w_out, ln1_g, ln1_b, w_router, b_router, w_gate, b_gate,
                                     w_up, b_up, w_down, b_down, ln2_g, ln2_b)))
    return _forward(x, params, CFG)
```

```python
import functools

import jax
import jax.numpy as jnp
import numpy as np
from jax import lax
from jax.experimental import pallas as pl
from jax.experimental.pallas import tpu as pltpu

F32 = jnp.float32
BF16 = jnp.bfloat16

D_MODEL = 1024
DEPTH = 2
MIX_W = D_MODEL // 2
N_BRANCHES = 3
RET_HEADS = 4
RET_DV = MIX_W // RET_HEADS
RET_DK = RET_DV // 2
ML_HEADS = 4
ML_DH = MIX_W // ML_HEADS
ML_CONV = 4
ML_QK_BLOCK = 4
ML_HEAD_GROUP = 1
ATT_PATTERNS = ((128, 1), (512, 4), (2048, 16))
ATT_GROUPS = len(ATT_PATTERNS)
ATT_HEADS = 4
ATT_DV = MIX_W // ATT_HEADS
ATT_DK = ATT_DV // 2
N_EXPERTS = 32
TOP_K = 4
D_FF = D_MODEL
SWIGLU_LIMIT = 7.0
SWIGLU_ALPHA = 1.702
DN_ALPHA = (2.0 * DEPTH) ** 0.25
EPS = 1e-5

CHUNK = 128
LANES = 128
NEG = -1e30
VMEM_LIMIT = 56 * 1024 * 1024

IN_SIZES = (RET_HEADS * RET_DK, RET_HEADS * RET_DK, MIX_W, MIX_W,
            MIX_W, ML_HEADS, ML_HEADS, MIX_W,
            ATT_GROUPS * ATT_HEADS * ATT_DK, ATT_GROUPS * ATT_HEADS * ATT_DK, ATT_GROUPS * ATT_HEADS * ATT_DV,
            N_BRANCHES * D_MODEL)
IN_OFFS = tuple(int(v) for v in np.cumsum((0,) + IN_SIZES))


def _params(sem):
    return pltpu.CompilerParams(dimension_semantics=sem, vmem_limit_bytes=VMEM_LIMIT)


def _dot(a, b):
    return jnp.dot(a, b, preferred_element_type=F32)


def _dot_nt(a, b):
    return lax.dot_general(a, b, (((1,), (1,)), ((), ())), preferred_element_type=F32)


def _dot_tn(a, b):
    return lax.dot_general(a, b, (((0,), (0,)), ((), ())), preferred_element_type=F32)


def _sigmoid(x):
    return 1.0 / (1.0 + jnp.exp(-x))


def _prep_kernel(x_ref, xb_ref, *rest, dilations):
    group_refs, cols_ref = rest[:-1], rest[-1]
    x = x_ref[...]
    xb_ref[...] = x.astype(xb_ref.dtype)
    tm, dm = x.shape
    for c in range(dm // LANES):
        cols_ref[c] = x[:, c * LANES:(c + 1) * LANES]
    for o_ref, d in zip(group_refs, dilations):
        n = tm // d
        for r in range(d):
            for c in range(dm // LANES):
                o_ref[:, r * dm + c * LANES:r * dm + (c + 1) * LANES] = (
                    cols_ref[c, pl.ds(r, n, stride=d), :].astype(o_ref.dtype))


def _prep(x, tm):
    t, dm = x.shape
    tm = min(tm, t)
    dilations = tuple(sorted({d for _, d in ATT_PATTERNS if d > 1}))
    assert t % tm == 0 and all(tm % (16 * d) == 0 for d in dilations)
    outs = pl.pallas_call(
        functools.partial(_prep_kernel, dilations=dilations),
        grid=(t // tm,),
        in_specs=[pl.BlockSpec((tm, dm), lambda i: (i, 0))],
        out_specs=[pl.BlockSpec((tm, dm), lambda i: (i, 0))]
                  + [pl.BlockSpec((tm // d, d * dm), lambda i: (i, 0)) for d in dilations],
        out_shape=[jax.ShapeDtypeStruct((t, dm), BF16)]
                  + [jax.ShapeDtypeStruct((t // d, d * dm), BF16) for d in dilations],
        scratch_shapes=[pltpu.VMEM((dm // LANES, tm, LANES), F32)],
        compiler_params=_params(("arbitrary",)),
        name="prep_bf16_groups",
    )(x)
    return outs[0], dict(zip(dilations, outs[1:]))


def _head_norm(o, gn):
    mu = jnp.mean(o, axis=-1, keepdims=True)
    oc = o - mu
    var = jnp.mean(oc * oc, axis=-1, keepdims=True)
    return oc * lax.rsqrt(var + EPS) * gn


def _ret_kernel(x_ref, w_ref, dm_ref, qd_ref, kd_ref, cd_ref, gn_ref, o_ref, st_ref, blk_ref, *, nc, nb):
    @pl.when(pl.program_id(0) == 0)
    def _():
        st_ref[...] = jnp.zeros_like(st_ref)

    for bi in range(nb):
        blk_ref[bi] = _dot(x_ref[bi], w_ref[...]).astype(blk_ref.dtype)

    hq = RET_HEADS * RET_DK
    state = [st_ref[j] for j in range(nb * RET_HEADS)]
    for c in range(nc):
        rows = pl.ds(c * CHUNK, CHUNK)
        chains = [(h, bi) for h in range(RET_HEADS) for bi in range(nb)]
        first = []
        for h, bi in chains:
            q = blk_ref[bi, rows, h * RET_DK:(h + 1) * RET_DK]
            k = blk_ref[bi, rows, hq + h * RET_DK:hq + (h + 1) * RET_DK]
            v = blk_ref[bi, rows, 2 * hq + h * RET_DV:2 * hq + (h + 1) * RET_DV]
            st = state[bi * RET_HEADS + h]
            first.append((v, _dot_nt(q, k), _dot(q, st.astype(BF16)),
                          _dot_tn(k, (v.astype(F32) * kd_ref[h]).astype(BF16))))
        for (h, bi), (v, s_qk, q_st, kv) in zip(chains, first):
            o = _dot((s_qk * dm_ref[h]).astype(BF16), v) + q_st * qd_ref[h]
            state[bi * RET_HEADS + h] = cd_ref[h] * state[bi * RET_HEADS + h] + kv
            g = blk_ref[bi, rows, 2 * hq + MIX_W + h * RET_DV:2 * hq + MIX_W + (h + 1) * RET_DV].astype(F32)
            y = _head_norm(o, gn_ref[:, h * RET_DV:(h + 1) * RET_DV]) * (g * _sigmoid(g))
            o_ref[bi, rows, h * RET_DV:(h + 1) * RET_DV] = y.astype(o_ref.dtype)
    for j in range(nb * RET_HEADS):
        st_ref[j] = state[j]


def _retention(xb, w_ret, ret_gn, b, s, rows):
    nc = rows // CHUNK
    scale = RET_DK ** -0.5
    log_gamma = jnp.log1p(-jnp.exp2(-5.0 - jnp.arange(RET_HEADS, dtype=F32)))
    pos = jnp.arange(CHUNK, dtype=F32)
    diff = pos[:, None] - pos[None, :]
    dm = jnp.where(diff >= 0, jnp.exp(log_gamma[:, None, None] * jnp.maximum(diff, 0.0)), 0.0) * scale
    kd = jnp.broadcast_to(jnp.exp(log_gamma[:, None] * (CHUNK - 1.0 - pos))[:, :, None], (RET_HEADS, CHUNK, RET_DV))
    qd = jnp.broadcast_to((jnp.exp(log_gamma[:, None] * (pos + 1.0)) * scale)[:, :, None], (RET_HEADS, CHUNK, RET_DV))
    cd = jnp.broadcast_to(jnp.exp(log_gamma * CHUNK)[:, None, None], (RET_HEADS, 1, RET_DV))
    dmod, w_in = w_ret.shape
    full = lambda shp: pl.BlockSpec(shp, lambda si: (0,) * len(shp))
    return pl.pallas_call(
        functools.partial(_ret_kernel, nc=nc, nb=b),
        grid=(s // rows,),
        in_specs=[pl.BlockSpec((b, rows, dmod), lambda si: (0, si, 0)), full((dmod, w_in)),
                  full((RET_HEADS, CHUNK, CHUNK)), full((RET_HEADS, CHUNK, RET_DV)),
                  full((RET_HEADS, CHUNK, RET_DV)), full((RET_HEADS, 1, RET_DV)), full((1, MIX_W))],
        out_specs=pl.BlockSpec((b, rows, MIX_W), lambda si: (0, si, 0)),
        out_shape=jax.ShapeDtypeStruct((b, s, MIX_W), BF16),
        scratch_shapes=[pltpu.VMEM((b * RET_HEADS, RET_DK, RET_DV), F32), pltpu.VMEM((b, rows, w_in), BF16)],
        compiler_params=_params(("arbitrary",)),
        name="retention",
    )(xb.reshape(b, s, dmod), w_ret, dm, qd, kd, cd, ret_gn.reshape(1, MIX_W).astype(F32))


def _log_sigmoid(x):
    return jnp.minimum(x, 0.0) - jnp.log(1.0 + jnp.exp(-jnp.abs(x)))


def _ml_kernel(x_ref, wml_ref, wif_ref, cw_ref, cb_ref, bq_ref, bk_ref, bv_ref, gb_ref, gn_ref, sk_ref,
               o_ref, c_ref, n_ref, m_ref, halo_ref, ml_ref, if_ref, qkv_ref, *, nc, nb):
    rows_total = nc * CHUNK

    @pl.when(pl.program_id(0) == 0)
    def _():
        c_ref[...] = jnp.zeros_like(c_ref)
        n_ref[...] = jnp.zeros_like(n_ref)
        m_ref[...] = jnp.zeros_like(m_ref)
        halo_ref[...] = jnp.zeros_like(halo_ref)

    for bi in range(nb):
        ml_ref[bi] = _dot(x_ref[bi], wml_ref[...]).astype(ml_ref.dtype)
        if_ref[bi] = _dot(x_ref[bi], wif_ref[...])

    x_all, mc_all = [], []
    for bi in range(nb):
        xa = ml_ref[bi, :, 0:MIX_W].astype(F32)
        xf = jnp.concatenate([halo_ref[bi], xa], axis=0)
        acc = jnp.broadcast_to(cb_ref[...], (rows_total, MIX_W))
        for j in range(ML_CONV):
            sh = ML_CONV - 1 - j
            xs = xf if sh == 0 else pltpu.roll(xf, sh, 0)
            acc = acc + xs[8:8 + rows_total] * cw_ref[j:j + 1, :]
        halo_ref[bi] = xa[rows_total - 8:rows_total]
        x_all.append(xa)
        mc_all.append(acc * _sigmoid(acc))

    ri = lax.broadcasted_iota(jnp.int32, (CHUNK, CHUNK), 0)
    ci = lax.broadcasted_iota(jnp.int32, (CHUNK, CHUNK), 1)
    tri = ri >= ci
    tril = tri.astype(F32)
    kscale = ML_DH ** -0.5
    log_kscale = float(np.log(kscale))

    c_st = [[c_ref[bi * ML_HEADS + h] for h in range(ML_HEADS)] for bi in range(nb)]
    n_st = [[n_ref[bi, h:h + 1, :] for h in range(ML_HEADS)] for bi in range(nb)]
    m_st = [[m_ref[bi, h:h + 1, 0:1] for h in range(ML_HEADS)] for bi in range(nb)]

    for bi in range(nb):
        mcb = mc_all[bi].astype(BF16)
        xvb = x_all[bi].astype(BF16)
        for h in range(ML_HEADS):
            cols = slice(h * ML_DH, (h + 1) * ML_DH)
            qkv_ref[bi, 0, :, cols] = _dot(mcb[:, cols], bq_ref[h]).astype(BF16)
            qkv_ref[bi, 1, :, cols] = _dot(mcb[:, cols], bk_ref[h]).astype(BF16)
            qkv_ref[bi, 2, :, cols] = _dot(xvb[:, cols], bv_ref[h]).astype(BF16)

    def run_chains(r0, gates, chains):
        for h, bi in chains:
            g_pre, cum, g_t, cum_t = gates[bi]
            cols = slice(h * ML_DH, (h + 1) * ML_DH)
            mc = mc_all[bi][r0:r0 + CHUNK, cols]
            q = qkv_ref[bi, 0, r0:r0 + CHUNK, cols]
            k = qkv_ref[bi, 1, r0:r0 + CHUNK, cols]
            v = qkv_ref[bi, 2, r0:r0 + CHUNK, cols]
            cum_col = cum[:, ML_HEADS + h:ML_HEADS + h + 1]
            ig_col = g_pre[:, h:h + 1]
            cum_row = cum_t[ML_HEADS + h:ML_HEADS + h + 1, :]
            ig_row = g_t[h:h + 1, :]
            tot = cum[CHUNK - 1:CHUNK, ML_HEADS + h:ML_HEADS + h + 1]
            m_prev, c_prev, n_prev = m_st[bi][h], c_st[bi][h], n_st[bi][h]
            dlog = jnp.where(tri, cum_col + (ig_row - cum_row), NEG)
            m_inter = cum_col + m_prev
            m_q = jnp.maximum(m_inter, jnp.max(dlog, axis=-1, keepdims=True))
            w_qk = jnp.exp(dlog - (m_q - log_kscale)) * _dot_nt(q, k)
            inter = jnp.exp(m_inter - m_q)
            num = _dot(w_qk.astype(BF16), v) + inter * _dot(q, c_prev.astype(BF16))
            den = (jnp.sum(w_qk, axis=-1, keepdims=True)
                   + inter * jnp.sum(q.astype(F32) * n_prev, axis=-1, keepdims=True))
            hh = num * (1.0 / jnp.maximum(jnp.abs(den), jnp.exp(-m_q)))
            a_col = tot - cum_col + ig_col
            a_max = jnp.max(a_col, axis=0, keepdims=True)
            wa = jnp.exp(a_col - a_max) * kscale
            chunk_c = _dot_tn(k, (wa * v.astype(F32)).astype(BF16))
            chunk_n = jnp.sum(wa * k.astype(F32), axis=0, keepdims=True)
            m_new = jnp.maximum(tot + m_prev, a_max)
            s_old = jnp.exp(tot + m_prev - m_new)
            s_new = jnp.exp(a_max - m_new)
            c_st[bi][h] = s_old * c_prev + s_new * chunk_c
            n_st[bi][h] = s_old * n_prev + s_new * chunk_n
            m_st[bi][h] = m_new
            mo = ml_ref[bi, r0:r0 + CHUNK, MIX_W + h * ML_DH:MIX_W + (h + 1) * ML_DH].astype(F32)
            y = _sigmoid(mo) * (_head_norm(hh, gn_ref[:, cols]) + sk_ref[:, cols] * mc)
            o_ref[bi, r0:r0 + CHUNK, cols] = y.astype(o_ref.dtype)

    for c in range(nc):
        r0 = c * CHUNK
        gates = []
        for bi in range(nb):
            g_pre = if_ref[bi, r0:r0 + CHUNK, :] + gb_ref[...]
            cum = jnp.dot(tril, _log_sigmoid(g_pre), preferred_element_type=F32, precision=lax.Precision.HIGHEST)
            gates.append((g_pre, cum, g_pre.T, cum.T))
        for h0 in range(0, ML_HEADS, ML_HEAD_GROUP):
            run_chains(r0, gates, [(h, bi) for h in range(h0, h0 + ML_HEAD_GROUP) for bi in range(nb)])

    for bi in range(nb):
        for h in range(ML_HEADS):
            c_ref[bi * ML_HEADS + h] = c_st[bi][h]
            n_ref[bi, h:h + 1, :] = n_st[bi][h]
            m_ref[bi, h:h + 1, :] = jnp.broadcast_to(m_st[bi][h], (1, LANES))


def _block_diag(w):
    nb = w.shape[0]
    per = nb // ML_HEADS
    wh = w.reshape(ML_HEADS, per, ML_QK_BLOCK, ML_QK_BLOCK)
    eye = jnp.eye(per, dtype=w.dtype)
    bd = jnp.einsum('hncd,nm->hncmd', wh, eye)
    return bd.reshape(ML_HEADS, per * ML_QK_BLOCK, per * ML_QK_BLOCK).astype(BF16)


def _mlstm(xb, w_ml, w_if, p, b, s, rows):
    nc = rows // CHUNK
    dm = xb.shape[1]
    full = lambda shp: pl.BlockSpec(shp, lambda si: (0,) * len(shp))
    gb = jnp.zeros((1, LANES), F32).at[0, 0:ML_HEADS].set(p['ml_bi'].astype(F32)).at[0, ML_HEADS:2 * ML_HEADS].set(p['ml_bf'].astype(F32))
    row = lambda a: a.reshape(1, MIX_W).astype(F32)
    return pl.pallas_call(
        functools.partial(_ml_kernel, nc=nc, nb=b),
        grid=(s // rows,),
        in_specs=[pl.BlockSpec((b, rows, dm), lambda si: (0, si, 0)),
                  full((dm, 2 * MIX_W)), full((dm, LANES)),
                  full((ML_CONV, MIX_W)), full((1, MIX_W)),
                  full((ML_HEADS, ML_DH, ML_DH)), full((ML_HEADS, ML_DH, ML_DH)), full((ML_HEADS, ML_DH, ML_DH)),
                  full((1, LANES)), full((1, MIX_W)), full((1, MIX_W))],
        out_specs=pl.BlockSpec((b, rows, MIX_W), lambda si: (0, si, 0)),
        out_shape=jax.ShapeDtypeStruct((b, s, MIX_W), BF16),
        scratch_shapes=[pltpu.VMEM((b * ML_HEADS, ML_DH, ML_DH), F32), pltpu.VMEM((b, 8, ML_DH), F32),
                        pltpu.VMEM((b, 8, LANES), F32), pltpu.VMEM((b, 8, MIX_W), F32),
                        pltpu.VMEM((b, rows, 2 * MIX_W), BF16), pltpu.VMEM((b, rows, LANES), F32),
                        pltpu.VMEM((b, 3, rows, MIX_W), BF16)],
        compiler_params=_params(("arbitrary",)),
        name="mlstm",
    )(xb.reshape(b, s, dm), w_ml, w_if,
      p['ml_conv_w'].astype(F32), row(p['ml_conv_b']),
      _block_diag(p['ml_wq']), _block_diag(p['ml_wk']), _block_diag(p['ml_wv']),
      gb, row(p['ml_gn']), row(p['ml_skip']))


def _att_kernel(x_ref, w_ref, bias_ref, o_ref, lse_ref, qkv_ref, *, nq):
    first = pl.program_id(2) == 0
    span = nq * CHUNK
    wq = ATT_HEADS * ATT_DK

    @pl.when(first)
    def _():
        qkv_ref[0:CHUNK, :] = jnp.zeros((CHUNK, qkv_ref.shape[1]), qkv_ref.dtype)

    qkv_ref[CHUNK:CHUNK + span, :] = _dot(x_ref[...], w_ref[...]).astype(qkv_ref.dtype)
    lane = lax.broadcasted_iota(jnp.int32, (CHUNK, LANES), 1)
    ones = jnp.ones((2 * CHUNK, ATT_DV), BF16)
    for j in range(nq):
        rows = slice(j * CHUNK, (j + 1) * CHUNK)
        kv_rows = slice(j * CHUNK, (j + 2) * CHUNK)
        lse_out = jnp.zeros((CHUNK, LANES), F32)
        for h in range(ATT_HEADS):
            dv = slice(h * ATT_DV, (h + 1) * ATT_DV)
            q = qkv_ref[CHUNK + j * CHUNK:CHUNK + (j + 1) * CHUNK, h * ATT_DK:(h + 1) * ATT_DK]
            kk = qkv_ref[kv_rows, wq + h * ATT_DK:wq + (h + 1) * ATT_DK]
            vv = qkv_ref[kv_rows, 2 * wq + h * ATT_DV:2 * wq + (h + 1) * ATT_DV]
            bias = bias_ref[jnp.where(first, 0, 1), h] if j == 0 else bias_ref[1, h]
            s = _dot_nt(q, kk) + bias
            m = jnp.max(s, axis=-1, keepdims=True)
            p = jnp.exp(s - m).astype(BF16)
            oa = _dot(p, jnp.concatenate([vv, ones], axis=1))
            l = oa[:, ATT_DV:2 * ATT_DV]
            o_ref[rows, dv] = (oa[:, 0:ATT_DV] * (1.0 / l)).astype(o_ref.dtype)
            lse_out = jnp.where(lane == h, m + jnp.log(l), lse_out)
        lse_ref[rows, :] = lse_out
    qkv_ref[0:CHUNK, :] = qkv_ref[span:span + CHUNK, :]


def _att_bias(window, dilation, slopes):
    wb = window // dilation
    qi = jnp.arange(wb)[:, None]
    kj = jnp.arange(2 * wb)[None, :]
    delta = qi + wb - kj
    band = (delta >= 0) & (delta <= wb)
    bias = -slopes[:, None, None] * (dilation * delta).astype(F32)
    later = jnp.where(band[None], bias, NEG)
    first = jnp.where((band & (kj >= wb))[None], bias, NEG)
    return jnp.stack([first, later]).astype(F32)


def _dilated_group(xg, w_qkv, g, b, s, nq):
    window, d = ATT_PATTERNS[g]
    assert window // d == CHUNK
    l_sub = s // d
    assert l_sub % CHUNK == 0
    n_blk = l_sub // CHUNK
    slopes = jnp.exp2(-8.0 * jnp.arange(1, ATT_GROUPS * ATT_HEADS + 1, dtype=F32) / (ATT_GROUPS * ATT_HEADS))
    bias = _att_bias(window, d, slopes.reshape(ATT_GROUPS, ATT_HEADS)[g])
    wq = ATT_HEADS * ATT_DK
    wv = ATT_HEADS * ATT_DV
    per_tok = 2 * wq + wv
    dm = w_qkv.shape[0]
    assert xg.shape == (b * l_sub, d * dm)
    x = xg.reshape(b, l_sub, d * dm)
    nq = min(nq, n_blk)
    assert n_blk % nq == 0
    span = nq * CHUNK
    o, lse = pl.pallas_call(
        functools.partial(_att_kernel, nq=nq),
        grid=(b, d, n_blk // nq),
        in_specs=[pl.BlockSpec((None, span, dm), lambda bi, r, n: (bi, n, r)),
                  pl.BlockSpec((dm, per_tok), lambda bi, r, n: (0, 0)),
                  pl.BlockSpec((2, ATT_HEADS, CHUNK, 2 * CHUNK), lambda bi, r, n: (0, 0, 0, 0))],
        out_specs=[pl.BlockSpec((None, span, wv), lambda bi, r, n: (bi, n, r)),
                   pl.BlockSpec((None, span, LANES), lambda bi, r, n: (bi, n, r))],
        out_shape=[jax.ShapeDtypeStruct((b, l_sub, d * wv), BF16),
                   jax.ShapeDtypeStruct((b, l_sub, d * LANES), F32)],
        scratch_shapes=[pltpu.VMEM((CHUNK + span, per_tok), BF16)],
        compiler_params=_params(("arbitrary", "arbitrary", "arbitrary")),
        name=f"dilated_attn_g{g}",
    )(x, w_qkv, bias)
    return o.reshape(b * l_sub, d * wv), lse.reshape(b * l_sub, d * LANES)


def _layer_norm(z, g, b):
    mu = jnp.mean(z, axis=-1, keepdims=True)
    zc = z - mu
    var = jnp.mean(zc * zc, axis=-1, keepdims=True)
    return zc * lax.rsqrt(var + EPS) * g + b


def _merge_kernel(x_ref, yr_ref, ym_ref, o0_ref, o1_ref, o2_ref, l0_ref, l1_ref, l2_ref, xb_ref,
                  wg_ref, wb_ref, wo_ref, g1_ref, b1_ref, wrh_ref, wrl_ref, br_ref,
                  x1_ref, x1b_ref, ir_ref, irt_ref, pr_ref, cnt_ref, og_ref, lg_ref, *, tm):
    for g, (src_o, src_l) in enumerate(((o0_ref, l0_ref), (o1_ref, l1_ref), (o2_ref, l2_ref))):
        d = ATT_PATTERNS[g][1]
        n = tm // d
        for r in range(d):
            rows = pl.ds(r, n, stride=d) if d > 1 else pl.ds(0, n)
            for h in range(ATT_HEADS):
                og_ref[g, h, rows, :] = src_o[:, r * MIX_W + h * ATT_DV:r * MIX_W + (h + 1) * ATT_DV].astype(F32)
            lg_ref[g, rows, :] = src_l[:, r * LANES:(r + 1) * LANES]

    l0, l1, l2 = lg_ref[0], lg_ref[1], lg_ref[2]
    lm = jnp.maximum(jnp.maximum(l0, l1), l2)
    e0, e1, e2 = jnp.exp(l0 - lm), jnp.exp(l1 - lm), jnp.exp(l2 - lm)
    inv = 1.0 / (e0 + e1 + e2)
    parts = []
    for h in range(ATT_HEADS):
        dv = slice(h * ATT_DV, (h + 1) * ATT_DV)
        hs = slice(h, h + 1)
        parts.append((e0[:, hs] * og_ref[0, h] + e1[:, hs] * og_ref[1, h] + e2[:, hs] * og_ref[2, h]) * inv[:, hs])
    y_att = jnp.concatenate(parts, axis=-1).astype(BF16)

    xb = xb_ref[...]
    gate = lambda j: _sigmoid(_dot(xb, wg_ref[:, j * D_MODEL:(j + 1) * D_MODEL]))
    merged = gate(0) * _dot(yr_ref[...], wb_ref[0])
    merged = merged + gate(1) * _dot(ym_ref[...], wb_ref[1])
    merged = merged + gate(2) * _dot(y_att, wb_ref[2])
    z = DN_ALPHA * x_ref[...] + _dot(merged.astype(BF16), wo_ref[...])
    x1 = _layer_norm(z, g1_ref[...], b1_ref[...])
    x1_ref[...] = x1
    x1b_ref[...] = x1.astype(BF16)

    xh = x1.astype(BF16)
    xl = (x1 - xh.astype(F32)).astype(BF16)
    logits = _dot(xh, wrh_ref[...]) + _dot(xl, wrh_ref[...]) + _dot(xh, wrl_ref[...]) + br_ref[...]
    lane = lax.broadcasted_iota(jnp.int32, (tm, LANES), 1)
    work = jnp.where(lane < N_EXPERTS, logits, -jnp.inf)
    vals, idxs = [], []
    for _ in range(TOP_K):
        m = jnp.max(work, axis=-1, keepdims=True)
        idx = jnp.min(jnp.where(work == m, lane, LANES), axis=-1, keepdims=True)
        vals.append(m)
        idxs.append(idx)
        work = jnp.where(lane == idx, -jnp.inf, work)
    es = [jnp.exp(v - vals[0]) for v in vals]
    tot = es[0] + es[1] + es[2] + es[3]
    pr = jnp.zeros((tm, LANES), F32)
    for k in range(TOP_K):
        pr = jnp.where(lane == k, es[k] / tot, pr)
    pr_ref[...] = pr

    onehot = jnp.zeros((tm, LANES), F32)
    for k in range(TOP_K):
        onehot = onehot + (lane == idxs[k]).astype(F32)
    ri = lax.broadcasted_iota(jnp.int32, (tm, tm), 0)
    ci = lax.broadcasted_iota(jnp.int32, (tm, tm), 1)
    same_tile = (ri // ROUTE_TM) == (ci // ROUTE_TM)
    before = _dot(((ri > ci) & same_tile).astype(BF16), onehot.astype(BF16))
    ir = jnp.zeros((tm, LANES), F32)
    for k in range(TOP_K):
        rank = jnp.sum(jnp.where(lane == idxs[k], before, 0.0), axis=-1, keepdims=True)
        ir = jnp.where(lane == k, idxs[k].astype(F32), ir)
        ir = jnp.where(lane == TOP_K + k, rank, ir)
    ir_ref[...] = ir.astype(jnp.int32)
    irt_ref[...] = ir.T[0:8, :]
    for j in range(tm // ROUTE_TM):
        cnt_j = jnp.sum(onehot[j * ROUTE_TM:(j + 1) * ROUTE_TM], axis=0, keepdims=True)
        cnt_ref[j] = jnp.broadcast_to(cnt_j, (8, LANES)).astype(jnp.int32)


def _merge(x, xb, y_ret, y_ml, outs, lses, w_gates, p, tm):
    t = x.shape[0]
    tm = min(tm, t)
    rowb = lambda w: pl.BlockSpec((tm, w), lambda i: (i, 0))
    full = lambda shp: pl.BlockSpec(shp, lambda i: (0,) * len(shp))
    wr = jnp.zeros((D_MODEL, LANES), F32).at[:, :N_EXPERTS].set(p['w_router'].astype(F32))
    wrh = wr.astype(BF16)
    wrl = (wr - wrh.astype(F32)).astype(BF16)
    br = jnp.zeros((1, LANES), F32).at[0, :N_EXPERTS].set(p['b_router'].astype(F32))
    row = lambda a: a.reshape(1, D_MODEL).astype(F32)
    return pl.pallas_call(
        functools.partial(_merge_kernel, tm=tm),
        grid=(t // tm,),
        in_specs=[rowb(D_MODEL), rowb(MIX_W), rowb(MIX_W)]
                 + [pl.BlockSpec((tm // d, d * MIX_W), lambda i: (i, 0)) for _, d in ATT_PATTERNS]
                 + [pl.BlockSpec((tm // d, d * LANES), lambda i: (i, 0)) for _, d in ATT_PATTERNS]
                 + [rowb(D_MODEL), full((D_MODEL, N_BRANCHES * D_MODEL)),
                  full((N_BRANCHES, MIX_W, D_MODEL)), full((D_MODEL, D_MODEL)),
                  full((1, D_MODEL)), full((1, D_MODEL)),
                  full((D_MODEL, LANES)), full((D_MODEL, LANES)), full((1, LANES))],
        out_specs=[rowb(D_MODEL), rowb(D_MODEL), rowb(LANES), pl.BlockSpec((8, tm), lambda i: (0, i)), rowb(LANES),
                   pl.BlockSpec((tm // ROUTE_TM, 8, LANES), lambda i: (i, 0, 0))],
        out_shape=[jax.ShapeDtypeStruct((t, D_MODEL), F32),
                   jax.ShapeDtypeStruct((t, D_MODEL), BF16),
                   jax.ShapeDtypeStruct((t, LANES), jnp.int32),
                   jax.ShapeDtypeStruct((8, t), F32),
                   jax.ShapeDtypeStruct((t, LANES), F32),
                   jax.ShapeDtypeStruct((t // ROUTE_TM, 8, LANES), jnp.int32)],
        scratch_shapes=[pltpu.VMEM((ATT_GROUPS, ATT_HEADS, tm, ATT_DV), F32),
                        pltpu.VMEM((ATT_GROUPS, tm, LANES), F32)],
        compiler_params=_params(("arbitrary",)),
        name="merge_ln1_router",
    )(x, y_ret, y_ml, outs[0], outs[1], outs[2], lses[0], lses[1], lses[2], xb, w_gates,
      p['w_branch'].astype(BF16), p['w_out'].astype(BF16), row(p['ln1_g']), row(p['ln1_b']), wrh, wrl, br)


ROUTE_TM = 256
REGION_CAP = 64
SEGMENT_SLACK = REGION_CAP
PAD_PIECES = (512, 256, 128, 64, 32, 16, 8)
ONEHOT_CHUNK = 256
HALF = D_MODEL // 2
HI_MASK = -65536


def _pack_bf16_pairs(y):
    lo = lax.shift_right_logical(lax.bitcast_convert_type(y[:, :HALF], jnp.int32), 16)
    hi = lax.bitcast_convert_type(y[:, HALF:], jnp.int32) & HI_MASK
    return lo | hi


def _unpack_bf16_pairs(w):
    lo = lax.bitcast_convert_type(lax.shift_left(w, 16), F32)
    hi = lax.bitcast_convert_type(w & HI_MASK, F32)
    return jnp.concatenate([lo, hi], axis=1).astype(BF16)


def _row_pieces(n, pieces, fn):
    for size in pieces:
        if size == pieces[0]:
            cond, pos = n >= size, 0
        else:
            cond, pos = (n & size) != 0, pl.multiple_of(n & (-2 * size), 8)

        @pl.when(cond)
        def _(pos=pos, size=size):
            fn(pos, size)


def _region_copies(off_ref, cnt_ref, tile, p, make_copy, start):
    for e in range(N_EXPERTS):
        base = off_ref[tile * N_EXPERTS + e] + p * REGION_CAP
        cp = make_copy(pl.ds(e * REGION_CAP, REGION_CAP), pl.ds(pl.multiple_of(base, 8), REGION_CAP))
        act = cp.start if start else cp.wait
        if cnt_ref is None:
            act()
        else:
            pl.when(cnt_ref[tile * N_EXPERTS + e] > p * REGION_CAP)(act)


def _slot_targets(idx, lrank, p):
    lr = lrank - p * REGION_CAP
    return jnp.where((lr >= 0) & (lr < REGION_CAP), idx * REGION_CAP + lr, -1)


def _small_int_bf16(v):
    return v.astype(F32).astype(BF16)


def _dispatch_kernel(off_ref, cnt_ref, np_ref, pad_ref, nv_ref, x_ref, irt_ref, xs_hbm,
                     stage_ref, zero_ref, sem, zsem, *, tm, n_tiles, bm, nb):
    i = pl.program_id(0)
    slot = i % 2
    row_id = _small_int_bf16(lax.broadcasted_iota(jnp.int32, (ONEHOT_CHUNK, tm), 0))
    meta = irt_ref[...].astype(jnp.int32)
    x = x_ref[...]
    mk = lambda s: (lambda src, dst: pltpu.make_async_copy(stage_ref.at[s, src], xs_hbm.at[dst], sem.at[s]))

    def build(p):
        tgt = [_slot_targets(meta[k:k + 1, :], meta[TOP_K + k:TOP_K + k + 1, :], p) for k in range(TOP_K)]
        for c in range(N_EXPERTS * REGION_CAP // ONEHOT_CHUNK):
            hit = row_id == _small_int_bf16(tgt[0] - c * ONEHOT_CHUNK)
            for k in range(1, TOP_K):
                hit = hit | (row_id == _small_int_bf16(tgt[k] - c * ONEHOT_CHUNK))
            sel = jnp.where(hit, jnp.ones((), BF16), jnp.zeros((), BF16))
            stage_ref[slot, c * ONEHOT_CHUNK:(c + 1) * ONEHOT_CHUNK, :] = _pack_bf16_pairs(_dot(sel, x))

    build(0)

    @pl.when((i > 0) & (np_ref[jnp.maximum(i - 1, 0)] == 1))
    def _():
        _region_copies(off_ref, None, i - 1, 0, mk(1 - slot), False)

    _region_copies(off_ref, None, i, 0, mk(slot), True)

    def extra_pass(p, carry):
        @pl.when(p == 1)
        def _():
            _region_copies(off_ref, None, i, 0, mk(slot), False)

        @pl.when(p > 1)
        def _():
            _region_copies(off_ref, cnt_ref, i, p - 1, mk(slot), False)

        build(p)
        _region_copies(off_ref, cnt_ref, i, p, mk(slot), True)
        return carry

    lax.fori_loop(1, np_ref[i], extra_pass, 0)

    @pl.when(np_ref[i] > 1)
    def _():
        _region_copies(off_ref, cnt_ref, i, np_ref[i] - 1, mk(slot), False)

    @pl.when(i == n_tiles - 1)
    def _():
        @pl.when(np_ref[i] == 1)
        def _():
            _region_copies(off_ref, None, i, 0, mk(slot), False)

        zero_ref[...] = jnp.zeros_like(zero_ref)
        for phase in (True, False):
            for e in range(N_EXPERTS):
                def piece(pos, size, e=e):
                    cp = pltpu.make_async_copy(zero_ref.at[pl.ds(0, size)],
                                               xs_hbm.at[pl.ds(pl.multiple_of(pad_ref[e] + pos, 8), size)], zsem)
                    cp.start() if phase else cp.wait()
                _row_pieces(pad_ref[N_EXPERTS + e], PAD_PIECES, piece)

        def zero_block(blk, carry):
            for half in range(bm // PAD_PIECES[0]):
                cp = pltpu.make_async_copy(
                    zero_ref, xs_hbm.at[pl.ds(pl.multiple_of(blk * bm + half * PAD_PIECES[0], 8), PAD_PIECES[0])], zsem)
                cp.start()
                cp.wait()
            return carry

        lax.fori_loop(nv_ref[0], nb, zero_block, 0)


def _dispatch(x1b, irt, off, cnt8, n_pass, pad, n_valid, nb, bm, tm):
    t = x1b.shape[0]
    assert bm % PAD_PIECES[0] == 0 and bm + SEGMENT_SLACK < 2 * PAD_PIECES[0]
    return pl.pallas_call(
        functools.partial(_dispatch_kernel, tm=tm, n_tiles=t // tm, bm=bm, nb=nb),
        grid_spec=pltpu.PrefetchScalarGridSpec(
            num_scalar_prefetch=5,
            grid=(t // tm,),
            in_specs=[pl.BlockSpec((tm, D_MODEL), lambda i, *_: (i, 0)),
                      pl.BlockSpec((8, tm), lambda i, *_: (0, i))],
            out_specs=pl.BlockSpec(memory_space=pl.ANY),
            scratch_shapes=[pltpu.VMEM((2, N_EXPERTS * REGION_CAP, HALF), jnp.int32),
                            pltpu.VMEM((PAD_PIECES[0], HALF), jnp.int32),
                            pltpu.SemaphoreType.DMA((2,)), pltpu.SemaphoreType.DMA(())]),
        out_shape=jax.ShapeDtypeStruct((nb * bm, HALF), jnp.int32),
        compiler_params=_params(("arbitrary",)),
        name="moe_dispatch",
    )(off, cnt8, n_pass, pad, n_valid, x1b, irt)


def _ffn_kernel(be_ref, nr_ref, nv_ref, sg_ref, nx_ref, xs_ref, wg_hbm, bg_ref, wu_hbm, bu_ref, wd_hbm, bd_ref, o_ref,
                wf_ref, wgb_ref, wub_ref, wdb_ref, sem, *, layer):
    i = pl.program_id(0)

    def fetch(e, s, start):
        for j, w_hbm in enumerate((wg_hbm, wu_hbm, wd_hbm)):
            cp = pltpu.make_async_copy(w_hbm.at[layer, e], wf_ref.at[s, j], sem.at[s])
            if start:
                cp.start()
            else:
                cp.wait()

    @pl.when(i < nv_ref[0])
    def _():
        seg = sg_ref[i]

        @pl.when((i == 0) | (seg != sg_ref[jnp.maximum(i - 1, 0)]))
        def _():
            s = seg % 2

            @pl.when(seg == 0)
            def _():
                fetch(be_ref[i], s, True)

            fetch(be_ref[i], s, False)
            wgb_ref[...] = wf_ref[s, 0].astype(BF16)
            wub_ref[...] = wf_ref[s, 1].astype(BF16)
            wdb_ref[...] = wf_ref[s, 2].astype(BF16)

            @pl.when(nx_ref[i] >= 0)
            def _():
                fetch(nx_ref[i], 1 - s, True)

        rows = lax.broadcasted_iota(jnp.int32, xs_ref.shape, 0)
        x = _unpack_bf16_pairs(jnp.where(rows < nr_ref[i], xs_ref[...], 0))
        gate = jnp.minimum(_dot(x, wgb_ref[...]) + bg_ref[...], SWIGLU_LIMIT)
        up = jnp.clip(_dot(x, wub_ref[...]) + bu_ref[...], -SWIGLU_LIMIT, SWIGLU_LIMIT)
        hid = (up + 1.0) * gate * _sigmoid(SWIGLU_ALPHA * gate)
        y = _dot(hid.astype(BF16), wdb_ref[...]) + bd_ref[...]
        o_ref[...] = _pack_bf16_pairs(y.astype(BF16).astype(F32))

    @pl.when(i >= nv_ref[0])
    def _():
        o_ref[...] = jnp.zeros_like(o_ref)


def _experts(xs, block_e, block_rows, n_valid, seg_idx, next_e, params, l, bm):
    n_slots = xs.shape[0]
    nb = n_slots // bm
    assert D_FF == D_MODEL
    blk = lambda i, be, nr, nv, *_: (jnp.minimum(i, nv[0] - 1), 0)
    oblk = lambda i, *_: (i, 0)
    wsp = pl.BlockSpec(memory_space=pl.ANY)
    bsp = pl.BlockSpec((None, None, 1, D_FF), lambda i, be, *_: (l, be[i], 0, 0))
    b4 = lambda a: a.reshape(DEPTH, N_EXPERTS, 1, -1)
    return pl.pallas_call(
        functools.partial(_ffn_kernel, layer=l),
        grid_spec=pltpu.PrefetchScalarGridSpec(
            num_scalar_prefetch=5,
            grid=(nb,),
            in_specs=[pl.BlockSpec((bm, HALF), blk), wsp, bsp, wsp, bsp, wsp, bsp],
            out_specs=pl.BlockSpec((bm, HALF), oblk),
            scratch_shapes=[pltpu.VMEM((2, 3, D_MODEL, D_FF), F32),
                            pltpu.VMEM((D_MODEL, D_FF), BF16), pltpu.VMEM((D_MODEL, D_FF), BF16),
                            pltpu.VMEM((D_FF, D_MODEL), BF16), pltpu.SemaphoreType.DMA((2,))]),
        out_shape=jax.ShapeDtypeStruct((n_slots, HALF), jnp.int32),
        compiler_params=_params(("arbitrary",)),
        name="moe_experts",
    )(block_e, block_rows, n_valid, seg_idx, next_e, xs, params['w_gate'], b4(params['b_gate']),
      params['w_up'], b4(params['b_up']), params['w_down'], b4(params['b_down']))


def _combine_kernel(off_ref, cnt_ref, np_ref, x1_ref, ir_ref, pr_ref, ys_hbm, g2_ref, b2_ref, o_ref,
                    stage_ref, ysb_ref, wgt_ref, acc_ref, sem, *, tm, n_tiles):
    i = pl.program_id(0)
    slot = i % 2
    mk = lambda s: (lambda dst, src: pltpu.make_async_copy(ys_hbm.at[src], stage_ref.at[s, dst], sem.at[s]))

    @pl.when(i == 0)
    def _():
        _region_copies(off_ref, None, 0, 0, mk(0), True)

    col_id = _small_int_bf16(lax.broadcasted_iota(jnp.int32, (tm, ONEHOT_CHUNK), 1))
    ir = ir_ref[...]
    pr = pr_ref[...].astype(BF16)
    zero = jnp.zeros((), BF16)

    def weighted_sum(p):
        tgt = [_slot_targets(ir[:, k:k + 1], ir[:, TOP_K + k:TOP_K + k + 1], p) for k in range(TOP_K)]
        for c in range(N_EXPERTS * REGION_CAP // ONEHOT_CHUNK):
            cols = slice(c * ONEHOT_CHUNK, (c + 1) * ONEHOT_CHUNK)
            wgt = zero
            for k in reversed(range(TOP_K)):
                wgt = jnp.where(col_id == _small_int_bf16(tgt[k] - c * ONEHOT_CHUNK), pr[:, k:k + 1], wgt)
            wgt_ref[:, cols] = wgt
            ysb_ref[cols, :] = _unpack_bf16_pairs(stage_ref[slot, cols, :])
        return _dot(wgt_ref[...], ysb_ref[...])

    _region_copies(off_ref, None, i, 0, mk(slot), False)

    @pl.when(i + 1 < n_tiles)
    def _():
        _region_copies(off_ref, None, i + 1, 0, mk(1 - slot), True)

    acc_ref[...] = weighted_sum(0)

    def extra_pass(p, carry):
        _region_copies(off_ref, cnt_ref, i, p, mk(slot), True)
        _region_copies(off_ref, cnt_ref, i, p, mk(slot), False)
        acc_ref[...] += weighted_sum(p)
        return carry

    lax.fori_loop(1, np_ref[i], extra_pass, 0)
    o_ref[...] = _layer_norm(DN_ALPHA * x1_ref[...] + acc_ref[...], g2_ref[...], b2_ref[...])


def _combine(x1, idx_rank, probs, ys, off, cnt8, n_pass, p, tm):
    t = x1.shape[0]
    row = lambda a: a.reshape(1, D_MODEL).astype(F32)
    full = lambda shp: pl.BlockSpec(shp, lambda i, *_: (0,) * len(shp))
    rowb = lambda w: pl.BlockSpec((tm, w), lambda i, *_: (i, 0))
    return pl.pallas_call(
        functools.partial(_combine_kernel, tm=tm, n_tiles=t // tm),
        grid_spec=pltpu.PrefetchScalarGridSpec(
            num_scalar_prefetch=3,
            grid=(t // tm,),
            in_specs=[rowb(D_MODEL), rowb(LANES), rowb(LANES), pl.BlockSpec(memory_space=pl.ANY),
                      full((1, D_MODEL)), full((1, D_MODEL))],
            out_specs=rowb(D_MODEL),
            scratch_shapes=[pltpu.VMEM((2, N_EXPERTS * REGION_CAP, HALF), jnp.int32),
                            pltpu.VMEM((N_EXPERTS * REGION_CAP, D_MODEL), BF16),
                            pltpu.VMEM((tm, N_EXPERTS * REGION_CAP), BF16),
                            pltpu.VMEM((tm, D_MODEL), F32),
                            pltpu.SemaphoreType.DMA((2,))]),
        out_shape=jax.ShapeDtypeStruct((t, D_MODEL), F32),
        compiler_params=_params(("arbitrary",)),
        name="moe_combine_ln2",
    )(off, cnt8, n_pass, x1, idx_rank, probs, ys, row(p['ln2_g']), row(p['ln2_b']))


def _moe(x1, x1b, idx_rank, irt, probs, tile_counts, p, params, l, bm, tm):
    t = x1.shape[0]
    n_tiles = t // tm
    cnt = tile_counts[:, 0, :N_EXPERTS]
    cnt8 = (cnt + 7) // 8 * 8
    seg = jnp.sum(cnt8, axis=0)
    padded = (seg + SEGMENT_SLACK + bm - 1) // bm * bm
    pend = jnp.cumsum(padded)
    pstart = pend - padded
    off = pstart[None, :] + jnp.cumsum(cnt8, axis=0) - cnt8
    n_pass = jnp.maximum((jnp.max(cnt, axis=1) + REGION_CAP - 1) // REGION_CAP, 1).astype(jnp.int32)
    nb = -(-(t * TOP_K + 7 * n_tiles * N_EXPERTS + SEGMENT_SLACK * N_EXPERTS) // bm) + N_EXPERTS
    n_valid = (pend[-1] // bm).astype(jnp.int32).reshape(1)
    blocks = jnp.arange(nb, dtype=jnp.int32)
    block_e = jnp.minimum(jnp.sum((pend // bm)[None, :] <= blocks[:, None], axis=1), N_EXPERTS - 1).astype(jnp.int32)
    block_rows = jnp.clip(seg[block_e] - (blocks * bm - pstart[block_e]), 0, bm).astype(jnp.int32)
    off = off.astype(jnp.int32).reshape(-1)
    cnt8 = cnt8.astype(jnp.int32).reshape(-1)
    pad = jnp.concatenate([pstart + seg, padded - seg]).astype(jnp.int32)
    first = jnp.concatenate([jnp.ones((1,), jnp.int32), (block_e[1:] != block_e[:-1]).astype(jnp.int32)])
    seg_idx = (jnp.cumsum(first) - 1).astype(jnp.int32)
    next_first = (pend // bm)[block_e]
    next_e = jnp.where(next_first < n_valid[0], block_e[jnp.minimum(next_first, nb - 1)], -1).astype(jnp.int32)
    xs = _dispatch(x1b, irt, off, cnt8, n_pass, pad, n_valid, nb, bm, tm)
    ys = _experts(xs, block_e, block_rows, n_valid, seg_idx, next_e, params, l, bm)
    return _combine(x1, idx_rank, probs, ys, off, cnt8, n_pass, p, tm)


_EXPERT_WEIGHTS = ('w_gate', 'b_gate', 'w_up', 'b_up', 'w_down', 'b_down')


def _layer(x, params, l, b, s, cfg):
    p = {k: v[l] for k, v in params.items() if k not in _EXPERT_WEIGHTS}
    xb, x_groups = _prep(x, cfg['prep_tm'])
    w_in = p['w_in'].astype(BF16)
    o = IN_OFFS
    pad = jnp.zeros((D_MODEL, LANES - 2 * ML_HEADS), w_in.dtype)
    w_ret = w_in[:, o[0]:o[4]]
    w_ml = jnp.concatenate([w_in[:, o[4]:o[5]], w_in[:, o[7]:o[8]]], axis=1)
    w_if = jnp.concatenate([w_in[:, o[5]:o[7]], pad], axis=1)
    w_gates = w_in[:, o[11]:o[12]]
    y_ret = _retention(xb, w_ret, p['ret_gn'], b, s, cfg['seq_rows']).reshape(b * s, MIX_W)
    y_ml = _mlstm(xb, w_ml, w_if, p, b, s, cfg['seq_rows']).reshape(b * s, MIX_W)
    outs, lses = [], []
    wq = ATT_HEADS * ATT_DK
    wv = ATT_HEADS * ATT_DV
    for g in range(ATT_GROUPS):
        assert ATT_DK == 64
        w_att = jnp.concatenate([w_in[:, o[8] + g * wq:o[8] + (g + 1) * wq] * (ATT_DK ** -0.5),
                                 w_in[:, o[9] + g * wq:o[9] + (g + 1) * wq],
                                 w_in[:, o[10] + g * wv:o[10] + (g + 1) * wv]], axis=1)
        d = ATT_PATTERNS[g][1]
        og, lg = _dilated_group(xb if d == 1 else x_groups[d], w_att, g, b, s, cfg['att_nq'])
        outs.append(og)
        lses.append(lg)
    x1, x1b, idx_rank, irt, probs, tile_counts = _merge(x, xb, y_ret, y_ml, outs, lses, w_gates, p, cfg['merge_tm'])
    return _moe(x1, x1b, idx_rank, irt, probs, tile_counts, p, params, l, cfg['moe_bm'], ROUTE_TM)


CFG = dict(prep_tm=1024, seq_rows=1024, att_nq=4, merge_tm=512, moe_bm=512)

_PARAM_NAMES = ('w_in', 'ret_gn', 'ml_conv_w', 'ml_conv_b', 'ml_wq', 'ml_wk', 'ml_wv', 'ml_bi', 'ml_bf', 'ml_gn',
                'ml_skip', 'w_branch', 'w_out', 'ln1_g', 'ln1_b', 'w_router', 'b_router', 'w_gate', 'b_gate',
                'w_up', 'b_up', 'w_down', 'b_down', 'ln2_g', 'ln2_b')


def _forward(x, params, cfg):
    b, s, d = x.shape
    xf = x.reshape(b * s, d).astype(F32)
    for l in range(DEPTH):
        xf = _layer(xf, params, l, b, s, cfg)
    return xf.reshape(b, s, d).astype(x.dtype)


def kernel(x, w_in, ret_gn, ml_conv_w, ml_conv_b, ml_wq, ml_wk, ml_wv, ml_bi, ml_bf, ml_gn, ml_skip, w_branch, w_out, ln1_g, ln1_b, w_router, b_router, w_gate, b_gate, w_up, b_up, w_down, b_down, ln2_g, ln2_b):
    params = dict(zip(_PARAM_NAMES, (w_in, ret_gn, ml_conv_w, ml_conv_b, ml_wq, ml_wk, ml_wv, ml_bi, ml_bf, ml_gn,
                                     ml_skip, w_branch, w_out, ln1_g, ln1_b, w_router, b_router, w_gate, b_gate,
                                     w_up, b_up, w_down, b_down, ln2_g, ln2_b)))
    return _forward(x, params, CFG)
```

```python
import functools

import jax
import jax.numpy as jnp
import numpy as np
from jax import lax
from jax.experimental import pallas as pl
from jax.experimental.pallas import tpu as pltpu

F32 = jnp.float32
BF16 = jnp.bfloat16

D_MODEL = 1024
DEPTH = 2
MIX_W = D_MODEL // 2
N_BRANCHES = 3
RET_HEADS = 4
RET_DV = MIX_W // RET_HEADS
RET_DK = RET_DV // 2
ML_HEADS = 4
ML_DH = MIX_W // ML_HEADS
ML_CONV = 4
ML_QK_BLOCK = 4
ML_HEAD_GROUP = 1
ATT_PATTERNS = ((128, 1), (512, 4), (2048, 16))
ATT_GROUPS = len(ATT_PATTERNS)
ATT_HEADS = 4
ATT_DV = MIX_W // ATT_HEADS
ATT_DK = ATT_DV // 2
N_EXPERTS = 32
TOP_K = 4
D_FF = D_MODEL
SWIGLU_LIMIT = 7.0
SWIGLU_ALPHA = 1.702
DN_ALPHA = (2.0 * DEPTH) ** 0.25
EPS = 1e-5

CHUNK = 128
LANES = 128
NEG = -1e30
VMEM_LIMIT = 56 * 1024 * 1024

IN_SIZES = (RET_HEADS * RET_DK, RET_HEADS * RET_DK, MIX_W, MIX_W,
            MIX_W, ML_HEADS, ML_HEADS, MIX_W,
            ATT_GROUPS * ATT_HEADS * ATT_DK, ATT_GROUPS * ATT_HEADS * ATT_DK, ATT_GROUPS * ATT_HEADS * ATT_DV,
            N_BRANCHES * D_MODEL)
IN_OFFS = tuple(int(v) for v in np.cumsum((0,) + IN_SIZES))


def _params(sem):
    return pltpu.CompilerParams(dimension_semantics=sem, vmem_limit_bytes=VMEM_LIMIT)


def _dot(a, b):
    return jnp.dot(a, b, preferred_element_type=F32)


def _dot_nt(a, b):
    return lax.dot_general(a, b, (((1,), (1,)), ((), ())), preferred_element_type=F32)


def _dot_tn(a, b):
    return lax.dot_general(a, b, (((0,), (0,)), ((), ())), preferred_element_type=F32)


def _sigmoid(x):
    return 1.0 / (1.0 + jnp.exp(-x))


def _prep_kernel(x_ref, xb_ref, *rest, dilations):
    group_refs, cols_ref = rest[:-1], rest[-1]
    x = x_ref[...]
    xb_ref[...] = x.astype(xb_ref.dtype)
    tm, dm = x.shape
    for c in range(dm // LANES):
        cols_ref[c] = x[:, c * LANES:(c + 1) * LANES]
    for o_ref, d in zip(group_refs, dilations):
        n = tm // d
        for r in range(d):
            for c in range(dm // LANES):
                o_ref[:, r * dm + c * LANES:r * dm + (c + 1) * LANES] = (
                    cols_ref[c, pl.ds(r, n, stride=d), :].astype(o_ref.dtype))


def _prep(x, tm):
    t, dm = x.shape
    tm = min(tm, t)
    dilations = tuple(sorted({d for _, d in ATT_PATTERNS if d > 1}))
    assert t % tm == 0 and all(tm % (16 * d) == 0 for d in dilations)
    outs = pl.pallas_call(
        functools.partial(_prep_kernel, dilations=dilations),
        grid=(t // tm,),
        in_specs=[pl.BlockSpec((tm, dm), lambda i: (i, 0))],
        out_specs=[pl.BlockSpec((tm, dm), lambda i: (i, 0))]
                  + [pl.BlockSpec((tm // d, d * dm), lambda i: (i, 0)) for d in dilations],
        out_shape=[jax.ShapeDtypeStruct((t, dm), BF16)]
                  + [jax.ShapeDtypeStruct((t // d, d * dm), BF16) for d in dilations],
        scratch_shapes=[pltpu.VMEM((dm // LANES, tm, LANES), F32)],
        compiler_params=_params(("arbitrary",)),
        name="prep_bf16_groups",
    )(x)
    return outs[0], dict(zip(dilations, outs[1:]))


def _head_norm(o, gn):
    mu = jnp.mean(o, axis=-1, keepdims=True)
    oc = o - mu
    var = jnp.mean(oc * oc, axis=-1, keepdims=True)
    return oc * lax.rsqrt(var + EPS) * gn


def _ret_kernel(x_ref, w_ref, dm_ref, qd_ref, kd_ref, cd_ref, gn_ref, o_ref, st_ref, blk_ref, *, nc, nb):
    @pl.when(pl.program_id(0) == 0)
    def _():
        st_ref[...] = jnp.zeros_like(st_ref)

    for bi in range(nb):
        blk_ref[bi] = _dot(x_ref[bi], w_ref[...]).astype(blk_ref.dtype)

    hq = RET_HEADS * RET_DK
    state = [st_ref[j] for j in range(nb * RET_HEADS)]
    for c in range(nc):
        rows = pl.ds(c * CHUNK, CHUNK)
        chains = [(h, bi) for h in range(RET_HEADS) for bi in range(nb)]
        first = []
        for h, bi in chains:
            q = blk_ref[bi, rows, h * RET_DK:(h + 1) * RET_DK]
            k = blk_ref[bi, rows, hq + h * RET_DK:hq + (h + 1) * RET_DK]
            v = blk_ref[bi, rows, 2 * hq + h * RET_DV:2 * hq + (h + 1) * RET_DV]
            st = state[bi * RET_HEADS + h]
            first.append((v, _dot_nt(q, k), _dot(q, st.astype(BF16)),
                          _dot_tn(k, (v.astype(F32) * kd_ref[h]).astype(BF16))))
        for (h, bi), (v, s_qk, q_st, kv) in zip(chains, first):
            o = _dot((s_qk * dm_ref[h]).astype(BF16), v) + q_st * qd_ref[h]
            state[bi * RET_HEADS + h] = cd_ref[h] * state[bi * RET_HEADS + h] + kv
            g = blk_ref[bi, rows, 2 * hq + MIX_W + h * RET_DV:2 * hq + MIX_W + (h + 1) * RET_DV].astype(F32)
            y = _head_norm(o, gn_ref[:, h * RET_DV:(h + 1) * RET_DV]) * (g * _sigmoid(g))
            o_ref[bi, rows, h * RET_DV:(h + 1) * RET_DV] = y.astype(o_ref.dtype)
    for j in range(nb * RET_HEADS):
        st_ref[j] = state[j]


def _retention(xb, w_ret, ret_gn, b, s, rows):
    nc = rows // CHUNK
    scale = RET_DK ** -0.5
    log_gamma = jnp.log1p(-jnp.exp2(-5.0 - jnp.arange(RET_HEADS, dtype=F32)))
    pos = jnp.arange(CHUNK, dtype=F32)
    diff = pos[:, None] - pos[None, :]
    dm = jnp.where(diff >= 0, jnp.exp(log_gamma[:, None, None] * jnp.maximum(diff, 0.0)), 0.0) * scale
    kd = jnp.broadcast_to(jnp.exp(log_gamma[:, None] * (CHUNK - 1.0 - pos))[:, :, None], (RET_HEADS, CHUNK, RET_DV))
    qd = jnp.broadcast_to((jnp.exp(log_gamma[:, None] * (pos + 1.0)) * scale)[:, :, None], (RET_HEADS, CHUNK, RET_DV))
    cd = jnp.broadcast_to(jnp.exp(log_gamma * CHUNK)[:, None, None], (RET_HEADS, 1, RET_DV))
    dmod, w_in = w_ret.shape
    full = lambda shp: pl.BlockSpec(shp, lambda si: (0,) * len(shp))
    return pl.pallas_call(
        functools.partial(_ret_kernel, nc=nc, nb=b),
        grid=(s // rows,),
        in_specs=[pl.BlockSpec((b, rows, dmod), lambda si: (0, si, 0)), full((dmod, w_in)),
                  full((RET_HEADS, CHUNK, CHUNK)), full((RET_HEADS, CHUNK, RET_DV)),
                  full((RET_HEADS, CHUNK, RET_DV)), full((RET_HEADS, 1, RET_DV)), full((1, MIX_W))],
        out_specs=pl.BlockSpec((b, rows, MIX_W), lambda si: (0, si, 0)),
        out_shape=jax.ShapeDtypeStruct((b, s, MIX_W), BF16),
        scratch_shapes=[pltpu.VMEM((b * RET_HEADS, RET_DK, RET_DV), F32), pltpu.VMEM((b, rows, w_in), BF16)],
        compiler_params=_params(("arbitrary",)),
        name="retention",
    )(xb.reshape(b, s, dmod), w_ret, dm, qd, kd, cd, ret_gn.reshape(1, MIX_W).astype(F32))


def _log_sigmoid(x):
    return jnp.minimum(x, 0.0) - jnp.log(1.0 + jnp.exp(-jnp.abs(x)))


def _ml_kernel(x_ref, wml_ref, wif_ref, cw_ref, cb_ref, bq_ref, bk_ref, bv_ref, gb_ref, gn_ref, sk_ref,
               o_ref, c_ref, n_ref, m_ref, halo_ref, ml_ref, if_ref, qkv_ref, *, nc, nb):
    rows_total = nc * CHUNK

    @pl.when(pl.program_id(0) == 0)
    def _():
        c_ref[...] = jnp.zeros_like(c_ref)
        n_ref[...] = jnp.zeros_like(n_ref)
        m_ref[...] = jnp.zeros_like(m_ref)
        halo_ref[...] = jnp.zeros_like(halo_ref)

    for bi in range(nb):
        ml_ref[bi] = _dot(x_ref[bi], wml_ref[...]).astype(ml_ref.dtype)
        if_ref[bi] = _dot(x_ref[bi], wif_ref[...])

    x_all, mc_all = [], []
    for bi in range(nb):
        xa = ml_ref[bi, :, 0:MIX_W].astype(F32)
        xf = jnp.concatenate([halo_ref[bi], xa], axis=0)
        acc = jnp.broadcast_to(cb_ref[...], (rows_total, MIX_W))
        for j in range(ML_CONV):
            sh = ML_CONV - 1 - j
            xs = xf if sh == 0 else pltpu.roll(xf, sh, 0)
            acc = acc + xs[8:8 + rows_total] * cw_ref[j:j + 1, :]
        halo_ref[bi] = xa[rows_total - 8:rows_total]
        x_all.append(xa)
        mc_all.append(acc * _sigmoid(acc))

    ri = lax.broadcasted_iota(jnp.int32, (CHUNK, CHUNK), 0)
    ci = lax.broadcasted_iota(jnp.int32, (CHUNK, CHUNK), 1)
    tri = ri >= ci
    tril = tri.astype(F32)
    kscale = ML_DH ** -0.5
    log_kscale = float(np.log(kscale))

    c_st = [[c_ref[bi * ML_HEADS + h] for h in range(ML_HEADS)] for bi in range(nb)]
    n_st = [[n_ref[bi, h:h + 1, :] for h in range(ML_HEADS)] for bi in range(nb)]
    m_st = [[m_ref[bi, h:h + 1, 0:1] for h in range(ML_HEADS)] for bi in range(nb)]

    for bi in range(nb):
        mcb = mc_all[bi].astype(BF16)
        xvb = x_all[bi].astype(BF16)
        for h in range(ML_HEADS):
            cols = slice(h * ML_DH, (h + 1) * ML_DH)
            qkv_ref[bi, 0, :, cols] = _dot(mcb[:, cols], bq_ref[h]).astype(BF16)
            qkv_ref[bi, 1, :, cols] = _dot(mcb[:, cols], bk_ref[h]).astype(BF16)
            qkv_ref[bi, 2, :, cols] = _dot(xvb[:, cols], bv_ref[h]).astype(BF16)

    def run_chains(r0, gates, chains):
        for h, bi in chains:
            g_pre, cum, g_t, cum_t = gates[bi]
            cols = slice(h * ML_DH, (h + 1) * ML_DH)
            mc = mc_all[bi][r0:r0 + CHUNK, cols]
            q = qkv_ref[bi, 0, r0:r0 + CHUNK, cols]
            k = qkv_ref[bi, 1, r0:r0 + CHUNK, cols]
            v = qkv_ref[bi, 2, r0:r0 + CHUNK, cols]
            cum_col = cum[:, ML_HEADS + h:ML_HEADS + h + 1]
            ig_col = g_pre[:, h:h + 1]
            cum_row = cum_t[ML_HEADS + h:ML_HEADS + h + 1, :]
            ig_row = g_t[h:h + 1, :]
            tot = cum[CHUNK - 1:CHUNK, ML_HEADS + h:ML_HEADS + h + 1]
            m_prev, c_prev, n_prev = m_st[bi][h], c_st[bi][h], n_st[bi][h]
            dlog = jnp.where(tri, cum_col + (ig_row - cum_row), NEG)
            m_inter = cum_col + m_prev
            m_q = jnp.maximum(m_inter, jnp.max(dlog, axis=-1, keepdims=True))
            w_qk = jnp.exp(dlog - (m_q - log_kscale)) * _dot_nt(q, k)
            inter = jnp.exp(m_inter - m_q)
            num = _dot(w_qk.astype(BF16), v) + inter * _dot(q, c_prev.astype(BF16))
            den = (jnp.sum(w_qk, axis=-1, keepdims=True)
                   + inter * jnp.sum(q.astype(F32) * n_prev, axis=-1, keepdims=True))
            hh = num * (1.0 / jnp.maximum(jnp.abs(den), jnp.exp(-m_q)))
            a_col = tot - cum_col + ig_col
            a_max = jnp.max(a_col, axis=0, keepdims=True)
            wa = jnp.exp(a_col - a_max) * kscale
            chunk_c = _dot_tn(k, (wa * v.astype(F32)).astype(BF16))
            chunk_n = jnp.sum(wa * k.astype(F32), axis=0, keepdims=True)
            m_new = jnp.maximum(tot + m_prev, a_max)
            s_old = jnp.exp(tot + m_prev - m_new)
            s_new = jnp.exp(a_max - m_new)
            c_st[bi][h] = s_old * c_prev + s_new * chunk_c
            n_st[bi][h] = s_old * n_prev + s_new * chunk_n
            m_st[bi][h] = m_new
            mo = ml_ref[bi, r0:r0 + CHUNK, MIX_W + h * ML_DH:MIX_W + (h + 1) * ML_DH].astype(F32)
            y = _sigmoid(mo) * (_head_norm(hh, gn_ref[:, cols]) + sk_ref[:, cols] * mc)
            o_ref[bi, r0:r0 + CHUNK, cols] = y.astype(o_ref.dtype)

    for c in range(nc):
        r0 = c * CHUNK
        gates = []
        for bi in range(nb):
            g_pre = if_ref[bi, r0:r0 + CHUNK, :] + gb_ref[...]
            cum = jnp.dot(tril, _log_sigmoid(g_pre), preferred_element_type=F32, precision=lax.Precision.HIGHEST)
            gates.append((g_pre, cum, g_pre.T, cum.T))
        for h0 in range(0, ML_HEADS, ML_HEAD_GROUP):
            run_chains(r0, gates, [(h, bi) for h in range(h0, h0 + ML_HEAD_GROUP) for bi in range(nb)])

    for bi in range(nb):
        for h in range(ML_HEADS):
            c_ref[bi * ML_HEADS + h] = c_st[bi][h]
            n_ref[bi, h:h + 1, :] = n_st[bi][h]
            m_ref[bi, h:h + 1, :] = jnp.broadcast_to(m_st[bi][h], (1, LANES))


def _block_diag(w):
    nb = w.shape[0]
    per = nb // ML_HEADS
    wh = w.reshape(ML_HEADS, per, ML_QK_BLOCK, ML_QK_BLOCK)
    eye = jnp.eye(per, dtype=w.dtype)
    bd = jnp.einsum('hncd,nm->hncmd', wh, eye)
    return bd.reshape(ML_HEADS, per * ML_QK_BLOCK, per * ML_QK_BLOCK).astype(BF16)


def _mlstm(xb, w_ml, w_if, p, b, s, rows):
    nc = rows // CHUNK
    dm = xb.shape[1]
    full = lambda shp: pl.BlockSpec(shp, lambda si: (0,) * len(shp))
    gb = jnp.zeros((1, LANES), F32).at[0, 0:ML_HEADS].set(p['ml_bi'].astype(F32)).at[0, ML_HEADS:2 * ML_HEADS].set(p['ml_bf'].astype(F32))
    row = lambda a: a.reshape(1, MIX_W).astype(F32)
    return pl.pallas_call(
        functools.partial(_ml_kernel, nc=nc, nb=b),
        grid=(s // rows,),
        in_specs=[pl.BlockSpec((b, rows, dm), lambda si: (0, si, 0)),
                  full((dm, 2 * MIX_W)), full((dm, LANES)),
                  full((ML_CONV, MIX_W)), full((1, MIX_W)),
                  full((ML_HEADS, ML_DH, ML_DH)), full((ML_HEADS, ML_DH, ML_DH)), full((ML_HEADS, ML_DH, ML_DH)),
                  full((1, LANES)), full((1, MIX_W)), full((1, MIX_W))],
        out_specs=pl.BlockSpec((b, rows, MIX_W), lambda si: (0, si, 0)),
        out_shape=jax.ShapeDtypeStruct((b, s, MIX_W), BF16),
        scratch_shapes=[pltpu.VMEM((b * ML_HEADS, ML_DH, ML_DH), F32), pltpu.VMEM((b, 8, ML_DH), F32),
                        pltpu.VMEM((b, 8, LANES), F32), pltpu.VMEM((b, 8, MIX_W), F32),
                        pltpu.VMEM((b, rows, 2 * MIX_W), BF16), pltpu.VMEM((b, rows, LANES), F32),
                        pltpu.VMEM((b, 3, rows, MIX_W), BF16)],
        compiler_params=_params(("arbitrary",)),
        name="mlstm",
    )(xb.reshape(b, s, dm), w_ml, w_if,
      p['ml_conv_w'].astype(F32), row(p['ml_conv_b']),
      _block_diag(p['ml_wq']), _block_diag(p['ml_wk']), _block_diag(p['ml_wv']),
      gb, row(p['ml_gn']), row(p['ml_skip']))


def _att_kernel(x_ref, w_ref, bias_ref, o_ref, lse_ref, qkv_ref, *, nq):
    first = pl.program_id(2) == 0
    span = nq * CHUNK
    wq = ATT_HEADS * ATT_DK

    @pl.when(first)
    def _():
        qkv_ref[0:CHUNK, :] = jnp.zeros((CHUNK, qkv_ref.shape[1]), qkv_ref.dtype)

    qkv_ref[CHUNK:CHUNK + span, :] = _dot(x_ref[...], w_ref[...]).astype(qkv_ref.dtype)
    lane = lax.broadcasted_iota(jnp.int32, (CHUNK, LANES), 1)
    ones = jnp.ones((2 * CHUNK, ATT_DV), BF16)
    for j in range(nq):
        rows = slice(j * CHUNK, (j + 1) * CHUNK)
        kv_rows = slice(j * CHUNK, (j + 2) * CHUNK)
        lse_out = jnp.zeros((CHUNK, LANES), F32)
        for h in range(ATT_HEADS):
            dv = slice(h * ATT_DV, (h + 1) * ATT_DV)
            q = qkv_ref[CHUNK + j * CHUNK:CHUNK + (j + 1) * CHUNK, h * ATT_DK:(h + 1) * ATT_DK]
            kk = qkv_ref[kv_rows, wq + h * ATT_DK:wq + (h + 1) * ATT_DK]
            vv = qkv_ref[kv_rows, 2 * wq + h * ATT_DV:2 * wq + (h + 1) * ATT_DV]
            bias = bias_ref[jnp.where(first, 0, 1), h] if j == 0 else bias_ref[1, h]
            s = _dot_nt(q, kk) + bias
            m = jnp.max(s, axis=-1, keepdims=True)
            p = jnp.exp(s - m).astype(BF16)
            oa = _dot(p, jnp.concatenate([vv, ones], axis=1))
            l = oa[:, ATT_DV:2 * ATT_DV]
            o_ref[rows, dv] = (oa[:, 0:ATT_DV] * (1.0 / l)).astype(o_ref.dtype)
            lse_out = jnp.where(lane == h, m + jnp.log(l), lse_out)
        lse_ref[rows, :] = lse_out
    qkv_ref[0:CHUNK, :] = qkv_ref[span:span + CHUNK, :]


def _att_bias(window, dilation, slopes):
    wb = window // dilation
    qi = jnp.arange(wb)[:, None]
    kj = jnp.arange(2 * wb)[None, :]
    delta = qi + wb - kj
    band = (delta >= 0) & (delta <= wb)
    bias = -slopes[:, None, None] * (dilation * delta).astype(F32)
    later = jnp.where(band[None], bias, NEG)
    first = jnp.where((band & (kj >= wb))[None], bias, NEG)
    return jnp.stack([first, later]).astype(F32)


def _dilated_group(xg, w_qkv, g, b, s, nq):
    window, d = ATT_PATTERNS[g]
    assert window // d == CHUNK
    l_sub = s // d
    assert l_sub % CHUNK == 0
    n_blk = l_sub // CHUNK
    slopes = jnp.exp2(-8.0 * jnp.arange(1, ATT_GROUPS * ATT_HEADS + 1, dtype=F32) / (ATT_GROUPS * ATT_HEADS))
    bias = _att_bias(window, d, slopes.reshape(ATT_GROUPS, ATT_HEADS)[g])
    wq = ATT_HEADS * ATT_DK
    wv = ATT_HEADS * ATT_DV
    per_tok = 2 * wq + wv
    dm = w_qkv.shape[0]
    assert xg.shape == (b * l_sub, d * dm)
    x = xg.reshape(b, l_sub, d * dm)
    nq = min(nq, n_blk)
    assert n_blk % nq == 0
    span = nq * CHUNK
    o, lse = pl.pallas_call(
        functools.partial(_att_kernel, nq=nq),
        grid=(b, d, n_blk // nq),
        in_specs=[pl.BlockSpec((None, span, dm), lambda bi, r, n: (bi, n, r)),
                  pl.BlockSpec((dm, per_tok), lambda bi, r, n: (0, 0)),
                  pl.BlockSpec((2, ATT_HEADS, CHUNK, 2 * CHUNK), lambda bi, r, n: (0, 0, 0, 0))],
        out_specs=[pl.BlockSpec((None, span, wv), lambda bi, r, n: (bi, n, r)),
                   pl.BlockSpec((None, span, LANES), lambda bi, r, n: (bi, n, r))],
        out_shape=[jax.ShapeDtypeStruct((b, l_sub, d * wv), BF16),
                   jax.ShapeDtypeStruct((b, l_sub, d * LANES), F32)],
        scratch_shapes=[pltpu.VMEM((CHUNK + span, per_tok), BF16)],
        compiler_params=_params(("arbitrary", "arbitrary", "arbitrary")),
        name=f"dilated_attn_g{g}",
    )(x, w_qkv, bias)
    return o.reshape(b * l_sub, d * wv), lse.reshape(b * l_sub, d * LANES)


def _layer_norm(z, g, b):
    mu = jnp.mean(z, axis=-1, keepdims=True)
    zc = z - mu
    var = jnp.mean(zc * zc, axis=-1, keepdims=True)
    return zc * lax.rsqrt(var + EPS) * g + b


def _merge_kernel(x_ref, yr_ref, ym_ref, o0_ref, o1_ref, o2_ref, l0_ref, l1_ref, l2_ref, xb_ref,
                  wg_ref, wb_ref, wo_ref, g1_ref, b1_ref, wrh_ref, wrl_ref, br_ref,
                  x1_ref, x1b_ref, ir_ref, irt_ref, pr_ref, cnt_ref, og_ref, lg_ref, *, tm):
    for g, (src_o, src_l) in enumerate(((o0_ref, l0_ref), (o1_ref, l1_ref), (o2_ref, l2_ref))):
        d = ATT_PATTERNS[g][1]
        n = tm // d
        for r in range(d):
            rows = pl.ds(r, n, stride=d) if d > 1 else pl.ds(0, n)
            for h in range(ATT_HEADS):
                og_ref[g, h, rows, :] = src_o[:, r * MIX_W + h * ATT_DV:r * MIX_W + (h + 1) * ATT_DV].astype(F32)
            lg_ref[g, rows, :] = src_l[:, r * LANES:(r + 1) * LANES]

    for j in range(tm // ROUTE_TM):
        _merge_rows(slice(j * ROUTE_TM, (j + 1) * ROUTE_TM), j, x_ref, yr_ref, ym_ref, xb_ref, wg_ref, wb_ref, wo_ref,
                    g1_ref, b1_ref, wrh_ref, wrl_ref, br_ref, x1_ref, x1b_ref, ir_ref, irt_ref, pr_ref, cnt_ref,
                    og_ref, lg_ref)


def _merge_rows(rs, j, x_ref, yr_ref, ym_ref, xb_ref, wg_ref, wb_ref, wo_ref, g1_ref, b1_ref, wrh_ref, wrl_ref, br_ref,
                x1_ref, x1b_ref, ir_ref, irt_ref, pr_ref, cnt_ref, og_ref, lg_ref):
    tm = ROUTE_TM
    l0, l1, l2 = lg_ref[0, rs, :], lg_ref[1, rs, :], lg_ref[2, rs, :]
    lm = jnp.maximum(jnp.maximum(l0, l1), l2)
    e0, e1, e2 = jnp.exp(l0 - lm), jnp.exp(l1 - lm), jnp.exp(l2 - lm)
    inv = 1.0 / (e0 + e1 + e2)
    parts = []
    for h in range(ATT_HEADS):
        hs = slice(h, h + 1)
        parts.append((e0[:, hs] * og_ref[0, h, rs, :] + e1[:, hs] * og_ref[1, h, rs, :]
                      + e2[:, hs] * og_ref[2, h, rs, :]) * inv[:, hs])
    y_att = jnp.concatenate(parts, axis=-1).astype(BF16)

    xb = xb_ref[rs, :]
    gate = lambda b: _sigmoid(_dot(xb, wg_ref[:, b * D_MODEL:(b + 1) * D_MODEL]))
    merged = gate(0) * _dot(yr_ref[rs, :], wb_ref[0])
    merged = merged + gate(1) * _dot(ym_ref[rs, :], wb_ref[1])
    merged = merged + gate(2) * _dot(y_att, wb_ref[2])
    z = DN_ALPHA * x_ref[rs, :] + _dot(merged.astype(BF16), wo_ref[...])
    x1 = _layer_norm(z, g1_ref[...], b1_ref[...])
    x1_ref[rs, :] = x1
    x1b_ref[rs, :] = x1.astype(BF16)

    xh = x1.astype(BF16)
    xl = (x1 - xh.astype(F32)).astype(BF16)
    logits = _dot(xh, wrh_ref[...]) + _dot(xl, wrh_ref[...]) + _dot(xh, wrl_ref[...]) + br_ref[...]
    lane = lax.broadcasted_iota(jnp.int32, (tm, LANES), 1)
    work = jnp.where(lane < N_EXPERTS, logits, -jnp.inf)
    vals, idxs = [], []
    for _ in range(TOP_K):
        m = jnp.max(work, axis=-1, keepdims=True)
        idx = jnp.min(jnp.where(work == m, lane, LANES), axis=-1, keepdims=True)
        vals.append(m)
        idxs.append(idx)
        work = jnp.where(lane == idx, -jnp.inf, work)
    es = [jnp.exp(v - vals[0]) for v in vals]
    tot = es[0] + es[1] + es[2] + es[3]
    pr = jnp.zeros((tm, LANES), F32)
    for k in range(TOP_K):
        pr = jnp.where(lane == k, es[k] / tot, pr)
    pr_ref[rs, :] = pr

    onehot = jnp.zeros((tm, LANES), F32)
    for k in range(TOP_K):
        onehot = onehot + (lane == idxs[k]).astype(F32)
    ri = lax.broadcasted_iota(jnp.int32, (tm, tm), 0)
    ci = lax.broadcasted_iota(jnp.int32, (tm, tm), 1)
    before = _dot((ri > ci).astype(BF16), onehot.astype(BF16))
    ir = jnp.zeros((tm, LANES), F32)
    for k in range(TOP_K):
        rank = jnp.sum(jnp.where(lane == idxs[k], before, 0.0), axis=-1, keepdims=True)
        ir = jnp.where(lane == k, idxs[k].astype(F32), ir)
        ir = jnp.where(lane == TOP_K + k, rank, ir)
    ir_ref[rs, :] = ir.astype(jnp.int32)
    irt_ref[:, rs] = ir.T[0:8, :]
    cnt_ref[j] = jnp.broadcast_to(jnp.sum(onehot, axis=0, keepdims=True), (8, LANES)).astype(jnp.int32)


def _merge(x, xb, y_ret, y_ml, outs, lses, w_gates, p, tm):
    t = x.shape[0]
    tm = min(tm, t)
    rowb = lambda w: pl.BlockSpec((tm, w), lambda i: (i, 0))
    full = lambda shp: pl.BlockSpec(shp, lambda i: (0,) * len(shp))
    wr = jnp.zeros((D_MODEL, LANES), F32).at[:, :N_EXPERTS].set(p['w_router'].astype(F32))
    wrh = wr.astype(BF16)
    wrl = (wr - wrh.astype(F32)).astype(BF16)
    br = jnp.zeros((1, LANES), F32).at[0, :N_EXPERTS].set(p['b_router'].astype(F32))
    row = lambda a: a.reshape(1, D_MODEL).astype(F32)
    return pl.pallas_call(
        functools.partial(_merge_kernel, tm=tm),
        grid=(t // tm,),
        in_specs=[rowb(D_MODEL), rowb(MIX_W), rowb(MIX_W)]
                 + [pl.BlockSpec((tm // d, d * MIX_W), lambda i: (i, 0)) for _, d in ATT_PATTERNS]
                 + [pl.BlockSpec((tm // d, d * LANES), lambda i: (i, 0)) for _, d in ATT_PATTERNS]
                 + [rowb(D_MODEL), full((D_MODEL, N_BRANCHES * D_MODEL)),
                  full((N_BRANCHES, MIX_W, D_MODEL)), full((D_MODEL, D_MODEL)),
                  full((1, D_MODEL)), full((1, D_MODEL)),
                  full((D_MODEL, LANES)), full((D_MODEL, LANES)), full((1, LANES))],
        out_specs=[rowb(D_MODEL), rowb(D_MODEL), rowb(LANES), pl.BlockSpec((8, tm), lambda i: (0, i)), rowb(LANES),
                   pl.BlockSpec((tm // ROUTE_TM, 8, LANES), lambda i: (i, 0, 0))],
        out_shape=[jax.ShapeDtypeStruct((t, D_MODEL), F32),
                   jax.ShapeDtypeStruct((t, D_MODEL), BF16),
                   jax.ShapeDtypeStruct((t, LANES), jnp.int32),
                   jax.ShapeDtypeStruct((8, t), F32),
                   jax.ShapeDtypeStruct((t, LANES), F32),
                   jax.ShapeDtypeStruct((t // ROUTE_TM, 8, LANES), jnp.int32)],
        scratch_shapes=[pltpu.VMEM((ATT_GROUPS, ATT_HEADS, tm, ATT_DV), F32),
                        pltpu.VMEM((ATT_GROUPS, tm, LANES), F32)],
        compiler_params=_params(("arbitrary",)),
        name="merge_ln1_router",
    )(x, y_ret, y_ml, outs[0], outs[1], outs[2], lses[0], lses[1], lses[2], xb, w_gates,
      p['w_branch'].astype(BF16), p['w_out'].astype(BF16), row(p['ln1_g']), row(p['ln1_b']), wrh, wrl, br)


ROUTE_TM = 256
REGION_CAP = 64
SEGMENT_SLACK = REGION_CAP
PAD_PIECES = (512, 256, 128, 64, 32, 16, 8)
ONEHOT_CHUNK = 256
HALF = D_MODEL // 2
HI_MASK = -65536


def _pack_bf16_pairs(y):
    lo = lax.shift_right_logical(lax.bitcast_convert_type(y[:, :HALF], jnp.int32), 16)
    hi = lax.bitcast_convert_type(y[:, HALF:], jnp.int32) & HI_MASK
    return lo | hi


def _unpack_bf16_pairs(w):
    lo = lax.bitcast_convert_type(lax.shift_left(w, 16), F32)
    hi = lax.bitcast_convert_type(w & HI_MASK, F32)
    return jnp.concatenate([lo, hi], axis=1).astype(BF16)


def _row_pieces(n, pieces, fn):
    for size in pieces:
        if size == pieces[0]:
            cond, pos = n >= size, 0
        else:
            cond, pos = (n & size) != 0, pl.multiple_of(n & (-2 * size), 8)

        @pl.when(cond)
        def _(pos=pos, size=size):
            fn(pos, size)


def _region_copies(off_ref, cnt_ref, tile, p, make_copy, start):
    for e in range(N_EXPERTS):
        base = off_ref[tile * N_EXPERTS + e] + p * REGION_CAP
        cp = make_copy(pl.ds(e * REGION_CAP, REGION_CAP), pl.ds(pl.multiple_of(base, 8), REGION_CAP))
        act = cp.start if start else cp.wait
        if cnt_ref is None:
            act()
        else:
            pl.when(cnt_ref[tile * N_EXPERTS + e] > p * REGION_CAP)(act)


def _slot_targets(idx, lrank, p):
    lr = lrank - p * REGION_CAP
    return jnp.where((lr >= 0) & (lr < REGION_CAP), idx * REGION_CAP + lr, -1)


def _small_int_bf16(v):
    return v.astype(F32).astype(BF16)


def _dispatch_kernel(off_ref, cnt_ref, np_ref, pad_ref, nv_ref, x_ref, irt_ref, xs_hbm,
                     stage_ref, zero_ref, sem, zsem, *, tm, n_tiles, bm, nb):
    i = pl.program_id(0)
    slot = i % 2
    row_id = _small_int_bf16(lax.broadcasted_iota(jnp.int32, (ONEHOT_CHUNK, tm), 0))
    meta = irt_ref[...].astype(jnp.int32)
    x = x_ref[...]
    mk = lambda s: (lambda src, dst: pltpu.make_async_copy(stage_ref.at[s, src], xs_hbm.at[dst], sem.at[s]))

    def build(p):
        tgt = [_slot_targets(meta[k:k + 1, :], meta[TOP_K + k:TOP_K + k + 1, :], p) for k in range(TOP_K)]
        for c in range(N_EXPERTS * REGION_CAP // ONEHOT_CHUNK):
            hit = row_id == _small_int_bf16(tgt[0] - c * ONEHOT_CHUNK)
            for k in range(1, TOP_K):
                hit = hit | (row_id == _small_int_bf16(tgt[k] - c * ONEHOT_CHUNK))
            sel = jnp.where(hit, jnp.ones((), BF16), jnp.zeros((), BF16))
            stage_ref[slot, c * ONEHOT_CHUNK:(c + 1) * ONEHOT_CHUNK, :] = _pack_bf16_pairs(_dot(sel, x))

    build(0)

    @pl.when((i > 0) & (np_ref[jnp.maximum(i - 1, 0)] == 1))
    def _():
        _region_copies(off_ref, None, i - 1, 0, mk(1 - slot), False)

    _region_copies(off_ref, None, i, 0, mk(slot), True)

    def extra_pass(p, carry):
        @pl.when(p == 1)
        def _():
            _region_copies(off_ref, None, i, 0, mk(slot), False)

        @pl.when(p > 1)
        def _():
            _region_copies(off_ref, cnt_ref, i, p - 1, mk(slot), False)

        build(p)
        _region_copies(off_ref, cnt_ref, i, p, mk(slot), True)
        return carry

    lax.fori_loop(1, np_ref[i], extra_pass, 0)

    @pl.when(np_ref[i] > 1)
    def _():
        _region_copies(off_ref, cnt_ref, i, np_ref[i] - 1, mk(slot), False)

    @pl.when(i == n_tiles - 1)
    def _():
        @pl.when(np_ref[i] == 1)
        def _():
            _region_copies(off_ref, None, i, 0, mk(slot), False)

        zero_ref[...] = jnp.zeros_like(zero_ref)
        for phase in (True, False):
            for e in range(N_EXPERTS):
                def piece(pos, size, e=e):
                    cp = pltpu.make_async_copy(zero_ref.at[pl.ds(0, size)],
                                               xs_hbm.at[pl.ds(pl.multiple_of(pad_ref[e] + pos, 8), size)], zsem)
                    cp.start() if phase else cp.wait()
                _row_pieces(pad_ref[N_EXPERTS + e], PAD_PIECES, piece)

        def zero_block(blk, carry):
            for half in range(bm // PAD_PIECES[0]):
                cp = pltpu.make_async_copy(
                    zero_ref, xs_hbm.at[pl.ds(pl.multiple_of(blk * bm + half * PAD_PIECES[0], 8), PAD_PIECES[0])], zsem)
                cp.start()
                cp.wait()
            return carry

        lax.fori_loop(nv_ref[0], nb, zero_block, 0)


def _dispatch(x1b, irt, off, cnt8, n_pass, pad, n_valid, nb, bm, tm):
    t = x1b.shape[0]
    assert bm % PAD_PIECES[0] == 0 and bm + SEGMENT_SLACK < 2 * PAD_PIECES[0]
    return pl.pallas_call(
        functools.partial(_dispatch_kernel, tm=tm, n_tiles=t // tm, bm=bm, nb=nb),
        grid_spec=pltpu.PrefetchScalarGridSpec(
            num_scalar_prefetch=5,
            grid=(t // tm,),
            in_specs=[pl.BlockSpec((tm, D_MODEL), lambda i, *_: (i, 0)),
                      pl.BlockSpec((8, tm), lambda i, *_: (0, i))],
            out_specs=pl.BlockSpec(memory_space=pl.ANY),
            scratch_shapes=[pltpu.VMEM((2, N_EXPERTS * REGION_CAP, HALF), jnp.int32),
                            pltpu.VMEM((PAD_PIECES[0], HALF), jnp.int32),
                            pltpu.SemaphoreType.DMA((2,)), pltpu.SemaphoreType.DMA(())]),
        out_shape=jax.ShapeDtypeStruct((nb * bm, HALF), jnp.int32),
        compiler_params=_params(("arbitrary",)),
        name="moe_dispatch",
    )(off, cnt8, n_pass, pad, n_valid, x1b, irt)


def _ffn_kernel(be_ref, nr_ref, nv_ref, sg_ref, nx_ref, xs_ref, wg_hbm, bg_ref, wu_hbm, bu_ref, wd_hbm, bd_ref, o_ref,
                wf_ref, wgb_ref, wub_ref, wdb_ref, sem, *, layer):
    i = pl.program_id(0)

    def fetch(e, s, start):
        for j, w_hbm in enumerate((wg_hbm, wu_hbm, wd_hbm)):
            cp = pltpu.make_async_copy(w_hbm.at[layer, e], wf_ref.at[s, j], sem.at[s])
            if start:
                cp.start()
            else:
                cp.wait()

    @pl.when(i < nv_ref[0])
    def _():
        seg = sg_ref[i]

        @pl.when((i == 0) | (seg != sg_ref[jnp.maximum(i - 1, 0)]))
        def _():
            s = seg % 2

            @pl.when(seg == 0)
            def _():
                fetch(be_ref[i], s, True)

            fetch(be_ref[i], s, False)
            wgb_ref[...] = wf_ref[s, 0].astype(BF16)
            wub_ref[...] = wf_ref[s, 1].astype(BF16)
            wdb_ref[...] = wf_ref[s, 2].astype(BF16)

            @pl.when(nx_ref[i] >= 0)
            def _():
                fetch(nx_ref[i], 1 - s, True)

        rows = lax.broadcasted_iota(jnp.int32, xs_ref.shape, 0)
        x = _unpack_bf16_pairs(jnp.where(rows < nr_ref[i], xs_ref[...], 0))
        gate = jnp.minimum(_dot(x, wgb_ref[...]) + bg_ref[...], SWIGLU_LIMIT)
        up = jnp.clip(_dot(x, wub_ref[...]) + bu_ref[...], -SWIGLU_LIMIT, SWIGLU_LIMIT)
        hid = (up + 1.0) * gate * _sigmoid(SWIGLU_ALPHA * gate)
        y = _dot(hid.astype(BF16), wdb_ref[...]) + bd_ref[...]
        o_ref[...] = _pack_bf16_pairs(y.astype(BF16).astype(F32))

    @pl.when(i >= nv_ref[0])
    def _():
        o_ref[...] = jnp.zeros_like(o_ref)


def _experts(xs, block_e, block_rows, n_valid, seg_idx, next_e, params, l, bm):
    n_slots = xs.shape[0]
    nb = n_slots // bm
    assert D_FF == D_MODEL
    blk = lambda i, be, nr, nv, *_: (jnp.minimum(i, nv[0] - 1), 0)
    oblk = lambda i, *_: (i, 0)
    wsp = pl.BlockSpec(memory_space=pl.ANY)
    bsp = pl.BlockSpec((None, None, 1, D_FF), lambda i, be, *_: (l, be[i], 0, 0))
    b4 = lambda a: a.reshape(DEPTH, N_EXPERTS, 1, -1)
    return pl.pallas_call(
        functools.partial(_ffn_kernel, layer=l),
        grid_spec=pltpu.PrefetchScalarGridSpec(
            num_scalar_prefetch=5,
            grid=(nb,),
            in_specs=[pl.BlockSpec((bm, HALF), blk), wsp, bsp, wsp, bsp, wsp, bsp],
            out_specs=pl.BlockSpec((bm, HALF), oblk),
            scratch_shapes=[pltpu.VMEM((2, 3, D_MODEL, D_FF), F32),
                            pltpu.VMEM((D_MODEL, D_FF), BF16), pltpu.VMEM((D_MODEL, D_FF), BF16),
                            pltpu.VMEM((D_FF, D_MODEL), BF16), pltpu.SemaphoreType.DMA((2,))]),
        out_shape=jax.ShapeDtypeStruct((n_slots, HALF), jnp.int32),
        compiler_params=_params(("arbitrary",)),
        name="moe_experts",
    )(block_e, block_rows, n_valid, seg_idx, next_e, xs, params['w_gate'], b4(params['b_gate']),
      params['w_up'], b4(params['b_up']), params['w_down'], b4(params['b_down']))


def _combine_kernel(off_ref, cnt_ref, np_ref, x1_ref, ir_ref, pr_ref, ys_hbm, g2_ref, b2_ref, o_ref,
                    stage_ref, ysb_ref, wgt_ref, acc_ref, sem, *, tm, n_tiles):
    i = pl.program_id(0)
    slot = i % 2
    mk = lambda s: (lambda dst, src: pltpu.make_async_copy(ys_hbm.at[src], stage_ref.at[s, dst], sem.at[s]))

    @pl.when(i == 0)
    def _():
        _region_copies(off_ref, None, 0, 0, mk(0), True)

    col_id = _small_int_bf16(lax.broadcasted_iota(jnp.int32, (tm, ONEHOT_CHUNK), 1))
    ir = ir_ref[...]
    pr = pr_ref[...].astype(BF16)
    zero = jnp.zeros((), BF16)

    def weighted_sum(p):
        tgt = [_slot_targets(ir[:, k:k + 1], ir[:, TOP_K + k:TOP_K + k + 1], p) for k in range(TOP_K)]
        for c in range(N_EXPERTS * REGION_CAP // ONEHOT_CHUNK):
            cols = slice(c * ONEHOT_CHUNK, (c + 1) * ONEHOT_CHUNK)
            wgt = zero
            for k in reversed(range(TOP_K)):
                wgt = jnp.where(col_id == _small_int_bf16(tgt[k] - c * ONEHOT_CHUNK), pr[:, k:k + 1], wgt)
            wgt_ref[:, cols] = wgt
            ysb_ref[cols, :] = _unpack_bf16_pairs(stage_ref[slot, cols, :])
        return _dot(wgt_ref[...], ysb_ref[...])

    _region_copies(off_ref, None, i, 0, mk(slot), False)

    @pl.when(i + 1 < n_tiles)
    def _():
        _region_copies(off_ref, None, i + 1, 0, mk(1 - slot), True)

    acc_ref[...] = weighted_sum(0)

    def extra_pass(p, carry):
        _region_copies(off_ref, cnt_ref, i, p, mk(slot), True)
        _region_copies(off_ref, cnt_ref, i, p, mk(slot), False)
        acc_ref[...] += weighted_sum(p)
        return carry

    lax.fori_loop(1, np_ref[i], extra_pass, 0)
    o_ref[...] = _layer_norm(DN_ALPHA * x1_ref[...] + acc_ref[...], g2_ref[...], b2_ref[...])


def _combine(x1, idx_rank, probs, ys, off, cnt8, n_pass, p, tm):
    t = x1.shape[0]
    row = lambda a: a.reshape(1, D_MODEL).astype(F32)
    full = lambda shp: pl.BlockSpec(shp, lambda i, *_: (0,) * len(shp))
    rowb = lambda w: pl.BlockSpec((tm, w), lambda i, *_: (i, 0))
    return pl.pallas_call(
        functools.partial(_combine_kernel, tm=tm, n_tiles=t // tm),
        grid_spec=pltpu.PrefetchScalarGridSpec(
            num_scalar_prefetch=3,
            grid=(t // tm,),
            in_specs=[rowb(D_MODEL), rowb(LANES), rowb(LANES), pl.BlockSpec(memory_space=pl.ANY),
                      full((1, D_MODEL)), full((1, D_MODEL))],
            out_specs=rowb(D_MODEL),
            scratch_shapes=[pltpu.VMEM((2, N_EXPERTS * REGION_CAP, HALF), jnp.int32),
                            pltpu.VMEM((N_EXPERTS * REGION_CAP, D_MODEL), BF16),
                            pltpu.VMEM((tm, N_EXPERTS * REGION_CAP), BF16),
                            pltpu.VMEM((tm, D_MODEL), F32),
                            pltpu.SemaphoreType.DMA((2,))]),
        out_shape=jax.ShapeDtypeStruct((t, D_MODEL), F32),
        compiler_params=_params(("arbitrary",)),
        name="moe_combine_ln2",
    )(off, cnt8, n_pass, x1, idx_rank, probs, ys, row(p['ln2_g']), row(p['ln2_b']))


def _moe(x1, x1b, idx_rank, irt, probs, tile_counts, p, params, l, bm, tm):
    t = x1.shape[0]
    n_tiles = t // tm
    cnt = tile_counts[:, 0, :N_EXPERTS]
    cnt8 = (cnt + 7) // 8 * 8
    seg = jnp.sum(cnt8, axis=0)
    padded = (seg + SEGMENT_SLACK + bm - 1) // bm * bm
    pend = jnp.cumsum(padded)
    pstart = pend - padded
    off = pstart[None, :] + jnp.cumsum(cnt8, axis=0) - cnt8
    n_pass = jnp.maximum((jnp.max(cnt, axis=1) + REGION_CAP - 1) // REGION_CAP, 1).astype(jnp.int32)
    nb = -(-(t * TOP_K + 7 * n_tiles * N_EXPERTS + SEGMENT_SLACK * N_EXPERTS) // bm) + N_EXPERTS
    n_valid = (pend[-1] // bm).astype(jnp.int32).reshape(1)
    blocks = jnp.arange(nb, dtype=jnp.int32)
    block_e = jnp.minimum(jnp.sum((pend // bm)[None, :] <= blocks[:, None], axis=1), N_EXPERTS - 1).astype(jnp.int32)
    block_rows = jnp.clip(seg[block_e] - (blocks * bm - pstart[block_e]), 0, bm).astype(jnp.int32)
    off = off.astype(jnp.int32).reshape(-1)
    cnt8 = cnt8.astype(jnp.int32).reshape(-1)
    pad = jnp.concatenate([pstart + seg, padded - seg]).astype(jnp.int32)
    first = jnp.concatenate([jnp.ones((1,), jnp.int32), (block_e[1:] != block_e[:-1]).astype(jnp.int32)])
    seg_idx = (jnp.cumsum(first) - 1).astype(jnp.int32)
    next_first = (pend // bm)[block_e]
    next_e = jnp.where(next_first < n_valid[0], block_e[jnp.minimum(next_first, nb - 1)], -1).astype(jnp.int32)
    xs = _dispatch(x1b, irt, off, cnt8, n_pass, pad, n_valid, nb, bm, tm)
    ys = _experts(xs, block_e, block_rows, n_valid, seg_idx, next_e, params, l, bm)
    return _combine(x1, idx_rank, probs, ys, off, cnt8, n_pass, p, tm)


_EXPERT_WEIGHTS = ('w_gate', 'b_gate', 'w_up', 'b_up', 'w_down', 'b_down')


def _layer(x, params, l, b, s, cfg):
    p = {k: v[l] for k, v in params.items() if k not in _EXPERT_WEIGHTS}
    xb, x_groups = _prep(x, cfg['prep_tm'])
    w_in = p['w_in'].astype(BF16)
    o = IN_OFFS
    pad = jnp.zeros((D_MODEL, LANES - 2 * ML_HEADS), w_in.dtype)
    w_ret = w_in[:, o[0]:o[4]]
    w_ml = jnp.concatenate([w_in[:, o[4]:o[5]], w_in[:, o[7]:o[8]]], axis=1)
    w_if = jnp.concatenate([w_in[:, o[5]:o[7]], pad], axis=1)
    w_gates = w_in[:, o[11]:o[12]]
    y_ret = _retention(xb, w_ret, p['ret_gn'], b, s, cfg['seq_rows']).reshape(b * s, MIX_W)
    y_ml = _mlstm(xb, w_ml, w_if, p, b, s, cfg['seq_rows']).reshape(b * s, MIX_W)
    outs, lses = [], []
    wq = ATT_HEADS * ATT_DK
    wv = ATT_HEADS * ATT_DV
    for g in range(ATT_GROUPS):
        assert ATT_DK == 64
        w_att = jnp.concatenate([w_in[:, o[8] + g * wq:o[8] + (g + 1) * wq] * (ATT_DK ** -0.5),
                                 w_in[:, o[9] + g * wq:o[9] + (g + 1) * wq],
                                 w_in[:, o[10] + g * wv:o[10] + (g + 1) * wv]], axis=1)
        d = ATT_PATTERNS[g][1]
        og, lg = _dilated_group(xb if d == 1 else x_groups[d], w_att, g, b, s, cfg['att_nq'])
        outs.append(og)
        lses.append(lg)
    x1, x1b, idx_rank, irt, probs, tile_counts = _merge(x, xb, y_ret, y_ml, outs, lses, w_gates, p, cfg['merge_tm'])
    return _moe(x1, x1b, idx_rank, irt, probs, tile_counts, p, params, l, cfg['moe_bm'], ROUTE_TM)


CFG = dict(prep_tm=1024, seq_rows=512, att_nq=4, merge_tm=512, moe_bm=512)

_PARAM_NAMES = ('w_in', 'ret_gn', 'ml_conv_w', 'ml_conv_b', 'ml_wq', 'ml_wk', 'ml_wv', 'ml_bi', 'ml_bf', 'ml_gn',
                'ml_skip', 'w_branch', 'w_out', 'ln1_g', 'ln1_b', 'w_router', 'b_router', 'w_gate', 'b_gate',
                'w_up', 'b_up', 'w_down', 'b_down', 'ln2_g', 'ln2_b')


def _forward(x, params, cfg):
    b, s, d = x.shape
    xf = x.reshape(b * s, d).astype(F32)
    for l in range(DEPTH):
        xf = _layer(xf, params, l, b, s, cfg)
    return xf.reshape(b, s, d).astype(x.dtype)


def kernel(x, w_in, ret_gn, ml_conv_w, ml_conv_b, ml_wq, ml_wk, ml_wv, ml_bi, ml_bf, ml_gn, ml_skip, w_branch, w_out, ln1_g, ln1_b, w_router, b_router, w_gate, b_gate, w_up, b_up, w_down, b_down, ln2_g, ln2_b):
    params = dict(zip(_PARAM_NAMES, (w_in, ret_gn, ml_conv_w, ml_conv_b, ml_wq, ml_wk, ml_wv, ml_bi, ml_bf, ml_gn,
                                     ml_skip, w_branch, w_out, ln1_g, ln1_b, w_router, b_router, w_gate, b_gate,
                                     w_up, b_up, w_down, b_down, ln2_g, ln2_b)))
    return _forward(x, params, CFG)
```

```python
import functools

import jax
import jax.numpy as jnp
import numpy as np
from jax import lax
from jax.experimental import pallas as pl
from jax.experimental.pallas import tpu as pltpu

F32 = jnp.float32
BF16 = jnp.bfloat16

D_MODEL = 1024
DEPTH = 2
MIX_W = D_MODEL // 2
N_BRANCHES = 3
RET_HEADS = 4
RET_DV = MIX_W // RET_HEADS
RET_DK = RET_DV // 2
ML_HEADS = 4
ML_DH = MIX_W // ML_HEADS
ML_CONV = 4
ML_QK_BLOCK = 4
ML_HEAD_GROUP = 1
ATT_PATTERNS = ((128, 1), (512, 4), (2048, 16))
ATT_GROUPS = len(ATT_PATTERNS)
ATT_HEADS = 4
ATT_DV = MIX_W // ATT_HEADS
ATT_DK = ATT_DV // 2
N_EXPERTS = 32
TOP_K = 4
D_FF = D_MODEL
SWIGLU_LIMIT = 7.0
SWIGLU_ALPHA = 1.702
DN_ALPHA = (2.0 * DEPTH) ** 0.25
EPS = 1e-5

CHUNK = 128
LANES = 128
NEG = -1e30
VMEM_LIMIT = 56 * 1024 * 1024

IN_SIZES = (RET_HEADS * RET_DK, RET_HEADS * RET_DK, MIX_W, MIX_W,
            MIX_W, ML_HEADS, ML_HEADS, MIX_W,
            ATT_GROUPS * ATT_HEADS * ATT_DK, ATT_GROUPS * ATT_HEADS * ATT_DK, ATT_GROUPS * ATT_HEADS * ATT_DV,
            N_BRANCHES * D_MODEL)
IN_OFFS = tuple(int(v) for v in np.cumsum((0,) + IN_SIZES))


def _params(sem):
    return pltpu.CompilerParams(dimension_semantics=sem, vmem_limit_bytes=VMEM_LIMIT)


def _dot(a, b):
    return jnp.dot(a, b, preferred_element_type=F32)


def _dot_nt(a, b):
    return lax.dot_general(a, b, (((1,), (1,)), ((), ())), preferred_element_type=F32)


def _dot_tn(a, b):
    return lax.dot_general(a, b, (((0,), (0,)), ((), ())), preferred_element_type=F32)


def _sigmoid(x):
    return 1.0 / (1.0 + jnp.exp(-x))


def _prep_kernel(x_ref, xb_ref, *rest, dilations):
    group_refs, cols_ref = rest[:-1], rest[-1]
    x = x_ref[...]
    xb_ref[...] = x.astype(xb_ref.dtype)
    tm, dm = x.shape
    for c in range(dm // LANES):
        cols_ref[c] = x[:, c * LANES:(c + 1) * LANES]
    for o_ref, d in zip(group_refs, dilations):
        n = tm // d
        for r in range(d):
            for c in range(dm // LANES):
                o_ref[:, r * dm + c * LANES:r * dm + (c + 1) * LANES] = (
                    cols_ref[c, pl.ds(r, n, stride=d), :].astype(o_ref.dtype))


def _prep(x, tm):
    t, dm = x.shape
    tm = min(tm, t)
    dilations = tuple(sorted({d for _, d in ATT_PATTERNS if d > 1}))
    assert t % tm == 0 and all(tm % (16 * d) == 0 for d in dilations)
    outs = pl.pallas_call(
        functools.partial(_prep_kernel, dilations=dilations),
        grid=(t // tm,),
        in_specs=[pl.BlockSpec((tm, dm), lambda i: (i, 0))],
        out_specs=[pl.BlockSpec((tm, dm), lambda i: (i, 0))]
                  + [pl.BlockSpec((tm // d, d * dm), lambda i: (i, 0)) for d in dilations],
        out_shape=[jax.ShapeDtypeStruct((t, dm), BF16)]
                  + [jax.ShapeDtypeStruct((t // d, d * dm), BF16) for d in dilations],
        scratch_shapes=[pltpu.VMEM((dm // LANES, tm, LANES), F32)],
        compiler_params=_params(("arbitrary",)),
        name="prep_bf16_groups",
    )(x)
    return outs[0], dict(zip(dilations, outs[1:]))


def _head_norm(o, gn):
    mu = jnp.mean(o, axis=-1, keepdims=True)
    oc = o - mu
    var = jnp.mean(oc * oc, axis=-1, keepdims=True)
    return oc * lax.rsqrt(var + EPS) * gn


def _ret_kernel(x_ref, w_ref, dm_ref, qd_ref, kd_ref, cd_ref, gn_ref, o_ref, st_ref, blk_ref, *, nc, nb):
    @pl.when(pl.program_id(0) == 0)
    def _():
        st_ref[...] = jnp.zeros_like(st_ref)

    for bi in range(nb):
        blk_ref[bi] = _dot(x_ref[bi], w_ref[...]).astype(blk_ref.dtype)

    hq = RET_HEADS * RET_DK
    state = [st_ref[j] for j in range(nb * RET_HEADS)]
    for c in range(nc):
        rows = pl.ds(c * CHUNK, CHUNK)
        chains = [(h, bi) for h in range(RET_HEADS) for bi in range(nb)]
        first = []
        for h, bi in chains:
            q = blk_ref[bi, rows, h * RET_DK:(h + 1) * RET_DK]
            k = blk_ref[bi, rows, hq + h * RET_DK:hq + (h + 1) * RET_DK]
            v = blk_ref[bi, rows, 2 * hq + h * RET_DV:2 * hq + (h + 1) * RET_DV]
            st = state[bi * RET_HEADS + h]
            first.append((v, _dot_nt(q, k), _dot(q, st.astype(BF16)),
                          _dot_tn(k, (v.astype(F32) * kd_ref[h]).astype(BF16))))
        for (h, bi), (v, s_qk, q_st, kv) in zip(chains, first):
            o = _dot((s_qk * dm_ref[h]).astype(BF16), v) + q_st * qd_ref[h]
            state[bi * RET_HEADS + h] = cd_ref[h] * state[bi * RET_HEADS + h] + kv
            g = blk_ref[bi, rows, 2 * hq + MIX_W + h * RET_DV:2 * hq + MIX_W + (h + 1) * RET_DV].astype(F32)
            y = _head_norm(o, gn_ref[:, h * RET_DV:(h + 1) * RET_DV]) * (g * _sigmoid(g))
            o_ref[bi, rows, h * RET_DV:(h + 1) * RET_DV] = y.astype(o_ref.dtype)
    for j in range(nb * RET_HEADS):
        st_ref[j] = state[j]


def _retention(xb, w_ret, ret_gn, b, s, rows):
    nc = rows // CHUNK
    scale = RET_DK ** -0.5
    log_gamma = jnp.log1p(-jnp.exp2(-5.0 - jnp.arange(RET_HEADS, dtype=F32)))
    pos = jnp.arange(CHUNK, dtype=F32)
    diff = pos[:, None] - pos[None, :]
    dm = jnp.where(diff >= 0, jnp.exp(log_gamma[:, None, None] * jnp.maximum(diff, 0.0)), 0.0) * scale
    kd = jnp.broadcast_to(jnp.exp(log_gamma[:, None] * (CHUNK - 1.0 - pos))[:, :, None], (RET_HEADS, CHUNK, RET_DV))
    qd = jnp.broadcast_to((jnp.exp(log_gamma[:, None] * (pos + 1.0)) * scale)[:, :, None], (RET_HEADS, CHUNK, RET_DV))
    cd = jnp.broadcast_to(jnp.exp(log_gamma * CHUNK)[:, None, None], (RET_HEADS, 1, RET_DV))
    dmod, w_in = w_ret.shape
    full = lambda shp: pl.BlockSpec(shp, lambda si: (0,) * len(shp))
    return pl.pallas_call(
        functools.partial(_ret_kernel, nc=nc, nb=b),
        grid=(s // rows,),
        in_specs=[pl.BlockSpec((b, rows, dmod), lambda si: (0, si, 0)), full((dmod, w_in)),
                  full((RET_HEADS, CHUNK, CHUNK)), full((RET_HEADS, CHUNK, RET_DV)),
                  full((RET_HEADS, CHUNK, RET_DV)), full((RET_HEADS, 1, RET_DV)), full((1, MIX_W))],
        out_specs=pl.BlockSpec((b, rows, MIX_W), lambda si: (0, si, 0)),
        out_shape=jax.ShapeDtypeStruct((b, s, MIX_W), BF16),
        scratch_shapes=[pltpu.VMEM((b * RET_HEADS, RET_DK, RET_DV), F32), pltpu.VMEM((b, rows, w_in), BF16)],
        compiler_params=_params(("arbitrary",)),
        name="retention",
    )(xb.reshape(b, s, dmod), w_ret, dm, qd, kd, cd, ret_gn.reshape(1, MIX_W).astype(F32))


def _log_sigmoid(x):
    return jnp.minimum(x, 0.0) - jnp.log(1.0 + jnp.exp(-jnp.abs(x)))


def _ml_kernel(x_ref, wml_ref, wif_ref, cw_ref, cb_ref, bq_ref, bk_ref, bv_ref, gb_ref, gn_ref, sk_ref,
               o_ref, c_ref, n_ref, m_ref, halo_ref, ml_ref, if_ref, qkv_ref, *, nc, nb):
    rows_total = nc * CHUNK

    @pl.when(pl.program_id(0) == 0)
    def _():
        c_ref[...] = jnp.zeros_like(c_ref)
        n_ref[...] = jnp.zeros_like(n_ref)
        m_ref[...] = jnp.zeros_like(m_ref)
        halo_ref[...] = jnp.zeros_like(halo_ref)

    for bi in range(nb):
        ml_ref[bi] = _dot(x_ref[bi], wml_ref[...]).astype(ml_ref.dtype)
        if_ref[bi] = _dot(x_ref[bi], wif_ref[...])

    x_all, mc_all = [], []
    for bi in range(nb):
        xa = ml_ref[bi, :, 0:MIX_W].astype(F32)
        xf = jnp.concatenate([halo_ref[bi], xa], axis=0)
        acc = jnp.broadcast_to(cb_ref[...], (rows_total, MIX_W))
        for j in range(ML_CONV):
            sh = ML_CONV - 1 - j
            xs = xf if sh == 0 else pltpu.roll(xf, sh, 0)
            acc = acc + xs[8:8 + rows_total] * cw_ref[j:j + 1, :]
        halo_ref[bi] = xa[rows_total - 8:rows_total]
        x_all.append(xa)
        mc_all.append(acc * _sigmoid(acc))

    ri = lax.broadcasted_iota(jnp.int32, (CHUNK, CHUNK), 0)
    ci = lax.broadcasted_iota(jnp.int32, (CHUNK, CHUNK), 1)
    tri = ri >= ci
    tril = tri.astype(F32)
    kscale = ML_DH ** -0.5
    log_kscale = float(np.log(kscale))

    c_st = [[c_ref[bi * ML_HEADS + h] for h in range(ML_HEADS)] for bi in range(nb)]
    n_st = [[n_ref[bi, h:h + 1, :] for h in range(ML_HEADS)] for bi in range(nb)]
    m_st = [[m_ref[bi, h:h + 1, 0:1] for h in range(ML_HEADS)] for bi in range(nb)]

    for bi in range(nb):
        mcb = mc_all[bi].astype(BF16)
        xvb = x_all[bi].astype(BF16)
        for h in range(ML_HEADS):
            cols = slice(h * ML_DH, (h + 1) * ML_DH)
            qkv_ref[bi, 0, :, cols] = _dot(mcb[:, cols], bq_ref[h]).astype(BF16)
            qkv_ref[bi, 1, :, cols] = _dot(mcb[:, cols], bk_ref[h]).astype(BF16)
            qkv_ref[bi, 2, :, cols] = _dot(xvb[:, cols], bv_ref[h]).astype(BF16)

    def run_chains(r0, gates, chains):
        for h, bi in chains:
            g_pre, cum, g_t, cum_t = gates[bi]
            cols = slice(h * ML_DH, (h + 1) * ML_DH)
            mc = mc_all[bi][r0:r0 + CHUNK, cols]
            q = qkv_ref[bi, 0, r0:r0 + CHUNK, cols]
            k = qkv_ref[bi, 1, r0:r0 + CHUNK, cols]
            v = qkv_ref[bi, 2, r0:r0 + CHUNK, cols]
            cum_col = cum[:, ML_HEADS + h:ML_HEADS + h + 1]
            ig_col = g_pre[:, h:h + 1]
            cum_row = cum_t[ML_HEADS + h:ML_HEADS + h + 1, :]
            ig_row = g_t[h:h + 1, :]
            tot = cum[CHUNK - 1:CHUNK, ML_HEADS + h:ML_HEADS + h + 1]
            m_prev, c_prev, n_prev = m_st[bi][h], c_st[bi][h], n_st[bi][h]
            dlog = jnp.where(tri, cum_col + (ig_row - cum_row), NEG)
            m_inter = cum_col + m_prev
            m_q = jnp.maximum(m_inter, jnp.max(dlog, axis=-1, keepdims=True))
            w_qk = jnp.exp(dlog - (m_q - log_kscale)) * _dot_nt(q, k)
            inter = jnp.exp(m_inter - m_q)
            num = _dot(w_qk.astype(BF16), v) + inter * _dot(q, c_prev.astype(BF16))
            den = (jnp.sum(w_qk, axis=-1, keepdims=True)
                   + inter * jnp.sum(q.astype(F32) * n_prev, axis=-1, keepdims=True))
            hh = num * (1.0 / jnp.maximum(jnp.abs(den), jnp.exp(-m_q)))
            a_col = tot - cum_col + ig_col
            a_max = jnp.max(a_col, axis=0, keepdims=True)
            wa = jnp.exp(a_col - a_max) * kscale
            chunk_c = _dot_tn(k, (wa * v.astype(F32)).astype(BF16))
            chunk_n = jnp.sum(wa * k.astype(F32), axis=0, keepdims=True)
            m_new = jnp.maximum(tot + m_prev, a_max)
            s_old = jnp.exp(tot + m_prev - m_new)
            s_new = jnp.exp(a_max - m_new)
            c_st[bi][h] = s_old * c_prev + s_new * chunk_c
            n_st[bi][h] = s_old * n_prev + s_new * chunk_n
            m_st[bi][h] = m_new
            mo = ml_ref[bi, r0:r0 + CHUNK, MIX_W + h * ML_DH:MIX_W + (h + 1) * ML_DH].astype(F32)
            y = _sigmoid(mo) * (_head_norm(hh, gn_ref[:, cols]) + sk_ref[:, cols] * mc)
            o_ref[bi, r0:r0 + CHUNK, cols] = y.astype(o_ref.dtype)

    for c in range(nc):
        r0 = c * CHUNK
        gates = []
        for bi in range(nb):
            g_pre = if_ref[bi, r0:r0 + CHUNK, :] + gb_ref[...]
            cum = jnp.dot(tril, _log_sigmoid(g_pre), preferred_element_type=F32, precision=lax.Precision.HIGHEST)
            gates.append((g_pre, cum, g_pre.T, cum.T))
        for h0 in range(0, ML_HEADS, ML_HEAD_GROUP):
            run_chains(r0, gates, [(h, bi) for h in range(h0, h0 + ML_HEAD_GROUP) for bi in range(nb)])

    for bi in range(nb):
        for h in range(ML_HEADS):
            c_ref[bi * ML_HEADS + h] = c_st[bi][h]
            n_ref[bi, h:h + 1, :] = n_st[bi][h]
            m_ref[bi, h:h + 1, :] = jnp.broadcast_to(m_st[bi][h], (1, LANES))


def _block_diag(w):
    nb = w.shape[0]
    per = nb // ML_HEADS
    wh = w.reshape(ML_HEADS, per, ML_QK_BLOCK, ML_QK_BLOCK)
    eye = jnp.eye(per, dtype=w.dtype)
    bd = jnp.einsum('hncd,nm->hncmd', wh, eye)
    return bd.reshape(ML_HEADS, per * ML_QK_BLOCK, per * ML_QK_BLOCK).astype(BF16)


def _mlstm(xb, w_ml, w_if, p, b, s, rows):
    nc = rows // CHUNK
    dm = xb.shape[1]
    full = lambda shp: pl.BlockSpec(shp, lambda si: (0,) * len(shp))
    gb = jnp.zeros((1, LANES), F32).at[0, 0:ML_HEADS].set(p['ml_bi'].astype(F32)).at[0, ML_HEADS:2 * ML_HEADS].set(p['ml_bf'].astype(F32))
    row = lambda a: a.reshape(1, MIX_W).astype(F32)
    return pl.pallas_call(
        functools.partial(_ml_kernel, nc=nc, nb=b),
        grid=(s // rows,),
        in_specs=[pl.BlockSpec((b, rows, dm), lambda si: (0, si, 0)),
                  full((dm, 2 * MIX_W)), full((dm, LANES)),
                  full((ML_CONV, MIX_W)), full((1, MIX_W)),
                  full((ML_HEADS, ML_DH, ML_DH)), full((ML_HEADS, ML_DH, ML_DH)), full((ML_HEADS, ML_DH, ML_DH)),
                  full((1, LANES)), full((1, MIX_W)), full((1, MIX_W))],
        out_specs=pl.BlockSpec((b, rows, MIX_W), lambda si: (0, si, 0)),
        out_shape=jax.ShapeDtypeStruct((b, s, MIX_W), BF16),
        scratch_shapes=[pltpu.VMEM((b * ML_HEADS, ML_DH, ML_DH), F32), pltpu.VMEM((b, 8, ML_DH), F32),
                        pltpu.VMEM((b, 8, LANES), F32), pltpu.VMEM((b, 8, MIX_W), F32),
                        pltpu.VMEM((b, rows, 2 * MIX_W), BF16), pltpu.VMEM((b, rows, LANES), F32),
                        pltpu.VMEM((b, 3, rows, MIX_W), BF16)],
        compiler_params=_params(("arbitrary",)),
        name="mlstm",
    )(xb.reshape(b, s, dm), w_ml, w_if,
      p['ml_conv_w'].astype(F32), row(p['ml_conv_b']),
      _block_diag(p['ml_wq']), _block_diag(p['ml_wk']), _block_diag(p['ml_wv']),
      gb, row(p['ml_gn']), row(p['ml_skip']))


def _att_kernel(x_ref, w_ref, bias_ref, o_ref, lse_ref, qkv_ref, *, nq):
    first = pl.program_id(2) == 0
    span = nq * CHUNK
    wq = ATT_HEADS * ATT_DK

    @pl.when(first)
    def _():
        qkv_ref[0:CHUNK, :] = jnp.zeros((CHUNK, qkv_ref.shape[1]), qkv_ref.dtype)

    qkv_ref[CHUNK:CHUNK + span, :] = _dot(x_ref[...], w_ref[...]).astype(qkv_ref.dtype)
    lane = lax.broadcasted_iota(jnp.int32, (CHUNK, LANES), 1)
    ones = jnp.ones((2 * CHUNK, ATT_DV), BF16)
    for j in range(nq):
        rows = slice(j * CHUNK, (j + 1) * CHUNK)
        kv_rows = slice(j * CHUNK, (j + 2) * CHUNK)
        lse_out = jnp.zeros((CHUNK, LANES), F32)
        for h in range(ATT_HEADS):
            dv = slice(h * ATT_DV, (h + 1) * ATT_DV)
            q = qkv_ref[CHUNK + j * CHUNK:CHUNK + (j + 1) * CHUNK, h * ATT_DK:(h + 1) * ATT_DK]
            kk = qkv_ref[kv_rows, wq + h * ATT_DK:wq + (h + 1) * ATT_DK]
            vv = qkv_ref[kv_rows, 2 * wq + h * ATT_DV:2 * wq + (h + 1) * ATT_DV]
            bias = bias_ref[jnp.where(first, 0, 1), h] if j == 0 else bias_ref[1, h]
            s = _dot_nt(q, kk) + bias
            m = jnp.max(s, axis=-1, keepdims=True)
            p = jnp.exp(s - m).astype(BF16)
            oa = _dot(p, jnp.concatenate([vv, ones], axis=1))
            l = oa[:, ATT_DV:2 * ATT_DV]
            o_ref[rows, dv] = (oa[:, 0:ATT_DV] * (1.0 / l)).astype(o_ref.dtype)
            lse_out = jnp.where(lane == h, m + jnp.log(l), lse_out)
        lse_ref[rows, :] = lse_out
    qkv_ref[0:CHUNK, :] = qkv_ref[span:span + CHUNK, :]


def _att_bias(window, dilation, slopes):
    wb = window // dilation
    qi = jnp.arange(wb)[:, None]
    kj = jnp.arange(2 * wb)[None, :]
    delta = qi + wb - kj
    band = (delta >= 0) & (delta <= wb)
    bias = -slopes[:, None, None] * (dilation * delta).astype(F32)
    later = jnp.where(band[None], bias, NEG)
    first = jnp.where((band & (kj >= wb))[None], bias, NEG)
    return jnp.stack([first, later]).astype(F32)


def _dilated_group(xg, w_qkv, g, b, s, nq):
    window, d = ATT_PATTERNS[g]
    assert window // d == CHUNK
    l_sub = s // d
    assert l_sub % CHUNK == 0
    n_blk = l_sub // CHUNK
    slopes = jnp.exp2(-8.0 * jnp.arange(1, ATT_GROUPS * ATT_HEADS + 1, dtype=F32) / (ATT_GROUPS * ATT_HEADS))
    bias = _att_bias(window, d, slopes.reshape(ATT_GROUPS, ATT_HEADS)[g])
    wq = ATT_HEADS * ATT_DK
    wv = ATT_HEADS * ATT_DV
    per_tok = 2 * wq + wv
    dm = w_qkv.shape[0]
    assert xg.shape == (b * l_sub, d * dm)
    x = xg.reshape(b, l_sub, d * dm)
    nq = min(nq, n_blk)
    assert n_blk % nq == 0
    span = nq * CHUNK
    o, lse = pl.pallas_call(
        functools.partial(_att_kernel, nq=nq),
        grid=(b, d, n_blk // nq),
        in_specs=[pl.BlockSpec((None, span, dm), lambda bi, r, n: (bi, n, r)),
                  pl.BlockSpec((dm, per_tok), lambda bi, r, n: (0, 0)),
                  pl.BlockSpec((2, ATT_HEADS, CHUNK, 2 * CHUNK), lambda bi, r, n: (0, 0, 0, 0))],
        out_specs=[pl.BlockSpec((None, span, wv), lambda bi, r, n: (bi, n, r)),
                   pl.BlockSpec((None, span, LANES), lambda bi, r, n: (bi, n, r))],
        out_shape=[jax.ShapeDtypeStruct((b, l_sub, d * wv), BF16),
                   jax.ShapeDtypeStruct((b, l_sub, d * LANES), F32)],
        scratch_shapes=[pltpu.VMEM((CHUNK + span, per_tok), BF16)],
        compiler_params=_params(("arbitrary", "arbitrary", "arbitrary")),
        name=f"dilated_attn_g{g}",
    )(x, w_qkv, bias)
    return o.reshape(b * l_sub, d * wv), lse.reshape(b * l_sub, d * LANES)


def _layer_norm(z, g, b):
    mu = jnp.mean(z, axis=-1, keepdims=True)
    zc = z - mu
    var = jnp.mean(zc * zc, axis=-1, keepdims=True)
    return zc * lax.rsqrt(var + EPS) * g + b


def _merge_kernel(x_ref, yr_ref, ym_ref, o0_ref, o1_ref, o2_ref, l0_ref, l1_ref, l2_ref, xb_ref,
                  wg_ref, wb_ref, wo_ref, g1_ref, b1_ref, wrh_ref, wrl_ref, br_ref,
                  x1_ref, x1b_ref, ir_ref, irt_ref, pr_ref, cnt_ref, og_ref, lg_ref, *, tm):
    for g, (src_o, src_l) in enumerate(((o0_ref, l0_ref), (o1_ref, l1_ref), (o2_ref, l2_ref))):
        d = ATT_PATTERNS[g][1]
        n = tm // d
        for r in range(d):
            rows = pl.ds(r, n, stride=d) if d > 1 else pl.ds(0, n)
            for h in range(ATT_HEADS):
                og_ref[g, h, rows, :] = src_o[:, r * MIX_W + h * ATT_DV:r * MIX_W + (h + 1) * ATT_DV].astype(F32)
            lg_ref[g, rows, :] = src_l[:, r * LANES:(r + 1) * LANES]

    l0, l1, l2 = lg_ref[0], lg_ref[1], lg_ref[2]
    lm = jnp.maximum(jnp.maximum(l0, l1), l2)
    e0, e1, e2 = jnp.exp(l0 - lm), jnp.exp(l1 - lm), jnp.exp(l2 - lm)
    inv = 1.0 / (e0 + e1 + e2)
    parts = []
    for h in range(ATT_HEADS):
        dv = slice(h * ATT_DV, (h + 1) * ATT_DV)
        hs = slice(h, h + 1)
        parts.append((e0[:, hs] * og_ref[0, h] + e1[:, hs] * og_ref[1, h] + e2[:, hs] * og_ref[2, h]) * inv[:, hs])
    y_att = jnp.concatenate(parts, axis=-1).astype(BF16)

    xb = xb_ref[...]
    gate = lambda j: _sigmoid(_dot(xb, wg_ref[:, j * D_MODEL:(j + 1) * D_MODEL]))
    merged = gate(0) * _dot(yr_ref[...], wb_ref[0])
    merged = merged + gate(1) * _dot(ym_ref[...], wb_ref[1])
    merged = merged + gate(2) * _dot(y_att, wb_ref[2])
    z = DN_ALPHA * x_ref[...] + _dot(merged.astype(BF16), wo_ref[...])
    x1 = _layer_norm(z, g1_ref[...], b1_ref[...])
    x1_ref[...] = x1
    x1b_ref[...] = x1.astype(BF16)

    xh = x1.astype(BF16)
    xl = (x1 - xh.astype(F32)).astype(BF16)
    logits = _dot(xh, wrh_ref[...]) + _dot(xl, wrh_ref[...]) + _dot(xh, wrl_ref[...]) + br_ref[...]
    lane = lax.broadcasted_iota(jnp.int32, (tm, LANES), 1)
    work = jnp.where(lane < N_EXPERTS, logits, -jnp.inf)
    vals, idxs = [], []
    for _ in range(TOP_K):
        m = jnp.max(work, axis=-1, keepdims=True)
        idx = jnp.min(jnp.where(work == m, lane, LANES), axis=-1, keepdims=True)
        vals.append(m)
        idxs.append(idx)
        work = jnp.where(lane == idx, -jnp.inf, work)
    es = [jnp.exp(v - vals[0]) for v in vals]
    tot = es[0] + es[1] + es[2] + es[3]
    pr = jnp.zeros((tm, LANES), F32)
    for k in range(TOP_K):
        pr = jnp.where(lane == k, es[k] / tot, pr)
    pr_ref[...] = pr

    onehot = jnp.zeros((tm, LANES), F32)
    for k in range(TOP_K):
        onehot = onehot + (lane == idxs[k]).astype(F32)
    ri = lax.broadcasted_iota(jnp.int32, (tm, tm), 0)
    ci = lax.broadcasted_iota(jnp.int32, (tm, tm), 1)
    same_tile = (ri // ROUTE_TM) == (ci // ROUTE_TM)
    before = _dot(((ri > ci) & same_tile).astype(BF16), onehot.astype(BF16))
    ir = jnp.zeros((tm, LANES), F32)
    for k in range(TOP_K):
        rank = jnp.sum(jnp.where(lane == idxs[k], before, 0.0), axis=-1, keepdims=True)
        ir = jnp.where(lane == k, idxs[k].astype(F32), ir)
        ir = jnp.where(lane == TOP_K + k, rank, ir)
    ir_ref[...] = ir.astype(jnp.int32)
    irt_ref[...] = ir.T[0:8, :]
    for j in range(tm // ROUTE_TM):
        cnt_j = jnp.sum(onehot[j * ROUTE_TM:(j + 1) * ROUTE_TM], axis=0, keepdims=True)
        cnt_ref[j] = jnp.broadcast_to(cnt_j, (8, LANES)).astype(jnp.int32)


def _merge(x, xb, y_ret, y_ml, outs, lses, w_gates, p, tm):
    t = x.shape[0]
    tm = min(tm, t)
    rowb = lambda w: pl.BlockSpec((tm, w), lambda i: (i, 0))
    full = lambda shp: pl.BlockSpec(shp, lambda i: (0,) * len(shp))
    wr = jnp.zeros((D_MODEL, LANES), F32).at[:, :N_EXPERTS].set(p['w_router'].astype(F32))
    wrh = wr.astype(BF16)
    wrl = (wr - wrh.astype(F32)).astype(BF16)
    br = jnp.zeros((1, LANES), F32).at[0, :N_EXPERTS].set(p['b_router'].astype(F32))
    row = lambda a: a.reshape(1, D_MODEL).astype(F32)
    return pl.pallas_call(
        functools.partial(_merge_kernel, tm=tm),
        grid=(t // tm,),
        in_specs=[rowb(D_MODEL), rowb(MIX_W), rowb(MIX_W)]
                 + [pl.BlockSpec((tm // d, d * MIX_W), lambda i: (i, 0)) for _, d in ATT_PATTERNS]
                 + [pl.BlockSpec((tm // d, d * LANES), lambda i: (i, 0)) for _, d in ATT_PATTERNS]
                 + [rowb(D_MODEL), full((D_MODEL, N_BRANCHES * D_MODEL)),
                  full((N_BRANCHES, MIX_W, D_MODEL)), full((D_MODEL, D_MODEL)),
                  full((1, D_MODEL)), full((1, D_MODEL)),
                  full((D_MODEL, LANES)), full((D_MODEL, LANES)), full((1, LANES))],
        out_specs=[rowb(D_MODEL), rowb(D_MODEL), rowb(LANES), pl.BlockSpec((8, tm), lambda i: (0, i)), rowb(LANES),
                   pl.BlockSpec((tm // ROUTE_TM, 8, LANES), lambda i: (i, 0, 0))],
        out_shape=[jax.ShapeDtypeStruct((t, D_MODEL), F32),
                   jax.ShapeDtypeStruct((t, D_MODEL), BF16),
                   jax.ShapeDtypeStruct((t, LANES), jnp.int32),
                   jax.ShapeDtypeStruct((8, t), F32),
                   jax.ShapeDtypeStruct((t, LANES), F32),
                   jax.ShapeDtypeStruct((t // ROUTE_TM, 8, LANES), jnp.int32)],
        scratch_shapes=[pltpu.VMEM((ATT_GROUPS, ATT_HEADS, tm, ATT_DV), F32),
                        pltpu.VMEM((ATT_GROUPS, tm, LANES), F32)],
        compiler_params=_params(("arbitrary",)),
        name="merge_ln1_router",
    )(x, y_ret, y_ml, outs[0], outs[1], outs[2], lses[0], lses[1], lses[2], xb, w_gates,
      p['w_branch'].astype(BF16), p['w_out'].astype(BF16), row(p['ln1_g']), row(p['ln1_b']), wrh, wrl, br)


ROUTE_TM = 256
REGION_CAP = 64
SEGMENT_SLACK = REGION_CAP
PAD_PIECES = (512, 256, 128, 64, 32, 16, 8)
ONEHOT_CHUNK = 256
HALF = D_MODEL // 2
HI_MASK = -65536


def _pack_bf16_pairs(y):
    lo = lax.shift_right_logical(lax.bitcast_convert_type(y[:, :HALF], jnp.int32), 16)
    hi = lax.bitcast_convert_type(y[:, HALF:], jnp.int32) & HI_MASK
    return lo | hi


def _unpack_bf16_pairs(w):
    lo = lax.bitcast_convert_type(lax.shift_left(w, 16), F32)
    hi = lax.bitcast_convert_type(w & HI_MASK, F32)
    return jnp.concatenate([lo, hi], axis=1).astype(BF16)


def _row_pieces(n, pieces, fn):
    for size in pieces:
        if size == pieces[0]:
            cond, pos = n >= size, 0
        else:
            cond, pos = (n & size) != 0, pl.multiple_of(n & (-2 * size), 8)

        @pl.when(cond)
        def _(pos=pos, size=size):
            fn(pos, size)


def _region_copies(off_ref, cnt_ref, tile, p, make_copy, start):
    for e in range(N_EXPERTS):
        base = off_ref[tile * N_EXPERTS + e] + p * REGION_CAP
        cp = make_copy(pl.ds(e * REGION_CAP, REGION_CAP), pl.ds(pl.multiple_of(base, 8), REGION_CAP))
        act = cp.start if start else cp.wait
        if cnt_ref is None:
            act()
        else:
            pl.when(cnt_ref[tile * N_EXPERTS + e] > p * REGION_CAP)(act)


def _slot_targets(idx, lrank, p):
    lr = lrank - p * REGION_CAP
    return jnp.where((lr >= 0) & (lr < REGION_CAP), idx * REGION_CAP + lr, -1)


def _small_int_bf16(v):
    return v.astype(F32).astype(BF16)


def _dispatch_kernel(off_ref, cnt_ref, np_ref, pad_ref, nv_ref, x_ref, irt_ref, xs_hbm,
                     stage_ref, zero_ref, sem, zsem, *, tm, n_tiles, bm, nb):
    i = pl.program_id(0)
    slot = i % 2
    row_id = _small_int_bf16(lax.broadcasted_iota(jnp.int32, (ONEHOT_CHUNK, tm), 0))
    meta = irt_ref[...].astype(jnp.int32)
    x = x_ref[...]
    mk = lambda s: (lambda src, dst: pltpu.make_async_copy(stage_ref.at[s, src], xs_hbm.at[dst], sem.at[s]))

    def build(p):
        tgt = [_slot_targets(meta[k:k + 1, :], meta[TOP_K + k:TOP_K + k + 1, :], p) for k in range(TOP_K)]
        for c in range(N_EXPERTS * REGION_CAP // ONEHOT_CHUNK):
            hit = row_id == _small_int_bf16(tgt[0] - c * ONEHOT_CHUNK)
            for k in range(1, TOP_K):
                hit = hit | (row_id == _small_int_bf16(tgt[k] - c * ONEHOT_CHUNK))
            sel = jnp.where(hit, jnp.ones((), BF16), jnp.zeros((), BF16))
            stage_ref[slot, c * ONEHOT_CHUNK:(c + 1) * ONEHOT_CHUNK, :] = _pack_bf16_pairs(_dot(sel, x))

    build(0)

    @pl.when((i > 0) & (np_ref[jnp.maximum(i - 1, 0)] == 1))
    def _():
        _region_copies(off_ref, None, i - 1, 0, mk(1 - slot), False)

    _region_copies(off_ref, None, i, 0, mk(slot), True)

    def extra_pass(p, carry):
        @pl.when(p == 1)
        def _():
            _region_copies(off_ref, None, i, 0, mk(slot), False)

        @pl.when(p > 1)
        def _():
            _region_copies(off_ref, cnt_ref, i, p - 1, mk(slot), False)

        build(p)
        _region_copies(off_ref, cnt_ref, i, p, mk(slot), True)
        return carry

    lax.fori_loop(1, np_ref[i], extra_pass, 0)

    @pl.when(np_ref[i] > 1)
    def _():
        _region_copies(off_ref, cnt_ref, i, np_ref[i] - 1, mk(slot), False)

    @pl.when(i == n_tiles - 1)
    def _():
        @pl.when(np_ref[i] == 1)
        def _():
            _region_copies(off_ref, None, i, 0, mk(slot), False)

        zero_ref[...] = jnp.zeros_like(zero_ref)
        for phase in (True, False):
            for e in range(N_EXPERTS):
                def piece(pos, size, e=e):
                    cp = pltpu.make_async_copy(zero_ref.at[pl.ds(0, size)],
                                               xs_hbm.at[pl.ds(pl.multiple_of(pad_ref[e] + pos, 8), size)], zsem)
                    cp.start() if phase else cp.wait()
                _row_pieces(pad_ref[N_EXPERTS + e], PAD_PIECES, piece)

        def zero_block(blk, carry):
            for half in range(bm // PAD_PIECES[0]):
                cp = pltpu.make_async_copy(
                    zero_ref, xs_hbm.at[pl.ds(pl.multiple_of(blk * bm + half * PAD_PIECES[0], 8), PAD_PIECES[0])], zsem)
                cp.start()
                cp.wait()
            return carry

        lax.fori_loop(nv_ref[0], nb, zero_block, 0)


def _dispatch(x1b, irt, off, cnt8, n_pass, pad, n_valid, nb, bm, tm):
    t = x1b.shape[0]
    assert bm % PAD_PIECES[0] == 0 and bm + SEGMENT_SLACK < 2 * PAD_PIECES[0]
    return pl.pallas_call(
        functools.partial(_dispatch_kernel, tm=tm, n_tiles=t // tm, bm=bm, nb=nb),
        grid_spec=pltpu.PrefetchScalarGridSpec(
            num_scalar_prefetch=5,
            grid=(t // tm,),
            in_specs=[pl.BlockSpec((tm, D_MODEL), lambda i, *_: (i, 0)),
                      pl.BlockSpec((8, tm), lambda i, *_: (0, i))],
            out_specs=pl.BlockSpec(memory_space=pl.ANY),
            scratch_shapes=[pltpu.VMEM((2, N_EXPERTS * REGION_CAP, HALF), jnp.int32),
                            pltpu.VMEM((PAD_PIECES[0], HALF), jnp.int32),
                            pltpu.SemaphoreType.DMA((2,)), pltpu.SemaphoreType.DMA(())]),
        out_shape=jax.ShapeDtypeStruct((nb * bm, HALF), jnp.int32),
        compiler_params=_params(("arbitrary",)),
        name="moe_dispatch",
    )(off, cnt8, n_pass, pad, n_valid, x1b, irt)


def _ffn_kernel(be_ref, nr_ref, nv_ref, sg_ref, nx_ref, xs_ref, wg_hbm, bg_ref, wu_hbm, bu_ref, wd_hbm, bd_ref, o_ref,
                wf_ref, wgb_ref, wub_ref, wdb_ref, sem, *, layer):
    i = pl.program_id(0)

    def fetch(e, s, start):
        for j, w_hbm in enumerate((wg_hbm, wu_hbm, wd_hbm)):
            cp = pltpu.make_async_copy(w_hbm.at[layer, e], wf_ref.at[s, j], sem.at[s])
            if start:
                cp.start()
            else:
                cp.wait()

    @pl.when(i < nv_ref[0])
    def _():
        seg = sg_ref[i]

        @pl.when((i == 0) | (seg != sg_ref[jnp.maximum(i - 1, 0)]))
        def _():
            s = seg % 2

            @pl.when(seg == 0)
            def _():
                fetch(be_ref[i], s, True)

            fetch(be_ref[i], s, False)
            wgb_ref[...] = wf_ref[s, 0].astype(BF16)
            wub_ref[...] = wf_ref[s, 1].astype(BF16)
            wdb_ref[...] = wf_ref[s, 2].astype(BF16)

            @pl.when(nx_ref[i] >= 0)
            def _():
                fetch(nx_ref[i], 1 - s, True)

        rows = lax.broadcasted_iota(jnp.int32, xs_ref.shape, 0)
        x = _unpack_bf16_pairs(jnp.where(rows < nr_ref[i], xs_ref[...], 0))
        gate = jnp.minimum(_dot(x, wgb_ref[...]) + bg_ref[...], SWIGLU_LIMIT)
        up = jnp.clip(_dot(x, wub_ref[...]) + bu_ref[...], -SWIGLU_LIMIT, SWIGLU_LIMIT)
        hid = (up + 1.0) * gate * _sigmoid(SWIGLU_ALPHA * gate)
        y = _dot(hid.astype(BF16), wdb_ref[...]) + bd_ref[...]
        o_ref[...] = _pack_bf16_pairs(y.astype(BF16).astype(F32))

    @pl.when(i >= nv_ref[0])
    def _():
        o_ref[...] = jnp.zeros_like(o_ref)


def _experts(xs, block_e, block_rows, n_valid, seg_idx, next_e, params, l, bm):
    n_slots = xs.shape[0]
    nb = n_slots // bm
    assert D_FF == D_MODEL
    blk = lambda i, be, nr, nv, *_: (jnp.minimum(i, nv[0] - 1), 0)
    oblk = lambda i, *_: (i, 0)
    wsp = pl.BlockSpec(memory_space=pl.ANY)
    bsp = pl.BlockSpec((None, None, 1, D_FF), lambda i, be, *_: (l, be[i], 0, 0))
    b4 = lambda a: a.reshape(DEPTH, N_EXPERTS, 1, -1)
    return pl.pallas_call(
        functools.partial(_ffn_kernel, layer=l),
        grid_spec=pltpu.PrefetchScalarGridSpec(
            num_scalar_prefetch=5,
            grid=(nb,),
            in_specs=[pl.BlockSpec((bm, HALF), blk), wsp, bsp, wsp, bsp, wsp, bsp],
            out_specs=pl.BlockSpec((bm, HALF), oblk),
            scratch_shapes=[pltpu.VMEM((2, 3, D_MODEL, D_FF), F32),
                            pltpu.VMEM((D_MODEL, D_FF), BF16), pltpu.VMEM((D_MODEL, D_FF), BF16),
                            pltpu.VMEM((D_FF, D_MODEL), BF16), pltpu.SemaphoreType.DMA((2,))]),
        out_shape=jax.ShapeDtypeStruct((n_slots, HALF), jnp.int32),
        compiler_params=_params(("arbitrary",)),
        name="moe_experts",
    )(block_e, block_rows, n_valid, seg_idx, next_e, xs, params['w_gate'], b4(params['b_gate']),
      params['w_up'], b4(params['b_up']), params['w_down'], b4(params['b_down']))


def _combine_kernel(off_ref, cnt_ref, np_ref, x1_ref, ir_ref, pr_ref, ys_hbm, g2_ref, b2_ref, o_ref,
                    stage_ref, ysb_ref, wgt_ref, acc_ref, sem, *, tm, n_tiles):
    i = pl.program_id(0)
    slot = i % 2
    mk = lambda s: (lambda dst, src: pltpu.make_async_copy(ys_hbm.at[src], stage_ref.at[s, dst], sem.at[s]))

    @pl.when(i == 0)
    def _():
        _region_copies(off_ref, None, 0, 0, mk(0), True)

    col_id = _small_int_bf16(lax.broadcasted_iota(jnp.int32, (tm, ONEHOT_CHUNK), 1))
    ir = ir_ref[...]
    pr = pr_ref[...].astype(BF16)
    zero = jnp.zeros((), BF16)

    def weighted_sum(p):
        tgt = [_slot_targets(ir[:, k:k + 1], ir[:, TOP_K + k:TOP_K + k + 1], p) for k in range(TOP_K)]
        for c in range(N_EXPERTS * REGION_CAP // ONEHOT_CHUNK):
            cols = slice(c * ONEHOT_CHUNK, (c + 1) * ONEHOT_CHUNK)
            wgt = zero
            for k in reversed(range(TOP_K)):
                wgt = jnp.where(col_id == _small_int_bf16(tgt[k] - c * ONEHOT_CHUNK), pr[:, k:k + 1], wgt)
            wgt_ref[:, cols] = wgt
            ysb_ref[cols, :] = _unpack_bf16_pairs(stage_ref[slot, cols, :])
        return _dot(wgt_ref[...], ysb_ref[...])

    _region_copies(off_ref, None, i, 0, mk(slot), False)

    @pl.when(i + 1 < n_tiles)
    def _():
        _region_copies(off_ref, None, i + 1, 0, mk(1 - slot), True)

    acc_ref[...] = weighted_sum(0)

    def extra_pass(p, carry):
        _region_copies(off_ref, cnt_ref, i, p, mk(slot), True)
        _region_copies(off_ref, cnt_ref, i, p, mk(slot), False)
        acc_ref[...] += weighted_sum(p)
        return carry

    lax.fori_loop(1, np_ref[i], extra_pass, 0)
    o_ref[...] = _layer_norm(DN_ALPHA * x1_ref[...] + acc_ref[...], g2_ref[...], b2_ref[...])


def _combine(x1, idx_rank, probs, ys, off, cnt8, n_pass, p, tm):
    t = x1.shape[0]
    row = lambda a: a.reshape(1, D_MODEL).astype(F32)
    full = lambda shp: pl.BlockSpec(shp, lambda i, *_: (0,) * len(shp))
    rowb = lambda w: pl.BlockSpec((tm, w), lambda i, *_: (i, 0))
    return pl.pallas_call(
        functools.partial(_combine_kernel, tm=tm, n_tiles=t // tm),
        grid_spec=pltpu.PrefetchScalarGridSpec(
            num_scalar_prefetch=3,
            grid=(t // tm,),
            in_specs=[rowb(D_MODEL), rowb(LANES), rowb(LANES), pl.BlockSpec(memory_space=pl.ANY),
                      full((1, D_MODEL)), full((1, D_MODEL))],
            out_specs=rowb(D_MODEL),
            scratch_shapes=[pltpu.VMEM((2, N_EXPERTS * REGION_CAP, HALF), jnp.int32),
                            pltpu.VMEM((N_EXPERTS * REGION_CAP, D_MODEL), BF16),
                            pltpu.VMEM((tm, N_EXPERTS * REGION_CAP), BF16),
                            pltpu.VMEM((tm, D_MODEL), F32),
                            pltpu.SemaphoreType.DMA((2,))]),
        out_shape=jax.ShapeDtypeStruct((t, D_MODEL), F32),
        compiler_params=_params(("arbitrary",)),
        name="moe_combine_ln2",
    )(off, cnt8, n_pass, x1, idx_rank, probs, ys, row(p['ln2_g']), row(p['ln2_b']))


def _moe(x1, x1b, idx_rank, irt, probs, tile_counts, p, params, l, bm, tm):
    t = x1.shape[0]
    n_tiles = t // tm
    cnt = tile_counts[:, 0, :N_EXPERTS]
    cnt8 = (cnt + 7) // 8 * 8
    seg = jnp.sum(cnt8, axis=0)
    padded = (seg + SEGMENT_SLACK + bm - 1) // bm * bm
    pend = jnp.cumsum(padded)
    pstart = pend - padded
    off = pstart[None, :] + jnp.cumsum(cnt8, axis=0) - cnt8
    n_pass = jnp.maximum((jnp.max(cnt, axis=1) + REGION_CAP - 1) // REGION_CAP, 1).astype(jnp.int32)
    nb = -(-(t * TOP_K + 7 * n_tiles * N_EXPERTS + SEGMENT_SLACK * N_EXPERTS) // bm) + N_EXPERTS
    n_valid = (pend[-1] // bm).astype(jnp.int32).reshape(1)
    blocks = jnp.arange(nb, dtype=jnp.int32)
    block_e = jnp.minimum(jnp.sum((pend // bm)[None, :] <= blocks[:, None], axis=1), N_EXPERTS - 1).astype(jnp.int32)
    block_rows = jnp.clip(seg[block_e] - (blocks * bm - pstart[block_e]), 0, bm).astype(jnp.int32)
    off = off.astype(jnp.int32).reshape(-1)
    cnt8 = cnt8.astype(jnp.int32).reshape(-1)
    pad = jnp.concatenate([pstart + seg, padded - seg]).astype(jnp.int32)
    first = jnp.concatenate([jnp.ones((1,), jnp.int32), (block_e[1:] != block_e[:-1]).astype(jnp.int32)])
    seg_idx = (jnp.cumsum(first) - 1).astype(jnp.int32)
    next_first = (pend // bm)[block_e]
    next_e = jnp.where(next_first < n_valid[0], block_e[jnp.minimum(next_first, nb - 1)], -1).astype(jnp.int32)
    xs = _dispatch(x1b, irt, off, cnt8, n_pass, pad, n_valid, nb, bm, tm)
    ys = _experts(xs, block_e, block_rows, n_valid, seg_idx, next_e, params, l, bm)
    return _combine(x1, idx_rank, probs, ys, off, cnt8, n_pass, p, tm)


_EXPERT_WEIGHTS = ('w_gate', 'b_gate', 'w_up', 'b_up', 'w_down', 'b_down')


def _layer(x, params, l, b, s, cfg):
    p = {k: v[l] for k, v in params.items() if k not in _EXPERT_WEIGHTS}
    xb, x_groups = _prep(x, cfg['prep_tm'])
    w_in = p['w_in'].astype(BF16)
    o = IN_OFFS
    pad = jnp.zeros((D_MODEL, LANES - 2 * ML_HEADS), w_in.dtype)
    w_ret = w_in[:, o[0]:o[4]]
    w_ml = jnp.concatenate([w_in[:, o[4]:o[5]], w_in[:, o[7]:o[8]]], axis=1)
    w_if = jnp.concatenate([w_in[:, o[5]:o[7]], pad], axis=1)
    w_gates = w_in[:, o[11]:o[12]]
    y_ret = _retention(xb, w_ret, p['ret_gn'], b, s, cfg['seq_rows']).reshape(b * s, MIX_W)
    y_ml = _mlstm(xb, w_ml, w_if, p, b, s, cfg['seq_rows']).reshape(b * s, MIX_W)
    outs, lses = [], []
    wq = ATT_HEADS * ATT_DK
    wv = ATT_HEADS * ATT_DV
    for g in range(ATT_GROUPS):
        assert ATT_DK == 64
        w_att = jnp.concatenate([w_in[:, o[8] + g * wq:o[8] + (g + 1) * wq] * (ATT_DK ** -0.5),
                                 w_in[:, o[9] + g * wq:o[9] + (g + 1) * wq],
                                 w_in[:, o[10] + g * wv:o[10] + (g + 1) * wv]], axis=1)
        d = ATT_PATTERNS[g][1]
        og, lg = _dilated_group(xb if d == 1 else x_groups[d], w_att, g, b, s, cfg['att_nq'])
        outs.append(og)
        lses.append(lg)
    x1, x1b, idx_rank, irt, probs, tile_counts = _merge(x, xb, y_ret, y_ml, outs, lses, w_gates, p, cfg['merge_tm'])
    return _moe(x1, x1b, idx_rank, irt, probs, tile_counts, p, params, l, cfg['moe_bm'], ROUTE_TM)


CFG = dict(prep_tm=1024, seq_rows=512, att_nq=8, merge_tm=512, moe_bm=512)

_PARAM_NAMES = ('w_in', 'ret_gn', 'ml_conv_w', 'ml_conv_b', 'ml_wq', 'ml_wk', 'ml_wv', 'ml_bi', 'ml_bf', 'ml_gn',
                'ml_skip', 'w_branch', 'w_out', 'ln1_g', 'ln1_b', 'w_router', 'b_router', 'w_gate', 'b_gate',
                'w_up', 'b_up', 'w_down', 'b_down', 'ln2_g', 'ln2_b')


def _forward(x, params, cfg):
    b, s, d = x.shape
    xf = x.reshape(b * s, d).astype(F32)
    for l in range(DEPTH):
        xf = _layer(xf, params, l, b, s, cfg)
    return xf.reshape(b, s, d).astype(x.dtype)


def kernel(x, w_in, ret_gn, ml_conv_w, ml_conv_b, ml_wq, ml_wk, ml_wv, ml_bi, ml_bf, ml_gn, ml_skip, w_branch, w_out, ln1_g, ln1_b, w_router, b_router, w_gate, b_gate, w_up, b_up, w_down, b_down, ln2_g, ln2_b):
    params = dict(zip(_PARAM_NAMES, (w_in, ret_gn, ml_conv_w, ml_conv_b, ml_wq, ml_wk, ml_wv, ml_bi, ml_bf, ml_gn,
                                     ml_skip, w_branch, w_out, ln1_g, ln1_b, w_router, b_router, w_gate, b_gate,
                                     w_up, b_up, w_down, b_down, ln2_g, ln2_b)))
    return _forward(x, params, CFG)
```
